```python
import math
import jax
import jax.numpy as jnp
from jax import lax
import numpy as np

D_MODEL = 2048
BATCH = 4
SEQ = 2048
DEPTH = 2
DEC_BATCH = 128
DEC_SEQ = 4
PAST_LEN = 2048
PAGE_SIZE = 128

HEAD_DIM = 64
N_BRANCH = 4
MIX_WIDTH = D_MODEL // N_BRANCH
N_HEADS = MIX_WIDTH // HEAD_DIM
NSA_KV_HEADS = 2
NSA_GROUP = N_HEADS // NSA_KV_HEADS
CMP_LEN = 32
CMP_STRIDE = 16
SLC_BLOCK = 64
SLC_TOPN = 16
WINDOW = 512
MOBA_BLOCK = 256
MOBA_TOPK = 3
S5_GROUP_CH = 16
S5_GROUPS = MIX_WIDTH // S5_GROUP_CH
S5_STATE = 64
D_FF = 5632
CONV_W = 3
Q_BLOCK = 128
GATHER_QUERIES = 64
ALPHA = (2 * DEPTH) ** 0.25
BETA = (8 * DEPTH) ** -0.25
LN_EPS = 1e-5
SCALE = HEAD_DIM ** -0.5
NEG = -1e30
FORCE = 1e4
F32 = jnp.float32

IN_SPLITS = (
    ('nsa_q', N_HEADS * HEAD_DIM),
    ('nsa_kv_cmp', 2 * NSA_KV_HEADS * HEAD_DIM),
    ('nsa_kv_slc', 2 * NSA_KV_HEADS * HEAD_DIM),
    ('nsa_kv_win', 2 * NSA_KV_HEADS * HEAD_DIM),
    ('nsa_gate', 3 * N_HEADS),
    ('s5_u', MIX_WIDTH),
    ('moba_qkv', 3 * N_HEADS * HEAD_DIM),
    ('fox_qkv', 3 * N_HEADS * HEAD_DIM),
    ('fox_f', N_HEADS),
    ('merge_gate', N_BRANCH * D_MODEL),
)
IN_WIDTH = sum(w for _, w in IN_SPLITS)

kernel_name = 'hybrid_nsa_s5_moba_fox_deepnorm_step'


def _split_cols(h):
    offs = np.cumsum([w for _, w in IN_SPLITS])[:-1].tolist()
    parts = jnp.split(h, offs, axis=-1)
    return {name: part for (name, _), part in zip(IN_SPLITS, parts)}


def _alibi_slopes(n):
    return jnp.asarray(2.0 ** (-8.0 * np.arange(1, n + 1) / n), dtype=F32)


def _qblock(T, cap):
    qb = 1
    while qb * 2 <= cap and T % (qb * 2) == 0:
        qb *= 2
    return qb


def _to_blocks(a, qb):
    B, T = a.shape[:2]
    return a.reshape(B, T // qb, qb, *a.shape[2:]).swapaxes(0, 1)


def _from_blocks(a):
    nb, B, qb = a.shape[:3]
    return a.swapaxes(0, 1).reshape(B, nb * qb, *a.shape[3:])


def _layer_norm(x, g, b):
    xf = x.astype(F32)
    mu = jnp.mean(xf, axis=-1, keepdims=True)
    var = jnp.mean(jnp.square(xf - mu), axis=-1, keepdims=True)
    return ((xf - mu) * lax.rsqrt(var + LN_EPS) * g.astype(F32) + b.astype(F32)).astype(x.dtype)


def _masked_softmax(s, valid):
    s = jnp.where(valid, s.astype(F32), NEG)
    m = jnp.max(s, axis=-1, keepdims=True)
    e = jnp.where(valid, jnp.exp(s - m), 0.0)
    return e / jnp.maximum(jnp.sum(e, axis=-1, keepdims=True), 1e-30)


def _paged(pool, page_table):
    rows = pool[page_table]
    return rows.reshape(rows.shape[0], rows.shape[1] * rows.shape[2], *rows.shape[3:])


def _nsa_compress(kv, pos, wk, wv):
    B, L = kv.shape[:2]
    R = CMP_LEN // CMP_STRIDE
    nchunk = L // CMP_STRIDE
    nc = nchunk - R + 1
    chunks = kv[:, :nchunk * CMP_STRIDE].reshape(B, nchunk, CMP_STRIDE, 2, NSA_KV_HEADS, HEAD_DIM)
    w = jnp.stack([wk, wv]).reshape(2, R, CMP_STRIDE, HEAD_DIM, HEAD_DIM)
    part = jnp.einsum('bclkgd,krlde->bcrkge', chunks, w)
    comp = part[:, 0:nc, 0]
    for r in range(1, R):
        comp = comp + part[:, r:r + nc, r]
    bias = jnp.einsum('rld,krlde->ke', pos.reshape(R, CMP_STRIDE, HEAD_DIM), w)
    comp = comp + bias[None, None, :, None, :]
    return comp[:, :, 0], comp[:, :, 1]


def _nsa(q, full_cmp, full_slc, full_win, gates, q_off, w_off, cmp_pos, cmp_wk, cmp_wv):
    B, T, H, dh = q.shape
    G, hg = NSA_KV_HEADS, NSA_GROUP
    L = full_cmp.shape[1]
    slopes = _alibi_slopes(H).reshape(G, hg)
    q_pos = q_off + jnp.arange(T)
    qg = q.reshape(B, T, G, hg, dh)

    kc, vc = _nsa_compress(full_cmp, cmp_pos, cmp_wk, cmp_wv)
    nc = kc.shape[1]
    c_end = jnp.arange(nc) * CMP_STRIDE + CMP_LEN - 1
    dist_c = (q_pos[:, None] - c_end[None, :]).astype(F32)
    s_c = jnp.einsum('btghd,bngd->bghtn', qg, kc).astype(F32) * SCALE - slopes[:, :, None, None] * dist_c
    p_c = _masked_softmax(s_c, dist_c >= 0)
    o_cmp = jnp.einsum('bghtn,bngd->btghd', p_c.astype(vc.dtype), vc)

    ns = -(-L // SLC_BLOCK)
    i_c = np.arange(nc)[:, None] * CMP_STRIDE
    j_s = np.arange(ns)[None, :] * SLC_BLOCK
    overlap = jnp.asarray((i_c < j_s + SLC_BLOCK) & (i_c + CMP_LEN > j_s), dtype=F32)
    imp = jnp.einsum('bghtn,nj->btgj', p_c, overlap)
    j = jnp.arange(ns)[None, :]
    cur = (q_pos // SLC_BLOCK)[:, None]
    forced = (j == 0) | (j == cur) | (j == cur - 1)
    imp = jnp.where(forced[None, :, None, :], FORCE, imp)
    imp = jnp.where((j <= cur)[None, :, None, :], imp, NEG)
    _, sel = lax.top_k(imp, min(SLC_TOPN, ns))
    n_sel = sel.shape[-1]

    kvs = jnp.pad(full_slc, ((0, 0), (0, ns * SLC_BLOCK - L), (0, 0), (0, 0), (0, 0)))
    kvs = kvs.reshape(B, ns, SLC_BLOCK, 2, G, dh).transpose(0, 4, 1, 2, 3, 5)
    kvw = jnp.pad(full_win, ((0, 0), (WINDOW, 0), (0, 0), (0, 0), (0, 0)))
    qb = _qblock(T, max(1, GATHER_QUERIES // B))
    n_win = WINDOW + qb - 1
    bi = jnp.arange(B)[:, None, None, None]
    gi = jnp.arange(G)[None, :, None, None]
    offs = jnp.arange(SLC_BLOCK)

    def block(args):
        start, q_b, sel_b = args
        t = q_off + start + jnp.arange(qb)
        sel_t = sel_b.transpose(0, 2, 1, 3)
        kv_sel = kvs[bi, gi, sel_t]
        d_s = (t[None, None, :, None, None] - (sel_t[..., None] * SLC_BLOCK + offs)).astype(F32)
        s_s = jnp.einsum('btghd,bgtnsd->bghtns', q_b, kv_sel[..., 0, :]).astype(F32) * SCALE
        s_s = s_s - slopes[None, :, :, None, None, None] * d_s[:, :, None]
        ok_s = jnp.broadcast_to((d_s >= 0)[:, :, None], s_s.shape)
        flat = (B, G, hg, qb, n_sel * SLC_BLOCK)
        p_s = _masked_softmax(s_s.reshape(flat), ok_s.reshape(flat))
        v_sel = kv_sel[..., 1, :].reshape(B, G, qb, n_sel * SLC_BLOCK, dh)
        o_s = jnp.einsum('bghtk,bgtkd->btghd', p_s.astype(v_sel.dtype), v_sel)
        kw = lax.dynamic_slice_in_dim(kvw, start + (q_off + 1 - w_off), n_win, axis=1)
        kpos = q_off + start - WINDOW + 1 + jnp.arange(n_win)
        d_w = t[:, None] - kpos[None, :]
        ok_w = (d_w >= 0) & (d_w < WINDOW) & (kpos >= w_off)[None, :]
        s_w = jnp.einsum('btghd,bsgd->bghts', q_b, kw[:, :, 0]).astype(F32) * SCALE
        s_w = s_w - slopes[:, :, None, None] * d_w.astype(F32)
        p_w = _masked_softmax(s_w, ok_w)
        o_w = jnp.einsum('bghts,bsgd->btghd', p_w.astype(kw.dtype), kw[:, :, 1])
        return o_s, o_w

    starts = jnp.arange(T // qb) * qb
    o_slc, o_win = lax.map(block, (starts, _to_blocks(qg, qb), _to_blocks(sel, qb)))
    o_slc, o_win = _from_blocks(o_slc), _from_blocks(o_win)
    g = gates.reshape(B, T, 3, G, hg)[..., None]
    o = g[:, :, 0] * o_cmp + g[:, :, 1] * o_slc + g[:, :, 2] * o_win
    return o.reshape(B, T, H * dh)


def _s5_combine(e1, e2):
    a1r, a1i, b1r, b1i = e1
    a2r, a2i, b2r, b2i = e2
    return (a1r * a2r - a1i * a2i, a1r * a2i + a1i * a2r,
            a2r * b1r - a2i * b1i + b2r, a2r * b1i + a2i * b1r + b2i)


def _s5(u, a_re, a_im, b_re, b_im, c_re, c_im, d, log_step, w_glu, h0):
    B, T, W = u.shape
    ug = u.astype(F32).reshape(B, T, S5_GROUPS, S5_GROUP_CH)
    step = jnp.exp(log_step.astype(F32))[:, None]
    ar, ai = a_re.astype(F32), a_im.astype(F32)
    mag = jnp.exp(ar * step)
    abar_re, abar_im = mag * jnp.cos(ai * step), mag * jnp.sin(ai * step)
    den = ar * ar + ai * ai
    z_re = (ar * (abar_re - 1.0) + ai * abar_im) / den
    z_im = (ar * abar_im - ai * (abar_re - 1.0)) / den
    br, bim = b_re.astype(F32), b_im.astype(F32)
    bbar_re = z_re[..., None] * br - z_im[..., None] * bim
    bbar_im = z_re[..., None] * bim + z_im[..., None] * br
    bu_re = jnp.einsum('btgc,gnc->btgn', ug, bbar_re)
    bu_im = jnp.einsum('btgc,gnc->btgn', ug, bbar_im)
    elems = (jnp.broadcast_to(abar_re, bu_re.shape), jnp.broadcast_to(abar_im, bu_re.shape), bu_re, bu_im)
    pa_r, pa_i, hr, hi = lax.associative_scan(_s5_combine, elems, axis=1)
    if h0 is not None:
        h0r = h0[0].astype(F32)[:, None]
        h0i = h0[1].astype(F32)[:, None]
        hr, hi = hr + pa_r * h0r - pa_i * h0i, hi + pa_r * h0i + pa_i * h0r
    y = jnp.einsum('btgn,gcn->btgc', hr, c_re.astype(F32)) - jnp.einsum('btgn,gcn->btgc', hi, c_im.astype(F32))
    y = y.reshape(B, T, W) + d.astype(F32) * u.astype(F32)
    z = jax.nn.gelu(y).astype(u.dtype) @ w_glu
    out = z[..., :W] * jax.nn.sigmoid(z[..., W:])
    return out, hr[:, -1], hi[:, -1]


def _moba(q, full_kv, q_off):
    B, T, H, dh = q.shape
    L = full_kv.shape[1]
    nb = -(-L // MOBA_BLOCK)
    kvp = jnp.pad(full_kv, ((0, 0), (0, nb * MOBA_BLOCK - L), (0, 0), (0, 0), (0, 0)))
    kvp = kvp.reshape(B, nb, MOBA_BLOCK, 2, H, dh).transpose(0, 4, 1, 2, 3, 5)
    k_mean = jnp.mean(kvp[..., 0, :].astype(F32), axis=3)
    q_pos = q_off + jnp.arange(T)
    cur = (q_pos // MOBA_BLOCK)[:, None]
    past_blk = jnp.arange(nb)[None, :] < cur
    gate = jnp.einsum('bthd,bhnd->bthn', q.astype(F32), k_mean)
    gate = jnp.where(past_blk[None, :, None, :], gate, NEG)
    _, sel = lax.top_k(gate, max(1, min(MOBA_TOPK, nb - 1)))
    k_sel = sel.shape[-1]
    slopes = _alibi_slopes(H)
    qb = _qblock(T, max(1, GATHER_QUERIES // B))
    bi = jnp.arange(B)[:, None, None, None]
    hi = jnp.arange(H)[None, :, None, None]
    offs = jnp.arange(MOBA_BLOCK)
    n_s = k_sel * MOBA_BLOCK

    def block(args):
        start, q_b, sel_b = args
        t = q_off + start + jnp.arange(qb)
        own = (q_off + start) // MOBA_BLOCK
        kv_own = lax.dynamic_index_in_dim(kvp, own, axis=2, keepdims=False)
        d_o = t[:, None] - (own * MOBA_BLOCK + offs)[None, :]
        s_o = jnp.einsum('bthd,bhsd->bhts', q_b, kv_own[..., 0, :]).astype(F32) * SCALE
        s_o = s_o - slopes[:, None, None] * d_o.astype(F32)
        ok_o = jnp.broadcast_to(d_o >= 0, s_o.shape)
        sel_t = sel_b.transpose(0, 2, 1, 3)
        kv_sel = kvp[bi, hi, sel_t]
        d_s = (t[None, None, :, None, None] - (sel_t[..., None] * MOBA_BLOCK + offs)).astype(F32)
        s_s = jnp.einsum('bthd,bhtksd->bhtks', q_b, kv_sel[..., 0, :]).astype(F32) * SCALE
        s_s = s_s - slopes[None, :, None, None, None] * d_s
        ok_s = jnp.broadcast_to((sel_t < own)[..., None], s_s.shape)
        p = _masked_softmax(jnp.concatenate([s_o, s_s.reshape(B, H, qb, n_s)], axis=-1),
                            jnp.concatenate([ok_o, ok_s.reshape(B, H, qb, n_s)], axis=-1))
        p = p.astype(kvp.dtype)
        o = jnp.einsum('bhts,bhsd->bthd', p[..., :MOBA_BLOCK], kv_own[..., 1, :])
        o = o + jnp.einsum('bhtks,bhtksd->bthd', p[..., MOBA_BLOCK:].reshape(B, H, qb, k_sel, MOBA_BLOCK), kv_sel[..., 1, :])
        return o

    starts = jnp.arange(T // qb) * qb
    o = _from_blocks(lax.map(block, (starts, _to_blocks(q, qb), _to_blocks(sel, qb))))
    return o.reshape(B, T, H * dh)


def _fox(q, full_kv, full_logf, q_off):
    B, T, H, dh = q.shape
    L = full_kv.shape[1]
    cum = jnp.cumsum(full_logf.astype(F32), axis=1).transpose(0, 2, 1)
    c_q = cum[:, :, q_off:q_off + T].transpose(0, 2, 1)
    k, v = full_kv[:, :, 0], full_kv[:, :, 1]
    kpos = jnp.arange(L)
    qb = _qblock(T, Q_BLOCK)

    def block(args):
        start, q_b, c_b = args
        t = q_off + start + jnp.arange(qb)
        s = jnp.einsum('bthd,bshd->bhts', q_b, k).astype(F32) * SCALE
        s = s + c_b.transpose(0, 2, 1)[..., None] - cum[:, :, None, :]
        p = _masked_softmax(s, kpos[None, :] <= t[:, None])
        return jnp.einsum('bhts,bshd->bthd', p.astype(v.dtype), v)

    starts = jnp.arange(T // qb) * qb
    o = _from_blocks(lax.map(block, (starts, _to_blocks(q, qb), _to_blocks(c_q, qb))))
    return o.reshape(B, T, H * dh)


def _conv_ffn(x, w_up, conv_w, conv_b, w_down, buf):
    B, T, _ = x.shape
    h = x @ w_up
    if buf is None:
        hp = jnp.pad(h, ((0, 0), (CONV_W - 1, 0), (0, 0)))
    else:
        hp = jnp.concatenate([buf.astype(h.dtype), h], axis=1)
    hc = conv_b + sum(conv_w[k] * hp[:, k:k + T] for k in range(CONV_W))
    gate, val = jnp.split(hc, 2, axis=-1)
    return (jax.nn.silu(gate) * val) @ w_down, hp[:, T:]


def _layer(x, q_off, lp, past):
    B, T, _ = x.shape
    G, H, dh = NSA_KV_HEADS, N_HEADS, HEAD_DIM
    c = _split_cols(x @ lp['w_in'])
    q_nsa = c['nsa_q'].reshape(B, T, H, dh)
    kv_cmp = c['nsa_kv_cmp'].reshape(B, T, 2, G, dh)
    kv_slc = c['nsa_kv_slc'].reshape(B, T, 2, G, dh)
    kv_win = c['nsa_kv_win'].reshape(B, T, 2, G, dh)
    nsa_g = jax.nn.sigmoid(c['nsa_gate'].reshape(B, T, 3, H))
    qkv_m = c['moba_qkv'].reshape(B, T, 3, H, dh)
    q_m, kv_m = qkv_m[:, :, 0], qkv_m[:, :, 1:]
    qkv_f = c['fox_qkv'].reshape(B, T, 3, H, dh)
    q_f, kv_f = qkv_f[:, :, 0], qkv_f[:, :, 1:]
    logf = jax.nn.log_sigmoid(c['fox_f'].astype(F32) + lp['fox_b_f'].astype(F32))

    if past is None:
        full_cmp, full_slc, full_win, full_m, full_f, full_logf = kv_cmp, kv_slc, kv_win, kv_m, kv_f, logf
        w_off, h0, conv_buf = 0, None, None
    else:
        def cat(old, new):
            return jnp.concatenate([old.astype(new.dtype), new], axis=1)
        full_cmp = cat(past['nsa_cmp'], kv_cmp)
        full_slc = cat(past['nsa_slc'], kv_slc)
        full_win = cat(past['nsa_win'], kv_win)
        full_m = cat(past['moba'], kv_m)
        full_f = cat(past['fox'], kv_f)
        full_logf = cat(past['fox_logf'], logf)
        w_off = q_off - past['nsa_win'].shape[1]
        h0, conv_buf = past['s5'], past['ffn_conv']

    o_nsa = _nsa(q_nsa, full_cmp, full_slc, full_win, nsa_g, q_off, w_off,
                 lp['nsa_cmp_pos'], lp['nsa_cmp_wk'], lp['nsa_cmp_wv'])
    o_s5, s5_re, s5_im = _s5(c['s5_u'], lp['s5_a_re'], lp['s5_a_im'], lp['s5_b_re'], lp['s5_b_im'],
                             lp['s5_c_re'], lp['s5_c_im'], lp['s5_d'], lp['s5_log_step'], lp['s5_w_glu'], h0)
    o_moba = _moba(q_m, full_m, q_off)
    o_fox = _fox(q_f, full_f, full_logf, q_off)

    branches = jnp.stack([o_nsa, o_s5, o_moba, o_fox], axis=2)
    proj = jnp.einsum('btim,imd->btid', branches, lp['w_branch'])
    merge = jax.nn.sigmoid(c['merge_gate'].reshape(B, T, N_BRANCH, D_MODEL))
    mixed = jnp.sum(merge * proj, axis=2) @ lp['w_out']
    h = _layer_norm(ALPHA * x + mixed, lp['ln1_g'], lp['ln1_b'])
    f, conv_state = _conv_ffn(h, lp['ffn_w_up'], lp['ffn_conv_w'], lp['ffn_conv_b'], lp['ffn_w_down'], conv_buf)
    y = _layer_norm(ALPHA * h + f, lp['ln2_g'], lp['ln2_b'])
    lw = full_win.shape[1]
    new_win = full_win[:, lw - min(WINDOW, lw):]
    return y, (kv_cmp, kv_slc, kv_m, kv_f, logf, new_win, s5_re, s5_im, conv_state)


def setup_inputs(seed: int = 0) -> dict:
    key = jax.random.key(seed)
    ks = jax.random.split(key, 40)

    def nrm(i, shape, scale=1.0):
        return jax.random.normal(ks[i], shape, jnp.float32) * scale

    n_pages = PAST_LEN // PAGE_SIZE
    n_used = DEC_BATCH * n_pages
    n_phys = n_used + -(-n_used // 4)
    wb = min(WINDOW, PAST_LEN)
    G, H, dh = NSA_KV_HEADS, N_HEADS, HEAD_DIM
    page_table = jax.random.permutation(ks[0], n_phys)[:n_used].reshape(DEC_BATCH, n_pages).astype(jnp.int32)
    s5_n = jnp.arange(S5_STATE, dtype=jnp.float32)
    return {
        'x_prompt': nrm(1, (BATCH, SEQ, D_MODEL)),
        'x_sample': nrm(2, (DEC_BATCH, DEC_SEQ, D_MODEL)),
        'cache_nsa_cmp_kv': nrm(3, (DEPTH, n_phys, PAGE_SIZE, 2, G, dh)),
        'cache_nsa_slc_kv': nrm(4, (DEPTH, n_phys, PAGE_SIZE, 2, G, dh)),
        'cache_moba_kv': nrm(5, (DEPTH, n_phys, PAGE_SIZE, 2, H, dh)),
        'cache_fox_kv': nrm(6, (DEPTH, n_phys, PAGE_SIZE, 2, H, dh)),
        'cache_fox_logf': jax.nn.log_sigmoid(nrm(7, (DEPTH, n_phys, PAGE_SIZE, H)) + 2.5),
        'page_table': page_table,
        'cache_nsa_win_kv': nrm(8, (DEPTH, DEC_BATCH, wb, 2, G, dh)),
        'state_s5_re': nrm(9, (DEPTH, DEC_BATCH, S5_GROUPS, S5_STATE), 0.3),
        'state_s5_im': nrm(10, (DEPTH, DEC_BATCH, S5_GROUPS, S5_STATE), 0.3),
        'state_ffn_conv': nrm(11, (DEPTH, DEC_BATCH, CONV_W - 1, 2 * D_FF)),
        'w_in': nrm(12, (DEPTH, D_MODEL, IN_WIDTH), D_MODEL ** -0.5),
        'fox_b_f': jax.random.uniform(ks[13], (DEPTH, N_HEADS), jnp.float32, 1.0, 4.0),
        'nsa_cmp_pos': nrm(14, (DEPTH, CMP_LEN, HEAD_DIM), 0.02),
        'nsa_cmp_wk': nrm(15, (DEPTH, CMP_LEN * HEAD_DIM, HEAD_DIM), (CMP_LEN * HEAD_DIM) ** -0.5),
        'nsa_cmp_wv': nrm(16, (DEPTH, CMP_LEN * HEAD_DIM, HEAD_DIM), (CMP_LEN * HEAD_DIM) ** -0.5),
        's5_a_re': -0.5 * jnp.exp(nrm(17, (DEPTH, S5_GROUPS, S5_STATE), 0.01)),
        's5_a_im': jnp.pi * s5_n + nrm(18, (DEPTH, S5_GROUPS, S5_STATE), 0.01),
        's5_b_re': nrm(19, (DEPTH, S5_GROUPS, S5_STATE, S5_GROUP_CH), (2 * S5_GROUP_CH) ** -0.5),
        's5_b_im': nrm(20, (DEPTH, S5_GROUPS, S5_STATE, S5_GROUP_CH), (2 * S5_GROUP_CH) ** -0.5),
        's5_c_re': nrm(21, (DEPTH, S5_GROUPS, S5_GROUP_CH, S5_STATE), S5_STATE ** -0.5),
        's5_c_im': nrm(22, (DEPTH, S5_GROUPS, S5_GROUP_CH, S5_STATE), S5_STATE ** -0.5),
        's5_d': nrm(23, (DEPTH, MIX_WIDTH)),
        's5_log_step': jax.random.uniform(ks[24], (DEPTH, S5_GROUPS), jnp.float32, math.log(1e-3), math.log(1e-1)),
        's5_w_glu': nrm(25, (DEPTH, MIX_WIDTH, 2 * MIX_WIDTH), MIX_WIDTH ** -0.5),
        'w_branch': nrm(26, (DEPTH, N_BRANCH, MIX_WIDTH, D_MODEL), MIX_WIDTH ** -0.5),
        'w_out': nrm(27, (DEPTH, D_MODEL, D_MODEL), BETA * D_MODEL ** -0.5),
        'ln1_g': 1.0 + nrm(28, (DEPTH, D_MODEL), 0.01),
        'ln1_b': nrm(29, (DEPTH, D_MODEL), 0.01),
        'ffn_w_up': nrm(30, (DEPTH, D_MODEL, 2 * D_FF), D_MODEL ** -0.5),
        'ffn_conv_w': nrm(31, (DEPTH, CONV_W, 2 * D_FF), CONV_W ** -0.5),
        'ffn_conv_b': nrm(32, (DEPTH, 2 * D_FF), 0.01),
        'ffn_w_down': nrm(33, (DEPTH, D_FF, D_MODEL), BETA * D_FF ** -0.5),
        'ln2_g': 1.0 + nrm(34, (DEPTH, D_MODEL), 0.01),
        'ln2_b': nrm(35, (DEPTH, D_MODEL), 0.01),
    }


def reference(x_prompt, x_sample, cache_nsa_cmp_kv, cache_nsa_slc_kv, cache_moba_kv, cache_fox_kv,
              cache_fox_logf, page_table, cache_nsa_win_kv, state_s5_re, state_s5_im, state_ffn_conv,
              w_in, fox_b_f, nsa_cmp_pos, nsa_cmp_wk, nsa_cmp_wv, s5_a_re, s5_a_im, s5_b_re, s5_b_im,
              s5_c_re, s5_c_im, s5_d, s5_log_step, s5_w_glu, w_branch, w_out, ln1_g, ln1_b,
              ffn_w_up, ffn_conv_w, ffn_conv_b, ffn_w_down, ln2_g, ln2_b):
    past_len = page_table.shape[1] * cache_nsa_cmp_kv.shape[2]
    y_prompt, y_sample = x_prompt, x_sample
    st_prompt, st_sample = [], []
    for l in range(DEPTH):
        lp = {
            'w_in': w_in[l], 'fox_b_f': fox_b_f[l], 'nsa_cmp_pos': nsa_cmp_pos[l],
            'nsa_cmp_wk': nsa_cmp_wk[l], 'nsa_cmp_wv': nsa_cmp_wv[l],
            's5_a_re': s5_a_re[l], 's5_a_im': s5_a_im[l], 's5_b_re': s5_b_re[l], 's5_b_im': s5_b_im[l],
            's5_c_re': s5_c_re[l], 's5_c_im': s5_c_im[l], 's5_d': s5_d[l], 's5_log_step': s5_log_step[l],
            's5_w_glu': s5_w_glu[l], 'w_branch': w_branch[l], 'w_out': w_out[l],
            'ln1_g': ln1_g[l], 'ln1_b': ln1_b[l], 'ffn_w_up': ffn_w_up[l], 'ffn_conv_w': ffn_conv_w[l],
            'ffn_conv_b': ffn_conv_b[l], 'ffn_w_down': ffn_w_down[l], 'ln2_g': ln2_g[l], 'ln2_b': ln2_b[l],
        }
        past = {
            'nsa_cmp': _paged(cache_nsa_cmp_kv[l], page_table),
            'nsa_slc': _paged(cache_nsa_slc_kv[l], page_table),
            'moba': _paged(cache_moba_kv[l], page_table),
            'fox': _paged(cache_fox_kv[l], page_table),
            'fox_logf': _paged(cache_fox_logf[l], page_table),
            'nsa_win': cache_nsa_win_kv[l],
            's5': (state_s5_re[l], state_s5_im[l]),
            'ffn_conv': state_ffn_conv[l],
        }
        y_prompt, st_p = _layer(y_prompt, 0, lp, None)
        y_sample, st_s = _layer(y_sample, past_len, lp, past)
        st_prompt.append(st_p)
        st_sample.append(st_s)
    (p_cmp, p_slc, p_moba, p_fox, p_logf, p_win, p_s5r, p_s5i, p_conv) = [jnp.stack(z) for z in zip(*st_prompt)]
    (s_cmp, s_slc, s_moba, s_fox, s_logf, s_win, s_s5r, s_s5i, s_conv) = [jnp.stack(z) for z in zip(*st_sample)]
    return (y_prompt, y_sample, p_cmp, s_cmp, p_slc, s_slc, p_moba, s_moba, p_fox, s_fox,
            p_logf, s_logf, p_win, s_win, p_s5r, s_s5r, p_s5i, s_s5i, p_conv, s_conv)
```

```python
import functools

import numpy as np
import jax
import jax.numpy as jnp
from jax import lax
from jax.experimental import pallas as pl
from jax.experimental.pallas import tpu as pltpu

F32 = jnp.float32
BF16 = jnp.bfloat16
HI = lax.Precision.HIGHEST

LANE = 128
SUBLANE = 8
VMEM_LIMIT = 56 * 1024 * 1024

D_MODEL = 2048
HEAD_DIM = 64
N_BRANCH = 4
MIX = D_MODEL // N_BRANCH
N_HEADS = MIX // HEAD_DIM
KV_G = 2
HG = N_HEADS // KV_G
CMP_LEN = 32
CMP_STRIDE = 16
SLC_BLOCK = 64
SLC_TOPN = 16
WINDOW = 512
MOBA_BLOCK = 256
MOBA_TOPK = 3
S5_CH = 16
S5_GROUPS = MIX // S5_CH
S5_STATE = 64
D_FF = 5632
CONV_W = 3
LN_EPS = 1e-5
SCALE = HEAD_DIM ** -0.5
NEG = -1e30
FORCE = 1e4
T_PAD = 8
SLOPES = tuple(float(v) for v in np.asarray(2.0 ** (-8.0 * np.arange(1, N_HEADS + 1) / N_HEADS), np.float32))

_SPLITS = (('nsa_q', MIX), ('kv_cmp', 2 * KV_G * HEAD_DIM), ('kv_slc', 2 * KV_G * HEAD_DIM),
           ('kv_win', 2 * KV_G * HEAD_DIM), ('nsa_gate', 3 * N_HEADS), ('s5_u', MIX),
           ('moba', 3 * MIX), ('fox', 3 * MIX), ('fox_f', N_HEADS), ('merge', N_BRANCH * D_MODEL))
_SRC = {}
_o = 0
for _n, _w in _SPLITS:
    _SRC[_n] = (_o, _w)
    _o += _w
IN_WIDTH = _o
_COL = dict(merge=0, nsa_q=8192, kv_cmp=8704, kv_slc=8960, kv_win=9216, nsa_gate=9472, s5_u=9600,
            moba=10240, fox=11776, fox_f=13312)
WP = 13824


def _cparams(sem):
    return pltpu.CompilerParams(dimension_semantics=sem, vmem_limit_bytes=VMEM_LIMIT)


def _dot_nt(a, b, precision=None):
    return lax.dot_general(a, b, (((1,), (1,)), ((), ())), precision=precision, preferred_element_type=F32)


def _sigmoid(x):
    return 1.0 / (1.0 + jnp.exp(-x))


def _pack_w_in(w):
    d = w.shape[0]
    order = ('merge', 'nsa_q', 'kv_cmp', 'kv_slc', 'kv_win', 'nsa_gate', 's5_u', 'moba', 'fox', 'fox_f')
    parts, pos = [], 0
    for name in order:
        if _COL[name] > pos:
            parts.append(jnp.zeros((d, _COL[name] - pos), w.dtype))
        s, wd = _SRC[name]
        parts.append(w[:, s:s + wd])
        pos = _COL[name] + wd
    parts.append(jnp.zeros((d, WP - pos), w.dtype))
    return jnp.concatenate(parts, axis=1).astype(BF16)


def _mm_kernel(x_ref, w_ref, o_ref, xb_ref):
    @pl.when(pl.program_id(1) == 0)
    def _():
        xb_ref[...] = x_ref[...].astype(BF16)

    o_ref[...] = jnp.dot(xb_ref[...], w_ref[...], preferred_element_type=F32)


def _matmul(x, w, tm, tn):
    m, k = x.shape
    n = w.shape[1]
    return pl.pallas_call(
        _mm_kernel, grid=(m // tm, n // tn),
        in_specs=[pl.BlockSpec((tm, k), lambda i, j: (i, 0)), pl.BlockSpec((k, tn), lambda i, j: (0, j))],
        out_specs=pl.BlockSpec((tm, tn), lambda i, j: (i, j)),
        out_shape=jax.ShapeDtypeStruct((m, n), F32),
        scratch_shapes=[pltpu.VMEM((tm, k), BF16)],
        compiler_params=_cparams(("parallel", "arbitrary")), name="mm")(x, w)


def _logf_kernel(c_ref, b_ref, o_ref):
    x = c_ref[...] + b_ref[...]
    y = jnp.minimum(x, 0.0) - jnp.log1p(jnp.exp(-jnp.abs(x)))
    o_ref[...] = y[:, :N_HEADS]


def _logf(c, b_pad, tm):
    n = c.shape[0]
    return pl.pallas_call(
        _logf_kernel, grid=(n // tm,),
        in_specs=[pl.BlockSpec((tm, LANE), lambda i: (i, _COL['fox_f'] // LANE)),
                  pl.BlockSpec((1, LANE), lambda i: (0, 0))],
        out_specs=pl.BlockSpec((tm, N_HEADS), lambda i: (i, 0)),
        out_shape=jax.ShapeDtypeStruct((n, N_HEADS), F32),
        compiler_params=_cparams(("parallel",)), name="logf")(c, b_pad)


def _cumsum_kernel(*refs, n_in):
    in_refs, o_ref = refs[-n_in - 1:-1], refs[-1]
    r = lax.broadcasted_iota(jnp.int32, (LANE, LANE), 0)
    c = lax.broadcasted_iota(jnp.int32, (LANE, LANE), 1)
    tri = jnp.where(r <= c, 1.0, 0.0).astype(F32)
    carry = jnp.zeros((N_HEADS, 1), F32)
    off = 0
    for ref in in_refs:
        for j in range(ref.shape[-1] // LANE):
            x = ref[0, :, j * LANE:(j + 1) * LANE]
            cs = jnp.dot(x, tri, precision=HI, preferred_element_type=F32) + carry
            o_ref[0, :, off:off + LANE] = cs
            carry = cs[:, LANE - 1:LANE]
            off += LANE


def _cumsum_prompt(logf_t):
    b, h, t = logf_t.shape
    return pl.pallas_call(
        functools.partial(_cumsum_kernel, n_in=1), grid=(b,),
        in_specs=[pl.BlockSpec((1, h, t), lambda i: (i, 0, 0))],
        out_specs=pl.BlockSpec((1, h, t), lambda i: (i, 0, 0)),
        out_shape=jax.ShapeDtypeStruct((b, h, t), F32),
        compiler_params=_cparams(("parallel",)), name="fox_cumsum_prompt")(logf_t)


def _cumsum_paged(page_table, pool_t, new_t):
    bs, npages = page_table.shape
    page = pool_t.shape[-1]
    in_specs = [pl.BlockSpec((1, N_HEADS, page), lambda b, pt, p=p: (pt[b, p], 0, 0)) for p in range(npages)]
    in_specs.append(pl.BlockSpec((1, N_HEADS, LANE), lambda b, pt: (b, 0, 0)))
    tot = npages * page + LANE
    return pl.pallas_call(
        functools.partial(_cumsum_kernel, n_in=npages + 1),
        grid_spec=pltpu.PrefetchScalarGridSpec(
            num_scalar_prefetch=1, grid=(bs,), in_specs=in_specs,
            out_specs=pl.BlockSpec((1, N_HEADS, tot), lambda b, pt: (b, 0, 0))),
        out_shape=jax.ShapeDtypeStruct((bs, N_HEADS, tot), F32),
        compiler_params=_cparams(("parallel",)), name="fox_cumsum_paged")(
            page_table, *([pool_t] * npages), new_t)


def _online_update(carry, s, valid, v):
    m, l, acc = carry
    m_new = jnp.maximum(m, jnp.max(s, axis=1, keepdims=True))
    p = jnp.where(valid, jnp.exp(s - m_new), 0.0)
    alpha = jnp.exp(m - m_new)
    l = alpha * l + jnp.sum(p, axis=1, keepdims=True)
    acc = alpha * acc + jnp.dot(p.astype(BF16), v, preferred_element_type=F32)
    return m_new, l, acc


def _online_init(rows, dv):
    return (jnp.full((rows, 1), NEG, F32), jnp.zeros((rows, 1), F32), jnp.zeros((rows, dv), F32))


def _online_finish(carry):
    _, l, acc = carry
    return acc / jnp.maximum(l, 1e-30)


def _attend_tiles(s_tiles, v_tiles):
    m = functools.reduce(jnp.maximum, [jnp.max(s, axis=1, keepdims=True) for s in s_tiles])
    l, acc = 0.0, 0.0
    for s, v in zip(s_tiles, v_tiles):
        e = jnp.where(s > 0.5 * NEG, jnp.exp(s - m), 0.0)
        l = l + jnp.sum(e, axis=1, keepdims=True)
        acc = acc + jnp.dot(e.astype(BF16), v, preferred_element_type=F32)
    return acc / jnp.maximum(l, 1e-30)


def _topn_mask(v, ncols, topn):
    jl = lax.broadcasted_iota(jnp.int32, v.shape, 1)
    rank = jnp.zeros(v.shape, F32)
    for j2 in range(ncols):
        col = v[:, j2:j2 + 1]
        beats = (col > v) | ((col == v) & (jl > j2))
        rank = rank + jnp.where(beats, 1.0, 0.0)
    return rank < topn


def _pad_rows(a, rows):
    return jnp.concatenate([a, jnp.zeros((rows - a.shape[0], a.shape[1]), a.dtype)], axis=0)


def _fox_prompt_kernel(q_ref, k_ref, v_ref, cc_ref, cr_ref, o_ref, *, tq):
    qi = pl.program_id(2)
    row = lax.broadcasted_iota(jnp.int32, (tq, tq), 0)
    col = lax.broadcasted_iota(jnp.int32, (tq, tq), 1)
    outs = []
    for h2 in range(2):
        lo = h2 * HEAD_DIM
        q = (q_ref[:, lo:lo + HEAD_DIM] * SCALE).astype(BF16)
        cq = cc_ref[0, h2]

        def body(kj, carry, lo=lo, q=q, cq=cq, h2=h2):
            ks = pl.multiple_of(kj * tq, tq)
            k = k_ref[pl.ds(ks, tq), lo:lo + HEAD_DIM].astype(BF16)
            v = v_ref[pl.ds(ks, tq), lo:lo + HEAD_DIM].astype(BF16)
            s = _dot_nt(q, k) + (cq - cr_ref[0, h2, :, pl.ds(ks, tq)])
            valid = (col <= row) | ((jnp.zeros((1, 1), jnp.int32) + kj) < qi)
            s = jnp.where(valid, s, NEG)
            return _online_update(carry, s, valid, v)

        outs.append(_online_finish(lax.fori_loop(0, qi + 1, body, _online_init(tq, HEAD_DIM))))
    o_ref[...] = jnp.concatenate(outs, axis=1)


def _fox_prompt(c, cumc, cumr, b, t, tq):
    nq = t // tq
    base = _COL['fox'] // LANE
    hp_n = N_HEADS // 2
    return pl.pallas_call(
        functools.partial(_fox_prompt_kernel, tq=tq), grid=(b, hp_n, nq),
        in_specs=[pl.BlockSpec((tq, LANE), lambda i, h, q: (i * nq + q, base + h)),
                  pl.BlockSpec((t, LANE), lambda i, h, q: (i, base + hp_n + h)),
                  pl.BlockSpec((t, LANE), lambda i, h, q: (i, base + 2 * hp_n + h)),
                  pl.BlockSpec((1, 2, tq, 1), lambda i, h, q: (i, h, q, 0)),
                  pl.BlockSpec((1, 2, 1, t), lambda i, h, q: (i, h, 0, 0))],
        out_specs=pl.BlockSpec((tq, LANE), lambda i, h, q: (i * nq + q, h)),
        out_shape=jax.ShapeDtypeStruct((b * t, MIX), F32),
        compiler_params=_cparams(("parallel", "arbitrary", "arbitrary")), name="fox_prompt")(
            c, c, c, cumc, cumr)


def _head_slope(hp, h2):
    s = jnp.float32(SLOPES[h2])
    for k in range(1, N_HEADS // 2):
        s = jnp.where(hp == k, jnp.float32(SLOPES[2 * k + h2]), s)
    return s


def _moba_prompt_kernel(q_ref, k_ref, v_ref, o_ref, kmean_ref, *, tq, nb):
    hp = pl.program_id(1)
    qi = pl.program_id(2)

    @pl.when(qi == 0)
    def _():
        kmean_ref[...] = jnp.zeros_like(kmean_ref)
        for n in range(nb):
            kmean_ref[n:n + 1, :] = jnp.mean(k_ref[n * tq:(n + 1) * tq, :], axis=0, keepdims=True)

    row = lax.broadcasted_iota(jnp.int32, (tq, tq), 0)
    col = lax.broadcasted_iota(jnp.int32, (tq, tq), 1)
    dloc = (row - col).astype(F32)
    jl = lax.broadcasted_iota(jnp.int32, (tq, LANE), 1)
    outs = []
    for h2 in range(2):
        lo = h2 * HEAD_DIM
        slope = _head_slope(hp, h2)
        qf = q_ref[:, lo:lo + HEAD_DIM]
        q = (qf * SCALE).astype(BF16)
        gate = _dot_nt(qf, kmean_ref[:, lo:lo + HEAD_DIM], precision=HI)
        gate = jnp.where(jl < qi, gate, NEG)
        sel = jnp.where(_topn_mask(gate, nb, MOBA_TOPK) & (jl < qi), 1.0, 0.0)

        def body(kj, carry, lo=lo, q=q, sel=sel, slope=slope):
            ks = pl.multiple_of(kj * tq, tq)
            k = k_ref[pl.ds(ks, tq), lo:lo + HEAD_DIM].astype(BF16)
            v = v_ref[pl.ds(ks, tq), lo:lo + HEAD_DIM].astype(BF16)
            picked = jnp.max(jnp.where(jl == kj, sel, 0.0), axis=1, keepdims=True) > 0.5
            on_diag = (jnp.zeros((1, 1), jnp.int32) + kj) == qi
            valid = (on_diag & (col <= row)) | (jnp.logical_not(on_diag) & picked)
            d = dloc + ((qi - kj) * tq).astype(F32)
            s = jnp.where(valid, _dot_nt(q, k) - slope * d, NEG)
            return _online_update(carry, s, valid, v)

        outs.append(_online_finish(lax.fori_loop(0, qi + 1, body, _online_init(tq, HEAD_DIM))))
    o_ref[...] = jnp.concatenate(outs, axis=1)


def _moba_prompt(c, b, t):
    tq = MOBA_BLOCK
    nq = t // tq
    base = _COL['moba'] // LANE
    hp_n = N_HEADS // 2
    return pl.pallas_call(
        functools.partial(_moba_prompt_kernel, tq=tq, nb=nq), grid=(b, hp_n, nq),
        in_specs=[pl.BlockSpec((tq, LANE), lambda i, h, q: (i * nq + q, base + h)),
                  pl.BlockSpec((t, LANE), lambda i, h, q: (i, base + hp_n + h)),
                  pl.BlockSpec((t, LANE), lambda i, h, q: (i, base + 2 * hp_n + h))],
        out_specs=pl.BlockSpec((tq, LANE), lambda i, h, q: (i * nq + q, h)),
        out_shape=jax.ShapeDtypeStruct((b * t, MIX), F32),
        scratch_shapes=[pltpu.VMEM((LANE, LANE), F32)],
        compiler_params=_cparams(("parallel", "arbitrary", "arbitrary")), name="moba_prompt")(c, c, c)


def _cmp_kernel(*refs, n_in):
    x_refs = refs[-5 - n_in:-5]
    pos_ref, wcat_ref, w0_ref, w1_ref, o_ref = refs[-5:]
    if n_in == 1:
        x = x_refs[0][0]
    else:
        x = jnp.concatenate([r[0] for r in x_refs], axis=0)
    xb = x.astype(BF16)
    p0 = jnp.dot(xb, w0_ref[...], preferred_element_type=F32)
    p1 = jnp.dot(xb, w1_ref[...], preferred_element_type=F32)
    bias = jnp.dot(pos_ref[...].astype(BF16), wcat_ref[...], preferred_element_type=F32)[0:1]
    o_ref[0] = p0 + pltpu.roll(p1, p1.shape[0] - 1, 0) + bias


def _cmp_weights(pos, wk, wv):
    r = CMP_LEN // CMP_STRIDE
    w = jnp.stack([wk, wv]).reshape(2, r, CMP_STRIDE, HEAD_DIM, HEAD_DIM)
    e2 = jnp.eye(2, dtype=w.dtype)
    eg = jnp.eye(KV_G, dtype=w.dtype)
    big = jnp.einsum('krlde,kK,gG->rlkgdKGe', w, e2, eg)
    big = big.reshape(r, CMP_STRIDE * 2 * KV_G * HEAD_DIM, 2 * KV_G * HEAD_DIM).astype(BF16)
    wcat = jnp.concatenate([wk, wk, wv, wv], axis=1).astype(BF16)
    posb = jnp.zeros((SUBLANE, CMP_LEN * HEAD_DIM), F32).at[0].set(pos.reshape(-1))
    return posb, wcat, big[0], big[1]


def _const_specs(arrays):
    return [pl.BlockSpec(a.shape, lambda *_, nd=a.ndim: (0,) * nd) for a in arrays]


def _cmp_prompt(c, b, t, cw):
    kw = 2 * KV_G * HEAD_DIM
    nchunk = t // CMP_STRIDE
    x = c[:, _COL['kv_cmp']:_COL['kv_cmp'] + kw].reshape(b, nchunk, CMP_STRIDE * kw)
    return pl.pallas_call(
        functools.partial(_cmp_kernel, n_in=1), grid=(b,),
        in_specs=[pl.BlockSpec((1, nchunk, CMP_STRIDE * kw), lambda i: (i, 0, 0))] + _const_specs(cw),
        out_specs=pl.BlockSpec((1, nchunk, kw), lambda i: (i, 0, 0)),
        out_shape=jax.ShapeDtypeStruct((b, nchunk, kw), F32),
        compiler_params=_cparams(("parallel",)), name="nsa_cmp_prompt")(x, *cw)


def _cmp_paged(page_table, pool, cw):
    bs, npages = page_table.shape
    kw = 2 * KV_G * HEAD_DIM
    per = pool.shape[1] // CMP_STRIDE
    poolr = pool.reshape(pool.shape[0], per, CMP_STRIDE * kw)
    x_specs = [pl.BlockSpec((1, per, CMP_STRIDE * kw), lambda b, pt, p=p: (pt[b, p], 0, 0)) for p in range(npages)]
    nchunk = npages * per
    return pl.pallas_call(
        functools.partial(_cmp_kernel, n_in=npages),
        grid_spec=pltpu.PrefetchScalarGridSpec(
            num_scalar_prefetch=1, grid=(bs,), in_specs=x_specs + _const_specs(cw),
            out_specs=pl.BlockSpec((1, nchunk, kw), lambda b, pt: (b, 0, 0))),
        out_shape=jax.ShapeDtypeStruct((bs, nchunk, kw), F32),
        compiler_params=_cparams(("parallel",)), name="nsa_cmp_paged")(page_table, *([poolr] * npages), *cw)


def _overlap_matrix(nc, ns):
    i_c = np.arange(LANE)[:, None] * CMP_STRIDE
    j_s = np.arange(LANE)[None, :] * SLC_BLOCK
    ov = (i_c < j_s + SLC_BLOCK) & (i_c + CMP_LEN > j_s)
    ov &= (np.arange(LANE)[:, None] < nc) & (np.arange(LANE)[None, :] < ns)
    return jnp.asarray(ov, F32)


def _nsa_prompt_kernel(q_ref, slc_ref, win_ref, cmp_ref, g_ref, ov_ref, o_ref, *, tq, nc, ns):
    qi = pl.program_id(1)
    q0 = qi * tq
    rows = HG * tq
    rl = lax.broadcasted_iota(jnp.int32, (tq, 1), 0)
    t1 = q0 + rl
    t4 = jnp.concatenate([t1] * HG, axis=0)
    lane = lax.broadcasted_iota(jnp.int32, (1, LANE), 1)
    sg = _sigmoid(g_ref[...])
    jl = lax.broadcasted_iota(jnp.int32, (tq, LANE), 1)
    cur = t1 // SLC_BLOCK
    erow = lax.broadcasted_iota(jnp.int32, (LANE, tq), 0)
    ecol = lax.broadcasted_iota(jnp.int32, (LANE, tq), 1)
    wtiles = WINDOW // tq
    for g in range(KV_G):
        kl = g * HEAD_DIM
        vl = KV_G * HEAD_DIM + g * HEAD_DIM
        qs = jnp.concatenate([q_ref[:, (g * HG + h) * HEAD_DIM:(g * HG + h + 1) * HEAD_DIM] for h in range(HG)],
                             axis=0)
        qs = (qs * SCALE).astype(BF16)
        slope = jnp.concatenate([jnp.full((tq, 1), SLOPES[g * HG + h], F32) for h in range(HG)], axis=0)

        kc = cmp_ref[0, :, kl:kl + HEAD_DIM].astype(BF16)
        vc = cmp_ref[0, :, vl:vl + HEAD_DIM].astype(BF16)
        dist = (t4 - (lane * CMP_STRIDE + CMP_LEN - 1)).astype(F32)
        ok_c = (dist >= 0) & (lane < nc)
        s = jnp.where(ok_c, _dot_nt(qs, kc) - slope * dist, NEG)
        m = jnp.max(s, axis=1, keepdims=True)
        e = jnp.where(ok_c, jnp.exp(s - m), 0.0)
        p_c = e / jnp.maximum(jnp.sum(e, axis=1, keepdims=True), 1e-30)
        o_cmp = jnp.dot(p_c.astype(BF16), vc, preferred_element_type=F32)

        psum = p_c[0:tq]
        for h in range(1, HG):
            psum = psum + p_c[h * tq:(h + 1) * tq]
        imp = jnp.dot(psum, ov_ref[...], precision=HI, preferred_element_type=F32)
        forced = (jl == 0) | (jl == cur) | (jl == cur - 1)
        imp = jnp.where(forced, FORCE, imp)
        imp = jnp.where(jl <= cur, imp, NEG)
        sel = jnp.where(_topn_mask(imp, ns, min(SLC_TOPN, ns)) & (jl <= cur), 1.0, 0.0).astype(BF16)

        def slc_body(kj, carry, qs=qs, slope=slope, sel=sel, kl=kl, vl=vl):
            ks = pl.multiple_of(kj * tq, tq)
            k = slc_ref[pl.ds(ks, tq), kl:kl + HEAD_DIM].astype(BF16)
            v = slc_ref[pl.ds(ks, tq), vl:vl + HEAD_DIM].astype(BF16)
            expand = jnp.where((ks + ecol) // SLC_BLOCK == erow, 1.0, 0.0).astype(BF16)
            picked = jnp.dot(sel, expand, preferred_element_type=F32)
            picked = jnp.concatenate([picked] * HG, axis=0)
            d = t4 - (ks + lax.broadcasted_iota(jnp.int32, (1, tq), 1))
            valid = (picked > 0.5) & (d >= 0)
            s = jnp.where(valid, _dot_nt(qs, k) - slope * d.astype(F32), NEG)
            return _online_update(carry, s, valid, v)

        o_slc = _online_finish(lax.fori_loop(0, qi + 1, slc_body, _online_init(rows, HEAD_DIM)))

        def win_body(kj, carry, qs=qs, slope=slope, kl=kl, vl=vl):
            ks = pl.multiple_of(kj * tq, tq)
            k = win_ref[pl.ds(ks, tq), kl:kl + HEAD_DIM].astype(BF16)
            v = win_ref[pl.ds(ks, tq), vl:vl + HEAD_DIM].astype(BF16)
            d = t4 - (ks + lax.broadcasted_iota(jnp.int32, (1, tq), 1))
            valid = (d >= 0) & (d < WINDOW)
            s = jnp.where(valid, _dot_nt(qs, k) - slope * d.astype(F32), NEG)
            return _online_update(carry, s, valid, v)

        o_win = _online_finish(lax.fori_loop(jnp.maximum(qi - wtiles, 0), qi + 1, win_body,
                                             _online_init(rows, HEAD_DIM)))

        for h in range(HG):
            hh = g * HG + h
            r0 = h * tq
            o = (sg[:, hh:hh + 1] * o_cmp[r0:r0 + tq]
                 + sg[:, N_HEADS + hh:N_HEADS + hh + 1] * o_slc[r0:r0 + tq]
                 + sg[:, 2 * N_HEADS + hh:2 * N_HEADS + hh + 1] * o_win[r0:r0 + tq])
            o_ref[:, hh * HEAD_DIM:(hh + 1) * HEAD_DIM] = o


def _nsa_prompt(c, kvc, ov, b, t, tq):
    nq = t // tq
    kw = 2 * KV_G * HEAD_DIM
    nchunk = t // CMP_STRIDE
    nc = nchunk - CMP_LEN // CMP_STRIDE + 1
    ns = -(-t // SLC_BLOCK)
    return pl.pallas_call(
        functools.partial(_nsa_prompt_kernel, tq=tq, nc=nc, ns=ns), grid=(b, nq),
        in_specs=[pl.BlockSpec((tq, MIX), lambda i, q: (i * nq + q, _COL['nsa_q'] // MIX)),
                  pl.BlockSpec((t, kw), lambda i, q: (i, _COL['kv_slc'] // kw)),
                  pl.BlockSpec((t, kw), lambda i, q: (i, _COL['kv_win'] // kw)),
                  pl.BlockSpec((1, nchunk, kw), lambda i, q: (i, 0, 0)),
                  pl.BlockSpec((tq, LANE), lambda i, q: (i * nq + q, _COL['nsa_gate'] // LANE)),
                  pl.BlockSpec((LANE, LANE), lambda i, q: (0, 0))],
        out_specs=pl.BlockSpec((tq, MIX), lambda i, q: (i * nq + q, 0)),
        out_shape=jax.ShapeDtypeStruct((b * t, MIX), F32),
        compiler_params=_cparams(("parallel", "arbitrary")), name="nsa_prompt")(c, c, c, kvc, c, ov)


def _dec_rows():
    rid = lax.broadcasted_iota(jnp.int32, (N_HEADS * T_PAD, 1), 0)
    t8 = rid % T_PAD
    slope = jnp.concatenate([jnp.full((T_PAD, 1), SLOPES[h], F32) for h in range(N_HEADS)], axis=0)
    return t8, slope


def _expand_heads(q8):
    lane = lax.broadcasted_iota(jnp.int32, q8.shape, 1)
    return jnp.concatenate([jnp.where(lane // HEAD_DIM == h, q8, 0.0) for h in range(N_HEADS)], axis=0)


def _collapse_heads(res):
    lane = lax.broadcasted_iota(jnp.int32, (T_PAD, res.shape[1]), 1)
    out = jnp.zeros((T_PAD, res.shape[1]), F32)
    for h in range(N_HEADS):
        out = out + jnp.where(lane // HEAD_DIM == h, res[h * T_PAD:(h + 1) * T_PAD], 0.0)
    return out


def _moba_dec_kernel(pt_ref, q_ref, kn_ref, vn_ref, *rest, npages, past):
    page_refs, o_ref = rest[:npages], rest[npages]
    page = page_refs[0].shape[1]
    t8, slope = _dec_rows()
    qpos = past + t8
    lane = lax.broadcasted_iota(jnp.int32, (1, LANE), 1)
    q8 = q_ref[0]
    qx_f = _expand_heads(q8)
    qx = (qx_f * SCALE).astype(BF16)
    per_blk = MOBA_BLOCK // page
    nb_past = past // MOBA_BLOCK
    rowi = lax.broadcasted_iota(jnp.int32, (LANE, MIX), 0)
    kmean = jnp.zeros((LANE, MIX), F32)
    for n in range(nb_past):
        tot = jnp.zeros((1, MIX), F32)
        for p in range(n * per_blk, (n + 1) * per_blk):
            tot = tot + jnp.sum(page_refs[p][0, :, 0:MIX], axis=0, keepdims=True)
        kmean = kmean + jnp.where(rowi == n, tot * (1.0 / MOBA_BLOCK), 0.0)
    gate = _dot_nt(qx_f, kmean, precision=HI)
    jl = lax.broadcasted_iota(jnp.int32, gate.shape, 1)
    cur = qpos // MOBA_BLOCK
    gate = jnp.where(jl < cur, gate, NEG)
    nb = -(-(past + T_PAD) // MOBA_BLOCK)
    sel = jnp.where(_topn_mask(gate, nb, max(1, min(MOBA_TOPK, nb - 1))) & (jl < cur), 1.0, 0.0)
    s_tiles, v_tiles = [], []
    for p in range(npages):
        k = page_refs[p][0, :, 0:MIX].astype(BF16)
        v_tiles.append(page_refs[p][0, :, MIX:2 * MIX].astype(BF16))
        n = (p * page) // MOBA_BLOCK
        d = (qpos - (p * page + lane)).astype(F32)
        valid = (sel[:, n:n + 1] > 0.5) & (d >= 0)
        s_tiles.append(jnp.where(valid, _dot_nt(qx, k) - slope * d, NEG))
    k = _pad_rows(kn_ref[0], LANE).astype(BF16)
    v_tiles.append(_pad_rows(vn_ref[0], LANE).astype(BF16))
    d = (t8 - lane).astype(F32)
    valid = (d >= 0) & (lane < T_PAD)
    s_tiles.append(jnp.where(valid, _dot_nt(qx, k) - slope * d, NEG))
    o_ref[0] = _collapse_heads(_attend_tiles(s_tiles, v_tiles))


def _fox_dec_kernel(pt_ref, q_ref, kn_ref, vn_ref, cq_ref, cr_ref, *rest, npages, past):
    page_refs, o_ref = rest[:npages], rest[npages]
    page = page_refs[0].shape[1]
    t8, _ = _dec_rows()
    lane = lax.broadcasted_iota(jnp.int32, (1, LANE), 1)
    qx = (_expand_heads(q_ref[0]) * SCALE).astype(BF16)
    cq8 = cq_ref[0]
    cq = jnp.concatenate([cq8[:, h:h + 1] for h in range(N_HEADS)], axis=0)

    def cum_tile(j):
        return jnp.concatenate([jnp.broadcast_to(cr_ref[0, h:h + 1, j * LANE:(j + 1) * LANE], (T_PAD, LANE))
                                for h in range(N_HEADS)], axis=0)

    s_tiles, v_tiles = [], []
    for p in range(npages):
        k = page_refs[p][0, :, 0:MIX].astype(BF16)
        v_tiles.append(page_refs[p][0, :, MIX:2 * MIX].astype(BF16))
        s_tiles.append(_dot_nt(qx, k) + (cq - cum_tile(p)))
    k = _pad_rows(kn_ref[0], LANE).astype(BF16)
    v_tiles.append(_pad_rows(vn_ref[0], LANE).astype(BF16))
    valid = (lane <= t8) & (lane < T_PAD)
    s_tiles.append(jnp.where(valid, _dot_nt(qx, k) + (cq - cum_tile(npages)), NEG))
    o_ref[0] = _collapse_heads(_attend_tiles(s_tiles, v_tiles))


def _paged_mha_dec(kernel, name, c3, col, page_table, pool, extra_args, extra_specs, past):
    bs, npages = page_table.shape
    page, width = pool.shape[1], pool.shape[2]
    qb = col // MIX
    in_specs = [pl.BlockSpec((1, T_PAD, MIX), lambda b, pt: (b, 0, qb)),
                pl.BlockSpec((1, T_PAD, MIX), lambda b, pt: (b, 0, qb + 1)),
                pl.BlockSpec((1, T_PAD, MIX), lambda b, pt: (b, 0, qb + 2))]
    in_specs += extra_specs
    in_specs += [pl.BlockSpec((1, page, width), lambda b, pt, p=p: (pt[b, p], 0, 0)) for p in range(npages)]
    return pl.pallas_call(
        functools.partial(kernel, npages=npages, past=past),
        grid_spec=pltpu.PrefetchScalarGridSpec(
            num_scalar_prefetch=1, grid=(bs,), in_specs=in_specs,
            out_specs=pl.BlockSpec((1, T_PAD, MIX), lambda b, pt: (b, 0, 0))),
        out_shape=jax.ShapeDtypeStruct((bs, T_PAD, MIX), F32),
        compiler_params=_cparams(("parallel",)), name=name)(
            page_table, c3, c3, c3, *extra_args, *([pool] * npages))


def _nsa_dec_kernel(pt_ref, q_ref, ns_ref, nw_ref, g_ref, cmp_ref, ov_ref, wc_ref, *rest, npages, past, nc, ns):
    page_refs, o_ref = rest[:npages], rest[npages]
    page = page_refs[0].shape[1]
    kw = KV_G * HEAD_DIM
    t8, slope = _dec_rows()
    qpos = past + t8
    lane = lax.broadcasted_iota(jnp.int32, (1, LANE), 1)
    lane8 = lax.broadcasted_iota(jnp.int32, (T_PAD, LANE), 1)
    q8 = q_ref[0] * SCALE
    rows = []
    for h in range(N_HEADS):
        x = q8[:, (h // 2) * LANE:(h // 2 + 1) * LANE]
        dst = h // HG
        if h % 2 != dst:
            x = pltpu.roll(x, HEAD_DIM, 1)
        rows.append(jnp.where((lane8 // HEAD_DIM) == dst, x, 0.0))
    qx = jnp.concatenate(rows, axis=0).astype(BF16)

    cm = cmp_ref[0]
    dist = (qpos - (lane * CMP_STRIDE + CMP_LEN - 1)).astype(F32)
    ok_c = (dist >= 0) & (lane < nc)
    s = jnp.where(ok_c, _dot_nt(qx, cm[:, 0:kw].astype(BF16)) - slope * dist, NEG)
    m = jnp.max(s, axis=1, keepdims=True)
    e = jnp.where(ok_c, jnp.exp(s - m), 0.0)
    p_c = e / jnp.maximum(jnp.sum(e, axis=1, keepdims=True), 1e-30)
    o_cmp = jnp.dot(p_c.astype(BF16), cm[:, kw:2 * kw].astype(BF16), preferred_element_type=F32)

    psum = []
    for g in range(KV_G):
        acc = p_c[g * HG * T_PAD:g * HG * T_PAD + T_PAD]
        for h in range(1, HG):
            r0 = (g * HG + h) * T_PAD
            acc = acc + p_c[r0:r0 + T_PAD]
        psum.append(acc)
    imp = jnp.dot(jnp.concatenate(psum, axis=0), ov_ref[...], precision=HI, preferred_element_type=F32)
    jl = lax.broadcasted_iota(jnp.int32, imp.shape, 1)
    tg = lax.broadcasted_iota(jnp.int32, (KV_G * T_PAD, 1), 0) % T_PAD
    cur = (past + tg) // SLC_BLOCK
    forced = (jl == 0) | (jl == cur) | (jl == cur - 1)
    imp = jnp.where(forced, FORCE, imp)
    imp = jnp.where(jl <= cur, imp, NEG)
    sel = jnp.where(_topn_mask(imp, ns, min(SLC_TOPN, ns)) & (jl <= cur), 1.0, 0.0)
    sel_rows = jnp.concatenate([sel[(h // HG) * T_PAD:(h // HG + 1) * T_PAD] for h in range(N_HEADS)], axis=0)

    def new_tile(ref, extra_valid):
        k = _pad_rows(ref[0, :, 0:kw], LANE).astype(BF16)
        v = _pad_rows(ref[0, :, kw:2 * kw], LANE).astype(BF16)
        d = (t8 - lane).astype(F32)
        valid = (d >= 0) & (lane < T_PAD) & extra_valid
        return jnp.where(valid, _dot_nt(qx, k) - slope * d, NEG), v

    s_tiles, v_tiles = [], []
    per = page // SLC_BLOCK
    for p in range(npages):
        k = page_refs[p][0, :, 0:kw].astype(BF16)
        v_tiles.append(page_refs[p][0, :, kw:2 * kw].astype(BF16))
        picked = jnp.zeros((N_HEADS * T_PAD, LANE), jnp.bool_)
        for j in range(per):
            blk = p * per + j
            picked = picked | ((lane // SLC_BLOCK == j) & (sel_rows[:, blk:blk + 1] > 0.5))
        d = (qpos - (p * page + lane)).astype(F32)
        s_tiles.append(jnp.where(picked & (d >= 0), _dot_nt(qx, k) - slope * d, NEG))
    blk_new = past // SLC_BLOCK
    s_new, v_new = new_tile(ns_ref, sel_rows[:, blk_new:blk_new + 1] > 0.5)
    o_slc = _attend_tiles(s_tiles + [s_new], v_tiles + [v_new])

    s_tiles, v_tiles = [], []
    wb = wc_ref.shape[1]
    w_off = past - wb
    for j in range(wb // LANE):
        k = wc_ref[0, j * LANE:(j + 1) * LANE, 0:kw].astype(BF16)
        v_tiles.append(wc_ref[0, j * LANE:(j + 1) * LANE, kw:2 * kw].astype(BF16))
        d = qpos - (w_off + j * LANE + lane)
        valid = (d >= 0) & (d < WINDOW)
        s_tiles.append(jnp.where(valid, _dot_nt(qx, k) - slope * d.astype(F32), NEG))
    s_new, v_new = new_tile(nw_ref, True)
    o_win = _attend_tiles(s_tiles + [s_new], v_tiles + [v_new])

    sg = _sigmoid(g_ref[0])
    for h in range(N_HEADS):
        r0, l0 = h * T_PAD, (h // HG) * HEAD_DIM
        o = (sg[:, h:h + 1] * o_cmp[r0:r0 + T_PAD, l0:l0 + HEAD_DIM]
             + sg[:, N_HEADS + h:N_HEADS + h + 1] * o_slc[r0:r0 + T_PAD, l0:l0 + HEAD_DIM]
             + sg[:, 2 * N_HEADS + h:2 * N_HEADS + h + 1] * o_win[r0:r0 + T_PAD, l0:l0 + HEAD_DIM])
        o_ref[0, :, h * HEAD_DIM:(h + 1) * HEAD_DIM] = o


def _nsa_dec(c3, kvc, ov, win_cache, page_table, pool, past, t_real):
    bs, npages = page_table.shape
    page = pool.shape[1]
    kw = 2 * KV_G * HEAD_DIM
    nchunk = kvc.shape[1]
    nc = nchunk - CMP_LEN // CMP_STRIDE + 1
    ns = -(-(past + t_real) // SLC_BLOCK)
    wb = win_cache.shape[1]
    in_specs = [pl.BlockSpec((1, T_PAD, MIX), lambda b, pt: (b, 0, _COL['nsa_q'] // MIX)),
                pl.BlockSpec((1, T_PAD, kw), lambda b, pt: (b, 0, _COL['kv_slc'] // kw)),
                pl.BlockSpec((1, T_PAD, kw), lambda b, pt: (b, 0, _COL['kv_win'] // kw)),
                pl.BlockSpec((1, T_PAD, LANE), lambda b, pt: (b, 0, _COL['nsa_gate'] // LANE)),
                pl.BlockSpec((1, nchunk, kw), lambda b, pt: (b, 0, 0)),
                pl.BlockSpec((LANE, LANE), lambda b, pt: (0, 0)),
                pl.BlockSpec((1, wb, kw), lambda b, pt: (b, 0, 0))]
    in_specs += [pl.BlockSpec((1, page, kw), lambda b, pt, p=p: (pt[b, p], 0, 0)) for p in range(npages)]
    return pl.pallas_call(
        functools.partial(_nsa_dec_kernel, npages=npages, past=past, nc=nc, ns=ns),
        grid_spec=pltpu.PrefetchScalarGridSpec(
            num_scalar_prefetch=1, grid=(bs,), in_specs=in_specs,
            out_specs=pl.BlockSpec((1, T_PAD, MIX), lambda b, pt: (b, 0, 0))),
        out_shape=jax.ShapeDtypeStruct((bs, T_PAD, MIX), F32),
        compiler_params=_cparams(("parallel",)), name="nsa_decode")(
            page_table, c3, c3, c3, c3, kvc, ov, win_cache, *([pool] * npages))


def _s5_disc_kernel(ar_ref, ai_ref, ls_ref, btr_ref, bti_ref, abr_ref, abi_ref, bbr_ref, bbi_ref):
    ar, ai = ar_ref[...], ai_ref[...]
    step = jnp.exp(ls_ref[...])
    mag = jnp.exp(ar * step)
    abr = mag * jnp.cos(ai * step)
    abi = mag * jnp.sin(ai * step)
    den = ar * ar + ai * ai
    zr = (ar * (abr - 1.0) + ai * abi) / den
    zi = (ar * abi - ai * (abr - 1.0)) / den
    abr_ref[...] = abr
    abi_ref[...] = abi
    btr, bti = btr_ref[...], bti_ref[...]
    bbr_ref[...] = zr * btr - zi * bti
    bbi_ref[...] = zr * bti + zi * btr


def _s5_disc(a_re, a_im, log_step, b_re, b_im):
    rep = lambda a: jnp.repeat(a, S5_CH, axis=0)
    n = S5_GROUPS * S5_CH
    args = (rep(a_re), rep(a_im), rep(jnp.broadcast_to(log_step[:, None], (S5_GROUPS, S5_STATE))),
            b_re.transpose(0, 2, 1).reshape(n, S5_STATE), b_im.transpose(0, 2, 1).reshape(n, S5_STATE))
    shp = jax.ShapeDtypeStruct((n, S5_STATE), F32)
    return pl.pallas_call(_s5_disc_kernel, out_shape=(shp, shp, shp, shp), name="s5_discretise")(*args)


_S5_CH_ROWS = 128


def _s5_scan_kernel(*refs, seg, has_h0):
    if has_h0:
        (u_ref, bre_ref, bim_ref, ar_ref, ai_ref, cre_ref, cim_ref, d_ref, h0r_ref, h0i_ref,
         y_ref, hlr_ref, hli_ref, hr_s, hi_s) = refs
    else:
        (u_ref, bre_ref, bim_ref, ar_ref, ai_ref, cre_ref, cim_ref, d_ref,
         y_ref, hlr_ref, hli_ref, hr_s, hi_s) = refs
    rows = u_ref.shape[0]
    ch = min(_S5_CH_ROWS, rows)
    pad = ch
    nch = rows // ch
    ar, ai = ar_ref[0], ai_ref[0]
    hr_s[0:pad, :] = jnp.zeros((pad, hr_s.shape[1]), F32)
    hi_s[0:pad, :] = jnp.zeros((pad, hi_s.shape[1]), F32)

    def init_body(i, _):
        r0 = pl.multiple_of(i * ch, ch)
        u = u_ref[pl.ds(r0, ch), :]
        br = jnp.dot(u, bre_ref[0], precision=HI, preferred_element_type=F32)
        bi = jnp.dot(u, bim_ref[0], precision=HI, preferred_element_type=F32)
        if has_h0:
            h0r, h0i = h0r_ref[pl.ds(r0, ch), :], h0i_ref[pl.ds(r0, ch), :]
            br = br + (ar * h0r - ai * h0i)
            bi = bi + (ar * h0i + ai * h0r)
        hr_s[pl.ds(pad + r0, ch), :] = br
        hi_s[pl.ds(pad + r0, ch), :] = bi
        return 0

    lax.fori_loop(0, nch, init_body, 0)

    rl = lax.broadcasted_iota(jnp.int32, (ch, 1), 0)
    pr, pi = ar, ai
    d = 1
    while d < seg:
        first = d // ch

        def pass_body(i, _, d=d, pr=pr, pi=pi):
            r0 = pl.multiple_of((nch - 1 - i) * ch, ch)
            cr = hr_s[pl.ds(pad + r0, ch), :]
            ci = hi_s[pl.ds(pad + r0, ch), :]
            if d < SUBLANE:
                lo = pad - SUBLANE
                sr = pltpu.roll(hr_s[pl.ds(lo + r0, ch + SUBLANE), :], d, 0)[SUBLANE:]
                si = pltpu.roll(hi_s[pl.ds(lo + r0, ch + SUBLANE), :], d, 0)[SUBLANE:]
            else:
                sr = hr_s[pl.ds(pad + r0 - d, ch), :]
                si = hi_s[pl.ds(pad + r0 - d, ch), :]
            keep = ((r0 + rl) % seg) >= d
            sr = jnp.where(keep, sr, 0.0)
            si = jnp.where(keep, si, 0.0)
            hr_s[pl.ds(pad + r0, ch), :] = cr + (pr * sr - pi * si)
            hi_s[pl.ds(pad + r0, ch), :] = ci + (pr * si + pi * sr)
            return 0

        lax.fori_loop(0, nch - first, pass_body, 0)
        pr, pi = pr * pr - pi * pi, 2.0 * pr * pi
        d *= 2

    def out_body(i, _):
        r0 = pl.multiple_of(i * ch, ch)
        hr = hr_s[pl.ds(pad + r0, ch), :]
        hi = hi_s[pl.ds(pad + r0, ch), :]
        y = (jnp.dot(hr, cre_ref[0], precision=HI, preferred_element_type=F32)
             - jnp.dot(hi, cim_ref[0], precision=HI, preferred_element_type=F32))
        y_ref[pl.ds(r0, ch), :] = y + d_ref[0] * u_ref[pl.ds(r0, ch), :]
        return 0

    lax.fori_loop(0, nch, out_body, 0)
    nl = hlr_ref.shape[0]
    hlr_ref[...] = hr_s[pad + rows - nl:pad + rows, :]
    hli_ref[...] = hi_s[pad + rows - nl:pad + rows, :]


def _s5_scan(c, sw, rows, seg, h0=None):
    n = c.shape[0]
    nt = n // rows
    gl = LANE // S5_CH
    lt = S5_GROUPS // gl
    w = gl * S5_STATE
    nl = (rows // seg) * SUBLANE if seg == SUBLANE else SUBLANE
    ub = _COL['s5_u'] // LANE
    in_specs = [pl.BlockSpec((rows, LANE), lambda i, j: (i, ub + j)),
                pl.BlockSpec((1, LANE, w), lambda i, j: (j, 0, 0)),
                pl.BlockSpec((1, LANE, w), lambda i, j: (j, 0, 0)),
                pl.BlockSpec((1, 1, w), lambda i, j: (j, 0, 0)),
                pl.BlockSpec((1, 1, w), lambda i, j: (j, 0, 0)),
                pl.BlockSpec((1, w, LANE), lambda i, j: (j, 0, 0)),
                pl.BlockSpec((1, w, LANE), lambda i, j: (j, 0, 0)),
                pl.BlockSpec((1, 1, LANE), lambda i, j: (j, 0, 0))]
    args = [c, sw['bre'], sw['bim'], sw['ar'], sw['ai'], sw['cre'], sw['cim'], sw['d']]
    if h0 is not None:
        in_specs += [pl.BlockSpec((rows, w), lambda i, j: (i, j))] * 2
        args += list(h0)
    hshape = jax.ShapeDtypeStruct((nt * nl, S5_GROUPS * S5_STATE), F32)
    return pl.pallas_call(
        functools.partial(_s5_scan_kernel, seg=seg, has_h0=h0 is not None), grid=(nt, lt),
        in_specs=in_specs,
        out_specs=(pl.BlockSpec((rows, LANE), lambda i, j: (i, j)),
                   pl.BlockSpec((nl, w), lambda i, j: (i, j)), pl.BlockSpec((nl, w), lambda i, j: (i, j))),
        out_shape=(jax.ShapeDtypeStruct((n, MIX), F32), hshape, hshape),
        scratch_shapes=[pltpu.VMEM((min(_S5_CH_ROWS, rows) + rows, w), F32)] * 2,
        compiler_params=_cparams(("parallel", "arbitrary")), name="s5_scan")(*args)


def _s5_weights(lp):
    abr, abi, bbr, bbi = _s5_disc(lp['s5_a_re'], lp['s5_a_im'], lp['s5_log_step'], lp['s5_b_re'], lp['s5_b_im'])
    gl = LANE // S5_CH
    lt = S5_GROUPS // gl
    eye = jnp.eye(gl, dtype=F32)

    def bdiag(bb):
        return jnp.einsum('jgcn,gh->jgchn', bb.reshape(lt, gl, S5_CH, S5_STATE), eye).reshape(
            lt, gl * S5_CH, gl * S5_STATE)

    def cdiag(cc):
        return jnp.einsum('jgcn,gh->jgnhc', cc.reshape(lt, gl, S5_CH, S5_STATE), eye).reshape(
            lt, gl * S5_STATE, gl * S5_CH)

    return dict(bre=bdiag(bbr), bim=bdiag(bbi),
                ar=abr[::S5_CH].reshape(lt, 1, gl * S5_STATE), ai=abi[::S5_CH].reshape(lt, 1, gl * S5_STATE),
                cre=cdiag(lp['s5_c_re']), cim=cdiag(lp['s5_c_im']), d=lp['s5_d'].reshape(lt, 1, LANE))


def _glu_kernel(y_ref, w_ref, o_ref):
    y = y_ref[...]
    g = 0.5 * y * (1.0 + jnp.tanh(np.float32(np.sqrt(2.0 / np.pi)) * (y + np.float32(0.044715) * (y * y * y))))
    z = jnp.dot(g.astype(BF16), w_ref[...], preferred_element_type=F32)
    o_ref[...] = z[:, :MIX] * _sigmoid(z[:, MIX:])


def _glu(y, w, tm):
    n = y.shape[0]
    return pl.pallas_call(
        _glu_kernel, grid=(n // tm,),
        in_specs=[pl.BlockSpec((tm, MIX), lambda i: (i, 0)), pl.BlockSpec((MIX, 2 * MIX), lambda i: (0, 0))],
        out_specs=pl.BlockSpec((tm, MIX), lambda i: (i, 0)),
        out_shape=jax.ShapeDtypeStruct((n, MIX), F32),
        compiler_params=_cparams(("parallel",)), name="s5_glu")(y, w)


def _layer_norm(x, g, b):
    mu = jnp.mean(x, axis=-1, keepdims=True)
    xc = x - mu
    var = jnp.mean(xc * xc, axis=-1, keepdims=True)
    return xc * lax.rsqrt(var + LN_EPS) * g + b


def _merge_kernel(o0_ref, o1_ref, o2_ref, o3_ref, mg_ref, wb_ref, x_ref, wo_ref, g_ref, b_ref, out_ref, acc_ref,
                  *, alpha):
    i = pl.program_id(1)

    @pl.when(i == 0)
    def _():
        acc_ref[...] = jnp.zeros_like(acc_ref)

    for k, o_ref in enumerate((o0_ref, o1_ref, o2_ref, o3_ref)):
        @pl.when(i == k)
        def _(o_ref=o_ref):
            proj = jnp.dot(o_ref[...].astype(BF16), wb_ref[0], preferred_element_type=F32)
            acc_ref[...] += _sigmoid(mg_ref[...]) * proj

    @pl.when(i == N_BRANCH - 1)
    def _():
        mixed = jnp.dot(acc_ref[...].astype(BF16), wo_ref[...], preferred_element_type=F32)
        out_ref[...] = _layer_norm(alpha * x_ref[...] + mixed, g_ref[...], b_ref[...])


def _merge(outs, c, x, wb, wo, g, b, tm, alpha):
    n = x.shape[0]
    o_spec = pl.BlockSpec((tm, MIX), lambda r, i: (r, 0))
    return pl.pallas_call(
        functools.partial(_merge_kernel, alpha=alpha), grid=(n // tm, N_BRANCH),
        in_specs=[o_spec, o_spec, o_spec, o_spec,
                  pl.BlockSpec((tm, D_MODEL), lambda r, i: (r, i)),
                  pl.BlockSpec((1, MIX, D_MODEL), lambda r, i: (i, 0, 0)),
                  pl.BlockSpec((tm, D_MODEL), lambda r, i: (r, 0)),
                  pl.BlockSpec((D_MODEL, D_MODEL), lambda r, i: (0, 0)),
                  pl.BlockSpec((1, D_MODEL), lambda r, i: (0, 0)),
                  pl.BlockSpec((1, D_MODEL), lambda r, i: (0, 0))],
        out_specs=pl.BlockSpec((tm, D_MODEL), lambda r, i: (r, 0)),
        out_shape=jax.ShapeDtypeStruct((n, D_MODEL), F32),
        scratch_shapes=[pltpu.VMEM((tm, D_MODEL), F32)],
        compiler_params=_cparams(("parallel", "arbitrary")), name="merge_out_ln")(
            *outs, c, wb, x, wo, g, b)


def _ffn_kernel(*refs, halo, seg, alpha):
    (h_ref, wg_ref, wv_ref, cwg_ref, cwv_ref, cbg_ref, cbv_ref, wd_ref, lg_ref, lb_ref) = refs[:10]
    if halo:
        pg_ref, pv_ref, out_ref, hb_ref, acc_ref = refs[10:]
    else:
        p1g_ref, p1v_ref, p2g_ref, p2v_ref, out_ref, hb_ref, acc_ref = refs[10:]
    j = pl.program_id(1)

    @pl.when(j == 0)
    def _():
        hb_ref[...] = h_ref[...].astype(BF16)
        acc_ref[...] = jnp.zeros_like(acc_ref)

    hb = hb_ref[...]
    tm = hb.shape[0]
    rid = lax.broadcasted_iota(jnp.int32, (tm, 1), 0)

    def conv(u, cw_ref, cb_ref, prev):
        r1 = pltpu.roll(u, 1, 0)
        r2 = pltpu.roll(u, 2, 0)
        if halo:
            p = prev[0][0]
            p6, p7 = p[SUBLANE - 2:SUBLANE - 1], p[SUBLANE - 1:SUBLANE]
            u1 = jnp.where(rid == 0, p7, r1)
            u2 = jnp.where(rid == 0, p6, jnp.where(rid == 1, p7, r2))
        else:
            t = rid % seg
            u1 = jnp.where(t >= 1, r1, prev[0][...])
            u2 = jnp.where(t >= 2, r2, prev[1][...])
        cw = cw_ref[...]
        return cb_ref[...] + (cw[0:1] * u2 + cw[1:2] * u1 + cw[2:3] * u)

    ug = jnp.dot(hb, wg_ref[...], preferred_element_type=F32)
    uv = jnp.dot(hb, wv_ref[...], preferred_element_type=F32)
    if halo:
        gate = conv(ug, cwg_ref, cbg_ref, (pg_ref,))
        val = conv(uv, cwv_ref, cbv_ref, (pv_ref,))
    else:
        gate = conv(ug, cwg_ref, cbg_ref, (p1g_ref, p2g_ref))
        val = conv(uv, cwv_ref, cbv_ref, (p1v_ref, p2v_ref))
    act = gate * _sigmoid(gate) * val
    acc_ref[...] += jnp.dot(act.astype(BF16), wd_ref[...], preferred_element_type=F32)

    @pl.when(j == pl.num_programs(1) - 1)
    def _():
        out_ref[...] = _layer_norm(alpha * h_ref[...] + acc_ref[...], lg_ref[...], lb_ref[...])


def _ffn(h, lw, tm, tf, alpha, prev=None, p12=None, seg=None):
    n = h.shape[0]
    nf = D_FF // tf
    halo = prev is not None
    in_specs = [pl.BlockSpec((tm, D_MODEL), lambda r, j: (r, 0)),
                pl.BlockSpec((D_MODEL, tf), lambda r, j: (0, j)),
                pl.BlockSpec((D_MODEL, tf), lambda r, j: (0, nf + j)),
                pl.BlockSpec((CONV_W, tf), lambda r, j: (0, j)),
                pl.BlockSpec((CONV_W, tf), lambda r, j: (0, nf + j)),
                pl.BlockSpec((1, tf), lambda r, j: (0, j)),
                pl.BlockSpec((1, tf), lambda r, j: (0, nf + j)),
                pl.BlockSpec((tf, D_MODEL), lambda r, j: (j, 0)),
                pl.BlockSpec((1, D_MODEL), lambda r, j: (0, 0)),
                pl.BlockSpec((1, D_MODEL), lambda r, j: (0, 0))]
    args = [h, lw['w_up'], lw['w_up'], lw['conv_w'], lw['conv_w'], lw['conv_b'], lw['conv_b'], lw['w_down'],
            lw['ln2_g'], lw['ln2_b']]
    if halo:
        in_specs += [pl.BlockSpec((1, SUBLANE, tf), lambda r, j: (r, 0, j)),
                     pl.BlockSpec((1, SUBLANE, tf), lambda r, j: (r, 0, nf + j))]
        args += [prev, prev]
    else:
        in_specs += [pl.BlockSpec((tm, tf), lambda r, j: (r, j)), pl.BlockSpec((tm, tf), lambda r, j: (r, nf + j)),
                     pl.BlockSpec((tm, tf), lambda r, j: (r, j)), pl.BlockSpec((tm, tf), lambda r, j: (r, nf + j))]
        args += [p12[0], p12[0], p12[1], p12[1]]
    return pl.pallas_call(
        functools.partial(_ffn_kernel, halo=halo, seg=seg, alpha=alpha), grid=(n // tm, nf),
        in_specs=in_specs,
        out_specs=pl.BlockSpec((tm, D_MODEL), lambda r, j: (r, 0)),
        out_shape=jax.ShapeDtypeStruct((n, D_MODEL), F32),
        scratch_shapes=[pltpu.VMEM((tm, D_MODEL), BF16), pltpu.VMEM((tm, D_MODEL), F32)],
        compiler_params=_cparams(("parallel", "arbitrary")), name="conv_ffn_ln")(*args)


def _prep_layer(l, p):
    lp = {k: v[l] for k, v in p.items()}
    lw = dict(
        w_in=_pack_w_in(lp['w_in']),
        fox_b=jnp.zeros((1, LANE), F32).at[0, :N_HEADS].set(lp['fox_b_f']),
        cmp=_cmp_weights(lp['nsa_cmp_pos'], lp['nsa_cmp_wk'], lp['nsa_cmp_wv']),
        s5=_s5_weights(lp),
        w_glu=lp['s5_w_glu'].astype(BF16),
        w_branch=lp['w_branch'].astype(BF16),
        w_out=lp['w_out'].astype(BF16),
        ln1_g=lp['ln1_g'].reshape(1, -1), ln1_b=lp['ln1_b'].reshape(1, -1),
        w_up=lp['ffn_w_up'].astype(BF16), conv_w=lp['ffn_conv_w'], conv_b=lp['ffn_conv_b'].reshape(1, -1),
        w_down=lp['ffn_w_down'].astype(BF16),
        ln2_g=lp['ln2_g'].reshape(1, -1), ln2_b=lp['ln2_b'].reshape(1, -1))
    return lw


def _prompt_layer(x, b, t, lw, alpha):
    n = b * t
    kw = 2 * KV_G * HEAD_DIM
    tr = min(1024, n)
    c = _matmul(x, lw['w_in'], tr, 512)
    logf = _logf(c, lw['fox_b'], tr)
    cumr = _cumsum_prompt(logf.reshape(b, t, N_HEADS).transpose(0, 2, 1))
    cumc = cumr[..., None]
    o_fox = _fox_prompt(c, cumc, cumr[:, :, None, :], b, t, 256)
    o_moba = _moba_prompt(c, b, t)
    kvc = _cmp_prompt(c, b, t, lw['cmp'])
    nchunk = t // CMP_STRIDE
    ov = _overlap_matrix(nchunk - CMP_LEN // CMP_STRIDE + 1, -(-t // SLC_BLOCK))
    o_nsa = _nsa_prompt(c, kvc, ov, b, t, 128)
    y_s5, hlr, hli = _s5_scan(c, lw['s5'], t, t)
    o_s5 = _glu(y_s5, lw['w_glu'], tr)
    h = _merge((o_nsa, o_s5, o_moba, o_fox), c, x, lw['w_branch'], lw['w_out'], lw['ln1_g'], lw['ln1_b'], 256, alpha)

    tm = 512
    nt = n // tm
    edge = h.reshape(nt, tm, D_MODEL)[:, tm - (CONV_W - 1):].reshape(nt * (CONV_W - 1), D_MODEL)
    edge = _pad_rows(edge, -(-edge.shape[0] // SUBLANE) * SUBLANE)
    u_edge = _matmul(edge, lw['w_up'], edge.shape[0], 512)[:nt * (CONV_W - 1)].reshape(nt, CONV_W - 1, 2 * D_FF)
    per_seq = t // tm
    conv_state = u_edge[per_seq - 1::per_seq]
    starts_seq = (jnp.arange(nt) % per_seq == 0)[:, None, None]
    prev = jnp.where(starts_seq, 0.0, jnp.roll(u_edge, 1, axis=0))
    prev = jnp.pad(prev, ((0, 0), (SUBLANE - (CONV_W - 1), 0), (0, 0)))
    y = _ffn(h, lw, tm, 512, alpha, prev=prev)

    seg = lambda name, w: c[:, _COL[name]:_COL[name] + w]
    win_rows = min(WINDOW, t)
    states = (
        seg('kv_cmp', kw).reshape(b, t, 2, KV_G, HEAD_DIM),
        seg('kv_slc', kw).reshape(b, t, 2, KV_G, HEAD_DIM),
        c[:, _COL['moba'] + MIX:_COL['moba'] + 3 * MIX].reshape(b, t, 2, N_HEADS, HEAD_DIM),
        c[:, _COL['fox'] + MIX:_COL['fox'] + 3 * MIX].reshape(b, t, 2, N_HEADS, HEAD_DIM),
        logf.reshape(b, t, N_HEADS),
        seg('kv_win', kw).reshape(b, t, 2, KV_G, HEAD_DIM)[:, t - win_rows:],
        hlr.reshape(b, SUBLANE, S5_GROUPS, S5_STATE)[:, SUBLANE - 1],
        hli.reshape(b, SUBLANE, S5_GROUPS, S5_STATE)[:, SUBLANE - 1],
        conv_state)
    return y, states


def _sample_layer(x, bs, t_real, lw, alpha, past, page_table, past_len):
    n = bs * T_PAD
    kw = 2 * KV_G * HEAD_DIM
    c = _matmul(x, lw['w_in'], n, 512)
    c3 = c.reshape(bs, T_PAD, WP)
    logf = _logf(c, lw['fox_b'], n)
    logf3 = logf.reshape(bs, T_PAD, N_HEADS)
    tmask = (jnp.arange(T_PAD) < t_real)[None, :, None]
    new_t = jnp.pad(jnp.where(tmask, logf3, 0.0).transpose(0, 2, 1), ((0, 0), (0, 0), (0, LANE - T_PAD)))
    cum = _cumsum_paged(page_table, past['fox_logf_t'], new_t)
    cq = cum[:, :, past_len:past_len + T_PAD].transpose(0, 2, 1)
    o_fox = _paged_mha_dec(
        _fox_dec_kernel, "fox_decode", c3, _COL['fox'], page_table, past['fox'], [cq, cum],
        [pl.BlockSpec((1, T_PAD, N_HEADS), lambda b, pt: (b, 0, 0)),
         pl.BlockSpec((1, N_HEADS, cum.shape[-1]), lambda b, pt: (b, 0, 0))], past_len)
    o_moba = _paged_mha_dec(_moba_dec_kernel, "moba_decode", c3, _COL['moba'], page_table, past['moba'], [], [],
                            past_len)
    kvc = _cmp_paged(page_table, past['nsa_cmp'], lw['cmp'])
    nchunk = kvc.shape[1]
    ov = _overlap_matrix(nchunk - CMP_LEN // CMP_STRIDE + 1, -(-(past_len + t_real) // SLC_BLOCK))
    o_nsa = _nsa_dec(c3, kvc, ov, past['nsa_win'], page_table, past['nsa_slc'], past_len, t_real)
    h0 = [jnp.pad(s.reshape(bs, 1, -1), ((0, 0), (0, T_PAD - 1), (0, 0))).reshape(n, -1) for s in past['s5']]
    y_s5, hlr, hli = _s5_scan(c, lw['s5'], n, T_PAD, h0=h0)
    o_s5 = _glu(y_s5, lw['w_glu'], n)
    h = _merge((o_nsa.reshape(n, MIX), o_s5, o_moba.reshape(n, MIX), o_fox.reshape(n, MIX)), c, x,
               lw['w_branch'], lw['w_out'], lw['ln1_g'], lw['ln1_b'], min(256, n), alpha)

    buf = past['ffn_conv']
    z = jnp.zeros((bs, T_PAD - 1, 2 * D_FF), F32)
    p1 = jnp.concatenate([buf[:, 1:2], z], axis=1).reshape(n, 2 * D_FF)
    p2 = jnp.concatenate([buf, z[:, 1:]], axis=1).reshape(n, 2 * D_FF)
    y = _ffn(h, lw, min(512, n), 512, alpha, p12=(p1, p2), seg=T_PAD)
    last2 = h.reshape(bs, T_PAD, D_MODEL)[:, t_real - (CONV_W - 1):t_real].reshape(bs * (CONV_W - 1), D_MODEL)
    conv_state = _matmul(last2, lw['w_up'], last2.shape[0], 512).reshape(bs, CONV_W - 1, 2 * D_FF)

    tr = lambda a: a[:, :t_real]
    kv_win_new = tr(c3[:, :, _COL['kv_win']:_COL['kv_win'] + kw])
    full_win = jnp.concatenate([past['nsa_win'], kv_win_new], axis=1)
    lw_ = full_win.shape[1]
    new_win = full_win[:, lw_ - min(WINDOW, lw_):]
    states = (
        tr(c3[:, :, _COL['kv_cmp']:_COL['kv_cmp'] + kw]).reshape(bs, t_real, 2, KV_G, HEAD_DIM),
        tr(c3[:, :, _COL['kv_slc']:_COL['kv_slc'] + kw]).reshape(bs, t_real, 2, KV_G, HEAD_DIM),
        tr(c3[:, :, _COL['moba'] + MIX:_COL['moba'] + 3 * MIX]).reshape(bs, t_real, 2, N_HEADS, HEAD_DIM),
        tr(c3[:, :, _COL['fox'] + MIX:_COL['fox'] + 3 * MIX]).reshape(bs, t_real, 2, N_HEADS, HEAD_DIM),
        tr(logf3),
        new_win.reshape(bs, new_win.shape[1], 2, KV_G, HEAD_DIM),
        hlr.reshape(bs, T_PAD, S5_GROUPS, S5_STATE)[:, t_real - 1],
        hli.reshape(bs, T_PAD, S5_GROUPS, S5_STATE)[:, t_real - 1],
        conv_state)
    return y, states


def kernel(x_prompt, x_sample, cache_nsa_cmp_kv, cache_nsa_slc_kv, cache_moba_kv, cache_fox_kv, cache_fox_logf,
           page_table, cache_nsa_win_kv, state_s5_re, state_s5_im, state_ffn_conv, w_in, fox_b_f, nsa_cmp_pos,
           nsa_cmp_wk, nsa_cmp_wv, s5_a_re, s5_a_im, s5_b_re, s5_b_im, s5_c_re, s5_c_im, s5_d, s5_log_step,
           s5_w_glu, w_branch, w_out, ln1_g, ln1_b, ffn_w_up, ffn_conv_w, ffn_conv_b, ffn_w_down, ln2_g, ln2_b):
    depth = w_in.shape[0]
    b, t, d = x_prompt.shape
    bs, ts, _ = x_sample.shape
    n_phys, page = cache_nsa_cmp_kv.shape[1:3]
    past_len = page_table.shape[1] * page
    assert d == D_MODEL and w_in.shape[2] == IN_WIDTH and ffn_w_down.shape[1] == D_FF
    assert ts <= T_PAD and past_len % MOBA_BLOCK == 0 and page == LANE and (past_len + ts) // CMP_STRIDE * CMP_STRIDE <= past_len
    alpha = float((2 * depth) ** 0.25)
    params = dict(w_in=w_in, fox_b_f=fox_b_f, nsa_cmp_pos=nsa_cmp_pos, nsa_cmp_wk=nsa_cmp_wk, nsa_cmp_wv=nsa_cmp_wv,
                  s5_a_re=s5_a_re, s5_a_im=s5_a_im, s5_b_re=s5_b_re, s5_b_im=s5_b_im, s5_c_re=s5_c_re,
                  s5_c_im=s5_c_im, s5_d=s5_d, s5_log_step=s5_log_step, s5_w_glu=s5_w_glu, w_branch=w_branch,
                  w_out=w_out, ln1_g=ln1_g, ln1_b=ln1_b, ffn_w_up=ffn_w_up, ffn_conv_w=ffn_conv_w,
                  ffn_conv_b=ffn_conv_b, ffn_w_down=ffn_w_down, ln2_g=ln2_g, ln2_b=ln2_b)
    kw = 2 * KV_G * HEAD_DIM
    yp = x_prompt.reshape(b * t, d)
    ys = jnp.pad(x_sample, ((0, 0), (0, T_PAD - ts), (0, 0))).reshape(bs * T_PAD, d)
    st_p, st_s = [], []
    for l in range(depth):
        lw = _prep_layer(l, params)
        past = dict(
            nsa_cmp=cache_nsa_cmp_kv[l].reshape(n_phys, page, kw),
            nsa_slc=cache_nsa_slc_kv[l].reshape(n_phys, page, kw),
            moba=cache_moba_kv[l].reshape(n_phys, page, 2 * MIX),
            fox=cache_fox_kv[l].reshape(n_phys, page, 2 * MIX),
            fox_logf_t=cache_fox_logf[l].transpose(0, 2, 1),
            nsa_win=cache_nsa_win_kv[l].reshape(bs, -1, kw),
            s5=(state_s5_re[l], state_s5_im[l]),
            ffn_conv=state_ffn_conv[l])
        yp, sp = _prompt_layer(yp, b, t, lw, alpha)
        ys, ss = _sample_layer(ys, bs, ts, lw, alpha, past, page_table, past_len)
        st_p.append(sp)
        st_s.append(ss)
    sp = [jnp.stack(z) for z in zip(*st_p)]
    ss = [jnp.stack(z) for z in zip(*st_s)]
    out = [yp.reshape(b, t, d), ys.reshape(bs, T_PAD, d)[:, :ts]]
    for a, c in zip(sp, ss):
        out += [a, c]
    return tuple(out)
```

```python
import functools

import numpy as np
import jax
import jax.numpy as jnp
from jax import lax
from jax.experimental import pallas as pl
from jax.experimental.pallas import tpu as pltpu

F32 = jnp.float32
BF16 = jnp.bfloat16
HI = lax.Precision.HIGHEST

LANE = 128
SUBLANE = 8
VMEM_LIMIT = 56 * 1024 * 1024

D_MODEL = 2048
HEAD_DIM = 64
N_BRANCH = 4
MIX = D_MODEL // N_BRANCH
N_HEADS = MIX // HEAD_DIM
KV_G = 2
HG = N_HEADS // KV_G
CMP_LEN = 32
CMP_STRIDE = 16
SLC_BLOCK = 64
SLC_TOPN = 16
WINDOW = 512
MOBA_BLOCK = 256
MOBA_TOPK = 3
S5_CH = 16
S5_GROUPS = MIX // S5_CH
S5_STATE = 64
D_FF = 5632
CONV_W = 3
LN_EPS = 1e-5
SCALE = HEAD_DIM ** -0.5
NEG = -1e30
FORCE = 1e4
T_PAD = 8
SLOPES = tuple(float(v) for v in np.asarray(2.0 ** (-8.0 * np.arange(1, N_HEADS + 1) / N_HEADS), np.float32))

_SPLITS = (('nsa_q', MIX), ('kv_cmp', 2 * KV_G * HEAD_DIM), ('kv_slc', 2 * KV_G * HEAD_DIM),
           ('kv_win', 2 * KV_G * HEAD_DIM), ('nsa_gate', 3 * N_HEADS), ('s5_u', MIX),
           ('moba', 3 * MIX), ('fox', 3 * MIX), ('fox_f', N_HEADS), ('merge', N_BRANCH * D_MODEL))
_SRC = {}
_o = 0
for _n, _w in _SPLITS:
    _SRC[_n] = (_o, _w)
    _o += _w
IN_WIDTH = _o
_COL = dict(merge=0, nsa_q=8192, kv_cmp=8704, kv_slc=8960, kv_win=9216, nsa_gate=9472, s5_u=9600,
            moba=10240, fox=11776, fox_f=13312)
WP = 13824


def _cparams(sem):
    return pltpu.CompilerParams(dimension_semantics=sem, vmem_limit_bytes=VMEM_LIMIT)


def _dot_nt(a, b, precision=None):
    return lax.dot_general(a, b, (((1,), (1,)), ((), ())), precision=precision, preferred_element_type=F32)


def _sigmoid(x):
    return 1.0 / (1.0 + jnp.exp(-x))


def _pack_w_in(w):
    d = w.shape[0]
    order = ('merge', 'nsa_q', 'kv_cmp', 'kv_slc', 'kv_win', 'nsa_gate', 's5_u', 'moba', 'fox', 'fox_f')
    parts, pos = [], 0
    for name in order:
        if _COL[name] > pos:
            parts.append(jnp.zeros((d, _COL[name] - pos), w.dtype))
        s, wd = _SRC[name]
        parts.append(w[:, s:s + wd])
        pos = _COL[name] + wd
    parts.append(jnp.zeros((d, WP - pos), w.dtype))
    return jnp.concatenate(parts, axis=1).astype(BF16)


def _mm_kernel(x_ref, w_ref, o_ref, xb_ref):
    @pl.when(pl.program_id(1) == 0)
    def _():
        xb_ref[...] = x_ref[...].astype(BF16)

    o_ref[...] = jnp.dot(xb_ref[...], w_ref[...], preferred_element_type=F32)


def _matmul(x, w, tm, tn):
    m, k = x.shape
    n = w.shape[1]
    return pl.pallas_call(
        _mm_kernel, grid=(m // tm, n // tn),
        in_specs=[pl.BlockSpec((tm, k), lambda i, j: (i, 0)), pl.BlockSpec((k, tn), lambda i, j: (0, j))],
        out_specs=pl.BlockSpec((tm, tn), lambda i, j: (i, j)),
        out_shape=jax.ShapeDtypeStruct((m, n), F32),
        scratch_shapes=[pltpu.VMEM((tm, k), BF16)],
        compiler_params=_cparams(("parallel", "arbitrary")), name="mm")(x, w)


def _logf_kernel(c_ref, b_ref, o_ref):
    x = c_ref[...] + b_ref[...]
    y = jnp.minimum(x, 0.0) - jnp.log1p(jnp.exp(-jnp.abs(x)))
    o_ref[...] = y[:, :N_HEADS]


def _logf(c, b_pad, tm):
    n = c.shape[0]
    return pl.pallas_call(
        _logf_kernel, grid=(n // tm,),
        in_specs=[pl.BlockSpec((tm, LANE), lambda i: (i, _COL['fox_f'] // LANE)),
                  pl.BlockSpec((1, LANE), lambda i: (0, 0))],
        out_specs=pl.BlockSpec((tm, N_HEADS), lambda i: (i, 0)),
        out_shape=jax.ShapeDtypeStruct((n, N_HEADS), F32),
        compiler_params=_cparams(("parallel",)), name="logf")(c, b_pad)


def _cumsum_kernel(*refs, n_in):
    in_refs, o_ref = refs[-n_in - 1:-1], refs[-1]
    r = lax.broadcasted_iota(jnp.int32, (LANE, LANE), 0)
    c = lax.broadcasted_iota(jnp.int32, (LANE, LANE), 1)
    tri = jnp.where(r <= c, 1.0, 0.0).astype(F32)
    carry = jnp.zeros((N_HEADS, 1), F32)
    off = 0
    for ref in in_refs:
        for j in range(ref.shape[-1] // LANE):
            x = ref[0, :, j * LANE:(j + 1) * LANE]
            cs = jnp.dot(x, tri, precision=HI, preferred_element_type=F32) + carry
            o_ref[0, :, off:off + LANE] = cs
            carry = cs[:, LANE - 1:LANE]
            off += LANE


def _cumsum_prompt(logf_t):
    b, h, t = logf_t.shape
    return pl.pallas_call(
        functools.partial(_cumsum_kernel, n_in=1), grid=(b,),
        in_specs=[pl.BlockSpec((1, h, t), lambda i: (i, 0, 0))],
        out_specs=pl.BlockSpec((1, h, t), lambda i: (i, 0, 0)),
        out_shape=jax.ShapeDtypeStruct((b, h, t), F32),
        compiler_params=_cparams(("parallel",)), name="fox_cumsum_prompt")(logf_t)


def _cumsum_paged(page_table, pool_t, first, new_t):
    bs, npages = page_table.shape
    page = pool_t.shape[-1]
    in_specs = [pl.BlockSpec((1, N_HEADS, page), lambda b, pt, p=p: (first + pt[b, p], 0, 0))
                for p in range(npages)]
    in_specs.append(pl.BlockSpec((1, N_HEADS, LANE), lambda b, pt: (b, 0, 0)))
    tot = npages * page + LANE
    return pl.pallas_call(
        functools.partial(_cumsum_kernel, n_in=npages + 1),
        grid_spec=pltpu.PrefetchScalarGridSpec(
            num_scalar_prefetch=1, grid=(bs,), in_specs=in_specs,
            out_specs=pl.BlockSpec((1, N_HEADS, tot), lambda b, pt: (b, 0, 0))),
        out_shape=jax.ShapeDtypeStruct((bs, N_HEADS, tot), F32),
        compiler_params=_cparams(("parallel",)), name="fox_cumsum_paged")(
            page_table, *([pool_t] * npages), new_t)


M_FLOOR = -1e29


def _online_update(carry, s, v):
    m, l, acc = carry
    m_new = jnp.maximum(m, jnp.max(s, axis=1, keepdims=True))
    p = jnp.exp(s - m_new)
    alpha = jnp.exp(m - m_new)
    l = alpha * l + jnp.sum(p, axis=1, keepdims=True)
    acc = alpha * acc + jnp.dot(p.astype(BF16), v, preferred_element_type=F32)
    return m_new, l, acc


def _online_init(rows, dv):
    return (jnp.full((rows, 1), M_FLOOR, F32), jnp.zeros((rows, 1), F32), jnp.zeros((rows, dv), F32))


def _online_finish(carry):
    _, l, acc = carry
    return acc / jnp.maximum(l, 1e-30)


def _attend_tiles(s_tiles, v_tiles):
    m = functools.reduce(jnp.maximum, [jnp.max(s, axis=1, keepdims=True) for s in s_tiles])
    l, acc = 0.0, 0.0
    for s, (v, feature_major) in zip(s_tiles, v_tiles):
        e = jnp.where(s > 0.5 * NEG, jnp.exp(s - m), 0.0)
        l = l + jnp.sum(e, axis=1, keepdims=True)
        eb = e.astype(BF16)
        acc = acc + (_dot_nt(eb, v) if feature_major else jnp.dot(eb, v, preferred_element_type=F32))
    return acc / jnp.maximum(l, 1e-30)


def _topn_mask(v, ncols, topn):
    jl = lax.broadcasted_iota(jnp.int32, v.shape, 1)
    rank = jnp.zeros(v.shape, F32)
    for j2 in range(ncols):
        col = v[:, j2:j2 + 1]
        beats = (col > v) | ((col == v) & (jl > j2))
        rank = rank + jnp.where(beats, 1.0, 0.0)
    return rank < topn


def _pad_rows(a, rows):
    return jnp.concatenate([a, jnp.zeros((rows - a.shape[0], a.shape[1]), a.dtype)], axis=0)


def _fox_prompt_kernel(q_ref, k_ref, v_ref, cr_ref, o_ref, *, tq):
    qi = pl.program_id(2)
    q0 = pl.multiple_of(qi * tq, tq)
    row = lax.broadcasted_iota(jnp.int32, (tq, tq), 0)
    col = lax.broadcasted_iota(jnp.int32, (tq, tq), 1)
    qs = [(q_ref[:, h2 * HEAD_DIM:(h2 + 1) * HEAD_DIM] * SCALE).astype(BF16) for h2 in range(2)]
    c0 = [cr_ref[0, h2, :, pl.ds(q0, tq)][:, 0:1] for h2 in range(2)]

    def tile(ks, carries, diag):
        out = []
        for h2 in range(2):
            lo = h2 * HEAD_DIM
            k = k_ref[pl.ds(ks, tq), lo:lo + HEAD_DIM].astype(BF16)
            v = v_ref[pl.ds(ks, tq), lo:lo + HEAD_DIM].astype(BF16)
            s = _dot_nt(qs[h2], k) + (c0[h2] - cr_ref[0, h2, :, pl.ds(ks, tq)])
            if diag:
                s = jnp.where(col <= row, s, NEG)
            out.append(_online_update(carries[h2], s, v))
        return tuple(out)

    init = (_online_init(tq, HEAD_DIM), _online_init(tq, HEAD_DIM))
    carries = lax.fori_loop(0, qi, lambda kj, c: tile(pl.multiple_of(kj * tq, tq), c, False), init)
    carries = tile(q0, carries, True)
    o_ref[...] = jnp.concatenate([_online_finish(c) for c in carries], axis=1)


def _fox_prompt(c, cumr, b, t, tq):
    nq = t // tq
    base = _COL['fox'] // LANE
    hp_n = N_HEADS // 2
    return pl.pallas_call(
        functools.partial(_fox_prompt_kernel, tq=tq), grid=(b, hp_n, nq),
        in_specs=[pl.BlockSpec((tq, LANE), lambda i, h, q: (i * nq + q, base + h)),
                  pl.BlockSpec((t, LANE), lambda i, h, q: (i, base + hp_n + h)),
                  pl.BlockSpec((t, LANE), lambda i, h, q: (i, base + 2 * hp_n + h)),
                  pl.BlockSpec((1, 2, 1, t), lambda i, h, q: (i, h, 0, 0))],
        out_specs=pl.BlockSpec((tq, LANE), lambda i, h, q: (i * nq + q, h)),
        out_shape=jax.ShapeDtypeStruct((b * t, MIX), F32),
        compiler_params=_cparams(("parallel", "arbitrary", "arbitrary")), name="fox_prompt")(
            c, c, c, cumr)


def _head_slope(hp, h2):
    s = jnp.float32(SLOPES[h2])
    for k in range(1, N_HEADS // 2):
        s = jnp.where(hp == k, jnp.float32(SLOPES[2 * k + h2]), s)
    return s


def _moba_prompt_kernel(q_ref, k_ref, v_ref, o_ref, kmean_ref, *, tq, nb):
    hp = pl.program_id(1)
    qi = pl.program_id(2)

    @pl.when(qi == 0)
    def _():
        kmean_ref[...] = jnp.zeros_like(kmean_ref)
        for n in range(nb):
            kmean_ref[n:n + 1, :] = jnp.mean(k_ref[n * tq:(n + 1) * tq, :], axis=0, keepdims=True)

    q0 = pl.multiple_of(qi * tq, tq)
    row = lax.broadcasted_iota(jnp.int32, (tq, tq), 0)
    col = lax.broadcasted_iota(jnp.int32, (tq, tq), 1)
    colpos = lax.broadcasted_iota(jnp.int32, (1, tq), 1)
    jl = lax.broadcasted_iota(jnp.int32, (tq, LANE), 1)
    qs, slopes, blockbias = [], [], []
    for h2 in range(2):
        lo = h2 * HEAD_DIM
        qf = q_ref[:, lo:lo + HEAD_DIM]
        gate = _dot_nt(qf, kmean_ref[:, lo:lo + HEAD_DIM], precision=HI)
        gate = jnp.where(jl < qi, gate, NEG)
        picked = _topn_mask(gate, nb, MOBA_TOPK) & (jl < qi)
        qs.append((qf * SCALE).astype(BF16))
        slopes.append(_head_slope(hp, h2))
        blockbias.append(jnp.where(picked, 0.0, NEG))

    def tile(kj, ks, carries, diag):
        out = []
        for h2 in range(2):
            lo = h2 * HEAD_DIM
            k = k_ref[pl.ds(ks, tq), lo:lo + HEAD_DIM].astype(BF16)
            v = v_ref[pl.ds(ks, tq), lo:lo + HEAD_DIM].astype(BF16)
            s = _dot_nt(qs[h2], k) + slopes[h2] * (ks - q0 + colpos).astype(F32)
            if diag:
                s = jnp.where(col <= row, s, NEG)
            else:
                s = s + jnp.min(jnp.where(jl == kj, blockbias[h2], 0.0), axis=1, keepdims=True)
            out.append(_online_update(carries[h2], s, v))
        return tuple(out)

    init = (_online_init(tq, HEAD_DIM), _online_init(tq, HEAD_DIM))
    carries = lax.fori_loop(0, qi, lambda kj, c: tile(kj, pl.multiple_of(kj * tq, tq), c, False), init)
    carries = tile(qi, q0, carries, True)
    o_ref[...] = jnp.concatenate([_online_finish(c) for c in carries], axis=1)


def _moba_prompt(c, b, t):
    tq = MOBA_BLOCK
    nq = t // tq
    base = _COL['moba'] // LANE
    hp_n = N_HEADS // 2
    return pl.pallas_call(
        functools.partial(_moba_prompt_kernel, tq=tq, nb=nq), grid=(b, hp_n, nq),
        in_specs=[pl.BlockSpec((tq, LANE), lambda i, h, q: (i * nq + q, base + h)),
                  pl.BlockSpec((t, LANE), lambda i, h, q: (i, base + hp_n + h)),
                  pl.BlockSpec((t, LANE), lambda i, h, q: (i, base + 2 * hp_n + h))],
        out_specs=pl.BlockSpec((tq, LANE), lambda i, h, q: (i * nq + q, h)),
        out_shape=jax.ShapeDtypeStruct((b * t, MIX), F32),
        scratch_shapes=[pltpu.VMEM((LANE, LANE), F32)],
        compiler_params=_cparams(("parallel", "arbitrary", "arbitrary")), name="moba_prompt")(c, c, c)


def _cmp_kernel(*refs, n_in, feature_major):
    if feature_major:
        xs_ref, refs = refs[-1], refs[:-1]
        x_refs = refs[-5 - n_in:-5]
        page = x_refs[0].shape[2]
        halves = x_refs[0].shape[1] // LANE
        for p, r in enumerate(x_refs):
            for j in range(halves):
                xs_ref[j, p * page:(p + 1) * page, :] = r[0, j * LANE:(j + 1) * LANE, :].T
        nchunk = n_in * page // CMP_STRIDE
        x = jnp.concatenate([xs_ref[j, pl.ds(l, nchunk, stride=CMP_STRIDE), :]
                             for l in range(CMP_STRIDE) for j in range(halves)], axis=1)
    else:
        x = refs[-6][0]
    pos_ref, wcat_ref, w0_ref, w1_ref, o_ref = refs[-5:]
    xb = x.astype(BF16)
    p0 = jnp.dot(xb, w0_ref[...], preferred_element_type=F32)
    p1 = jnp.dot(xb, w1_ref[...], preferred_element_type=F32)
    bias = jnp.dot(pos_ref[...].astype(BF16), wcat_ref[...], preferred_element_type=F32)[0:1]
    o_ref[0] = p0 + pltpu.roll(p1, p1.shape[0] - 1, 0) + bias


def _cmp_weights(pos, wk, wv):
    r = CMP_LEN // CMP_STRIDE
    w = jnp.stack([wk, wv]).reshape(2, r, CMP_STRIDE, HEAD_DIM, HEAD_DIM)
    e2 = jnp.eye(2, dtype=w.dtype)
    eg = jnp.eye(KV_G, dtype=w.dtype)
    big = jnp.einsum('krlde,kK,gG->rlkgdKGe', w, e2, eg)
    big = big.reshape(r, CMP_STRIDE * 2 * KV_G * HEAD_DIM, 2 * KV_G * HEAD_DIM).astype(BF16)
    wcat = jnp.concatenate([wk, wk, wv, wv], axis=1).astype(BF16)
    posb = jnp.zeros((SUBLANE, CMP_LEN * HEAD_DIM), F32).at[0].set(pos.reshape(-1))
    return posb, wcat, big[0], big[1]


def _const_specs(arrays):
    return [pl.BlockSpec(a.shape, lambda *_, nd=a.ndim: (0,) * nd) for a in arrays]


def _cmp_prompt(c, b, t, cw):
    kw = 2 * KV_G * HEAD_DIM
    nchunk = t // CMP_STRIDE
    x = c[:, _COL['kv_cmp']:_COL['kv_cmp'] + kw].reshape(b, nchunk, CMP_STRIDE * kw)
    return pl.pallas_call(
        functools.partial(_cmp_kernel, n_in=1, feature_major=False), grid=(b,),
        in_specs=[pl.BlockSpec((1, nchunk, CMP_STRIDE * kw), lambda i: (i, 0, 0))] + _const_specs(cw),
        out_specs=pl.BlockSpec((1, nchunk, kw), lambda i: (i, 0, 0)),
        out_shape=jax.ShapeDtypeStruct((b, nchunk, kw), F32),
        compiler_params=_cparams(("parallel",)), name="nsa_cmp_prompt")(x, *cw)


def _cmp_paged(page_table, pool, first, cw):
    bs, npages = page_table.shape
    kw, page = pool.shape[1:]
    nchunk = npages * page // CMP_STRIDE
    return pl.pallas_call(
        functools.partial(_cmp_kernel, n_in=npages, feature_major=True),
        grid_spec=pltpu.PrefetchScalarGridSpec(
            num_scalar_prefetch=1, grid=(bs,), in_specs=_page_specs(pool, npages, first) + _const_specs(cw),
            out_specs=pl.BlockSpec((1, nchunk, kw), lambda b, pt: (b, 0, 0)),
            scratch_shapes=[pltpu.VMEM((kw // LANE, npages * page, LANE), F32)]),
        out_shape=jax.ShapeDtypeStruct((bs, nchunk, kw), F32),
        compiler_params=_cparams(("parallel",)), name="nsa_cmp_paged")(page_table, *([pool] * npages), *cw)


def _overlap_matrix(nc, ns):
    i_c = np.arange(LANE)[:, None] * CMP_STRIDE
    j_s = np.arange(LANE)[None, :] * SLC_BLOCK
    ov = (i_c < j_s + SLC_BLOCK) & (i_c + CMP_LEN > j_s)
    ov &= (np.arange(LANE)[:, None] < nc) & (np.arange(LANE)[None, :] < ns)
    return jnp.asarray(ov, F32)


def _nsa_prompt_kernel(q_ref, slc_ref, win_ref, cmp_ref, g_ref, ov_ref, o_ref, *, tq, nc, ns):
    qi = pl.program_id(1)
    q0 = pl.multiple_of(qi * tq, tq)
    rows = HG * tq
    rl = lax.broadcasted_iota(jnp.int32, (tq, 1), 0)
    t1 = q0 + rl
    t4 = jnp.concatenate([t1] * HG, axis=0)
    lane = lax.broadcasted_iota(jnp.int32, (1, LANE), 1)
    colpos = lax.broadcasted_iota(jnp.int32, (1, tq), 1)
    dloc = lax.broadcasted_iota(jnp.int32, (tq, tq), 0) - lax.broadcasted_iota(jnp.int32, (tq, tq), 1)
    sg = _sigmoid(g_ref[...])
    jl = lax.broadcasted_iota(jnp.int32, (tq, LANE), 1)
    cur = t1 // SLC_BLOCK
    erow = lax.broadcasted_iota(jnp.int32, (LANE, tq), 0)
    ecol = lax.broadcasted_iota(jnp.int32, (LANE, tq), 1)
    wtiles = WINDOW // tq
    qs, slope, o_cmp, blockbias = [], [], [], []
    for g in range(KV_G):
        kl = g * HEAD_DIM
        vl = KV_G * HEAD_DIM + g * HEAD_DIM
        qg = jnp.concatenate([q_ref[:, (g * HG + h) * HEAD_DIM:(g * HG + h + 1) * HEAD_DIM] for h in range(HG)],
                             axis=0)
        qs.append((qg * SCALE).astype(BF16))
        slope.append(jnp.concatenate([jnp.full((tq, 1), SLOPES[g * HG + h], F32) for h in range(HG)], axis=0))

        kc = cmp_ref[0, :, kl:kl + HEAD_DIM].astype(BF16)
        vc = cmp_ref[0, :, vl:vl + HEAD_DIM].astype(BF16)
        dist = (t4 - (lane * CMP_STRIDE + CMP_LEN - 1)).astype(F32)
        ok_c = (dist >= 0) & (lane < nc)
        s = jnp.where(ok_c, _dot_nt(qs[g], kc) - slope[g] * dist, NEG)
        m = jnp.max(s, axis=1, keepdims=True)
        e = jnp.where(ok_c, jnp.exp(s - m), 0.0)
        p_c = e / jnp.maximum(jnp.sum(e, axis=1, keepdims=True), 1e-30)
        o_cmp.append(jnp.dot(p_c.astype(BF16), vc, preferred_element_type=F32))

        psum = p_c[0:tq]
        for h in range(1, HG):
            psum = psum + p_c[h * tq:(h + 1) * tq]
        imp = jnp.dot(psum, ov_ref[...], precision=HI, preferred_element_type=F32)
        forced = (jl == 0) | (jl == cur) | (jl == cur - 1)
        imp = jnp.where(forced, FORCE, imp)
        imp = jnp.where(jl <= cur, imp, NEG)
        picked = _topn_mask(imp, ns, min(SLC_TOPN, ns)) & (jl <= cur)
        blockbias.append(jnp.where(picked, 0.0, NEG).astype(BF16))

    def attend(ref, ks, carries, bias):
        cpos = (ks - q0 + colpos).astype(F32)
        out = []
        for g in range(KV_G):
            kl = g * HEAD_DIM
            vl = KV_G * HEAD_DIM + g * HEAD_DIM
            k = ref[pl.ds(ks, tq), kl:kl + HEAD_DIM].astype(BF16)
            v = ref[pl.ds(ks, tq), vl:vl + HEAD_DIM].astype(BF16)
            s = _dot_nt(qs[g], k) + slope[g] * cpos + jnp.concatenate([bias[g]] * HG, axis=0)
            out.append(_online_update(carries[g], s, v))
        return tuple(out)

    def slc_tile(ks, carries, diag):
        expand = jnp.where((ks + ecol) // SLC_BLOCK == erow, 1.0, 0.0).astype(BF16)
        bias = [jnp.dot(blockbias[g], expand, preferred_element_type=F32) for g in range(KV_G)]
        if diag:
            bias = [jnp.where(dloc >= 0, b, NEG) for b in bias]
        return attend(slc_ref, ks, carries, bias)

    def win_tile(ks, carries):
        d = dloc + (q0 - ks)
        wb = jnp.where((d >= 0) & (d < WINDOW), 0.0, NEG)
        return attend(win_ref, ks, carries, [wb] * KV_G)

    init = tuple(_online_init(rows, HEAD_DIM) for _ in range(KV_G))
    c_slc = lax.fori_loop(0, qi, lambda kj, c: slc_tile(pl.multiple_of(kj * tq, tq), c, False), init)
    o_slc = [_online_finish(c) for c in slc_tile(q0, c_slc, True)]
    c_win = lax.fori_loop(jnp.maximum(qi - wtiles, 0), qi + 1,
                          lambda kj, c: win_tile(pl.multiple_of(kj * tq, tq), c), init)
    o_win = [_online_finish(c) for c in c_win]

    for g in range(KV_G):
        for h in range(HG):
            hh = g * HG + h
            r0 = h * tq
            o = (sg[:, hh:hh + 1] * o_cmp[g][r0:r0 + tq]
                 + sg[:, N_HEADS + hh:N_HEADS + hh + 1] * o_slc[g][r0:r0 + tq]
                 + sg[:, 2 * N_HEADS + hh:2 * N_HEADS + hh + 1] * o_win[g][r0:r0 + tq])
            o_ref[:, hh * HEAD_DIM:(hh + 1) * HEAD_DIM] = o


def _nsa_prompt(c, kvc, ov, b, t, tq):
    nq = t // tq
    kw = 2 * KV_G * HEAD_DIM
    nchunk = t // CMP_STRIDE
    nc = nchunk - CMP_LEN // CMP_STRIDE + 1
    ns = -(-t // SLC_BLOCK)
    return pl.pallas_call(
        functools.partial(_nsa_prompt_kernel, tq=tq, nc=nc, ns=ns), grid=(b, nq),
        in_specs=[pl.BlockSpec((tq, MIX), lambda i, q: (i * nq + q, _COL['nsa_q'] // MIX)),
                  pl.BlockSpec((t, kw), lambda i, q: (i, _COL['kv_slc'] // kw)),
                  pl.BlockSpec((t, kw), lambda i, q: (i, _COL['kv_win'] // kw)),
                  pl.BlockSpec((1, nchunk, kw), lambda i, q: (i, 0, 0)),
                  pl.BlockSpec((tq, LANE), lambda i, q: (i * nq + q, _COL['nsa_gate'] // LANE)),
                  pl.BlockSpec((LANE, LANE), lambda i, q: (0, 0))],
        out_specs=pl.BlockSpec((tq, MIX), lambda i, q: (i * nq + q, 0)),
        out_shape=jax.ShapeDtypeStruct((b * t, MIX), F32),
        compiler_params=_cparams(("parallel", "arbitrary")), name="nsa_prompt")(c, c, c, kvc, c, ov)


def _dec_rows():
    rid = lax.broadcasted_iota(jnp.int32, (N_HEADS * T_PAD, 1), 0)
    t8 = rid % T_PAD
    slope = jnp.concatenate([jnp.full((T_PAD, 1), SLOPES[h], F32) for h in range(N_HEADS)], axis=0)
    return t8, slope


def _expand_heads(q8):
    lane = lax.broadcasted_iota(jnp.int32, q8.shape, 1)
    return jnp.concatenate([jnp.where(lane // HEAD_DIM == h, q8, 0.0) for h in range(N_HEADS)], axis=0)


def _collapse_heads(res):
    lane = lax.broadcasted_iota(jnp.int32, (T_PAD, res.shape[1]), 1)
    out = jnp.zeros((T_PAD, res.shape[1]), F32)
    for h in range(N_HEADS):
        out = out + jnp.where(lane // HEAD_DIM == h, res[h * T_PAD:(h + 1) * T_PAD], 0.0)
    return out


def _moba_dec_kernel(pt_ref, q_ref, kn_ref, vn_ref, *rest, npages, past):
    page_refs, o_ref = rest[:npages], rest[npages]
    page = page_refs[0].shape[2]
    t8, slope = _dec_rows()
    qpos = past + t8
    lane = lax.broadcasted_iota(jnp.int32, (1, LANE), 1)
    q8 = q_ref[0]
    qx_f = _expand_heads(q8)
    qx = (qx_f * SCALE).astype(BF16)
    per_blk = MOBA_BLOCK // page
    nb_past = past // MOBA_BLOCK
    lanei = lax.broadcasted_iota(jnp.int32, (MIX, LANE), 1)
    kmean = jnp.zeros((MIX, LANE), F32)
    for n in range(nb_past):
        tot = page_refs[n * per_blk][0, 0:MIX, :]
        for p in range(n * per_blk + 1, (n + 1) * per_blk):
            tot = tot + page_refs[p][0, 0:MIX, :]
        col = jnp.sum(tot, axis=1, keepdims=True) * (1.0 / MOBA_BLOCK)
        kmean = kmean + jnp.where(lanei == n, col, 0.0)
    gate = jnp.dot(qx_f, kmean, precision=HI, preferred_element_type=F32)
    jl = lax.broadcasted_iota(jnp.int32, gate.shape, 1)
    cur = qpos // MOBA_BLOCK
    gate = jnp.where(jl < cur, gate, NEG)
    nb = -(-(past + T_PAD) // MOBA_BLOCK)
    sel = jnp.where(_topn_mask(gate, nb, max(1, min(MOBA_TOPK, nb - 1))) & (jl < cur), 1.0, 0.0)
    s_tiles, v_tiles = [], []
    for p in range(npages):
        kt = page_refs[p][0, 0:MIX, :].astype(BF16)
        v_tiles.append((page_refs[p][0, MIX:2 * MIX, :].astype(BF16), True))
        n = (p * page) // MOBA_BLOCK
        d = (qpos - (p * page + lane)).astype(F32)
        valid = (sel[:, n:n + 1] > 0.5) & (d >= 0)
        s_tiles.append(jnp.where(valid, jnp.dot(qx, kt, preferred_element_type=F32) - slope * d, NEG))
    k = _pad_rows(kn_ref[0], LANE).astype(BF16)
    v_tiles.append((_pad_rows(vn_ref[0], LANE).astype(BF16), False))
    d = (t8 - lane).astype(F32)
    valid = (d >= 0) & (lane < T_PAD)
    s_tiles.append(jnp.where(valid, _dot_nt(qx, k) - slope * d, NEG))
    o_ref[0] = _collapse_heads(_attend_tiles(s_tiles, v_tiles))


def _fox_dec_kernel(pt_ref, q_ref, kn_ref, vn_ref, cq_ref, cr_ref, *rest, npages, past):
    page_refs, o_ref = rest[:npages], rest[npages]
    t8, _ = _dec_rows()
    lane = lax.broadcasted_iota(jnp.int32, (1, LANE), 1)
    qx = (_expand_heads(q_ref[0]) * SCALE).astype(BF16)
    cq8 = cq_ref[0]
    cq = jnp.concatenate([cq8[:, h:h + 1] for h in range(N_HEADS)], axis=0)

    def cum_tile(j):
        return jnp.concatenate([jnp.broadcast_to(cr_ref[0, h:h + 1, j * LANE:(j + 1) * LANE], (T_PAD, LANE))
                                for h in range(N_HEADS)], axis=0)

    s_tiles, v_tiles = [], []
    for p in range(npages):
        kt = page_refs[p][0, 0:MIX, :].astype(BF16)
        v_tiles.append((page_refs[p][0, MIX:2 * MIX, :].astype(BF16), True))
        s_tiles.append(jnp.dot(qx, kt, preferred_element_type=F32) + (cq - cum_tile(p)))
    k = _pad_rows(kn_ref[0], LANE).astype(BF16)
    v_tiles.append((_pad_rows(vn_ref[0], LANE).astype(BF16), False))
    valid = (lane <= t8) & (lane < T_PAD)
    s_tiles.append(jnp.where(valid, _dot_nt(qx, k) + (cq - cum_tile(npages)), NEG))
    o_ref[0] = _collapse_heads(_attend_tiles(s_tiles, v_tiles))


def _page_specs(pool, npages, first):
    blk = (1,) + pool.shape[1:]
    return [pl.BlockSpec(blk, lambda b, pt, p=p: (first + pt[b, p], 0, 0)) for p in range(npages)]


def _paged_mha_dec(kernel, name, c3, col, page_table, pool, first, extra_args, extra_specs, past):
    bs, npages = page_table.shape
    qb = col // MIX
    in_specs = [pl.BlockSpec((1, T_PAD, MIX), lambda b, pt: (b, 0, qb)),
                pl.BlockSpec((1, T_PAD, MIX), lambda b, pt: (b, 0, qb + 1)),
                pl.BlockSpec((1, T_PAD, MIX), lambda b, pt: (b, 0, qb + 2))]
    in_specs += extra_specs
    in_specs += _page_specs(pool, npages, first)
    return pl.pallas_call(
        functools.partial(kernel, npages=npages, past=past),
        grid_spec=pltpu.PrefetchScalarGridSpec(
            num_scalar_prefetch=1, grid=(bs,), in_specs=in_specs,
            out_specs=pl.BlockSpec((1, T_PAD, MIX), lambda b, pt: (b, 0, 0))),
        out_shape=jax.ShapeDtypeStruct((bs, T_PAD, MIX), F32),
        compiler_params=_cparams(("parallel",)), name=name)(
            page_table, c3, c3, c3, *extra_args, *([pool] * npages))


def _nsa_dec_kernel(pt_ref, q_ref, ns_ref, nw_ref, g_ref, cmp_ref, ov_ref, wc_ref, *rest, npages, past, nc, ns):
    page_refs, o_ref = rest[:npages], rest[npages]
    page = page_refs[0].shape[2]
    kw = KV_G * HEAD_DIM
    t8, slope = _dec_rows()
    qpos = past + t8
    lane = lax.broadcasted_iota(jnp.int32, (1, LANE), 1)
    lane8 = lax.broadcasted_iota(jnp.int32, (T_PAD, LANE), 1)
    q8 = q_ref[0] * SCALE
    rows = []
    for h in range(N_HEADS):
        x = q8[:, (h // 2) * LANE:(h // 2 + 1) * LANE]
        dst = h // HG
        if h % 2 != dst:
            x = pltpu.roll(x, HEAD_DIM, 1)
        rows.append(jnp.where((lane8 // HEAD_DIM) == dst, x, 0.0))
    qx = jnp.concatenate(rows, axis=0).astype(BF16)

    cm = cmp_ref[0]
    dist = (qpos - (lane * CMP_STRIDE + CMP_LEN - 1)).astype(F32)
    ok_c = (dist >= 0) & (lane < nc)
    s = jnp.where(ok_c, _dot_nt(qx, cm[:, 0:kw].astype(BF16)) - slope * dist, NEG)
    m = jnp.max(s, axis=1, keepdims=True)
    e = jnp.where(ok_c, jnp.exp(s - m), 0.0)
    p_c = e / jnp.maximum(jnp.sum(e, axis=1, keepdims=True), 1e-30)
    o_cmp = jnp.dot(p_c.astype(BF16), cm[:, kw:2 * kw].astype(BF16), preferred_element_type=F32)

    psum = []
    for g in range(KV_G):
        acc = p_c[g * HG * T_PAD:g * HG * T_PAD + T_PAD]
        for h in range(1, HG):
            r0 = (g * HG + h) * T_PAD
            acc = acc + p_c[r0:r0 + T_PAD]
        psum.append(acc)
    imp = jnp.dot(jnp.concatenate(psum, axis=0), ov_ref[...], precision=HI, preferred_element_type=F32)
    jl = lax.broadcasted_iota(jnp.int32, imp.shape, 1)
    tg = lax.broadcasted_iota(jnp.int32, (KV_G * T_PAD, 1), 0) % T_PAD
    cur = (past + tg) // SLC_BLOCK
    forced = (jl == 0) | (jl == cur) | (jl == cur - 1)
    imp = jnp.where(forced, FORCE, imp)
    imp = jnp.where(jl <= cur, imp, NEG)
    sel = jnp.where(_topn_mask(imp, ns, min(SLC_TOPN, ns)) & (jl <= cur), 1.0, 0.0)
    sel_rows = jnp.concatenate([sel[(h // HG) * T_PAD:(h // HG + 1) * T_PAD] for h in range(N_HEADS)], axis=0)

    def new_tile(ref, extra_valid):
        k = _pad_rows(ref[0, :, 0:kw], LANE).astype(BF16)
        v = _pad_rows(ref[0, :, kw:2 * kw], LANE).astype(BF16)
        d = (t8 - lane).astype(F32)
        valid = (d >= 0) & (lane < T_PAD) & extra_valid
        return jnp.where(valid, _dot_nt(qx, k) - slope * d, NEG), (v, False)

    s_tiles, v_tiles = [], []
    per = page // SLC_BLOCK
    for p in range(npages):
        kt = page_refs[p][0, 0:kw, :].astype(BF16)
        v_tiles.append((page_refs[p][0, kw:2 * kw, :].astype(BF16), True))
        picked = jnp.zeros((N_HEADS * T_PAD, LANE), jnp.bool_)
        for j in range(per):
            blk = p * per + j
            picked = picked | ((lane // SLC_BLOCK == j) & (sel_rows[:, blk:blk + 1] > 0.5))
        d = (qpos - (p * page + lane)).astype(F32)
        s_tiles.append(jnp.where(picked & (d >= 0), jnp.dot(qx, kt, preferred_element_type=F32) - slope * d, NEG))
    blk_new = past // SLC_BLOCK
    s_new, v_new = new_tile(ns_ref, sel_rows[:, blk_new:blk_new + 1] > 0.5)
    o_slc = _attend_tiles(s_tiles + [s_new], v_tiles + [v_new])

    s_tiles, v_tiles = [], []
    wb = wc_ref.shape[2]
    w_off = past - wb
    for j in range(wb // LANE):
        kt = wc_ref[0, 0:kw, j * LANE:(j + 1) * LANE].astype(BF16)
        v_tiles.append((wc_ref[0, kw:2 * kw, j * LANE:(j + 1) * LANE].astype(BF16), True))
        d = qpos - (w_off + j * LANE + lane)
        valid = (d >= 0) & (d < WINDOW)
        s_tiles.append(jnp.where(valid, jnp.dot(qx, kt, preferred_element_type=F32) - slope * d.astype(F32), NEG))
    s_new, v_new = new_tile(nw_ref, True)
    o_win = _attend_tiles(s_tiles + [s_new], v_tiles + [v_new])

    sg = _sigmoid(g_ref[0])
    for h in range(N_HEADS):
        r0, l0 = h * T_PAD, (h // HG) * HEAD_DIM
        o = (sg[:, h:h + 1] * o_cmp[r0:r0 + T_PAD, l0:l0 + HEAD_DIM]
             + sg[:, N_HEADS + h:N_HEADS + h + 1] * o_slc[r0:r0 + T_PAD, l0:l0 + HEAD_DIM]
             + sg[:, 2 * N_HEADS + h:2 * N_HEADS + h + 1] * o_win[r0:r0 + T_PAD, l0:l0 + HEAD_DIM])
        o_ref[0, :, h * HEAD_DIM:(h + 1) * HEAD_DIM] = o


def _nsa_dec(c3, kvc, ov, win_cache, win_first, page_table, pool, first, past, t_real):
    bs, npages = page_table.shape
    kw = 2 * KV_G * HEAD_DIM
    nchunk = kvc.shape[1]
    nc = nchunk - CMP_LEN // CMP_STRIDE + 1
    ns = -(-(past + t_real) // SLC_BLOCK)
    wb = win_cache.shape[2]
    in_specs = [pl.BlockSpec((1, T_PAD, MIX), lambda b, pt: (b, 0, _COL['nsa_q'] // MIX)),
                pl.BlockSpec((1, T_PAD, kw), lambda b, pt: (b, 0, _COL['kv_slc'] // kw)),
                pl.BlockSpec((1, T_PAD, kw), lambda b, pt: (b, 0, _COL['kv_win'] // kw)),
                pl.BlockSpec((1, T_PAD, LANE), lambda b, pt: (b, 0, _COL['nsa_gate'] // LANE)),
                pl.BlockSpec((1, nchunk, kw), lambda b, pt: (b, 0, 0)),
                pl.BlockSpec((LANE, LANE), lambda b, pt: (0, 0)),
                pl.BlockSpec((1, kw, wb), lambda b, pt: (win_first + b, 0, 0))]
    in_specs += _page_specs(pool, npages, first)
    return pl.pallas_call(
        functools.partial(_nsa_dec_kernel, npages=npages, past=past, nc=nc, ns=ns),
        grid_spec=pltpu.PrefetchScalarGridSpec(
            num_scalar_prefetch=1, grid=(bs,), in_specs=in_specs,
            out_specs=pl.BlockSpec((1, T_PAD, MIX), lambda b, pt: (b, 0, 0))),
        out_shape=jax.ShapeDtypeStruct((bs, T_PAD, MIX), F32),
        compiler_params=_cparams(("parallel",)), name="nsa_decode")(
            page_table, c3, c3, c3, c3, kvc, ov, win_cache, *([pool] * npages))


def _s5_disc_kernel(ar_ref, ai_ref, ls_ref, btr_ref, bti_ref, abr_ref, abi_ref, bbr_ref, bbi_ref):
    ar, ai = ar_ref[...], ai_ref[...]
    step = jnp.exp(ls_ref[...])
    mag = jnp.exp(ar * step)
    abr = mag * jnp.cos(ai * step)
    abi = mag * jnp.sin(ai * step)
    den = ar * ar + ai * ai
    zr = (ar * (abr - 1.0) + ai * abi) / den
    zi = (ar * abi - ai * (abr - 1.0)) / den
    abr_ref[...] = abr
    abi_ref[...] = abi
    btr, bti = btr_ref[...], bti_ref[...]
    bbr_ref[...] = zr * btr - zi * bti
    bbi_ref[...] = zr * bti + zi * btr


def _s5_disc(a_re, a_im, log_step, b_re, b_im):
    rep = lambda a: jnp.repeat(a, S5_CH, axis=0)
    n = S5_GROUPS * S5_CH
    args = (rep(a_re), rep(a_im), rep(jnp.broadcast_to(log_step[:, None], (S5_GROUPS, S5_STATE))),
            b_re.transpose(0, 2, 1).reshape(n, S5_STATE), b_im.transpose(0, 2, 1).reshape(n, S5_STATE))
    shp = jax.ShapeDtypeStruct((n, S5_STATE), F32)
    return pl.pallas_call(_s5_disc_kernel, out_shape=(shp, shp, shp, shp), name="s5_discretise")(*args)


_S5_CH_ROWS = 128


def _s5_scan_kernel(*refs, seg, has_h0):
    if has_h0:
        (u_ref, bre_ref, bim_ref, ar_ref, ai_ref, cre_ref, cim_ref, d_ref, h0r_ref, h0i_ref,
         y_ref, hlr_ref, hli_ref, hr_s, hi_s) = refs
    else:
        (u_ref, bre_ref, bim_ref, ar_ref, ai_ref, cre_ref, cim_ref, d_ref,
         y_ref, hlr_ref, hli_ref, hr_s, hi_s) = refs
    rows = u_ref.shape[0]
    ch = min(_S5_CH_ROWS, rows)
    pad = ch
    nch = rows // ch
    ar, ai = ar_ref[0], ai_ref[0]
    hr_s[0:pad, :] = jnp.zeros((pad, hr_s.shape[1]), F32)
    hi_s[0:pad, :] = jnp.zeros((pad, hi_s.shape[1]), F32)

    def init_body(i, _):
        r0 = pl.multiple_of(i * ch, ch)
        u = u_ref[pl.ds(r0, ch), :]
        br = jnp.dot(u, bre_ref[0], precision=HI, preferred_element_type=F32)
        bi = jnp.dot(u, bim_ref[0], precision=HI, preferred_element_type=F32)
        if has_h0:
            h0r, h0i = h0r_ref[pl.ds(r0, ch), :], h0i_ref[pl.ds(r0, ch), :]
            br = br + (ar * h0r - ai * h0i)
            bi = bi + (ar * h0i + ai * h0r)
        hr_s[pl.ds(pad + r0, ch), :] = br
        hi_s[pl.ds(pad + r0, ch), :] = bi
        return 0

    lax.fori_loop(0, nch, init_body, 0)

    rl = lax.broadcasted_iota(jnp.int32, (ch, 1), 0)
    pr, pi = ar, ai
    d = 1
    while d < seg:
        first = d // ch

        def pass_body(i, _, d=d, pr=pr, pi=pi):
            r0 = pl.multiple_of((nch - 1 - i) * ch, ch)
            cr = hr_s[pl.ds(pad + r0, ch), :]
            ci = hi_s[pl.ds(pad + r0, ch), :]
            if d < SUBLANE:
                lo = pad - SUBLANE
                sr = pltpu.roll(hr_s[pl.ds(lo + r0, ch + SUBLANE), :], d, 0)[SUBLANE:]
                si = pltpu.roll(hi_s[pl.ds(lo + r0, ch + SUBLANE), :], d, 0)[SUBLANE:]
            else:
                sr = hr_s[pl.ds(pad + r0 - d, ch), :]
                si = hi_s[pl.ds(pad + r0 - d, ch), :]
            keep = ((r0 + rl) % seg) >= d
            sr = jnp.where(keep, sr, 0.0)
            si = jnp.where(keep, si, 0.0)
            hr_s[pl.ds(pad + r0, ch), :] = cr + (pr * sr - pi * si)
            hi_s[pl.ds(pad + r0, ch), :] = ci + (pr * si + pi * sr)
            return 0

        lax.fori_loop(0, nch - first, pass_body, 0)
        pr, pi = pr * pr - pi * pi, 2.0 * pr * pi
        d *= 2

    def out_body(i, _):
        r0 = pl.multiple_of(i * ch, ch)
        hr = hr_s[pl.ds(pad + r0, ch), :]
        hi = hi_s[pl.ds(pad + r0, ch), :]
        y = (jnp.dot(hr, cre_ref[0], precision=HI, preferred_element_type=F32)
             - jnp.dot(hi, cim_ref[0], precision=HI, preferred_element_type=F32))
        y_ref[pl.ds(r0, ch), :] = y + d_ref[0] * u_ref[pl.ds(r0, ch), :]
        return 0

    lax.fori_loop(0, nch, out_body, 0)
    nl = hlr_ref.shape[0]
    hlr_ref[...] = hr_s[pad + rows - nl:pad + rows, :]
    hli_ref[...] = hi_s[pad + rows - nl:pad + rows, :]


def _s5_scan(c, sw, rows, seg, h0=None):
    n = c.shape[0]
    nt = n // rows
    gl = LANE // S5_CH
    lt = S5_GROUPS // gl
    w = gl * S5_STATE
    nl = (rows // seg) * SUBLANE if seg == SUBLANE else SUBLANE
    ub = _COL['s5_u'] // LANE
    in_specs = [pl.BlockSpec((rows, LANE), lambda i, j: (i, ub + j)),
                pl.BlockSpec((1, LANE, w), lambda i, j: (j, 0, 0)),
                pl.BlockSpec((1, LANE, w), lambda i, j: (j, 0, 0)),
                pl.BlockSpec((1, 1, w), lambda i, j: (j, 0, 0)),
                pl.BlockSpec((1, 1, w), lambda i, j: (j, 0, 0)),
                pl.BlockSpec((1, w, LANE), lambda i, j: (j, 0, 0)),
                pl.BlockSpec((1, w, LANE), lambda i, j: (j, 0, 0)),
                pl.BlockSpec((1, 1, LANE), lambda i, j: (j, 0, 0))]
    args = [c, sw['bre'], sw['bim'], sw['ar'], sw['ai'], sw['cre'], sw['cim'], sw['d']]
    if h0 is not None:
        in_specs += [pl.BlockSpec((rows, w), lambda i, j: (i, j))] * 2
        args += list(h0)
    hshape = jax.ShapeDtypeStruct((nt * nl, S5_GROUPS * S5_STATE), F32)
    return pl.pallas_call(
        functools.partial(_s5_scan_kernel, seg=seg, has_h0=h0 is not None), grid=(nt, lt),
        in_specs=in_specs,
        out_specs=(pl.BlockSpec((rows, LANE), lambda i, j: (i, j)),
                   pl.BlockSpec((nl, w), lambda i, j: (i, j)), pl.BlockSpec((nl, w), lambda i, j: (i, j))),
        out_shape=(jax.ShapeDtypeStruct((n, MIX), F32), hshape, hshape),
        scratch_shapes=[pltpu.VMEM((min(_S5_CH_ROWS, rows) + rows, w), F32)] * 2,
        compiler_params=_cparams(("parallel", "arbitrary")), name="s5_scan")(*args)


def _s5_weights(lp):
    abr, abi, bbr, bbi = _s5_disc(lp['s5_a_re'], lp['s5_a_im'], lp['s5_log_step'], lp['s5_b_re'], lp['s5_b_im'])
    gl = LANE // S5_CH
    lt = S5_GROUPS // gl
    eye = jnp.eye(gl, dtype=F32)

    def bdiag(bb):
        return jnp.einsum('jgcn,gh->jgchn', bb.reshape(lt, gl, S5_CH, S5_STATE), eye).reshape(
            lt, gl * S5_CH, gl * S5_STATE)

    def cdiag(cc):
        return jnp.einsum('jgcn,gh->jgnhc', cc.reshape(lt, gl, S5_CH, S5_STATE), eye).reshape(
            lt, gl * S5_STATE, gl * S5_CH)

    return dict(bre=bdiag(bbr), bim=bdiag(bbi),
                ar=abr[::S5_CH].reshape(lt, 1, gl * S5_STATE), ai=abi[::S5_CH].reshape(lt, 1, gl * S5_STATE),
                cre=cdiag(lp['s5_c_re']), cim=cdiag(lp['s5_c_im']), d=lp['s5_d'].reshape(lt, 1, LANE))


def _glu_kernel(y_ref, w_ref, o_ref):
    y = y_ref[...]
    g = 0.5 * y * (1.0 + jnp.tanh(np.float32(np.sqrt(2.0 / np.pi)) * (y + np.float32(0.044715) * (y * y * y))))
    z = jnp.dot(g.astype(BF16), w_ref[...], preferred_element_type=F32)
    o_ref[...] = z[:, :MIX] * _sigmoid(z[:, MIX:])


def _glu(y, w, tm):
    n = y.shape[0]
    return pl.pallas_call(
        _glu_kernel, grid=(n // tm,),
        in_specs=[pl.BlockSpec((tm, MIX), lambda i: (i, 0)), pl.BlockSpec((MIX, 2 * MIX), lambda i: (0, 0))],
        out_specs=pl.BlockSpec((tm, MIX), lambda i: (i, 0)),
        out_shape=jax.ShapeDtypeStruct((n, MIX), F32),
        compiler_params=_cparams(("parallel",)), name="s5_glu")(y, w)


def _layer_norm(x, g, b):
    mu = jnp.mean(x, axis=-1, keepdims=True)
    xc = x - mu
    var = jnp.mean(xc * xc, axis=-1, keepdims=True)
    return xc * lax.rsqrt(var + LN_EPS) * g + b


def _merge_kernel(o0_ref, o1_ref, o2_ref, o3_ref, mg_ref, wb_ref, x_ref, wo_ref, g_ref, b_ref, out_ref, acc_ref,
                  *, alpha):
    i = pl.program_id(1)

    @pl.when(i == 0)
    def _():
        acc_ref[...] = jnp.zeros_like(acc_ref)

    for k, o_ref in enumerate((o0_ref, o1_ref, o2_ref, o3_ref)):
        @pl.when(i == k)
        def _(o_ref=o_ref):
            proj = jnp.dot(o_ref[...].astype(BF16), wb_ref[0], preferred_element_type=F32)
            acc_ref[...] += _sigmoid(mg_ref[...]) * proj

    @pl.when(i == N_BRANCH - 1)
    def _():
        mixed = jnp.dot(acc_ref[...].astype(BF16), wo_ref[...], preferred_element_type=F32)
        out_ref[...] = _layer_norm(alpha * x_ref[...] + mixed, g_ref[...], b_ref[...])


def _merge(outs, c, x, wb, wo, g, b, tm, alpha):
    n = x.shape[0]
    o_spec = pl.BlockSpec((tm, MIX), lambda r, i: (r, 0))
    return pl.pallas_call(
        functools.partial(_merge_kernel, alpha=alpha), grid=(n // tm, N_BRANCH),
        in_specs=[o_spec, o_spec, o_spec, o_spec,
                  pl.BlockSpec((tm, D_MODEL), lambda r, i: (r, i)),
                  pl.BlockSpec((1, MIX, D_MODEL), lambda r, i: (i, 0, 0)),
                  pl.BlockSpec((tm, D_MODEL), lambda r, i: (r, 0)),
                  pl.BlockSpec((D_MODEL, D_MODEL), lambda r, i: (0, 0)),
                  pl.BlockSpec((1, D_MODEL), lambda r, i: (0, 0)),
                  pl.BlockSpec((1, D_MODEL), lambda r, i: (0, 0))],
        out_specs=pl.BlockSpec((tm, D_MODEL), lambda r, i: (r, 0)),
        out_shape=jax.ShapeDtypeStruct((n, D_MODEL), F32),
        scratch_shapes=[pltpu.VMEM((tm, D_MODEL), F32)],
        compiler_params=_cparams(("parallel", "arbitrary")), name="merge_out_ln")(
            *outs, c, wb, x, wo, g, b)


def _ffn_kernel(*refs, halo, seg, alpha):
    (h_ref, wg_ref, wv_ref, cwg_ref, cwv_ref, cbg_ref, cbv_ref, wd_ref, lg_ref, lb_ref) = refs[:10]
    if halo:
        pg_ref, pv_ref, out_ref, hb_ref, acc_ref = refs[10:]
    else:
        p1g_ref, p1v_ref, p2g_ref, p2v_ref, out_ref, hb_ref, acc_ref = refs[10:]
    j = pl.program_id(1)

    @pl.when(j == 0)
    def _():
        hb_ref[...] = h_ref[...].astype(BF16)
        acc_ref[...] = jnp.zeros_like(acc_ref)

    hb = hb_ref[...]
    tm = hb.shape[0]
    rid = lax.broadcasted_iota(jnp.int32, (tm, 1), 0)

    def conv(u, cw_ref, cb_ref, prev):
        r1 = pltpu.roll(u, 1, 0)
        r2 = pltpu.roll(u, 2, 0)
        if halo:
            p = prev[0][0]
            p6, p7 = p[SUBLANE - 2:SUBLANE - 1], p[SUBLANE - 1:SUBLANE]
            u1 = jnp.where(rid == 0, p7, r1)
            u2 = jnp.where(rid == 0, p6, jnp.where(rid == 1, p7, r2))
        else:
            t = rid % seg
            u1 = jnp.where(t >= 1, r1, prev[0][...])
            u2 = jnp.where(t >= 2, r2, prev[1][...])
        cw = cw_ref[...]
        return cb_ref[...] + (cw[0:1] * u2 + cw[1:2] * u1 + cw[2:3] * u)

    ug = jnp.dot(hb, wg_ref[...], preferred_element_type=F32)
    uv = jnp.dot(hb, wv_ref[...], preferred_element_type=F32)
    if halo:
        gate = conv(ug, cwg_ref, cbg_ref, (pg_ref,))
        val = conv(uv, cwv_ref, cbv_ref, (pv_ref,))
    else:
        gate = conv(ug, cwg_ref, cbg_ref, (p1g_ref, p2g_ref))
        val = conv(uv, cwv_ref, cbv_ref, (p1v_ref, p2v_ref))
    act = gate * _sigmoid(gate) * val
    acc_ref[...] += jnp.dot(act.astype(BF16), wd_ref[...], preferred_element_type=F32)

    @pl.when(j == pl.num_programs(1) - 1)
    def _():
        out_ref[...] = _layer_norm(alpha * h_ref[...] + acc_ref[...], lg_ref[...], lb_ref[...])


def _ffn(h, lw, tm, tf, alpha, prev=None, p12=None, seg=None):
    n = h.shape[0]
    nf = D_FF // tf
    halo = prev is not None
    in_specs = [pl.BlockSpec((tm, D_MODEL), lambda r, j: (r, 0)),
                pl.BlockSpec((D_MODEL, tf), lambda r, j: (0, j)),
                pl.BlockSpec((D_MODEL, tf), lambda r, j: (0, nf + j)),
                pl.BlockSpec((CONV_W, tf), lambda r, j: (0, j)),
                pl.BlockSpec((CONV_W, tf), lambda r, j: (0, nf + j)),
                pl.BlockSpec((1, tf), lambda r, j: (0, j)),
                pl.BlockSpec((1, tf), lambda r, j: (0, nf + j)),
                pl.BlockSpec((tf, D_MODEL), lambda r, j: (j, 0)),
                pl.BlockSpec((1, D_MODEL), lambda r, j: (0, 0)),
                pl.BlockSpec((1, D_MODEL), lambda r, j: (0, 0))]
    args = [h, lw['w_up'], lw['w_up'], lw['conv_w'], lw['conv_w'], lw['conv_b'], lw['conv_b'], lw['w_down'],
            lw['ln2_g'], lw['ln2_b']]
    if halo:
        in_specs += [pl.BlockSpec((1, SUBLANE, tf), lambda r, j: (r, 0, j)),
                     pl.BlockSpec((1, SUBLANE, tf), lambda r, j: (r, 0, nf + j))]
        args += [prev, prev]
    else:
        in_specs += [pl.BlockSpec((tm, tf), lambda r, j: (r, j)), pl.BlockSpec((tm, tf), lambda r, j: (r, nf + j)),
                     pl.BlockSpec((tm, tf), lambda r, j: (r, j)), pl.BlockSpec((tm, tf), lambda r, j: (r, nf + j))]
        args += [p12[0], p12[0], p12[1], p12[1]]
    return pl.pallas_call(
        functools.partial(_ffn_kernel, halo=halo, seg=seg, alpha=alpha), grid=(n // tm, nf),
        in_specs=in_specs,
        out_specs=pl.BlockSpec((tm, D_MODEL), lambda r, j: (r, 0)),
        out_shape=jax.ShapeDtypeStruct((n, D_MODEL), F32),
        scratch_shapes=[pltpu.VMEM((tm, D_MODEL), BF16), pltpu.VMEM((tm, D_MODEL), F32)],
        compiler_params=_cparams(("parallel", "arbitrary")), name="conv_ffn_ln")(*args)


def _prep_layer(l, p):
    lp = {k: v[l] for k, v in p.items()}
    lw = dict(
        w_in=_pack_w_in(lp['w_in']),
        fox_b=jnp.zeros((1, LANE), F32).at[0, :N_HEADS].set(lp['fox_b_f']),
        cmp=_cmp_weights(lp['nsa_cmp_pos'], lp['nsa_cmp_wk'], lp['nsa_cmp_wv']),
        s5=_s5_weights(lp),
        w_glu=lp['s5_w_glu'].astype(BF16),
        w_branch=lp['w_branch'].astype(BF16),
        w_out=lp['w_out'].astype(BF16),
        ln1_g=lp['ln1_g'].reshape(1, -1), ln1_b=lp['ln1_b'].reshape(1, -1),
        w_up=lp['ffn_w_up'].astype(BF16), conv_w=lp['ffn_conv_w'], conv_b=lp['ffn_conv_b'].reshape(1, -1),
        w_down=lp['ffn_w_down'].astype(BF16),
        ln2_g=lp['ln2_g'].reshape(1, -1), ln2_b=lp['ln2_b'].reshape(1, -1))
    return lw


def _prompt_layer(x, b, t, lw, alpha):
    n = b * t
    kw = 2 * KV_G * HEAD_DIM
    tr = min(1024, n)
    c = _matmul(x, lw['w_in'], tr, 512)
    logf = _logf(c, lw['fox_b'], tr)
    cumr = _cumsum_prompt(logf.reshape(b, t, N_HEADS).transpose(0, 2, 1))
    o_fox = _fox_prompt(c, cumr[:, :, None, :], b, t, 256)
    o_moba = _moba_prompt(c, b, t)
    kvc = _cmp_prompt(c, b, t, lw['cmp'])
    nchunk = t // CMP_STRIDE
    ov = _overlap_matrix(nchunk - CMP_LEN // CMP_STRIDE + 1, -(-t // SLC_BLOCK))
    o_nsa = _nsa_prompt(c, kvc, ov, b, t, 128)
    y_s5, hlr, hli = _s5_scan(c, lw['s5'], t, t)
    o_s5 = _glu(y_s5, lw['w_glu'], tr)
    h = _merge((o_nsa, o_s5, o_moba, o_fox), c, x, lw['w_branch'], lw['w_out'], lw['ln1_g'], lw['ln1_b'], 256, alpha)

    tm = 512
    nt = n // tm
    edge = h.reshape(nt, tm, D_MODEL)[:, tm - (CONV_W - 1):].reshape(nt * (CONV_W - 1), D_MODEL)
    edge = _pad_rows(edge, -(-edge.shape[0] // SUBLANE) * SUBLANE)
    u_edge = _matmul(edge, lw['w_up'], edge.shape[0], 512)[:nt * (CONV_W - 1)].reshape(nt, CONV_W - 1, 2 * D_FF)
    per_seq = t // tm
    conv_state = u_edge[per_seq - 1::per_seq]
    starts_seq = (jnp.arange(nt) % per_seq == 0)[:, None, None]
    prev = jnp.where(starts_seq, 0.0, jnp.roll(u_edge, 1, axis=0))
    prev = jnp.pad(prev, ((0, 0), (SUBLANE - (CONV_W - 1), 0), (0, 0)))
    y = _ffn(h, lw, tm, 512, alpha, prev=prev)

    seg = lambda name, w: c[:, _COL[name]:_COL[name] + w]
    win_rows = min(WINDOW, t)
    states = (
        seg('kv_cmp', kw).reshape(b, t, 2, KV_G, HEAD_DIM),
        seg('kv_slc', kw).reshape(b, t, 2, KV_G, HEAD_DIM),
        c[:, _COL['moba'] + MIX:_COL['moba'] + 3 * MIX].reshape(b, t, 2, N_HEADS, HEAD_DIM),
        c[:, _COL['fox'] + MIX:_COL['fox'] + 3 * MIX].reshape(b, t, 2, N_HEADS, HEAD_DIM),
        logf.reshape(b, t, N_HEADS),
        seg('kv_win', kw).reshape(b, t, 2, KV_G, HEAD_DIM)[:, t - win_rows:],
        hlr.reshape(b, SUBLANE, S5_GROUPS, S5_STATE)[:, SUBLANE - 1],
        hli.reshape(b, SUBLANE, S5_GROUPS, S5_STATE)[:, SUBLANE - 1],
        conv_state)
    return y, states


def _sample_layer(x, bs, t_real, lw, alpha, past, page_table, past_len):
    n = bs * T_PAD
    kw = 2 * KV_G * HEAD_DIM
    c = _matmul(x, lw['w_in'], n, 512)
    c3 = c.reshape(bs, T_PAD, WP)
    logf = _logf(c, lw['fox_b'], n)
    logf3 = logf.reshape(bs, T_PAD, N_HEADS)
    tmask = (jnp.arange(T_PAD) < t_real)[None, :, None]
    new_t = jnp.pad(jnp.where(tmask, logf3, 0.0).transpose(0, 2, 1), ((0, 0), (0, 0), (0, LANE - T_PAD)))
    first = past['first_page']
    cum = _cumsum_paged(page_table, past['fox_logf'], first, new_t)
    cq = cum[:, :, past_len:past_len + T_PAD].transpose(0, 2, 1)
    o_fox = _paged_mha_dec(
        _fox_dec_kernel, "fox_decode", c3, _COL['fox'], page_table, past['fox'], first, [cq, cum],
        [pl.BlockSpec((1, T_PAD, N_HEADS), lambda b, pt: (b, 0, 0)),
         pl.BlockSpec((1, N_HEADS, cum.shape[-1]), lambda b, pt: (b, 0, 0))], past_len)
    o_moba = _paged_mha_dec(_moba_dec_kernel, "moba_decode", c3, _COL['moba'], page_table, past['moba'], first,
                            [], [], past_len)
    kvc = _cmp_paged(page_table, past['nsa_cmp'], first, lw['cmp'])
    nchunk = kvc.shape[1]
    ov = _overlap_matrix(nchunk - CMP_LEN // CMP_STRIDE + 1, -(-(past_len + t_real) // SLC_BLOCK))
    o_nsa = _nsa_dec(c3, kvc, ov, past['nsa_win'], past['first_seq'], page_table, past['nsa_slc'], first,
                     past_len, t_real)
    h0 = [jnp.pad(s.reshape(bs, 1, -1), ((0, 0), (0, T_PAD - 1), (0, 0))).reshape(n, -1) for s in past['s5']]
    y_s5, hlr, hli = _s5_scan(c, lw['s5'], n, T_PAD, h0=h0)
    o_s5 = _glu(y_s5, lw['w_glu'], n)
    h = _merge((o_nsa.reshape(n, MIX), o_s5, o_moba.reshape(n, MIX), o_fox.reshape(n, MIX)), c, x,
               lw['w_branch'], lw['w_out'], lw['ln1_g'], lw['ln1_b'], min(256, n), alpha)

    buf = past['ffn_conv']
    z = jnp.zeros((bs, T_PAD - 1, 2 * D_FF), F32)
    p1 = jnp.concatenate([buf[:, 1:2], z], axis=1).reshape(n, 2 * D_FF)
    p2 = jnp.concatenate([buf, z[:, 1:]], axis=1).reshape(n, 2 * D_FF)
    y = _ffn(h, lw, min(512, n), 512, alpha, p12=(p1, p2), seg=T_PAD)
    last2 = h.reshape(bs, T_PAD, D_MODEL)[:, t_real - (CONV_W - 1):t_real].reshape(bs * (CONV_W - 1), D_MODEL)
    conv_state = _matmul(last2, lw['w_up'], last2.shape[0], 512).reshape(bs, CONV_W - 1, 2 * D_FF)

    tr = lambda a: a[:, :t_real]
    kv_win_new = tr(c3[:, :, _COL['kv_win']:_COL['kv_win'] + kw]).transpose(0, 2, 1)
    old_win = past['nsa_win'][past['first_seq']:past['first_seq'] + bs]
    full_win = jnp.concatenate([old_win, kv_win_new], axis=2)
    lw_ = full_win.shape[2]
    new_win = full_win[:, :, lw_ - min(WINDOW, lw_):].transpose(0, 2, 1)
    states = (
        tr(c3[:, :, _COL['kv_cmp']:_COL['kv_cmp'] + kw]).reshape(bs, t_real, 2, KV_G, HEAD_DIM),
        tr(c3[:, :, _COL['kv_slc']:_COL['kv_slc'] + kw]).reshape(bs, t_real, 2, KV_G, HEAD_DIM),
        tr(c3[:, :, _COL['moba'] + MIX:_COL['moba'] + 3 * MIX]).reshape(bs, t_real, 2, N_HEADS, HEAD_DIM),
        tr(c3[:, :, _COL['fox'] + MIX:_COL['fox'] + 3 * MIX]).reshape(bs, t_real, 2, N_HEADS, HEAD_DIM),
        tr(logf3),
        new_win.reshape(bs, new_win.shape[1], 2, KV_G, HEAD_DIM),
        hlr.reshape(bs, T_PAD, S5_GROUPS, S5_STATE)[:, t_real - 1],
        hli.reshape(bs, T_PAD, S5_GROUPS, S5_STATE)[:, t_real - 1],
        conv_state)
    return y, states


def kernel(x_prompt, x_sample, cache_nsa_cmp_kv, cache_nsa_slc_kv, cache_moba_kv, cache_fox_kv, cache_fox_logf,
           page_table, cache_nsa_win_kv, state_s5_re, state_s5_im, state_ffn_conv, w_in, fox_b_f, nsa_cmp_pos,
           nsa_cmp_wk, nsa_cmp_wv, s5_a_re, s5_a_im, s5_b_re, s5_b_im, s5_c_re, s5_c_im, s5_d, s5_log_step,
           s5_w_glu, w_branch, w_out, ln1_g, ln1_b, ffn_w_up, ffn_conv_w, ffn_conv_b, ffn_w_down, ln2_g, ln2_b):
    depth = w_in.shape[0]
    b, t, d = x_prompt.shape
    bs, ts, _ = x_sample.shape
    n_phys, page = cache_nsa_cmp_kv.shape[1:3]
    past_len = page_table.shape[1] * page
    assert d == D_MODEL and w_in.shape[2] == IN_WIDTH and ffn_w_down.shape[1] == D_FF
    assert ts <= T_PAD and past_len % MOBA_BLOCK == 0 and page == LANE and (past_len + ts) // CMP_STRIDE * CMP_STRIDE <= past_len
    alpha = float((2 * depth) ** 0.25)
    params = dict(w_in=w_in, fox_b_f=fox_b_f, nsa_cmp_pos=nsa_cmp_pos, nsa_cmp_wk=nsa_cmp_wk, nsa_cmp_wv=nsa_cmp_wv,
                  s5_a_re=s5_a_re, s5_a_im=s5_a_im, s5_b_re=s5_b_re, s5_b_im=s5_b_im, s5_c_re=s5_c_re,
                  s5_c_im=s5_c_im, s5_d=s5_d, s5_log_step=s5_log_step, s5_w_glu=s5_w_glu, w_branch=w_branch,
                  w_out=w_out, ln1_g=ln1_g, ln1_b=ln1_b, ffn_w_up=ffn_w_up, ffn_conv_w=ffn_conv_w,
                  ffn_conv_b=ffn_conv_b, ffn_w_down=ffn_w_down, ln2_g=ln2_g, ln2_b=ln2_b)
    kw = 2 * KV_G * HEAD_DIM
    yp = x_prompt.reshape(b * t, d)
    ys = jnp.pad(x_sample, ((0, 0), (0, T_PAD - ts), (0, 0))).reshape(bs * T_PAD, d)
    st_p, st_s = [], []

    def feature_major(cache):
        dd, nn, rr = cache.shape[:3]
        return cache.transpose(0, 1, 3, 4, 5, 2).reshape(dd * nn, -1, rr)

    cmp_fm, slc_fm, moba_fm, fox_fm, win_fm = (
        feature_major(a) for a in (cache_nsa_cmp_kv, cache_nsa_slc_kv, cache_moba_kv, cache_fox_kv, cache_nsa_win_kv))
    logf_hm = cache_fox_logf.transpose(0, 1, 3, 2).reshape(depth * n_phys, N_HEADS, page)
    for l in range(depth):
        lw = _prep_layer(l, params)
        past = dict(
            nsa_cmp=cmp_fm, nsa_slc=slc_fm, moba=moba_fm, fox=fox_fm, fox_logf=logf_hm, nsa_win=win_fm,
            first_page=l * n_phys, first_seq=l * bs,
            s5=(state_s5_re[l], state_s5_im[l]),
            ffn_conv=state_ffn_conv[l])
        yp, sp = _prompt_layer(yp, b, t, lw, alpha)
        ys, ss = _sample_layer(ys, bs, ts, lw, alpha, past, page_table, past_len)
        st_p.append(sp)
        st_s.append(ss)
    sp = [jnp.stack(z) for z in zip(*st_p)]
    ss = [jnp.stack(z) for z in zip(*st_s)]
    out = [yp.reshape(b, t, d), ys.reshape(bs, T_PAD, d)[:, :ts]]
    for a, c in zip(sp, ss):
        out += [a, c]
    return tuple(out)
```

```python
import functools

import numpy as np
import jax
import jax.numpy as jnp
from jax import lax
from jax.experimental import pallas as pl
from jax.experimental.pallas import tpu as pltpu

F32 = jnp.float32
BF16 = jnp.bfloat16
HI = lax.Precision.HIGHEST

LANE = 128
SUBLANE = 8
VMEM_LIMIT = 56 * 1024 * 1024

D_MODEL = 2048
HEAD_DIM = 64
N_BRANCH = 4
MIX = D_MODEL // N_BRANCH
N_HEADS = MIX // HEAD_DIM
KV_G = 2
HG = N_HEADS // KV_G
CMP_LEN = 32
CMP_STRIDE = 16
SLC_BLOCK = 64
SLC_TOPN = 16
WINDOW = 512
MOBA_BLOCK = 256
MOBA_TOPK = 3
S5_CH = 16
S5_GROUPS = MIX // S5_CH
S5_STATE = 64
D_FF = 5632
CONV_W = 3
LN_EPS = 1e-5
SCALE = HEAD_DIM ** -0.5
NEG = -1e30
FORCE = 1e4
T_PAD = 8
SLOPES = tuple(float(v) for v in np.asarray(2.0 ** (-8.0 * np.arange(1, N_HEADS + 1) / N_HEADS), np.float32))

_SPLITS = (('nsa_q', MIX), ('kv_cmp', 2 * KV_G * HEAD_DIM), ('kv_slc', 2 * KV_G * HEAD_DIM),
           ('kv_win', 2 * KV_G * HEAD_DIM), ('nsa_gate', 3 * N_HEADS), ('s5_u', MIX),
           ('moba', 3 * MIX), ('fox', 3 * MIX), ('fox_f', N_HEADS), ('merge', N_BRANCH * D_MODEL))
_SRC = {}
_o = 0
for _n, _w in _SPLITS:
    _SRC[_n] = (_o, _w)
    _o += _w
IN_WIDTH = _o
_COL = dict(merge=0, nsa_q=8192, kv_cmp=8704, kv_slc=8960, kv_win=9216, nsa_gate=9472, s5_u=9600,
            moba=10240, fox=11776, fox_f=13312)
WP = 13824


def _cparams(sem):
    return pltpu.CompilerParams(dimension_semantics=sem, vmem_limit_bytes=VMEM_LIMIT)


def _dot_nt(a, b, precision=None):
    return lax.dot_general(a, b, (((1,), (1,)), ((), ())), precision=precision, preferred_element_type=F32)


def _split_bf16(x):
    hi = x.astype(BF16)
    return hi, (x - hi.astype(F32)).astype(BF16)


def _dot3(a, b_split):
    a_hi, a_lo = _split_bf16(a)
    b_hi, b_lo = b_split
    dot = functools.partial(jnp.dot, preferred_element_type=F32)
    return dot(a_hi, b_hi) + (dot(a_hi, b_lo) + dot(a_lo, b_hi))


def _sigmoid(x):
    return 1.0 / (1.0 + jnp.exp(-x))


def _pack_w_in(w):
    d = w.shape[0]
    order = ('merge', 'nsa_q', 'kv_cmp', 'kv_slc', 'kv_win', 'nsa_gate', 's5_u', 'moba', 'fox', 'fox_f')
    parts, pos = [], 0
    for name in order:
        if _COL[name] > pos:
            parts.append(jnp.zeros((d, _COL[name] - pos), w.dtype))
        s, wd = _SRC[name]
        parts.append(w[:, s:s + wd])
        pos = _COL[name] + wd
    parts.append(jnp.zeros((d, WP - pos), w.dtype))
    return jnp.concatenate(parts, axis=1).astype(BF16)


def _mm_kernel(x_ref, w_ref, o_ref, xb_ref):
    @pl.when(pl.program_id(1) == 0)
    def _():
        xb_ref[...] = x_ref[...].astype(BF16)

    o_ref[...] = jnp.dot(xb_ref[...], w_ref[...], preferred_element_type=F32)


def _matmul(x, w, tm, tn):
    m, k = x.shape
    n = w.shape[1]
    return pl.pallas_call(
        _mm_kernel, grid=(m // tm, n // tn),
        in_specs=[pl.BlockSpec((tm, k), lambda i, j: (i, 0)), pl.BlockSpec((k, tn), lambda i, j: (0, j))],
        out_specs=pl.BlockSpec((tm, tn), lambda i, j: (i, j)),
        out_shape=jax.ShapeDtypeStruct((m, n), F32),
        scratch_shapes=[pltpu.VMEM((tm, k), BF16)],
        compiler_params=_cparams(("parallel", "arbitrary")), name="mm")(x, w)


def _logf_kernel(c_ref, b_ref, o_ref):
    x = c_ref[...] + b_ref[...]
    y = jnp.minimum(x, 0.0) - jnp.log1p(jnp.exp(-jnp.abs(x)))
    o_ref[...] = y[:, :N_HEADS]


def _logf(c, b_pad, tm):
    n = c.shape[0]
    return pl.pallas_call(
        _logf_kernel, grid=(n // tm,),
        in_specs=[pl.BlockSpec((tm, LANE), lambda i: (i, _COL['fox_f'] // LANE)),
                  pl.BlockSpec((1, LANE), lambda i: (0, 0))],
        out_specs=pl.BlockSpec((tm, N_HEADS), lambda i: (i, 0)),
        out_shape=jax.ShapeDtypeStruct((n, N_HEADS), F32),
        compiler_params=_cparams(("parallel",)), name="logf")(c, b_pad)


def _cumsum_kernel(*refs, n_in):
    in_refs, o_ref = refs[-n_in - 1:-1], refs[-1]
    r = lax.broadcasted_iota(jnp.int32, (LANE, LANE), 0)
    c = lax.broadcasted_iota(jnp.int32, (LANE, LANE), 1)
    tri = jnp.where(r <= c, 1.0, 0.0).astype(F32)
    carry = jnp.zeros((N_HEADS, 1), F32)
    off = 0
    for ref in in_refs:
        for j in range(ref.shape[-1] // LANE):
            x = ref[0, :, j * LANE:(j + 1) * LANE]
            cs = jnp.dot(x, tri, precision=HI, preferred_element_type=F32) + carry
            o_ref[0, :, off:off + LANE] = cs
            carry = cs[:, LANE - 1:LANE]
            off += LANE


def _cumsum_prompt(logf_t):
    b, h, t = logf_t.shape
    return pl.pallas_call(
        functools.partial(_cumsum_kernel, n_in=1), grid=(b,),
        in_specs=[pl.BlockSpec((1, h, t), lambda i: (i, 0, 0))],
        out_specs=pl.BlockSpec((1, h, t), lambda i: (i, 0, 0)),
        out_shape=jax.ShapeDtypeStruct((b, h, t), F32),
        compiler_params=_cparams(("parallel",)), name="fox_cumsum_prompt")(logf_t)


M_FLOOR = -1e29


def _softmax_step(m, l, s):
    m_new = jnp.maximum(m, jnp.max(s, axis=1, keepdims=True))
    p = jnp.exp(s - m_new)
    alpha = jnp.exp(m - m_new)
    return m_new, alpha * l + jnp.sum(p, axis=1, keepdims=True), alpha, p.astype(BF16)


def _pipelined_attention(chains, lo, hi, score_fn, value_fn, rows, tk, dv, last_fix=None):
    def flush(c, kj, s, p_prev, a_prev, acc):
        prev = jnp.maximum(kj - 1, lo)
        return a_prev * acc + jnp.dot(p_prev, value_fn(c, prev), preferred_element_type=F32)

    def body(kj, carry):
        out = []
        for c in range(chains):
            s, p_prev, a_prev, m, l, acc = carry[c]
            acc = flush(c, kj, s, p_prev, a_prev, acc)
            m, l, alpha, p = _softmax_step(m, l, s)
            out.append((score_fn(c, kj + 1), p, alpha, m, l, acc))
        return tuple(out)

    init = tuple((score_fn(c, lo), jnp.zeros((rows, tk), BF16), jnp.ones((rows, 1), F32),
                  jnp.full((rows, 1), M_FLOOR, F32), jnp.zeros((rows, 1), F32), jnp.zeros((rows, dv), F32))
                 for c in range(chains))
    carry = lax.fori_loop(lo, hi, body, init)
    outs = []
    for c in range(chains):
        s, p_prev, a_prev, m, l, acc = carry[c]
        acc = flush(c, hi, s, p_prev, a_prev, acc)
        if last_fix is not None:
            s = last_fix(c, s)
        m, l, alpha, p = _softmax_step(m, l, s)
        acc = alpha * acc + jnp.dot(p, value_fn(c, hi), preferred_element_type=F32)
        outs.append(acc / jnp.maximum(l, 1e-30))
    return outs


def _attend_tiles(s_tiles, v_tiles):
    m = functools.reduce(jnp.maximum, [jnp.max(s, axis=1, keepdims=True) for s in s_tiles])
    l, acc = 0.0, 0.0
    for s, (v, feature_major) in zip(s_tiles, v_tiles):
        e = jnp.where(s > 0.5 * NEG, jnp.exp(s - m), 0.0)
        l = l + jnp.sum(e, axis=1, keepdims=True)
        eb = e.astype(BF16)
        acc = acc + (_dot_nt(eb, v) if feature_major else jnp.dot(eb, v, preferred_element_type=F32))
    return acc / jnp.maximum(l, 1e-30)


def _topn_mask(v, ncols, topn):
    jl = lax.broadcasted_iota(jnp.int32, v.shape, 1)
    rank = jnp.zeros(v.shape, F32)
    for j2 in range(ncols):
        col = v[:, j2:j2 + 1]
        beats = (col > v) | ((col == v) & (jl > j2))
        rank = rank + jnp.where(beats, 1.0, 0.0)
    return rank < topn


def _pad_rows(a, rows):
    return jnp.concatenate([a, jnp.zeros((rows - a.shape[0], a.shape[1]), a.dtype)], axis=0)


def _fox_prompt_kernel(q_ref, k_ref, v_ref, cr_ref, o_ref, *, tq):
    qi = pl.program_id(2)
    q0 = pl.multiple_of(qi * tq, tq)
    row = lax.broadcasted_iota(jnp.int32, (tq, tq), 0)
    col = lax.broadcasted_iota(jnp.int32, (tq, tq), 1)
    qs = [(q_ref[:, h2 * HEAD_DIM:(h2 + 1) * HEAD_DIM] * SCALE).astype(BF16) for h2 in range(2)]
    c0 = [cr_ref[0, h2, :, pl.ds(q0, LANE)][:, 0:1] for h2 in range(2)]

    def score(h2, kj):
        ks = pl.multiple_of(kj * tq, tq)
        k = k_ref[pl.ds(ks, tq), h2 * HEAD_DIM:(h2 + 1) * HEAD_DIM].astype(BF16)
        return _dot_nt(qs[h2], k) + (c0[h2] - cr_ref[0, h2, :, pl.ds(ks, tq)])

    def value(h2, kj):
        ks = pl.multiple_of(kj * tq, tq)
        return v_ref[pl.ds(ks, tq), h2 * HEAD_DIM:(h2 + 1) * HEAD_DIM].astype(BF16)

    outs = _pipelined_attention(2, 0, qi, score, value, tq, tq, HEAD_DIM,
                                last_fix=lambda h2, s: jnp.where(col <= row, s, NEG))
    o_ref[...] = jnp.concatenate(outs, axis=1)


def _fox_prompt(c, cumr, b, t, tq):
    nq = t // tq
    base = _COL['fox'] // LANE
    hp_n = N_HEADS // 2
    return pl.pallas_call(
        functools.partial(_fox_prompt_kernel, tq=tq), grid=(b, hp_n, nq),
        in_specs=[pl.BlockSpec((tq, LANE), lambda i, h, q: (i * nq + q, base + h)),
                  pl.BlockSpec((t, LANE), lambda i, h, q: (i, base + hp_n + h)),
                  pl.BlockSpec((t, LANE), lambda i, h, q: (i, base + 2 * hp_n + h)),
                  pl.BlockSpec((1, 2, 1, t), lambda i, h, q: (i, h, 0, 0))],
        out_specs=pl.BlockSpec((tq, LANE), lambda i, h, q: (i * nq + q, h)),
        out_shape=jax.ShapeDtypeStruct((b * t, MIX), F32),
        compiler_params=_cparams(("parallel", "arbitrary", "arbitrary")), name="fox_prompt")(
            c, c, c, cumr)


def _head_slope(hp, h2):
    s = jnp.float32(SLOPES[h2])
    for k in range(1, N_HEADS // 2):
        s = jnp.where(hp == k, jnp.float32(SLOPES[2 * k + h2]), s)
    return s


def _moba_prompt_kernel(q_ref, k_ref, v_ref, o_ref, kmean_ref, *, tq, nb):
    hp = pl.program_id(1)
    qi = pl.program_id(2)

    @pl.when(qi == 0)
    def _():
        kmean_ref[...] = jnp.zeros_like(kmean_ref)
        for n in range(nb):
            kmean_ref[n:n + 1, :] = jnp.mean(k_ref[n * tq:(n + 1) * tq, :], axis=0, keepdims=True)

    q0 = pl.multiple_of(qi * tq, tq)
    row = lax.broadcasted_iota(jnp.int32, (tq, tq), 0)
    col = lax.broadcasted_iota(jnp.int32, (tq, tq), 1)
    colpos = lax.broadcasted_iota(jnp.int32, (1, tq), 1)
    jl = lax.broadcasted_iota(jnp.int32, (tq, LANE), 1)
    qs, slopes, blockbias = [], [], []
    for h2 in range(2):
        lo = h2 * HEAD_DIM
        qf = q_ref[:, lo:lo + HEAD_DIM]
        gate = _dot_nt(qf, kmean_ref[:, lo:lo + HEAD_DIM], precision=HI)
        gate = jnp.where(jl < qi, gate, NEG)
        picked = (_topn_mask(gate, nb, MOBA_TOPK) & (jl < qi)) | (jl == qi)
        qs.append((qf * SCALE).astype(BF16))
        slopes.append(_head_slope(hp, h2))
        blockbias.append(jnp.where(picked, 0.0, NEG))

    def score(h2, kj):
        ks = pl.multiple_of(kj * tq, tq)
        k = k_ref[pl.ds(ks, tq), h2 * HEAD_DIM:(h2 + 1) * HEAD_DIM].astype(BF16)
        rowbias = jnp.min(jnp.where(jl == kj, blockbias[h2], 0.0), axis=1, keepdims=True)
        return _dot_nt(qs[h2], k) + slopes[h2] * (ks - q0 + colpos).astype(F32) + rowbias

    def value(h2, kj):
        ks = pl.multiple_of(kj * tq, tq)
        return v_ref[pl.ds(ks, tq), h2 * HEAD_DIM:(h2 + 1) * HEAD_DIM].astype(BF16)

    outs = _pipelined_attention(2, 0, qi, score, value, tq, tq, HEAD_DIM,
                                last_fix=lambda h2, s: jnp.where(col <= row, s, NEG))
    o_ref[...] = jnp.concatenate(outs, axis=1)


def _moba_prompt(c, b, t):
    tq = MOBA_BLOCK
    nq = t // tq
    base = _COL['moba'] // LANE
    hp_n = N_HEADS // 2
    return pl.pallas_call(
        functools.partial(_moba_prompt_kernel, tq=tq, nb=nq), grid=(b, hp_n, nq),
        in_specs=[pl.BlockSpec((tq, LANE), lambda i, h, q: (i * nq + q, base + h)),
                  pl.BlockSpec((t, LANE), lambda i, h, q: (i, base + hp_n + h)),
                  pl.BlockSpec((t, LANE), lambda i, h, q: (i, base + 2 * hp_n + h))],
        out_specs=pl.BlockSpec((tq, LANE), lambda i, h, q: (i * nq + q, h)),
        out_shape=jax.ShapeDtypeStruct((b * t, MIX), F32),
        scratch_shapes=[pltpu.VMEM((LANE, LANE), F32)],
        compiler_params=_cparams(("parallel", "arbitrary", "arbitrary")), name="moba_prompt")(c, c, c)


def _cmp_kernel(*refs, n_in, feature_major):
    if feature_major:
        xs_ref, refs = refs[-1], refs[:-1]
        x_refs = refs[-5 - n_in:-5]
        page = x_refs[0].shape[2]
        halves = x_refs[0].shape[1] // LANE
        for p, r in enumerate(x_refs):
            for j in range(halves):
                xs_ref[j, p * page:(p + 1) * page, :] = r[0, j * LANE:(j + 1) * LANE, :].T
        nchunk = n_in * page // CMP_STRIDE
        x = jnp.concatenate([xs_ref[j, pl.ds(l, nchunk, stride=CMP_STRIDE), :]
                             for l in range(CMP_STRIDE) for j in range(halves)], axis=1)
    else:
        x = refs[-6][0]
    pos_ref, wcat_ref, w0_ref, w1_ref, o_ref = refs[-5:]
    xb = x.astype(BF16)
    p0 = jnp.dot(xb, w0_ref[...], preferred_element_type=F32)
    p1 = jnp.dot(xb, w1_ref[...], preferred_element_type=F32)
    bias = jnp.dot(pos_ref[...].astype(BF16), wcat_ref[...], preferred_element_type=F32)[0:1]
    o_ref[0] = p0 + pltpu.roll(p1, p1.shape[0] - 1, 0) + bias


def _cmp_weights(pos, wk, wv):
    r = CMP_LEN // CMP_STRIDE
    w = jnp.stack([wk, wv]).reshape(2, r, CMP_STRIDE, HEAD_DIM, HEAD_DIM)
    e2 = jnp.eye(2, dtype=w.dtype)
    eg = jnp.eye(KV_G, dtype=w.dtype)
    big = jnp.einsum('krlde,kK,gG->rlkgdKGe', w, e2, eg)
    big = big.reshape(r, CMP_STRIDE * 2 * KV_G * HEAD_DIM, 2 * KV_G * HEAD_DIM).astype(BF16)
    wcat = jnp.concatenate([wk, wk, wv, wv], axis=1).astype(BF16)
    posb = jnp.zeros((SUBLANE, CMP_LEN * HEAD_DIM), F32).at[0].set(pos.reshape(-1))
    return posb, wcat, big[0], big[1]


def _const_specs(arrays):
    return [pl.BlockSpec(a.shape, lambda *_, nd=a.ndim: (0,) * nd) for a in arrays]


def _cmp_prompt(c, b, t, cw):
    kw = 2 * KV_G * HEAD_DIM
    nchunk = t // CMP_STRIDE
    x = c[:, _COL['kv_cmp']:_COL['kv_cmp'] + kw].reshape(b, nchunk, CMP_STRIDE * kw)
    return pl.pallas_call(
        functools.partial(_cmp_kernel, n_in=1, feature_major=False), grid=(b,),
        in_specs=[pl.BlockSpec((1, nchunk, CMP_STRIDE * kw), lambda i: (i, 0, 0))] + _const_specs(cw),
        out_specs=pl.BlockSpec((1, nchunk, kw), lambda i: (i, 0, 0)),
        out_shape=jax.ShapeDtypeStruct((b, nchunk, kw), F32),
        compiler_params=_cparams(("parallel",)), name="nsa_cmp_prompt")(x, *cw)


def _cmp_paged(page_table, pool, first, cw):
    bs, npages = page_table.shape
    kw, page = pool.shape[1:]
    nchunk = npages * page // CMP_STRIDE
    return pl.pallas_call(
        functools.partial(_cmp_kernel, n_in=npages, feature_major=True),
        grid_spec=pltpu.PrefetchScalarGridSpec(
            num_scalar_prefetch=1, grid=(bs,), in_specs=_page_specs(pool, npages, first) + _const_specs(cw),
            out_specs=pl.BlockSpec((1, nchunk, kw), lambda b, pt: (b, 0, 0)),
            scratch_shapes=[pltpu.VMEM((kw // LANE, npages * page, LANE), F32)]),
        out_shape=jax.ShapeDtypeStruct((bs, nchunk, kw), F32),
        compiler_params=_cparams(("parallel",)), name="nsa_cmp_paged")(page_table, *([pool] * npages), *cw)


def _overlap_matrix(nc, ns):
    i_c = np.arange(LANE)[:, None] * CMP_STRIDE
    j_s = np.arange(LANE)[None, :] * SLC_BLOCK
    ov = (i_c < j_s + SLC_BLOCK) & (i_c + CMP_LEN > j_s)
    ov &= (np.arange(LANE)[:, None] < nc) & (np.arange(LANE)[None, :] < ns)
    return jnp.asarray(ov, F32)


def _nsa_prompt_kernel(q_ref, slc_ref, win_ref, cmp_ref, g_ref, ov_ref, o_ref, *, tq, nc, ns):
    qi = pl.program_id(1)
    q0 = pl.multiple_of(qi * tq, tq)
    rows = HG * tq
    rl = lax.broadcasted_iota(jnp.int32, (tq, 1), 0)
    t1 = q0 + rl
    t4 = jnp.concatenate([t1] * HG, axis=0)
    lane = lax.broadcasted_iota(jnp.int32, (1, LANE), 1)
    colpos = lax.broadcasted_iota(jnp.int32, (1, tq), 1)
    dloc = lax.broadcasted_iota(jnp.int32, (tq, tq), 0) - lax.broadcasted_iota(jnp.int32, (tq, tq), 1)
    sg = _sigmoid(g_ref[...])
    jl = lax.broadcasted_iota(jnp.int32, (tq, LANE), 1)
    cur = t1 // SLC_BLOCK
    erow = lax.broadcasted_iota(jnp.int32, (LANE, tq), 0)
    ecol = lax.broadcasted_iota(jnp.int32, (LANE, tq), 1)
    wtiles = WINDOW // tq
    qs, slope, o_cmp, blockbias = [], [], [], []
    for g in range(KV_G):
        kl = g * HEAD_DIM
        vl = KV_G * HEAD_DIM + g * HEAD_DIM
        qg = jnp.concatenate([q_ref[:, (g * HG + h) * HEAD_DIM:(g * HG + h + 1) * HEAD_DIM] for h in range(HG)],
                             axis=0)
        qs.append((qg * SCALE).astype(BF16))
        slope.append(jnp.concatenate([jnp.full((tq, 1), SLOPES[g * HG + h], F32) for h in range(HG)], axis=0))

        kc = cmp_ref[0, :, kl:kl + HEAD_DIM].astype(BF16)
        vc = cmp_ref[0, :, vl:vl + HEAD_DIM].astype(BF16)
        dist = (t4 - (lane * CMP_STRIDE + CMP_LEN - 1)).astype(F32)
        ok_c = (dist >= 0) & (lane < nc)
        s = jnp.where(ok_c, _dot_nt(qs[g], kc) - slope[g] * dist, NEG)
        m = jnp.max(s, axis=1, keepdims=True)
        e = jnp.where(ok_c, jnp.exp(s - m), 0.0)
        p_c = e / jnp.maximum(jnp.sum(e, axis=1, keepdims=True), 1e-30)
        o_cmp.append(jnp.dot(p_c.astype(BF16), vc, preferred_element_type=F32))

        psum = p_c[0:tq]
        for h in range(1, HG):
            psum = psum + p_c[h * tq:(h + 1) * tq]
        imp = jnp.dot(psum, ov_ref[...], precision=HI, preferred_element_type=F32)
        forced = (jl == 0) | (jl == cur) | (jl == cur - 1)
        imp = jnp.where(forced, FORCE, imp)
        imp = jnp.where(jl <= cur, imp, NEG)
        picked = _topn_mask(imp, ns, min(SLC_TOPN, ns)) & (jl <= cur)
        blockbias.append(jnp.where(picked, 0.0, NEG).astype(BF16))

    def scores(ref, g, ks, bias):
        k = ref[pl.ds(ks, tq), g * HEAD_DIM:(g + 1) * HEAD_DIM].astype(BF16)
        cpos = (ks - q0 + colpos).astype(F32)
        return _dot_nt(qs[g], k) + slope[g] * cpos + jnp.concatenate([bias] * HG, axis=0)

    def values(ref, g, kj):
        ks = pl.multiple_of(kj * tq, tq)
        vl = KV_G * HEAD_DIM + g * HEAD_DIM
        return ref[pl.ds(ks, tq), vl:vl + HEAD_DIM].astype(BF16)

    def slc_score(g, kj):
        ks = pl.multiple_of(kj * tq, tq)
        expand = jnp.where((ks + ecol) // SLC_BLOCK == erow, 1.0, 0.0).astype(BF16)
        return scores(slc_ref, g, ks, jnp.dot(blockbias[g], expand, preferred_element_type=F32))

    def win_score(g, kj):
        ks = pl.multiple_of(kj * tq, tq)
        d = dloc + (q0 - ks)
        return scores(win_ref, g, ks, jnp.where((d >= 0) & (d < WINDOW), 0.0, NEG))

    causal = jnp.concatenate([dloc] * HG, axis=0) >= 0
    o_slc = _pipelined_attention(KV_G, 0, qi, slc_score, functools.partial(values, slc_ref), rows, tq, HEAD_DIM,
                                 last_fix=lambda g, s: jnp.where(causal, s, NEG))
    o_win = _pipelined_attention(KV_G, jnp.maximum(qi - wtiles, 0), qi, win_score,
                                 functools.partial(values, win_ref), rows, tq, HEAD_DIM)

    for g in range(KV_G):
        for h in range(HG):
            hh = g * HG + h
            r0 = h * tq
            o = (sg[:, hh:hh + 1] * o_cmp[g][r0:r0 + tq]
                 + sg[:, N_HEADS + hh:N_HEADS + hh + 1] * o_slc[g][r0:r0 + tq]
                 + sg[:, 2 * N_HEADS + hh:2 * N_HEADS + hh + 1] * o_win[g][r0:r0 + tq])
            o_ref[:, hh * HEAD_DIM:(hh + 1) * HEAD_DIM] = o


def _nsa_prompt(c, kvc, ov, b, t, tq):
    nq = t // tq
    kw = 2 * KV_G * HEAD_DIM
    nchunk = t // CMP_STRIDE
    nc = nchunk - CMP_LEN // CMP_STRIDE + 1
    ns = -(-t // SLC_BLOCK)
    return pl.pallas_call(
        functools.partial(_nsa_prompt_kernel, tq=tq, nc=nc, ns=ns), grid=(b, nq),
        in_specs=[pl.BlockSpec((tq, MIX), lambda i, q: (i * nq + q, _COL['nsa_q'] // MIX)),
                  pl.BlockSpec((t, kw), lambda i, q: (i, _COL['kv_slc'] // kw)),
                  pl.BlockSpec((t, kw), lambda i, q: (i, _COL['kv_win'] // kw)),
                  pl.BlockSpec((1, nchunk, kw), lambda i, q: (i, 0, 0)),
                  pl.BlockSpec((tq, LANE), lambda i, q: (i * nq + q, _COL['nsa_gate'] // LANE)),
                  pl.BlockSpec((LANE, LANE), lambda i, q: (0, 0))],
        out_specs=pl.BlockSpec((tq, MIX), lambda i, q: (i * nq + q, 0)),
        out_shape=jax.ShapeDtypeStruct((b * t, MIX), F32),
        compiler_params=_cparams(("parallel", "arbitrary")), name="nsa_prompt")(c, c, c, kvc, c, ov)


def _dec_rows():
    rid = lax.broadcasted_iota(jnp.int32, (N_HEADS * T_PAD, 1), 0)
    t8 = rid % T_PAD
    slope = jnp.concatenate([jnp.full((T_PAD, 1), SLOPES[h], F32) for h in range(N_HEADS)], axis=0)
    return t8, slope


def _expand_heads(q8):
    lane = lax.broadcasted_iota(jnp.int32, q8.shape, 1)
    return jnp.concatenate([jnp.where(lane // HEAD_DIM == h, q8, 0.0) for h in range(N_HEADS)], axis=0)


def _collapse_heads(res):
    lane = lax.broadcasted_iota(jnp.int32, (T_PAD, res.shape[1]), 1)
    out = jnp.zeros((T_PAD, res.shape[1]), F32)
    for h in range(N_HEADS):
        out = out + jnp.where(lane // HEAD_DIM == h, res[h * T_PAD:(h + 1) * T_PAD], 0.0)
    return out


def _moba_dec_kernel(pt_ref, q_ref, kn_ref, vn_ref, *rest, npages, past):
    page_refs, o_ref = rest[:npages], rest[npages]
    page = page_refs[0].shape[2]
    t8, slope = _dec_rows()
    qpos = past + t8
    lane = lax.broadcasted_iota(jnp.int32, (1, LANE), 1)
    q8 = q_ref[0]
    qx_f = _expand_heads(q8)
    qx = (qx_f * SCALE).astype(BF16)
    per_blk = MOBA_BLOCK // page
    nb_past = past // MOBA_BLOCK
    lanei = lax.broadcasted_iota(jnp.int32, (MIX, LANE), 1)
    kmean = jnp.zeros((MIX, LANE), F32)
    for n in range(nb_past):
        tot = page_refs[n * per_blk][0, 0:MIX, :]
        for p in range(n * per_blk + 1, (n + 1) * per_blk):
            tot = tot + page_refs[p][0, 0:MIX, :]
        col = jnp.sum(tot, axis=1, keepdims=True) * (1.0 / MOBA_BLOCK)
        kmean = kmean + jnp.where(lanei == n, col, 0.0)
    gate = jnp.dot(qx_f, kmean, precision=HI, preferred_element_type=F32)
    jl = lax.broadcasted_iota(jnp.int32, gate.shape, 1)
    cur = qpos // MOBA_BLOCK
    gate = jnp.where(jl < cur, gate, NEG)
    nb = -(-(past + T_PAD) // MOBA_BLOCK)
    sel = jnp.where(_topn_mask(gate, nb, max(1, min(MOBA_TOPK, nb - 1))) & (jl < cur), 1.0, 0.0)
    s_tiles, v_tiles = [], []
    for p in range(npages):
        kt = page_refs[p][0, 0:MIX, :].astype(BF16)
        v_tiles.append((page_refs[p][0, MIX:2 * MIX, :].astype(BF16), True))
        n = (p * page) // MOBA_BLOCK
        d = (qpos - (p * page + lane)).astype(F32)
        valid = (sel[:, n:n + 1] > 0.5) & (d >= 0)
        s_tiles.append(jnp.where(valid, jnp.dot(qx, kt, preferred_element_type=F32) - slope * d, NEG))
    k = _pad_rows(kn_ref[0], LANE).astype(BF16)
    v_tiles.append((_pad_rows(vn_ref[0], LANE).astype(BF16), False))
    d = (t8 - lane).astype(F32)
    valid = (d >= 0) & (lane < T_PAD)
    s_tiles.append(jnp.where(valid, _dot_nt(qx, k) - slope * d, NEG))
    o_ref[0] = _collapse_heads(_attend_tiles(s_tiles, v_tiles))


def _fox_dec_kernel(pt_ref, q_ref, kn_ref, vn_ref, ln_ref, *rest, npages, past):
    logf_refs, page_refs, o_ref = rest[:npages], rest[npages:2 * npages], rest[2 * npages]
    t8, _ = _dec_rows()
    lane = lax.broadcasted_iota(jnp.int32, (1, LANE), 1)
    qx = (_expand_heads(q_ref[0]) * SCALE).astype(BF16)
    r = lax.broadcasted_iota(jnp.int32, (LANE, LANE), 0)
    c = lax.broadcasted_iota(jnp.int32, (LANE, LANE), 1)
    tri = jnp.where(r <= c, 1.0, 0.0).astype(F32)

    def head_rows(x):
        return jnp.concatenate([jnp.broadcast_to(x[h:h + 1], (T_PAD, x.shape[1])) for h in range(N_HEADS)], axis=0)

    carry = jnp.zeros((N_HEADS, 1), F32)
    cum = []
    for ref in list(logf_refs) + [ln_ref]:
        cs = jnp.dot(ref[0], tri, precision=HI, preferred_element_type=F32) + carry
        cum.append(cs)
        carry = cs[:, LANE - 1:LANE]
    c_ref = head_rows(cum[npages - 1][:, LANE - 1:LANE])

    s_tiles, v_tiles = [], []
    for p in range(npages):
        kt = page_refs[p][0, 0:MIX, :].astype(BF16)
        v_tiles.append((page_refs[p][0, MIX:2 * MIX, :].astype(BF16), True))
        s_tiles.append(jnp.dot(qx, kt, preferred_element_type=F32) + (c_ref - head_rows(cum[p])))
    k = _pad_rows(kn_ref[0], LANE).astype(BF16)
    v_tiles.append((_pad_rows(vn_ref[0], LANE).astype(BF16), False))
    valid = (lane <= t8) & (lane < T_PAD)
    s_tiles.append(jnp.where(valid, _dot_nt(qx, k) + (c_ref - head_rows(cum[npages])), NEG))
    o_ref[0] = _collapse_heads(_attend_tiles(s_tiles, v_tiles))


def _page_specs(pool, npages, first):
    blk = (1,) + pool.shape[1:]
    return [pl.BlockSpec(blk, lambda b, pt, p=p: (first + pt[b, p], 0, 0)) for p in range(npages)]


def _paged_mha_dec(kernel, name, c3, col, page_table, pool, first, extra_args, extra_specs, past):
    bs, npages = page_table.shape
    qb = col // MIX
    in_specs = [pl.BlockSpec((1, T_PAD, MIX), lambda b, pt: (b, 0, qb)),
                pl.BlockSpec((1, T_PAD, MIX), lambda b, pt: (b, 0, qb + 1)),
                pl.BlockSpec((1, T_PAD, MIX), lambda b, pt: (b, 0, qb + 2))]
    in_specs += extra_specs
    in_specs += _page_specs(pool, npages, first)
    return pl.pallas_call(
        functools.partial(kernel, npages=npages, past=past),
        grid_spec=pltpu.PrefetchScalarGridSpec(
            num_scalar_prefetch=1, grid=(bs,), in_specs=in_specs,
            out_specs=pl.BlockSpec((1, T_PAD, MIX), lambda b, pt: (b, 0, 0))),
        out_shape=jax.ShapeDtypeStruct((bs, T_PAD, MIX), F32),
        compiler_params=_cparams(("parallel",)), name=name)(
            page_table, c3, c3, c3, *extra_args, *([pool] * npages))


def _nsa_dec_kernel(pt_ref, q_ref, ns_ref, nw_ref, g_ref, cmp_ref, ov_ref, wc_ref, *rest, npages, past, nc, ns):
    page_refs, o_ref = rest[:npages], rest[npages]
    page = page_refs[0].shape[2]
    kw = KV_G * HEAD_DIM
    t8, slope = _dec_rows()
    qpos = past + t8
    lane = lax.broadcasted_iota(jnp.int32, (1, LANE), 1)
    lane8 = lax.broadcasted_iota(jnp.int32, (T_PAD, LANE), 1)
    q8 = q_ref[0] * SCALE
    rows = []
    for h in range(N_HEADS):
        x = q8[:, (h // 2) * LANE:(h // 2 + 1) * LANE]
        dst = h // HG
        if h % 2 != dst:
            x = pltpu.roll(x, HEAD_DIM, 1)
        rows.append(jnp.where((lane8 // HEAD_DIM) == dst, x, 0.0))
    qx = jnp.concatenate(rows, axis=0).astype(BF16)

    cm = cmp_ref[0]
    dist = (qpos - (lane * CMP_STRIDE + CMP_LEN - 1)).astype(F32)
    ok_c = (dist >= 0) & (lane < nc)
    s = jnp.where(ok_c, _dot_nt(qx, cm[:, 0:kw].astype(BF16)) - slope * dist, NEG)
    m = jnp.max(s, axis=1, keepdims=True)
    e = jnp.where(ok_c, jnp.exp(s - m), 0.0)
    p_c = e / jnp.maximum(jnp.sum(e, axis=1, keepdims=True), 1e-30)
    o_cmp = jnp.dot(p_c.astype(BF16), cm[:, kw:2 * kw].astype(BF16), preferred_element_type=F32)

    psum = []
    for g in range(KV_G):
        acc = p_c[g * HG * T_PAD:g * HG * T_PAD + T_PAD]
        for h in range(1, HG):
            r0 = (g * HG + h) * T_PAD
            acc = acc + p_c[r0:r0 + T_PAD]
        psum.append(acc)
    imp = jnp.dot(jnp.concatenate(psum, axis=0), ov_ref[...], precision=HI, preferred_element_type=F32)
    jl = lax.broadcasted_iota(jnp.int32, imp.shape, 1)
    tg = lax.broadcasted_iota(jnp.int32, (KV_G * T_PAD, 1), 0) % T_PAD
    cur = (past + tg) // SLC_BLOCK
    forced = (jl == 0) | (jl == cur) | (jl == cur - 1)
    imp = jnp.where(forced, FORCE, imp)
    imp = jnp.where(jl <= cur, imp, NEG)
    sel = jnp.where(_topn_mask(imp, ns, min(SLC_TOPN, ns)) & (jl <= cur), 1.0, 0.0)
    sel_rows = jnp.concatenate([sel[(h // HG) * T_PAD:(h // HG + 1) * T_PAD] for h in range(N_HEADS)], axis=0)

    def new_tile(ref, extra_valid):
        k = _pad_rows(ref[0, :, 0:kw], LANE).astype(BF16)
        v = _pad_rows(ref[0, :, kw:2 * kw], LANE).astype(BF16)
        d = (t8 - lane).astype(F32)
        valid = (d >= 0) & (lane < T_PAD) & extra_valid
        return jnp.where(valid, _dot_nt(qx, k) - slope * d, NEG), (v, False)

    s_tiles, v_tiles = [], []
    per = page // SLC_BLOCK
    for p in range(npages):
        kt = page_refs[p][0, 0:kw, :].astype(BF16)
        v_tiles.append((page_refs[p][0, kw:2 * kw, :].astype(BF16), True))
        picked = jnp.zeros((N_HEADS * T_PAD, LANE), jnp.bool_)
        for j in range(per):
            blk = p * per + j
            picked = picked | ((lane // SLC_BLOCK == j) & (sel_rows[:, blk:blk + 1] > 0.5))
        d = (qpos - (p * page + lane)).astype(F32)
        s_tiles.append(jnp.where(picked & (d >= 0), jnp.dot(qx, kt, preferred_element_type=F32) - slope * d, NEG))
    blk_new = past // SLC_BLOCK
    s_new, v_new = new_tile(ns_ref, sel_rows[:, blk_new:blk_new + 1] > 0.5)
    o_slc = _attend_tiles(s_tiles + [s_new], v_tiles + [v_new])

    s_tiles, v_tiles = [], []
    wb = wc_ref.shape[2]
    w_off = past - wb
    for j in range(wb // LANE):
        kt = wc_ref[0, 0:kw, j * LANE:(j + 1) * LANE].astype(BF16)
        v_tiles.append((wc_ref[0, kw:2 * kw, j * LANE:(j + 1) * LANE].astype(BF16), True))
        d = qpos - (w_off + j * LANE + lane)
        valid = (d >= 0) & (d < WINDOW)
        s_tiles.append(jnp.where(valid, jnp.dot(qx, kt, preferred_element_type=F32) - slope * d.astype(F32), NEG))
    s_new, v_new = new_tile(nw_ref, True)
    o_win = _attend_tiles(s_tiles + [s_new], v_tiles + [v_new])

    sg = _sigmoid(g_ref[0])
    for h in range(N_HEADS):
        r0, l0 = h * T_PAD, (h // HG) * HEAD_DIM
        o = (sg[:, h:h + 1] * o_cmp[r0:r0 + T_PAD, l0:l0 + HEAD_DIM]
             + sg[:, N_HEADS + h:N_HEADS + h + 1] * o_slc[r0:r0 + T_PAD, l0:l0 + HEAD_DIM]
             + sg[:, 2 * N_HEADS + h:2 * N_HEADS + h + 1] * o_win[r0:r0 + T_PAD, l0:l0 + HEAD_DIM])
        o_ref[0, :, h * HEAD_DIM:(h + 1) * HEAD_DIM] = o


def _nsa_dec(c3, kvc, ov, win_cache, win_first, page_table, pool, first, past, t_real):
    bs, npages = page_table.shape
    kw = 2 * KV_G * HEAD_DIM
    nchunk = kvc.shape[1]
    nc = nchunk - CMP_LEN // CMP_STRIDE + 1
    ns = -(-(past + t_real) // SLC_BLOCK)
    wb = win_cache.shape[2]
    in_specs = [pl.BlockSpec((1, T_PAD, MIX), lambda b, pt: (b, 0, _COL['nsa_q'] // MIX)),
                pl.BlockSpec((1, T_PAD, kw), lambda b, pt: (b, 0, _COL['kv_slc'] // kw)),
                pl.BlockSpec((1, T_PAD, kw), lambda b, pt: (b, 0, _COL['kv_win'] // kw)),
                pl.BlockSpec((1, T_PAD, LANE), lambda b, pt: (b, 0, _COL['nsa_gate'] // LANE)),
                pl.BlockSpec((1, nchunk, kw), lambda b, pt: (b, 0, 0)),
                pl.BlockSpec((LANE, LANE), lambda b, pt: (0, 0)),
                pl.BlockSpec((1, kw, wb), lambda b, pt: (win_first + b, 0, 0))]
    in_specs += _page_specs(pool, npages, first)
    return pl.pallas_call(
        functools.partial(_nsa_dec_kernel, npages=npages, past=past, nc=nc, ns=ns),
        grid_spec=pltpu.PrefetchScalarGridSpec(
            num_scalar_prefetch=1, grid=(bs,), in_specs=in_specs,
            out_specs=pl.BlockSpec((1, T_PAD, MIX), lambda b, pt: (b, 0, 0))),
        out_shape=jax.ShapeDtypeStruct((bs, T_PAD, MIX), F32),
        compiler_params=_cparams(("parallel",)), name="nsa_decode")(
            page_table, c3, c3, c3, c3, kvc, ov, win_cache, *([pool] * npages))


def _s5_disc_kernel(ar_ref, ai_ref, ls_ref, btr_ref, bti_ref, abr_ref, abi_ref, bbr_ref, bbi_ref):
    ar, ai = ar_ref[...], ai_ref[...]
    step = jnp.exp(ls_ref[...])
    mag = jnp.exp(ar * step)
    abr = mag * jnp.cos(ai * step)
    abi = mag * jnp.sin(ai * step)
    den = ar * ar + ai * ai
    zr = (ar * (abr - 1.0) + ai * abi) / den
    zi = (ar * abi - ai * (abr - 1.0)) / den
    abr_ref[...] = abr
    abi_ref[...] = abi
    btr, bti = btr_ref[...], bti_ref[...]
    bbr_ref[...] = zr * btr - zi * bti
    bbi_ref[...] = zr * bti + zi * btr


def _s5_disc(a_re, a_im, log_step, b_re, b_im):
    rep = lambda a: jnp.repeat(a, S5_CH, axis=0)
    n = S5_GROUPS * S5_CH
    args = (rep(a_re), rep(a_im), rep(jnp.broadcast_to(log_step[:, None], (S5_GROUPS, S5_STATE))),
            b_re.transpose(0, 2, 1).reshape(n, S5_STATE), b_im.transpose(0, 2, 1).reshape(n, S5_STATE))
    shp = jax.ShapeDtypeStruct((n, S5_STATE), F32)
    return pl.pallas_call(_s5_disc_kernel, out_shape=(shp, shp, shp, shp), name="s5_discretise")(*args)


_S5_CH_ROWS = 128


def _s5_scan_kernel(*refs, seg, has_h0):
    if has_h0:
        (u_ref, bre_ref, bim_ref, ar_ref, ai_ref, cre_ref, cim_ref, d_ref, h0r_ref, h0i_ref,
         y_ref, hlr_ref, hli_ref, hr_s, hi_s) = refs
    else:
        (u_ref, bre_ref, bim_ref, ar_ref, ai_ref, cre_ref, cim_ref, d_ref,
         y_ref, hlr_ref, hli_ref, hr_s, hi_s) = refs
    rows = u_ref.shape[0]
    ch = min(_S5_CH_ROWS, rows)
    pad = ch
    nch = rows // ch
    ar, ai = ar_ref[0], ai_ref[0]
    bre, bim, cre, cim = (_split_bf16(r[0]) for r in (bre_ref, bim_ref, cre_ref, cim_ref))
    hr_s[0:pad, :] = jnp.zeros((pad, hr_s.shape[1]), F32)
    hi_s[0:pad, :] = jnp.zeros((pad, hi_s.shape[1]), F32)

    def init_body(i, _):
        r0 = pl.multiple_of(i * ch, ch)
        u = u_ref[pl.ds(r0, ch), :]
        br = _dot3(u, bre)
        bi = _dot3(u, bim)
        if has_h0:
            h0r, h0i = h0r_ref[pl.ds(r0, ch), :], h0i_ref[pl.ds(r0, ch), :]
            br = br + (ar * h0r - ai * h0i)
            bi = bi + (ar * h0i + ai * h0r)
        hr_s[pl.ds(pad + r0, ch), :] = br
        hi_s[pl.ds(pad + r0, ch), :] = bi
        return 0

    lax.fori_loop(0, nch, init_body, 0)

    rl = lax.broadcasted_iota(jnp.int32, (ch, 1), 0)
    pr, pi = ar, ai
    d = 1
    while d < seg:
        first = d // ch

        def pass_body(i, _, d=d, pr=pr, pi=pi):
            r0 = pl.multiple_of((nch - 1 - i) * ch, ch)
            cr = hr_s[pl.ds(pad + r0, ch), :]
            ci = hi_s[pl.ds(pad + r0, ch), :]
            if d < SUBLANE:
                lo = pad - SUBLANE
                sr = pltpu.roll(hr_s[pl.ds(lo + r0, ch + SUBLANE), :], d, 0)[SUBLANE:]
                si = pltpu.roll(hi_s[pl.ds(lo + r0, ch + SUBLANE), :], d, 0)[SUBLANE:]
            else:
                sr = hr_s[pl.ds(pad + r0 - d, ch), :]
                si = hi_s[pl.ds(pad + r0 - d, ch), :]
            keep = ((r0 + rl) % seg) >= d
            sr = jnp.where(keep, sr, 0.0)
            si = jnp.where(keep, si, 0.0)
            hr_s[pl.ds(pad + r0, ch), :] = cr + (pr * sr - pi * si)
            hi_s[pl.ds(pad + r0, ch), :] = ci + (pr * si + pi * sr)
            return 0

        lax.fori_loop(0, nch - first, pass_body, 0)
        pr, pi = pr * pr - pi * pi, 2.0 * pr * pi
        d *= 2

    def out_body(i, _):
        r0 = pl.multiple_of(i * ch, ch)
        hr = hr_s[pl.ds(pad + r0, ch), :]
        hi = hi_s[pl.ds(pad + r0, ch), :]
        y = _dot3(hr, cre) - _dot3(hi, cim)
        y_ref[pl.ds(r0, ch), :] = y + d_ref[0] * u_ref[pl.ds(r0, ch), :]
        return 0

    lax.fori_loop(0, nch, out_body, 0)
    nl = hlr_ref.shape[0]
    hlr_ref[...] = hr_s[pad + rows - nl:pad + rows, :]
    hli_ref[...] = hi_s[pad + rows - nl:pad + rows, :]


def _s5_scan(c, sw, rows, seg, h0=None):
    n = c.shape[0]
    nt = n // rows
    gl = LANE // S5_CH
    lt = S5_GROUPS // gl
    w = gl * S5_STATE
    nl = (rows // seg) * SUBLANE if seg == SUBLANE else SUBLANE
    ub = _COL['s5_u'] // LANE
    in_specs = [pl.BlockSpec((rows, LANE), lambda i, j: (i, ub + j)),
                pl.BlockSpec((1, LANE, w), lambda i, j: (j, 0, 0)),
                pl.BlockSpec((1, LANE, w), lambda i, j: (j, 0, 0)),
                pl.BlockSpec((1, 1, w), lambda i, j: (j, 0, 0)),
                pl.BlockSpec((1, 1, w), lambda i, j: (j, 0, 0)),
                pl.BlockSpec((1, w, LANE), lambda i, j: (j, 0, 0)),
                pl.BlockSpec((1, w, LANE), lambda i, j: (j, 0, 0)),
                pl.BlockSpec((1, 1, LANE), lambda i, j: (j, 0, 0))]
    args = [c, sw['bre'], sw['bim'], sw['ar'], sw['ai'], sw['cre'], sw['cim'], sw['d']]
    if h0 is not None:
        in_specs += [pl.BlockSpec((rows, w), lambda i, j: (i, j))] * 2
        args += list(h0)
    hshape = jax.ShapeDtypeStruct((nt * nl, S5_GROUPS * S5_STATE), F32)
    return pl.pallas_call(
        functools.partial(_s5_scan_kernel, seg=seg, has_h0=h0 is not None), grid=(nt, lt),
        in_specs=in_specs,
        out_specs=(pl.BlockSpec((rows, LANE), lambda i, j: (i, j)),
                   pl.BlockSpec((nl, w), lambda i, j: (i, j)), pl.BlockSpec((nl, w), lambda i, j: (i, j))),
        out_shape=(jax.ShapeDtypeStruct((n, MIX), F32), hshape, hshape),
        scratch_shapes=[pltpu.VMEM((min(_S5_CH_ROWS, rows) + rows, w), F32)] * 2,
        compiler_params=_cparams(("parallel", "arbitrary")), name="s5_scan")(*args)


def _s5_weights(lp):
    abr, abi, bbr, bbi = _s5_disc(lp['s5_a_re'], lp['s5_a_im'], lp['s5_log_step'], lp['s5_b_re'], lp['s5_b_im'])
    gl = LANE // S5_CH
    lt = S5_GROUPS // gl
    eye = jnp.eye(gl, dtype=F32)

    def bdiag(bb):
        return jnp.einsum('jgcn,gh->jgchn', bb.reshape(lt, gl, S5_CH, S5_STATE), eye).reshape(
            lt, gl * S5_CH, gl * S5_STATE)

    def cdiag(cc):
        return jnp.einsum('jgcn,gh->jgnhc', cc.reshape(lt, gl, S5_CH, S5_STATE), eye).reshape(
            lt, gl * S5_STATE, gl * S5_CH)

    return dict(bre=bdiag(bbr), bim=bdiag(bbi),
                ar=abr[::S5_CH].reshape(lt, 1, gl * S5_STATE), ai=abi[::S5_CH].reshape(lt, 1, gl * S5_STATE),
                cre=cdiag(lp['s5_c_re']), cim=cdiag(lp['s5_c_im']), d=lp['s5_d'].reshape(lt, 1, LANE))


def _glu_kernel(y_ref, w_ref, o_ref):
    y = y_ref[...]
    g = 0.5 * y * (1.0 + jnp.tanh(np.float32(np.sqrt(2.0 / np.pi)) * (y + np.float32(0.044715) * (y * y * y))))
    z = jnp.dot(g.astype(BF16), w_ref[...], preferred_element_type=F32)
    o_ref[...] = z[:, :MIX] * _sigmoid(z[:, MIX:])


def _glu(y, w, tm):
    n = y.shape[0]
    return pl.pallas_call(
        _glu_kernel, grid=(n // tm,),
        in_specs=[pl.BlockSpec((tm, MIX), lambda i: (i, 0)), pl.BlockSpec((MIX, 2 * MIX), lambda i: (0, 0))],
        out_specs=pl.BlockSpec((tm, MIX), lambda i: (i, 0)),
        out_shape=jax.ShapeDtypeStruct((n, MIX), F32),
        compiler_params=_cparams(("parallel",)), name="s5_glu")(y, w)


def _layer_norm(x, g, b):
    mu = jnp.mean(x, axis=-1, keepdims=True)
    xc = x - mu
    var = jnp.mean(xc * xc, axis=-1, keepdims=True)
    return xc * lax.rsqrt(var + LN_EPS) * g + b


def _merge_kernel(o0_ref, o1_ref, o2_ref, o3_ref, mg_ref, wb_ref, x_ref, wo_ref, g_ref, b_ref, out_ref, acc_ref,
                  *, alpha):
    i = pl.program_id(1)

    @pl.when(i == 0)
    def _():
        acc_ref[...] = jnp.zeros_like(acc_ref)

    for k, o_ref in enumerate((o0_ref, o1_ref, o2_ref, o3_ref)):
        @pl.when(i == k)
        def _(o_ref=o_ref):
            proj = jnp.dot(o_ref[...].astype(BF16), wb_ref[0], preferred_element_type=F32)
            acc_ref[...] += _sigmoid(mg_ref[...]) * proj

    @pl.when(i == N_BRANCH - 1)
    def _():
        mixed = jnp.dot(acc_ref[...].astype(BF16), wo_ref[...], preferred_element_type=F32)
        out_ref[...] = _layer_norm(alpha * x_ref[...] + mixed, g_ref[...], b_ref[...])


def _merge(outs, c, x, wb, wo, g, b, tm, alpha):
    n = x.shape[0]
    o_spec = pl.BlockSpec((tm, MIX), lambda r, i: (r, 0))
    return pl.pallas_call(
        functools.partial(_merge_kernel, alpha=alpha), grid=(n // tm, N_BRANCH),
        in_specs=[o_spec, o_spec, o_spec, o_spec,
                  pl.BlockSpec((tm, D_MODEL), lambda r, i: (r, i)),
                  pl.BlockSpec((1, MIX, D_MODEL), lambda r, i: (i, 0, 0)),
                  pl.BlockSpec((tm, D_MODEL), lambda r, i: (r, 0)),
                  pl.BlockSpec((D_MODEL, D_MODEL), lambda r, i: (0, 0)),
                  pl.BlockSpec((1, D_MODEL), lambda r, i: (0, 0)),
                  pl.BlockSpec((1, D_MODEL), lambda r, i: (0, 0))],
        out_specs=pl.BlockSpec((tm, D_MODEL), lambda r, i: (r, 0)),
        out_shape=jax.ShapeDtypeStruct((n, D_MODEL), F32),
        scratch_shapes=[pltpu.VMEM((tm, D_MODEL), F32)],
        compiler_params=_cparams(("parallel", "arbitrary")), name="merge_out_ln")(
            *outs, c, wb, x, wo, g, b)


def _ffn_kernel(*refs, halo, seg, alpha):
    (h_ref, wg_ref, wv_ref, cwg_ref, cwv_ref, cbg_ref, cbv_ref, wd_ref, lg_ref, lb_ref) = refs[:10]
    pg_ref, pv_ref, out_ref, hb_ref, acc_ref = refs[10:]
    j = pl.program_id(1)

    @pl.when(j == 0)
    def _():
        hb_ref[...] = h_ref[...].astype(BF16)
        acc_ref[...] = jnp.zeros_like(acc_ref)

    hb = hb_ref[...]
    tm = hb.shape[0]
    rid = lax.broadcasted_iota(jnp.int32, (tm, 1), 0)

    def conv(u, cw_ref, cb_ref, prev):
        r1 = pltpu.roll(u, 1, 0)
        r2 = pltpu.roll(u, 2, 0)
        if halo:
            p = prev[0][0]
            p6, p7 = p[SUBLANE - 2:SUBLANE - 1], p[SUBLANE - 1:SUBLANE]
            u1 = jnp.where(rid == 0, p7, r1)
            u2 = jnp.where(rid == 0, p6, jnp.where(rid == 1, p7, r2))
        else:
            t = rid % seg
            state = jnp.where(t >= seg - (CONV_W - 1), prev[0][...], u)
            u1 = jnp.where(t >= 1, r1, pltpu.roll(state, tm - (seg - 1), 0))
            u2 = jnp.where(t >= 2, r2, pltpu.roll(state, tm - (seg - 2), 0))
        cw = cw_ref[...]
        return cb_ref[...] + (cw[0:1] * u2 + cw[1:2] * u1 + cw[2:3] * u)

    ug = jnp.dot(hb, wg_ref[...], preferred_element_type=F32)
    uv = jnp.dot(hb, wv_ref[...], preferred_element_type=F32)
    gate = conv(ug, cwg_ref, cbg_ref, (pg_ref,))
    val = conv(uv, cwv_ref, cbv_ref, (pv_ref,))
    act = gate * _sigmoid(gate) * val
    acc_ref[...] += jnp.dot(act.astype(BF16), wd_ref[...], preferred_element_type=F32)

    @pl.when(j == pl.num_programs(1) - 1)
    def _():
        out_ref[...] = _layer_norm(alpha * h_ref[...] + acc_ref[...], lg_ref[...], lb_ref[...])


def _ffn(h, lw, tm, tf, alpha, prev=None, state_rows=None, seg=None):
    n = h.shape[0]
    nf = D_FF // tf
    halo = prev is not None
    in_specs = [pl.BlockSpec((tm, D_MODEL), lambda r, j: (r, 0)),
                pl.BlockSpec((D_MODEL, tf), lambda r, j: (0, j)),
                pl.BlockSpec((D_MODEL, tf), lambda r, j: (0, nf + j)),
                pl.BlockSpec((CONV_W, tf), lambda r, j: (0, j)),
                pl.BlockSpec((CONV_W, tf), lambda r, j: (0, nf + j)),
                pl.BlockSpec((1, tf), lambda r, j: (0, j)),
                pl.BlockSpec((1, tf), lambda r, j: (0, nf + j)),
                pl.BlockSpec((tf, D_MODEL), lambda r, j: (j, 0)),
                pl.BlockSpec((1, D_MODEL), lambda r, j: (0, 0)),
                pl.BlockSpec((1, D_MODEL), lambda r, j: (0, 0))]
    args = [h, lw['w_up'], lw['w_up'], lw['conv_w'], lw['conv_w'], lw['conv_b'], lw['conv_b'], lw['w_down'],
            lw['ln2_g'], lw['ln2_b']]
    if halo:
        in_specs += [pl.BlockSpec((1, SUBLANE, tf), lambda r, j: (r, 0, j)),
                     pl.BlockSpec((1, SUBLANE, tf), lambda r, j: (r, 0, nf + j))]
        args += [prev, prev]
    else:
        in_specs += [pl.BlockSpec((tm, tf), lambda r, j: (r, j)), pl.BlockSpec((tm, tf), lambda r, j: (r, nf + j))]
        args += [state_rows, state_rows]
    return pl.pallas_call(
        functools.partial(_ffn_kernel, halo=halo, seg=seg, alpha=alpha), grid=(n // tm, nf),
        in_specs=in_specs,
        out_specs=pl.BlockSpec((tm, D_MODEL), lambda r, j: (r, 0)),
        out_shape=jax.ShapeDtypeStruct((n, D_MODEL), F32),
        scratch_shapes=[pltpu.VMEM((tm, D_MODEL), BF16), pltpu.VMEM((tm, D_MODEL), F32)],
        compiler_params=_cparams(("parallel", "arbitrary")), name="conv_ffn_ln")(*args)


def _prep_layer(l, p):
    lp = {k: v[l] for k, v in p.items()}
    lw = dict(
        w_in=_pack_w_in(lp['w_in']),
        fox_b=jnp.zeros((1, LANE), F32).at[0, :N_HEADS].set(lp['fox_b_f']),
        cmp=_cmp_weights(lp['nsa_cmp_pos'], lp['nsa_cmp_wk'], lp['nsa_cmp_wv']),
        s5=_s5_weights(lp),
        w_glu=lp['s5_w_glu'].astype(BF16),
        w_branch=lp['w_branch'].astype(BF16),
        w_out=lp['w_out'].astype(BF16),
        ln1_g=lp['ln1_g'].reshape(1, -1), ln1_b=lp['ln1_b'].reshape(1, -1),
        w_up=lp['ffn_w_up'].astype(BF16), conv_w=lp['ffn_conv_w'], conv_b=lp['ffn_conv_b'].reshape(1, -1),
        w_down=lp['ffn_w_down'].astype(BF16),
        ln2_g=lp['ln2_g'].reshape(1, -1), ln2_b=lp['ln2_b'].reshape(1, -1))
    return lw


def _prompt_layer(x, b, t, lw, alpha):
    n = b * t
    kw = 2 * KV_G * HEAD_DIM
    tr = min(1024, n)
    c = _matmul(x, lw['w_in'], tr, 512)
    logf = _logf(c, lw['fox_b'], tr)
    cumr = _cumsum_prompt(logf.reshape(b, t, N_HEADS).transpose(0, 2, 1))
    o_fox = _fox_prompt(c, cumr[:, :, None, :], b, t, 512)
    o_moba = _moba_prompt(c, b, t)
    kvc = _cmp_prompt(c, b, t, lw['cmp'])
    nchunk = t // CMP_STRIDE
    ov = _overlap_matrix(nchunk - CMP_LEN // CMP_STRIDE + 1, -(-t // SLC_BLOCK))
    o_nsa = _nsa_prompt(c, kvc, ov, b, t, 128)
    y_s5, hlr, hli = _s5_scan(c, lw['s5'], t, t)
    o_s5 = _glu(y_s5, lw['w_glu'], tr)
    h = _merge((o_nsa, o_s5, o_moba, o_fox), c, x, lw['w_branch'], lw['w_out'], lw['ln1_g'], lw['ln1_b'], 256, alpha)

    tm = 512
    nt = n // tm
    edge = h.reshape(nt, tm, D_MODEL)[:, tm - (CONV_W - 1):].reshape(nt * (CONV_W - 1), D_MODEL)
    edge = _pad_rows(edge, -(-edge.shape[0] // SUBLANE) * SUBLANE)
    u_edge = _matmul(edge, lw['w_up'], edge.shape[0], 512)[:nt * (CONV_W - 1)].reshape(nt, CONV_W - 1, 2 * D_FF)
    per_seq = t // tm
    conv_state = u_edge[per_seq - 1::per_seq]
    starts_seq = (jnp.arange(nt) % per_seq == 0)[:, None, None]
    prev = jnp.where(starts_seq, 0.0, jnp.roll(u_edge, 1, axis=0))
    prev = jnp.pad(prev, ((0, 0), (SUBLANE - (CONV_W - 1), 0), (0, 0)))
    y = _ffn(h, lw, tm, 512, alpha, prev=prev)

    seg = lambda name, w: c[:, _COL[name]:_COL[name] + w]
    win_rows = min(WINDOW, t)
    states = (
        seg('kv_cmp', kw).reshape(b, t, 2, KV_G, HEAD_DIM),
        seg('kv_slc', kw).reshape(b, t, 2, KV_G, HEAD_DIM),
        c[:, _COL['moba'] + MIX:_COL['moba'] + 3 * MIX].reshape(b, t, 2, N_HEADS, HEAD_DIM),
        c[:, _COL['fox'] + MIX:_COL['fox'] + 3 * MIX].reshape(b, t, 2, N_HEADS, HEAD_DIM),
        logf.reshape(b, t, N_HEADS),
        seg('kv_win', kw).reshape(b, t, 2, KV_G, HEAD_DIM)[:, t - win_rows:],
        hlr.reshape(b, SUBLANE, S5_GROUPS, S5_STATE)[:, SUBLANE - 1],
        hli.reshape(b, SUBLANE, S5_GROUPS, S5_STATE)[:, SUBLANE - 1],
        conv_state)
    return y, states


def _sample_layer(x, bs, t_real, lw, alpha, past, page_table, past_len):
    n = bs * T_PAD
    kw = 2 * KV_G * HEAD_DIM
    c = _matmul(x, lw['w_in'], n, 512)
    c3 = c.reshape(bs, T_PAD, WP)
    logf = _logf(c, lw['fox_b'], n)
    logf3 = logf.reshape(bs, T_PAD, N_HEADS)
    tmask = (jnp.arange(T_PAD) < t_real)[None, :, None]
    new_t = jnp.pad(jnp.where(tmask, logf3, 0.0).transpose(0, 2, 1), ((0, 0), (0, 0), (0, LANE - T_PAD)))
    first = past['first_page']
    npages = page_table.shape[1]
    o_fox = _paged_mha_dec(
        _fox_dec_kernel, "fox_decode", c3, _COL['fox'], page_table, past['fox'], first,
        [new_t] + [past['fox_logf']] * npages,
        [pl.BlockSpec((1, N_HEADS, LANE), lambda b, pt: (b, 0, 0))] + _page_specs(past['fox_logf'], npages, first),
        past_len)
    o_moba = _paged_mha_dec(_moba_dec_kernel, "moba_decode", c3, _COL['moba'], page_table, past['moba'], first,
                            [], [], past_len)
    kvc = _cmp_paged(page_table, past['nsa_cmp'], first, lw['cmp'])
    nchunk = kvc.shape[1]
    ov = _overlap_matrix(nchunk - CMP_LEN // CMP_STRIDE + 1, -(-(past_len + t_real) // SLC_BLOCK))
    o_nsa = _nsa_dec(c3, kvc, ov, past['nsa_win'], past['first_seq'], page_table, past['nsa_slc'], first,
                     past_len, t_real)
    h0 = [jnp.pad(s.reshape(bs, 1, -1), ((0, 0), (0, T_PAD - 1), (0, 0))).reshape(n, -1) for s in past['s5']]
    y_s5, hlr, hli = _s5_scan(c, lw['s5'], n, T_PAD, h0=h0)
    o_s5 = _glu(y_s5, lw['w_glu'], n)
    h = _merge((o_nsa.reshape(n, MIX), o_s5, o_moba.reshape(n, MIX), o_fox.reshape(n, MIX)), c, x,
               lw['w_branch'], lw['w_out'], lw['ln1_g'], lw['ln1_b'], min(256, n), alpha)

    buf = past['ffn_conv']
    state_rows = jnp.pad(buf, ((0, 0), (T_PAD - (CONV_W - 1), 0), (0, 0))).reshape(n, 2 * D_FF)
    y = _ffn(h, lw, min(512, n), 512, alpha, state_rows=state_rows, seg=T_PAD)
    last2 = h.reshape(bs, T_PAD, D_MODEL)[:, t_real - (CONV_W - 1):t_real].reshape(bs * (CONV_W - 1), D_MODEL)
    conv_state = _matmul(last2, lw['w_up'], last2.shape[0], 512).reshape(bs, CONV_W - 1, 2 * D_FF)

    tr = lambda a: a[:, :t_real]
    kv_win_new = tr(c3[:, :, _COL['kv_win']:_COL['kv_win'] + kw]).transpose(0, 2, 1)
    old_win = past['nsa_win'][past['first_seq']:past['first_seq'] + bs]
    full_win = jnp.concatenate([old_win, kv_win_new], axis=2)
    lw_ = full_win.shape[2]
    new_win = full_win[:, :, lw_ - min(WINDOW, lw_):].transpose(0, 2, 1)
    states = (
        tr(c3[:, :, _COL['kv_cmp']:_COL['kv_cmp'] + kw]).reshape(bs, t_real, 2, KV_G, HEAD_DIM),
        tr(c3[:, :, _COL['kv_slc']:_COL['kv_slc'] + kw]).reshape(bs, t_real, 2, KV_G, HEAD_DIM),
        tr(c3[:, :, _COL['moba'] + MIX:_COL['moba'] + 3 * MIX]).reshape(bs, t_real, 2, N_HEADS, HEAD_DIM),
        tr(c3[:, :, _COL['fox'] + MIX:_COL['fox'] + 3 * MIX]).reshape(bs, t_real, 2, N_HEADS, HEAD_DIM),
        tr(logf3),
        new_win.reshape(bs, new_win.shape[1], 2, KV_G, HEAD_DIM),
        hlr.reshape(bs, T_PAD, S5_GROUPS, S5_STATE)[:, t_real - 1],
        hli.reshape(bs, T_PAD, S5_GROUPS, S5_STATE)[:, t_real - 1],
        conv_state)
    return y, states


def kernel(x_prompt, x_sample, cache_nsa_cmp_kv, cache_nsa_slc_kv, cache_moba_kv, cache_fox_kv, cache_fox_logf,
           page_table, cache_nsa_win_kv, state_s5_re, state_s5_im, state_ffn_conv, w_in, fox_b_f, nsa_cmp_pos,
           nsa_cmp_wk, nsa_cmp_wv, s5_a_re, s5_a_im, s5_b_re, s5_b_im, s5_c_re, s5_c_im, s5_d, s5_log_step,
           s5_w_glu, w_branch, w_out, ln1_g, ln1_b, ffn_w_up, ffn_conv_w, ffn_conv_b, ffn_w_down, ln2_g, ln2_b):
    depth = w_in.shape[0]
    b, t, d = x_prompt.shape
    bs, ts, _ = x_sample.shape
    n_phys, page = cache_nsa_cmp_kv.shape[1:3]
    past_len = page_table.shape[1] * page
    assert d == D_MODEL and w_in.shape[2] == IN_WIDTH and ffn_w_down.shape[1] == D_FF
    assert ts <= T_PAD - (CONV_W - 1) and past_len % MOBA_BLOCK == 0 and page == LANE and (past_len + ts) // CMP_STRIDE * CMP_STRIDE <= past_len
    alpha = float((2 * depth) ** 0.25)
    params = dict(w_in=w_in, fox_b_f=fox_b_f, nsa_cmp_pos=nsa_cmp_pos, nsa_cmp_wk=nsa_cmp_wk, nsa_cmp_wv=nsa_cmp_wv,
                  s5_a_re=s5_a_re, s5_a_im=s5_a_im, s5_b_re=s5_b_re, s5_b_im=s5_b_im, s5_c_re=s5_c_re,
                  s5_c_im=s5_c_im, s5_d=s5_d, s5_log_step=s5_log_step, s5_w_glu=s5_w_glu, w_branch=w_branch,
                  w_out=w_out, ln1_g=ln1_g, ln1_b=ln1_b, ffn_w_up=ffn_w_up, ffn_conv_w=ffn_conv_w,
                  ffn_conv_b=ffn_conv_b, ffn_w_down=ffn_w_down, ln2_g=ln2_g, ln2_b=ln2_b)
    kw = 2 * KV_G * HEAD_DIM
    yp = x_prompt.reshape(b * t, d)
    ys = jnp.pad(x_sample, ((0, 0), (0, T_PAD - ts), (0, 0))).reshape(bs * T_PAD, d)
    st_p, st_s = [], []

    def feature_major(cache):
        dd, nn, rr = cache.shape[:3]
        return cache.transpose(0, 1, 3, 4, 5, 2).reshape(dd * nn, -1, rr)

    cmp_fm, slc_fm, moba_fm, fox_fm, win_fm = (
        feature_major(a) for a in (cache_nsa_cmp_kv, cache_nsa_slc_kv, cache_moba_kv, cache_fox_kv, cache_nsa_win_kv))
    logf_hm = cache_fox_logf.transpose(0, 1, 3, 2).reshape(depth * n_phys, N_HEADS, page)
    for l in range(depth):
        lw = _prep_layer(l, params)
        past = dict(
            nsa_cmp=cmp_fm, nsa_slc=slc_fm, moba=moba_fm, fox=fox_fm, fox_logf=logf_hm, nsa_win=win_fm,
            first_page=l * n_phys, first_seq=l * bs,
            s5=(state_s5_re[l], state_s5_im[l]),
            ffn_conv=state_ffn_conv[l])
        yp, sp = _prompt_layer(yp, b, t, lw, alpha)
        ys, ss = _sample_layer(ys, bs, ts, lw, alpha, past, page_table, past_len)
        st_p.append(sp)
        st_s.append(ss)
    sp = [jnp.stack(z) for z in zip(*st_p)]
    ss = [jnp.stack(z) for z in zip(*st_s)]
    out = [yp.reshape(b, t, d), ys.reshape(bs, T_PAD, d)[:, :ts]]
    for a, c in zip(sp, ss):
        out += [a, c]
    return tuple(out)
```

```python
import functools

import numpy as np
import jax
import jax.numpy as jnp
from jax import lax
from jax.experimental import pallas as pl
from jax.experimental.pallas import tpu as pltpu

F32 = jnp.float32
BF16 = jnp.bfloat16
HI = lax.Precision.HIGHEST

LANE = 128
SUBLANE = 8
VMEM_LIMIT = 56 * 1024 * 1024

D_MODEL = 2048
HEAD_DIM = 64
N_BRANCH = 4
MIX = D_MODEL // N_BRANCH
N_HEADS = MIX // HEAD_DIM
KV_G = 2
HG = N_HEADS // KV_G
CMP_LEN = 32
CMP_STRIDE = 16
SLC_BLOCK = 64
SLC_TOPN = 16
WINDOW = 512
MOBA_BLOCK = 256
MOBA_TOPK = 3
S5_CH = 16
S5_GROUPS = MIX // S5_CH
S5_STATE = 64
D_FF = 5632
CONV_W = 3
LN_EPS = 1e-5
SCALE = HEAD_DIM ** -0.5
NEG = -1e30
FORCE = 1e4
T_PAD = 8
SLOPES = tuple(float(v) for v in np.asarray(2.0 ** (-8.0 * np.arange(1, N_HEADS + 1) / N_HEADS), np.float32))

_SPLITS = (('nsa_q', MIX), ('kv_cmp', 2 * KV_G * HEAD_DIM), ('kv_slc', 2 * KV_G * HEAD_DIM),
           ('kv_win', 2 * KV_G * HEAD_DIM), ('nsa_gate', 3 * N_HEADS), ('s5_u', MIX),
           ('moba', 3 * MIX), ('fox', 3 * MIX), ('fox_f', N_HEADS), ('merge', N_BRANCH * D_MODEL))
_SRC = {}
_o = 0
for _n, _w in _SPLITS:
    _SRC[_n] = (_o, _w)
    _o += _w
IN_WIDTH = _o
_COL = dict(merge=0, nsa_q=8192, kv_cmp=8704, kv_slc=8960, kv_win=9216, nsa_gate=9472, s5_u=9600,
            moba=10240, fox=11776, fox_f=13312)
WP = 13824


def _cparams(sem):
    return pltpu.CompilerParams(dimension_semantics=sem, vmem_limit_bytes=VMEM_LIMIT)


def _dot_nt(a, b, precision=None):
    return lax.dot_general(a, b, (((1,), (1,)), ((), ())), precision=precision, preferred_element_type=F32)


def _split_bf16(x):
    hi = x.astype(BF16)
    return hi, (x - hi.astype(F32)).astype(BF16)


def _dot3(a, b_split):
    a_hi, a_lo = _split_bf16(a)
    b_hi, b_lo = b_split
    dot = functools.partial(jnp.dot, preferred_element_type=F32)
    return dot(a_hi, b_hi) + (dot(a_hi, b_lo) + dot(a_lo, b_hi))


def _sigmoid(x):
    return 1.0 / (1.0 + jnp.exp(-x))


def _pack_w_in(w):
    d = w.shape[0]
    order = ('merge', 'nsa_q', 'kv_cmp', 'kv_slc', 'kv_win', 'nsa_gate', 's5_u', 'moba', 'fox', 'fox_f')
    parts, pos = [], 0
    for name in order:
        if _COL[name] > pos:
            parts.append(jnp.zeros((d, _COL[name] - pos), w.dtype))
        s, wd = _SRC[name]
        parts.append(w[:, s:s + wd])
        pos = _COL[name] + wd
    parts.append(jnp.zeros((d, WP - pos), w.dtype))
    return jnp.concatenate(parts, axis=1).astype(BF16)


def _mm_kernel(x_ref, w_ref, o_ref, xb_ref):
    @pl.when(pl.program_id(1) == 0)
    def _():
        xb_ref[...] = x_ref[...].astype(BF16)

    o_ref[...] = jnp.dot(xb_ref[...], w_ref[...], preferred_element_type=F32)


def _matmul(x, w, tm, tn):
    m, k = x.shape
    n = w.shape[1]
    return pl.pallas_call(
        _mm_kernel, grid=(m // tm, n // tn),
        in_specs=[pl.BlockSpec((tm, k), lambda i, j: (i, 0)), pl.BlockSpec((k, tn), lambda i, j: (0, j))],
        out_specs=pl.BlockSpec((tm, tn), lambda i, j: (i, j)),
        out_shape=jax.ShapeDtypeStruct((m, n), F32),
        scratch_shapes=[pltpu.VMEM((tm, k), BF16)],
        compiler_params=_cparams(("parallel", "arbitrary")), name="mm")(x, w)


def _logf_kernel(c_ref, b_ref, o_ref):
    x = c_ref[...] + b_ref[...]
    y = jnp.minimum(x, 0.0) - jnp.log1p(jnp.exp(-jnp.abs(x)))
    o_ref[...] = y[:, :N_HEADS]


def _logf(c, b_pad, tm):
    n = c.shape[0]
    return pl.pallas_call(
        _logf_kernel, grid=(n // tm,),
        in_specs=[pl.BlockSpec((tm, LANE), lambda i: (i, _COL['fox_f'] // LANE)),
                  pl.BlockSpec((1, LANE), lambda i: (0, 0))],
        out_specs=pl.BlockSpec((tm, N_HEADS), lambda i: (i, 0)),
        out_shape=jax.ShapeDtypeStruct((n, N_HEADS), F32),
        compiler_params=_cparams(("parallel",)), name="logf")(c, b_pad)


def _cumsum_kernel(*refs, n_in):
    in_refs, o_ref = refs[-n_in - 1:-1], refs[-1]
    r = lax.broadcasted_iota(jnp.int32, (LANE, LANE), 0)
    c = lax.broadcasted_iota(jnp.int32, (LANE, LANE), 1)
    tri = jnp.where(r <= c, 1.0, 0.0).astype(F32)
    carry = jnp.zeros((N_HEADS, 1), F32)
    off = 0
    for ref in in_refs:
        for j in range(ref.shape[-1] // LANE):
            x = ref[0, :, j * LANE:(j + 1) * LANE]
            cs = jnp.dot(x, tri, precision=HI, preferred_element_type=F32) + carry
            o_ref[0, :, off:off + LANE] = cs
            carry = cs[:, LANE - 1:LANE]
            off += LANE


def _cumsum_prompt(logf_t):
    b, h, t = logf_t.shape
    return pl.pallas_call(
        functools.partial(_cumsum_kernel, n_in=1), grid=(b,),
        in_specs=[pl.BlockSpec((1, h, t), lambda i: (i, 0, 0))],
        out_specs=pl.BlockSpec((1, h, t), lambda i: (i, 0, 0)),
        out_shape=jax.ShapeDtypeStruct((b, h, t), F32),
        compiler_params=_cparams(("parallel",)), name="fox_cumsum_prompt")(logf_t)


M_FLOOR = -1e29


def _softmax_step(m, l, s):
    m_new = jnp.maximum(m, jnp.max(s, axis=1, keepdims=True))
    p = jnp.exp(s - m_new)
    alpha = jnp.exp(m - m_new)
    return m_new, alpha * l + jnp.sum(p, axis=1, keepdims=True), alpha, p.astype(BF16)


def _pipelined_attention(chains, lo, hi, score_fn, value_fn, rows, tk, dv, last_fix=None):
    def flush(c, kj, s, p_prev, a_prev, acc):
        prev = jnp.maximum(kj - 1, lo)
        return a_prev * acc + jnp.dot(p_prev, value_fn(c, prev), preferred_element_type=F32)

    def body(kj, carry):
        out = []
        for c in range(chains):
            s, p_prev, a_prev, m, l, acc = carry[c]
            acc = flush(c, kj, s, p_prev, a_prev, acc)
            m, l, alpha, p = _softmax_step(m, l, s)
            out.append((score_fn(c, kj + 1), p, alpha, m, l, acc))
        return tuple(out)

    init = tuple((score_fn(c, lo), jnp.zeros((rows, tk), BF16), jnp.ones((rows, 1), F32),
                  jnp.full((rows, 1), M_FLOOR, F32), jnp.zeros((rows, 1), F32), jnp.zeros((rows, dv), F32))
                 for c in range(chains))
    carry = lax.fori_loop(lo, hi, body, init)
    outs = []
    for c in range(chains):
        s, p_prev, a_prev, m, l, acc = carry[c]
        acc = flush(c, hi, s, p_prev, a_prev, acc)
        if last_fix is not None:
            s = last_fix(c, s)
        m, l, alpha, p = _softmax_step(m, l, s)
        acc = alpha * acc + jnp.dot(p, value_fn(c, hi), preferred_element_type=F32)
        outs.append(acc / jnp.maximum(l, 1e-30))
    return outs


def _attend_tiles(s_tiles, v_tiles):
    m = functools.reduce(jnp.maximum, [jnp.max(s, axis=1, keepdims=True) for s in s_tiles])
    l, acc = 0.0, 0.0
    for s, (v, feature_major) in zip(s_tiles, v_tiles):
        e = jnp.where(s > 0.5 * NEG, jnp.exp(s - m), 0.0)
        l = l + jnp.sum(e, axis=1, keepdims=True)
        eb = e.astype(BF16)
        acc = acc + (_dot_nt(eb, v) if feature_major else jnp.dot(eb, v, preferred_element_type=F32))
    return acc / jnp.maximum(l, 1e-30)


def _topn_mask(v, ncols, topn):
    jl = lax.broadcasted_iota(jnp.int32, v.shape, 1)
    rank = jnp.zeros(v.shape, F32)
    for j2 in range(ncols):
        col = v[:, j2:j2 + 1]
        beats = (col > v) | ((col == v) & (jl > j2))
        rank = rank + jnp.where(beats, 1.0, 0.0)
    return rank < topn


def _pad_rows(a, rows):
    return jnp.concatenate([a, jnp.zeros((rows - a.shape[0], a.shape[1]), a.dtype)], axis=0)


def _fox_prompt_kernel(q_ref, k_ref, v_ref, cr_ref, o_ref, *, tq):
    qi = pl.program_id(2)
    q0 = pl.multiple_of(qi * tq, tq)
    row = lax.broadcasted_iota(jnp.int32, (tq, tq), 0)
    col = lax.broadcasted_iota(jnp.int32, (tq, tq), 1)
    qs = [(q_ref[:, h2 * HEAD_DIM:(h2 + 1) * HEAD_DIM] * SCALE).astype(BF16) for h2 in range(2)]
    c0 = [cr_ref[0, h2, :, pl.ds(q0, LANE)][:, 0:1] for h2 in range(2)]

    def score(h2, kj):
        ks = pl.multiple_of(kj * tq, tq)
        k = k_ref[pl.ds(ks, tq), h2 * HEAD_DIM:(h2 + 1) * HEAD_DIM].astype(BF16)
        return _dot_nt(qs[h2], k) + (c0[h2] - cr_ref[0, h2, :, pl.ds(ks, tq)])

    def value(h2, kj):
        ks = pl.multiple_of(kj * tq, tq)
        return v_ref[pl.ds(ks, tq), h2 * HEAD_DIM:(h2 + 1) * HEAD_DIM].astype(BF16)

    outs = _pipelined_attention(2, 0, qi, score, value, tq, tq, HEAD_DIM,
                                last_fix=lambda h2, s: jnp.where(col <= row, s, NEG))
    o_ref[...] = jnp.concatenate(outs, axis=1)


def _fox_prompt(c, cumr, b, t, tq):
    nq = t // tq
    base = _COL['fox'] // LANE
    hp_n = N_HEADS // 2
    return pl.pallas_call(
        functools.partial(_fox_prompt_kernel, tq=tq), grid=(b, hp_n, nq),
        in_specs=[pl.BlockSpec((tq, LANE), lambda i, h, q: (i * nq + q, base + h)),
                  pl.BlockSpec((t, LANE), lambda i, h, q: (i, base + hp_n + h)),
                  pl.BlockSpec((t, LANE), lambda i, h, q: (i, base + 2 * hp_n + h)),
                  pl.BlockSpec((1, 2, 1, t), lambda i, h, q: (i, h, 0, 0))],
        out_specs=pl.BlockSpec((tq, LANE), lambda i, h, q: (i * nq + q, h)),
        out_shape=jax.ShapeDtypeStruct((b * t, MIX), F32),
        compiler_params=_cparams(("parallel", "arbitrary", "arbitrary")), name="fox_prompt")(
            c, c, c, cumr)


def _head_slope(hp, h2):
    s = jnp.float32(SLOPES[h2])
    for k in range(1, N_HEADS // 2):
        s = jnp.where(hp == k, jnp.float32(SLOPES[2 * k + h2]), s)
    return s


def _moba_prompt_kernel(q_ref, k_ref, v_ref, o_ref, kmean_ref, *, tq, nb):
    hp = pl.program_id(1)
    qi = pl.program_id(2)
    bpt = tq // MOBA_BLOCK

    @pl.when(qi == 0)
    def _():
        kmean_ref[...] = jnp.zeros_like(kmean_ref)
        for n in range(nb):
            kmean_ref[n:n + 1, :] = jnp.mean(k_ref[n * MOBA_BLOCK:(n + 1) * MOBA_BLOCK, :], axis=0, keepdims=True)

    q0 = pl.multiple_of(qi * tq, tq)
    row = lax.broadcasted_iota(jnp.int32, (tq, tq), 0)
    col = lax.broadcasted_iota(jnp.int32, (tq, tq), 1)
    colpos = lax.broadcasted_iota(jnp.int32, (1, tq), 1)
    jl = lax.broadcasted_iota(jnp.int32, (tq, LANE), 1)
    cur = qi * bpt + lax.broadcasted_iota(jnp.int32, (tq, 1), 0) // MOBA_BLOCK
    qs, slopes, blockbias = [], [], []
    for h2 in range(2):
        lo = h2 * HEAD_DIM
        qf = q_ref[:, lo:lo + HEAD_DIM]
        gate = _dot_nt(qf, kmean_ref[:, lo:lo + HEAD_DIM], precision=HI)
        gate = jnp.where(jl < cur, gate, NEG)
        picked = (_topn_mask(gate, nb, MOBA_TOPK) & (jl < cur)) | (jl == cur)
        qs.append((qf * SCALE).astype(BF16))
        slopes.append(_head_slope(hp, h2))
        blockbias.append(jnp.where(picked, 0.0, NEG))

    def score(h2, kj):
        ks = pl.multiple_of(kj * tq, tq)
        k = k_ref[pl.ds(ks, tq), h2 * HEAD_DIM:(h2 + 1) * HEAD_DIM].astype(BF16)
        bias = None
        for j in reversed(range(bpt)):
            rowbias = jnp.min(jnp.where(jl == kj * bpt + j, blockbias[h2], 0.0), axis=1, keepdims=True)
            bias = rowbias if bias is None else jnp.where(colpos < (j + 1) * MOBA_BLOCK, rowbias, bias)
        return _dot_nt(qs[h2], k) + slopes[h2] * (ks - q0 + colpos).astype(F32) + bias

    def value(h2, kj):
        ks = pl.multiple_of(kj * tq, tq)
        return v_ref[pl.ds(ks, tq), h2 * HEAD_DIM:(h2 + 1) * HEAD_DIM].astype(BF16)

    outs = _pipelined_attention(2, 0, qi, score, value, tq, tq, HEAD_DIM,
                                last_fix=lambda h2, s: jnp.where(col <= row, s, NEG))
    o_ref[...] = jnp.concatenate(outs, axis=1)


def _moba_prompt(c, b, t, tq):
    nq = t // tq
    base = _COL['moba'] // LANE
    hp_n = N_HEADS // 2
    return pl.pallas_call(
        functools.partial(_moba_prompt_kernel, tq=tq, nb=t // MOBA_BLOCK), grid=(b, hp_n, nq),
        in_specs=[pl.BlockSpec((tq, LANE), lambda i, h, q: (i * nq + q, base + h)),
                  pl.BlockSpec((t, LANE), lambda i, h, q: (i, base + hp_n + h)),
                  pl.BlockSpec((t, LANE), lambda i, h, q: (i, base + 2 * hp_n + h))],
        out_specs=pl.BlockSpec((tq, LANE), lambda i, h, q: (i * nq + q, h)),
        out_shape=jax.ShapeDtypeStruct((b * t, MIX), F32),
        scratch_shapes=[pltpu.VMEM((LANE, LANE), F32)],
        compiler_params=_cparams(("parallel", "arbitrary", "arbitrary")), name="moba_prompt")(c, c, c)


def _cmp_kernel(*refs, n_in, feature_major):
    if feature_major:
        xs_ref, refs = refs[-1], refs[:-1]
        x_refs = refs[-5 - n_in:-5]
        page = x_refs[0].shape[2]
        halves = x_refs[0].shape[1] // LANE
        for p, r in enumerate(x_refs):
            for j in range(halves):
                xs_ref[j, p * page:(p + 1) * page, :] = r[0, j * LANE:(j + 1) * LANE, :].T
        nchunk = n_in * page // CMP_STRIDE
        x = jnp.concatenate([xs_ref[j, pl.ds(l, nchunk, stride=CMP_STRIDE), :]
                             for l in range(CMP_STRIDE) for j in range(halves)], axis=1)
    else:
        x = refs[-6][0]
    pos_ref, wcat_ref, w0_ref, w1_ref, o_ref = refs[-5:]
    xb = x.astype(BF16)
    p0 = jnp.dot(xb, w0_ref[...], preferred_element_type=F32)
    p1 = jnp.dot(xb, w1_ref[...], preferred_element_type=F32)
    bias = jnp.dot(pos_ref[...].astype(BF16), wcat_ref[...], preferred_element_type=F32)[0:1]
    o_ref[0] = p0 + pltpu.roll(p1, p1.shape[0] - 1, 0) + bias


def _cmp_weights(pos, wk, wv):
    r = CMP_LEN // CMP_STRIDE
    w = jnp.stack([wk, wv]).reshape(2, r, CMP_STRIDE, HEAD_DIM, HEAD_DIM)
    e2 = jnp.eye(2, dtype=w.dtype)
    eg = jnp.eye(KV_G, dtype=w.dtype)
    big = jnp.einsum('krlde,kK,gG->rlkgdKGe', w, e2, eg)
    big = big.reshape(r, CMP_STRIDE * 2 * KV_G * HEAD_DIM, 2 * KV_G * HEAD_DIM).astype(BF16)
    wcat = jnp.concatenate([wk, wk, wv, wv], axis=1).astype(BF16)
    posb = jnp.zeros((SUBLANE, CMP_LEN * HEAD_DIM), F32).at[0].set(pos.reshape(-1))
    return posb, wcat, big[0], big[1]


def _const_specs(arrays):
    return [pl.BlockSpec(a.shape, lambda *_, nd=a.ndim: (0,) * nd) for a in arrays]


def _cmp_prompt(c, b, t, cw):
    kw = 2 * KV_G * HEAD_DIM
    nchunk = t // CMP_STRIDE
    x = c[:, _COL['kv_cmp']:_COL['kv_cmp'] + kw].reshape(b, nchunk, CMP_STRIDE * kw)
    return pl.pallas_call(
        functools.partial(_cmp_kernel, n_in=1, feature_major=False), grid=(b,),
        in_specs=[pl.BlockSpec((1, nchunk, CMP_STRIDE * kw), lambda i: (i, 0, 0))] + _const_specs(cw),
        out_specs=pl.BlockSpec((1, nchunk, kw), lambda i: (i, 0, 0)),
        out_shape=jax.ShapeDtypeStruct((b, nchunk, kw), F32),
        compiler_params=_cparams(("parallel",)), name="nsa_cmp_prompt")(x, *cw)


def _cmp_paged(page_table, pool, first, cw):
    bs, npages = page_table.shape
    kw, page = pool.shape[1:]
    nchunk = npages * page // CMP_STRIDE
    return pl.pallas_call(
        functools.partial(_cmp_kernel, n_in=npages, feature_major=True),
        grid_spec=pltpu.PrefetchScalarGridSpec(
            num_scalar_prefetch=1, grid=(bs,), in_specs=_page_specs(pool, npages, first) + _const_specs(cw),
            out_specs=pl.BlockSpec((1, nchunk, kw), lambda b, pt: (b, 0, 0)),
            scratch_shapes=[pltpu.VMEM((kw // LANE, npages * page, LANE), F32)]),
        out_shape=jax.ShapeDtypeStruct((bs, nchunk, kw), F32),
        compiler_params=_cparams(("parallel",)), name="nsa_cmp_paged")(page_table, *([pool] * npages), *cw)


def _overlap_matrix(nc, ns):
    i_c = np.arange(LANE)[:, None] * CMP_STRIDE
    j_s = np.arange(LANE)[None, :] * SLC_BLOCK
    ov = (i_c < j_s + SLC_BLOCK) & (i_c + CMP_LEN > j_s)
    ov &= (np.arange(LANE)[:, None] < nc) & (np.arange(LANE)[None, :] < ns)
    return jnp.asarray(ov, F32)


def _nsa_prompt_kernel(q_ref, slc_ref, win_ref, cmp_ref, g_ref, ov_ref, o_ref, *, tq, nc, ns):
    qi = pl.program_id(1)
    q0 = pl.multiple_of(qi * tq, tq)
    rows = HG * tq
    rl = lax.broadcasted_iota(jnp.int32, (tq, 1), 0)
    t1 = q0 + rl
    t4 = jnp.concatenate([t1] * HG, axis=0)
    lane = lax.broadcasted_iota(jnp.int32, (1, LANE), 1)
    colpos = lax.broadcasted_iota(jnp.int32, (1, tq), 1)
    dloc = lax.broadcasted_iota(jnp.int32, (tq, tq), 0) - lax.broadcasted_iota(jnp.int32, (tq, tq), 1)
    sg = _sigmoid(g_ref[...])
    jl = lax.broadcasted_iota(jnp.int32, (tq, LANE), 1)
    cur = t1 // SLC_BLOCK
    erow = lax.broadcasted_iota(jnp.int32, (LANE, tq), 0)
    ecol = lax.broadcasted_iota(jnp.int32, (LANE, tq), 1)
    wtiles = WINDOW // tq
    qs, slope, o_cmp, blockbias = [], [], [], []
    for g in range(KV_G):
        kl = g * HEAD_DIM
        vl = KV_G * HEAD_DIM + g * HEAD_DIM
        qg = jnp.concatenate([q_ref[:, (g * HG + h) * HEAD_DIM:(g * HG + h + 1) * HEAD_DIM] for h in range(HG)],
                             axis=0)
        qs.append((qg * SCALE).astype(BF16))
        slope.append(jnp.concatenate([jnp.full((tq, 1), SLOPES[g * HG + h], F32) for h in range(HG)], axis=0))

        kc = cmp_ref[0, :, kl:kl + HEAD_DIM].astype(BF16)
        vc = cmp_ref[0, :, vl:vl + HEAD_DIM].astype(BF16)
        dist = (t4 - (lane * CMP_STRIDE + CMP_LEN - 1)).astype(F32)
        ok_c = (dist >= 0) & (lane < nc)
        s = jnp.where(ok_c, _dot_nt(qs[g], kc) - slope[g] * dist, NEG)
        m = jnp.max(s, axis=1, keepdims=True)
        e = jnp.where(ok_c, jnp.exp(s - m), 0.0)
        p_c = e / jnp.maximum(jnp.sum(e, axis=1, keepdims=True), 1e-30)
        o_cmp.append(jnp.dot(p_c.astype(BF16), vc, preferred_element_type=F32))

        psum = p_c[0:tq]
        for h in range(1, HG):
            psum = psum + p_c[h * tq:(h + 1) * tq]
        imp = jnp.dot(psum, ov_ref[...], precision=HI, preferred_element_type=F32)
        forced = (jl == 0) | (jl == cur) | (jl == cur - 1)
        imp = jnp.where(forced, FORCE, imp)
        imp = jnp.where(jl <= cur, imp, NEG)
        picked = _topn_mask(imp, ns, min(SLC_TOPN, ns)) & (jl <= cur)
        blockbias.append(jnp.where(picked, 0.0, NEG).astype(BF16))

    def scores(ref, g, ks, bias):
        k = ref[pl.ds(ks, tq), g * HEAD_DIM:(g + 1) * HEAD_DIM].astype(BF16)
        cpos = (ks - q0 + colpos).astype(F32)
        return _dot_nt(qs[g], k) + slope[g] * cpos + jnp.concatenate([bias] * HG, axis=0)

    def values(ref, g, kj):
        ks = pl.multiple_of(kj * tq, tq)
        vl = KV_G * HEAD_DIM + g * HEAD_DIM
        return ref[pl.ds(ks, tq), vl:vl + HEAD_DIM].astype(BF16)

    def slc_score(g, kj):
        ks = pl.multiple_of(kj * tq, tq)
        expand = jnp.where((ks + ecol) // SLC_BLOCK == erow, 1.0, 0.0).astype(BF16)
        return scores(slc_ref, g, ks, jnp.dot(blockbias[g], expand, preferred_element_type=F32))

    def win_score(g, kj):
        ks = pl.multiple_of(kj * tq, tq)
        d = dloc + (q0 - ks)
        return scores(win_ref, g, ks, jnp.where((d >= 0) & (d < WINDOW), 0.0, NEG))

    causal = jnp.concatenate([dloc] * HG, axis=0) >= 0
    o_slc = _pipelined_attention(KV_G, 0, qi, slc_score, functools.partial(values, slc_ref), rows, tq, HEAD_DIM,
                                 last_fix=lambda g, s: jnp.where(causal, s, NEG))
    o_win = _pipelined_attention(KV_G, jnp.maximum(qi - wtiles, 0), qi, win_score,
                                 functools.partial(values, win_ref), rows, tq, HEAD_DIM)

    for g in range(KV_G):
        for h in range(HG):
            hh = g * HG + h
            r0 = h * tq
            o = (sg[:, hh:hh + 1] * o_cmp[g][r0:r0 + tq]
                 + sg[:, N_HEADS + hh:N_HEADS + hh + 1] * o_slc[g][r0:r0 + tq]
                 + sg[:, 2 * N_HEADS + hh:2 * N_HEADS + hh + 1] * o_win[g][r0:r0 + tq])
            o_ref[:, hh * HEAD_DIM:(hh + 1) * HEAD_DIM] = o


def _nsa_prompt(c, kvc, ov, b, t, tq):
    nq = t // tq
    kw = 2 * KV_G * HEAD_DIM
    nchunk = t // CMP_STRIDE
    nc = nchunk - CMP_LEN // CMP_STRIDE + 1
    ns = -(-t // SLC_BLOCK)
    return pl.pallas_call(
        functools.partial(_nsa_prompt_kernel, tq=tq, nc=nc, ns=ns), grid=(b, nq),
        in_specs=[pl.BlockSpec((tq, MIX), lambda i, q: (i * nq + q, _COL['nsa_q'] // MIX)),
                  pl.BlockSpec((t, kw), lambda i, q: (i, _COL['kv_slc'] // kw)),
                  pl.BlockSpec((t, kw), lambda i, q: (i, _COL['kv_win'] // kw)),
                  pl.BlockSpec((1, nchunk, kw), lambda i, q: (i, 0, 0)),
                  pl.BlockSpec((tq, LANE), lambda i, q: (i * nq + q, _COL['nsa_gate'] // LANE)),
                  pl.BlockSpec((LANE, LANE), lambda i, q: (0, 0))],
        out_specs=pl.BlockSpec((tq, MIX), lambda i, q: (i * nq + q, 0)),
        out_shape=jax.ShapeDtypeStruct((b * t, MIX), F32),
        compiler_params=_cparams(("parallel", "arbitrary")), name="nsa_prompt")(c, c, c, kvc, c, ov)


def _dec_rows():
    rid = lax.broadcasted_iota(jnp.int32, (N_HEADS * T_PAD, 1), 0)
    t8 = rid % T_PAD
    slope = jnp.concatenate([jnp.full((T_PAD, 1), SLOPES[h], F32) for h in range(N_HEADS)], axis=0)
    return t8, slope


def _expand_heads(q8):
    lane = lax.broadcasted_iota(jnp.int32, q8.shape, 1)
    return jnp.concatenate([jnp.where(lane // HEAD_DIM == h, q8, 0.0) for h in range(N_HEADS)], axis=0)


def _collapse_heads(res):
    lane = lax.broadcasted_iota(jnp.int32, (T_PAD, res.shape[1]), 1)
    out = jnp.zeros((T_PAD, res.shape[1]), F32)
    for h in range(N_HEADS):
        out = out + jnp.where(lane // HEAD_DIM == h, res[h * T_PAD:(h + 1) * T_PAD], 0.0)
    return out


def _moba_dec_kernel(pt_ref, q_ref, kn_ref, vn_ref, *rest, npages, past):
    page_refs, o_ref = rest[:npages], rest[npages]
    page = page_refs[0].shape[2]
    t8, slope = _dec_rows()
    qpos = past + t8
    lane = lax.broadcasted_iota(jnp.int32, (1, LANE), 1)
    q8 = q_ref[0]
    qx_f = _expand_heads(q8)
    qx = (qx_f * SCALE).astype(BF16)
    per_blk = MOBA_BLOCK // page
    nb_past = past // MOBA_BLOCK
    lanei = lax.broadcasted_iota(jnp.int32, (MIX, LANE), 1)
    kmean = jnp.zeros((MIX, LANE), F32)
    for n in range(nb_past):
        tot = page_refs[n * per_blk][0, 0:MIX, :]
        for p in range(n * per_blk + 1, (n + 1) * per_blk):
            tot = tot + page_refs[p][0, 0:MIX, :]
        col = jnp.sum(tot, axis=1, keepdims=True) * (1.0 / MOBA_BLOCK)
        kmean = kmean + jnp.where(lanei == n, col, 0.0)
    gate = jnp.dot(qx_f, kmean, precision=HI, preferred_element_type=F32)
    jl = lax.broadcasted_iota(jnp.int32, gate.shape, 1)
    cur = qpos // MOBA_BLOCK
    gate = jnp.where(jl < cur, gate, NEG)
    nb = -(-(past + T_PAD) // MOBA_BLOCK)
    sel = jnp.where(_topn_mask(gate, nb, max(1, min(MOBA_TOPK, nb - 1))) & (jl < cur), 1.0, 0.0)
    s_tiles, v_tiles = [], []
    for p in range(npages):
        kt = page_refs[p][0, 0:MIX, :].astype(BF16)
        v_tiles.append((page_refs[p][0, MIX:2 * MIX, :].astype(BF16), True))
        n = (p * page) // MOBA_BLOCK
        d = (qpos - (p * page + lane)).astype(F32)
        valid = (sel[:, n:n + 1] > 0.5) & (d >= 0)
        s_tiles.append(jnp.where(valid, jnp.dot(qx, kt, preferred_element_type=F32) - slope * d, NEG))
    k = _pad_rows(kn_ref[0], LANE).astype(BF16)
    v_tiles.append((_pad_rows(vn_ref[0], LANE).astype(BF16), False))
    d = (t8 - lane).astype(F32)
    valid = (d >= 0) & (lane < T_PAD)
    s_tiles.append(jnp.where(valid, _dot_nt(qx, k) - slope * d, NEG))
    o_ref[0] = _collapse_heads(_attend_tiles(s_tiles, v_tiles))


def _fox_dec_kernel(pt_ref, q_ref, kn_ref, vn_ref, ln_ref, *rest, npages, past):
    logf_refs, page_refs, o_ref = rest[:npages], rest[npages:2 * npages], rest[2 * npages]
    t8, _ = _dec_rows()
    lane = lax.broadcasted_iota(jnp.int32, (1, LANE), 1)
    qx = (_expand_heads(q_ref[0]) * SCALE).astype(BF16)
    r = lax.broadcasted_iota(jnp.int32, (LANE, LANE), 0)
    c = lax.broadcasted_iota(jnp.int32, (LANE, LANE), 1)
    tri = jnp.where(r <= c, 1.0, 0.0).astype(F32)

    def head_rows(x):
        return jnp.concatenate([jnp.broadcast_to(x[h:h + 1], (T_PAD, x.shape[1])) for h in range(N_HEADS)], axis=0)

    carry = jnp.zeros((N_HEADS, 1), F32)
    cum = []
    for ref in list(logf_refs) + [ln_ref]:
        cs = jnp.dot(ref[0], tri, precision=HI, preferred_element_type=F32) + carry
        cum.append(cs)
        carry = cs[:, LANE - 1:LANE]
    c_ref = head_rows(cum[npages - 1][:, LANE - 1:LANE])

    s_tiles, v_tiles = [], []
    for p in range(npages):
        kt = page_refs[p][0, 0:MIX, :].astype(BF16)
        v_tiles.append((page_refs[p][0, MIX:2 * MIX, :].astype(BF16), True))
        s_tiles.append(jnp.dot(qx, kt, preferred_element_type=F32) + (c_ref - head_rows(cum[p])))
    k = _pad_rows(kn_ref[0], LANE).astype(BF16)
    v_tiles.append((_pad_rows(vn_ref[0], LANE).astype(BF16), False))
    valid = (lane <= t8) & (lane < T_PAD)
    s_tiles.append(jnp.where(valid, _dot_nt(qx, k) + (c_ref - head_rows(cum[npages])), NEG))
    o_ref[0] = _collapse_heads(_attend_tiles(s_tiles, v_tiles))


def _page_specs(pool, npages, first):
    blk = (1,) + pool.shape[1:]
    return [pl.BlockSpec(blk, lambda b, pt, p=p: (first + pt[b, p], 0, 0)) for p in range(npages)]


def _paged_mha_dec(kernel, name, c3, col, page_table, pool, first, extra_args, extra_specs, past):
    bs, npages = page_table.shape
    qb = col // MIX
    in_specs = [pl.BlockSpec((1, T_PAD, MIX), lambda b, pt: (b, 0, qb)),
                pl.BlockSpec((1, T_PAD, MIX), lambda b, pt: (b, 0, qb + 1)),
                pl.BlockSpec((1, T_PAD, MIX), lambda b, pt: (b, 0, qb + 2))]
    in_specs += extra_specs
    in_specs += _page_specs(pool, npages, first)
    return pl.pallas_call(
        functools.partial(kernel, npages=npages, past=past),
        grid_spec=pltpu.PrefetchScalarGridSpec(
            num_scalar_prefetch=1, grid=(bs,), in_specs=in_specs,
            out_specs=pl.BlockSpec((1, T_PAD, MIX), lambda b, pt: (b, 0, 0))),
        out_shape=jax.ShapeDtypeStruct((bs, T_PAD, MIX), F32),
        compiler_params=_cparams(("parallel",)), name=name)(
            page_table, c3, c3, c3, *extra_args, *([pool] * npages))


def _nsa_dec_kernel(pt_ref, q_ref, ns_ref, nw_ref, g_ref, cmp_ref, ov_ref, wc_ref, *rest, npages, past, nc, ns):
    page_refs, o_ref = rest[:npages], rest[npages]
    page = page_refs[0].shape[2]
    kw = KV_G * HEAD_DIM
    t8, slope = _dec_rows()
    qpos = past + t8
    lane = lax.broadcasted_iota(jnp.int32, (1, LANE), 1)
    lane8 = lax.broadcasted_iota(jnp.int32, (T_PAD, LANE), 1)
    q8 = q_ref[0] * SCALE
    rows = []
    for h in range(N_HEADS):
        x = q8[:, (h // 2) * LANE:(h // 2 + 1) * LANE]
        dst = h // HG
        if h % 2 != dst:
            x = pltpu.roll(x, HEAD_DIM, 1)
        rows.append(jnp.where((lane8 // HEAD_DIM) == dst, x, 0.0))
    qx = jnp.concatenate(rows, axis=0).astype(BF16)

    cm = cmp_ref[0]
    dist = (qpos - (lane * CMP_STRIDE + CMP_LEN - 1)).astype(F32)
    ok_c = (dist >= 0) & (lane < nc)
    s = jnp.where(ok_c, _dot_nt(qx, cm[:, 0:kw].astype(BF16)) - slope * dist, NEG)
    m = jnp.max(s, axis=1, keepdims=True)
    e = jnp.where(ok_c, jnp.exp(s - m), 0.0)
    p_c = e / jnp.maximum(jnp.sum(e, axis=1, keepdims=True), 1e-30)
    o_cmp = jnp.dot(p_c.astype(BF16), cm[:, kw:2 * kw].astype(BF16), preferred_element_type=F32)

    psum = []
    for g in range(KV_G):
        acc = p_c[g * HG * T_PAD:g * HG * T_PAD + T_PAD]
        for h in range(1, HG):
            r0 = (g * HG + h) * T_PAD
            acc = acc + p_c[r0:r0 + T_PAD]
        psum.append(acc)
    imp = jnp.dot(jnp.concatenate(psum, axis=0), ov_ref[...], precision=HI, preferred_element_type=F32)
    jl = lax.broadcasted_iota(jnp.int32, imp.shape, 1)
    tg = lax.broadcasted_iota(jnp.int32, (KV_G * T_PAD, 1), 0) % T_PAD
    cur = (past + tg) // SLC_BLOCK
    forced = (jl == 0) | (jl == cur) | (jl == cur - 1)
    imp = jnp.where(forced, FORCE, imp)
    imp = jnp.where(jl <= cur, imp, NEG)
    sel = jnp.where(_topn_mask(imp, ns, min(SLC_TOPN, ns)) & (jl <= cur), 1.0, 0.0)
    sel_rows = jnp.concatenate([sel[(h // HG) * T_PAD:(h // HG + 1) * T_PAD] for h in range(N_HEADS)], axis=0)

    def new_tile(ref, extra_valid):
        k = _pad_rows(ref[0, :, 0:kw], LANE).astype(BF16)
        v = _pad_rows(ref[0, :, kw:2 * kw], LANE).astype(BF16)
        d = (t8 - lane).astype(F32)
        valid = (d >= 0) & (lane < T_PAD) & extra_valid
        return jnp.where(valid, _dot_nt(qx, k) - slope * d, NEG), (v, False)

    s_tiles, v_tiles = [], []
    per = page // SLC_BLOCK
    for p in range(npages):
        kt = page_refs[p][0, 0:kw, :].astype(BF16)
        v_tiles.append((page_refs[p][0, kw:2 * kw, :].astype(BF16), True))
        picked = jnp.zeros((N_HEADS * T_PAD, LANE), jnp.bool_)
        for j in range(per):
            blk = p * per + j
            picked = picked | ((lane // SLC_BLOCK == j) & (sel_rows[:, blk:blk + 1] > 0.5))
        d = (qpos - (p * page + lane)).astype(F32)
        s_tiles.append(jnp.where(picked & (d >= 0), jnp.dot(qx, kt, preferred_element_type=F32) - slope * d, NEG))
    blk_new = past // SLC_BLOCK
    s_new, v_new = new_tile(ns_ref, sel_rows[:, blk_new:blk_new + 1] > 0.5)
    o_slc = _attend_tiles(s_tiles + [s_new], v_tiles + [v_new])

    s_tiles, v_tiles = [], []
    wb = wc_ref.shape[2]
    w_off = past - wb
    for j in range(wb // LANE):
        kt = wc_ref[0, 0:kw, j * LANE:(j + 1) * LANE].astype(BF16)
        v_tiles.append((wc_ref[0, kw:2 * kw, j * LANE:(j + 1) * LANE].astype(BF16), True))
        d = qpos - (w_off + j * LANE + lane)
        valid = (d >= 0) & (d < WINDOW)
        s_tiles.append(jnp.where(valid, jnp.dot(qx, kt, preferred_element_type=F32) - slope * d.astype(F32), NEG))
    s_new, v_new = new_tile(nw_ref, True)
    o_win = _attend_tiles(s_tiles + [s_new], v_tiles + [v_new])

    sg = _sigmoid(g_ref[0])
    for h in range(N_HEADS):
        r0, l0 = h * T_PAD, (h // HG) * HEAD_DIM
        o = (sg[:, h:h + 1] * o_cmp[r0:r0 + T_PAD, l0:l0 + HEAD_DIM]
             + sg[:, N_HEADS + h:N_HEADS + h + 1] * o_slc[r0:r0 + T_PAD, l0:l0 + HEAD_DIM]
             + sg[:, 2 * N_HEADS + h:2 * N_HEADS + h + 1] * o_win[r0:r0 + T_PAD, l0:l0 + HEAD_DIM])
        o_ref[0, :, h * HEAD_DIM:(h + 1) * HEAD_DIM] = o


def _nsa_dec(c3, kvc, ov, win_cache, win_first, page_table, pool, first, past, t_real):
    bs, npages = page_table.shape
    kw = 2 * KV_G * HEAD_DIM
    nchunk = kvc.shape[1]
    nc = nchunk - CMP_LEN // CMP_STRIDE + 1
    ns = -(-(past + t_real) // SLC_BLOCK)
    wb = win_cache.shape[2]
    in_specs = [pl.BlockSpec((1, T_PAD, MIX), lambda b, pt: (b, 0, _COL['nsa_q'] // MIX)),
                pl.BlockSpec((1, T_PAD, kw), lambda b, pt: (b, 0, _COL['kv_slc'] // kw)),
                pl.BlockSpec((1, T_PAD, kw), lambda b, pt: (b, 0, _COL['kv_win'] // kw)),
                pl.BlockSpec((1, T_PAD, LANE), lambda b, pt: (b, 0, _COL['nsa_gate'] // LANE)),
                pl.BlockSpec((1, nchunk, kw), lambda b, pt: (b, 0, 0)),
                pl.BlockSpec((LANE, LANE), lambda b, pt: (0, 0)),
                pl.BlockSpec((1, kw, wb), lambda b, pt: (win_first + b, 0, 0))]
    in_specs += _page_specs(pool, npages, first)
    return pl.pallas_call(
        functools.partial(_nsa_dec_kernel, npages=npages, past=past, nc=nc, ns=ns),
        grid_spec=pltpu.PrefetchScalarGridSpec(
            num_scalar_prefetch=1, grid=(bs,), in_specs=in_specs,
            out_specs=pl.BlockSpec((1, T_PAD, MIX), lambda b, pt: (b, 0, 0))),
        out_shape=jax.ShapeDtypeStruct((bs, T_PAD, MIX), F32),
        compiler_params=_cparams(("parallel",)), name="nsa_decode")(
            page_table, c3, c3, c3, c3, kvc, ov, win_cache, *([pool] * npages))


def _s5_disc_kernel(ar_ref, ai_ref, ls_ref, btr_ref, bti_ref, abr_ref, abi_ref, bbr_ref, bbi_ref):
    ar, ai = ar_ref[...], ai_ref[...]
    step = jnp.exp(ls_ref[...])
    mag = jnp.exp(ar * step)
    abr = mag * jnp.cos(ai * step)
    abi = mag * jnp.sin(ai * step)
    den = ar * ar + ai * ai
    zr = (ar * (abr - 1.0) + ai * abi) / den
    zi = (ar * abi - ai * (abr - 1.0)) / den
    abr_ref[...] = abr
    abi_ref[...] = abi
    btr, bti = btr_ref[...], bti_ref[...]
    bbr_ref[...] = zr * btr - zi * bti
    bbi_ref[...] = zr * bti + zi * btr


def _s5_disc(a_re, a_im, log_step, b_re, b_im):
    rep = lambda a: jnp.repeat(a, S5_CH, axis=0)
    n = S5_GROUPS * S5_CH
    args = (rep(a_re), rep(a_im), rep(jnp.broadcast_to(log_step[:, None], (S5_GROUPS, S5_STATE))),
            b_re.transpose(0, 2, 1).reshape(n, S5_STATE), b_im.transpose(0, 2, 1).reshape(n, S5_STATE))
    shp = jax.ShapeDtypeStruct((n, S5_STATE), F32)
    return pl.pallas_call(_s5_disc_kernel, out_shape=(shp, shp, shp, shp), name="s5_discretise")(*args)


_S5_CH_ROWS = 128


def _s5_scan_kernel(*refs, seg, has_h0):
    if has_h0:
        (u_ref, bre_ref, bim_ref, ar_ref, ai_ref, cre_ref, cim_ref, d_ref, h0r_ref, h0i_ref,
         y_ref, hlr_ref, hli_ref, hr_s, hi_s) = refs
    else:
        (u_ref, bre_ref, bim_ref, ar_ref, ai_ref, cre_ref, cim_ref, d_ref,
         y_ref, hlr_ref, hli_ref, hr_s, hi_s) = refs
    rows = u_ref.shape[0]
    ch = min(_S5_CH_ROWS, rows)
    pad = ch
    nch = rows // ch
    ar, ai = ar_ref[0], ai_ref[0]
    bre, bim, cre, cim = (_split_bf16(r[0]) for r in (bre_ref, bim_ref, cre_ref, cim_ref))
    hr_s[0:pad, :] = jnp.zeros((pad, hr_s.shape[1]), F32)
    hi_s[0:pad, :] = jnp.zeros((pad, hi_s.shape[1]), F32)

    def init_body(i, _):
        r0 = pl.multiple_of(i * ch, ch)
        u = u_ref[pl.ds(r0, ch), :]
        br = _dot3(u, bre)
        bi = _dot3(u, bim)
        if has_h0:
            h0r, h0i = h0r_ref[pl.ds(r0, ch), :], h0i_ref[pl.ds(r0, ch), :]
            br = br + (ar * h0r - ai * h0i)
            bi = bi + (ar * h0i + ai * h0r)
        hr_s[pl.ds(pad + r0, ch), :] = br
        hi_s[pl.ds(pad + r0, ch), :] = bi
        return 0

    lax.fori_loop(0, nch, init_body, 0)

    rl = lax.broadcasted_iota(jnp.int32, (ch, 1), 0)
    pr, pi = ar, ai
    d = 1
    while d < seg:
        first = d // ch

        def pass_body(i, _, d=d, pr=pr, pi=pi):
            r0 = pl.multiple_of((nch - 1 - i) * ch, ch)
            cr = hr_s[pl.ds(pad + r0, ch), :]
            ci = hi_s[pl.ds(pad + r0, ch), :]
            if d < SUBLANE:
                lo = pad - SUBLANE
                sr = pltpu.roll(hr_s[pl.ds(lo + r0, ch + SUBLANE), :], d, 0)[SUBLANE:]
                si = pltpu.roll(hi_s[pl.ds(lo + r0, ch + SUBLANE), :], d, 0)[SUBLANE:]
            else:
                sr = hr_s[pl.ds(pad + r0 - d, ch), :]
                si = hi_s[pl.ds(pad + r0 - d, ch), :]
            if d < ch or seg < rows:
                keep = ((r0 + rl) % seg) >= d
                sr = jnp.where(keep, sr, 0.0)
                si = jnp.where(keep, si, 0.0)
            hr_s[pl.ds(pad + r0, ch), :] = cr + (pr * sr - pi * si)
            hi_s[pl.ds(pad + r0, ch), :] = ci + (pr * si + pi * sr)
            return 0

        lax.fori_loop(0, nch - first, pass_body, 0)
        pr, pi = pr * pr - pi * pi, 2.0 * pr * pi
        d *= 2

    def out_body(i, _):
        r0 = pl.multiple_of(i * ch, ch)
        hr = hr_s[pl.ds(pad + r0, ch), :]
        hi = hi_s[pl.ds(pad + r0, ch), :]
        y = _dot3(hr, cre) - _dot3(hi, cim)
        y_ref[pl.ds(r0, ch), :] = y + d_ref[0] * u_ref[pl.ds(r0, ch), :]
        return 0

    lax.fori_loop(0, nch, out_body, 0)
    nl = hlr_ref.shape[0]
    hlr_ref[...] = hr_s[pad + rows - nl:pad + rows, :]
    hli_ref[...] = hi_s[pad + rows - nl:pad + rows, :]


def _s5_scan(c, sw, rows, seg, h0=None):
    n = c.shape[0]
    nt = n // rows
    gl = LANE // S5_CH
    lt = S5_GROUPS // gl
    w = gl * S5_STATE
    nl = (rows // seg) * SUBLANE if seg == SUBLANE else SUBLANE
    ub = _COL['s5_u'] // LANE
    in_specs = [pl.BlockSpec((rows, LANE), lambda i, j: (i, ub + j)),
                pl.BlockSpec((1, LANE, w), lambda i, j: (j, 0, 0)),
                pl.BlockSpec((1, LANE, w), lambda i, j: (j, 0, 0)),
                pl.BlockSpec((1, 1, w), lambda i, j: (j, 0, 0)),
                pl.BlockSpec((1, 1, w), lambda i, j: (j, 0, 0)),
                pl.BlockSpec((1, w, LANE), lambda i, j: (j, 0, 0)),
                pl.BlockSpec((1, w, LANE), lambda i, j: (j, 0, 0)),
                pl.BlockSpec((1, 1, LANE), lambda i, j: (j, 0, 0))]
    args = [c, sw['bre'], sw['bim'], sw['ar'], sw['ai'], sw['cre'], sw['cim'], sw['d']]
    if h0 is not None:
        in_specs += [pl.BlockSpec((rows, w), lambda i, j: (i, j))] * 2
        args += list(h0)
    hshape = jax.ShapeDtypeStruct((nt * nl, S5_GROUPS * S5_STATE), F32)
    return pl.pallas_call(
        functools.partial(_s5_scan_kernel, seg=seg, has_h0=h0 is not None), grid=(nt, lt),
        in_specs=in_specs,
        out_specs=(pl.BlockSpec((rows, LANE), lambda i, j: (i, j)),
                   pl.BlockSpec((nl, w), lambda i, j: (i, j)), pl.BlockSpec((nl, w), lambda i, j: (i, j))),
        out_shape=(jax.ShapeDtypeStruct((n, MIX), F32), hshape, hshape),
        scratch_shapes=[pltpu.VMEM((min(_S5_CH_ROWS, rows) + rows, w), F32)] * 2,
        compiler_params=_cparams(("parallel", "arbitrary")), name="s5_scan")(*args)


def _s5_weights(lp):
    abr, abi, bbr, bbi = _s5_disc(lp['s5_a_re'], lp['s5_a_im'], lp['s5_log_step'], lp['s5_b_re'], lp['s5_b_im'])
    gl = LANE // S5_CH
    lt = S5_GROUPS // gl
    eye = jnp.eye(gl, dtype=F32)

    def bdiag(bb):
        return jnp.einsum('jgcn,gh->jgchn', bb.reshape(lt, gl, S5_CH, S5_STATE), eye).reshape(
            lt, gl * S5_CH, gl * S5_STATE)

    def cdiag(cc):
        return jnp.einsum('jgcn,gh->jgnhc', cc.reshape(lt, gl, S5_CH, S5_STATE), eye).reshape(
            lt, gl * S5_STATE, gl * S5_CH)

    return dict(bre=bdiag(bbr), bim=bdiag(bbi),
                ar=abr[::S5_CH].reshape(lt, 1, gl * S5_STATE), ai=abi[::S5_CH].reshape(lt, 1, gl * S5_STATE),
                cre=cdiag(lp['s5_c_re']), cim=cdiag(lp['s5_c_im']), d=lp['s5_d'].reshape(lt, 1, LANE))


def _glu_kernel(y_ref, w_ref, o_ref):
    y = y_ref[...]
    g = 0.5 * y * (1.0 + jnp.tanh(np.float32(np.sqrt(2.0 / np.pi)) * (y + np.float32(0.044715) * (y * y * y))))
    z = jnp.dot(g.astype(BF16), w_ref[...], preferred_element_type=F32)
    o_ref[...] = z[:, :MIX] * _sigmoid(z[:, MIX:])


def _glu(y, w, tm):
    n = y.shape[0]
    return pl.pallas_call(
        _glu_kernel, grid=(n // tm,),
        in_specs=[pl.BlockSpec((tm, MIX), lambda i: (i, 0)), pl.BlockSpec((MIX, 2 * MIX), lambda i: (0, 0))],
        out_specs=pl.BlockSpec((tm, MIX), lambda i: (i, 0)),
        out_shape=jax.ShapeDtypeStruct((n, MIX), F32),
        compiler_params=_cparams(("parallel",)), name="s5_glu")(y, w)


def _layer_norm(x, g, b):
    mu = jnp.mean(x, axis=-1, keepdims=True)
    xc = x - mu
    var = jnp.mean(xc * xc, axis=-1, keepdims=True)
    return xc * lax.rsqrt(var + LN_EPS) * g + b


def _merge_kernel(o0_ref, o1_ref, o2_ref, o3_ref, mg_ref, wb_ref, x_ref, wo_ref, g_ref, b_ref, out_ref, acc_ref,
                  *, alpha):
    i = pl.program_id(1)

    @pl.when(i == 0)
    def _():
        acc_ref[...] = jnp.zeros_like(acc_ref)

    for k, o_ref in enumerate((o0_ref, o1_ref, o2_ref, o3_ref)):
        @pl.when(i == k)
        def _(o_ref=o_ref):
            proj = jnp.dot(o_ref[...].astype(BF16), wb_ref[0], preferred_element_type=F32)
            acc_ref[...] += _sigmoid(mg_ref[...]) * proj

    @pl.when(i == N_BRANCH - 1)
    def _():
        mixed = jnp.dot(acc_ref[...].astype(BF16), wo_ref[...], preferred_element_type=F32)
        out_ref[...] = _layer_norm(alpha * x_ref[...] + mixed, g_ref[...], b_ref[...])


def _merge(outs, c, x, wb, wo, g, b, tm, alpha):
    n = x.shape[0]
    o_spec = pl.BlockSpec((tm, MIX), lambda r, i: (r, 0))
    return pl.pallas_call(
        functools.partial(_merge_kernel, alpha=alpha), grid=(n // tm, N_BRANCH),
        in_specs=[o_spec, o_spec, o_spec, o_spec,
                  pl.BlockSpec((tm, D_MODEL), lambda r, i: (r, i)),
                  pl.BlockSpec((1, MIX, D_MODEL), lambda r, i: (i, 0, 0)),
                  pl.BlockSpec((tm, D_MODEL), lambda r, i: (r, 0)),
                  pl.BlockSpec((D_MODEL, D_MODEL), lambda r, i: (0, 0), pipeline_mode=pl.Buffered(1)),
                  pl.BlockSpec((1, D_MODEL), lambda r, i: (0, 0)),
                  pl.BlockSpec((1, D_MODEL), lambda r, i: (0, 0))],
        out_specs=pl.BlockSpec((tm, D_MODEL), lambda r, i: (r, 0)),
        out_shape=jax.ShapeDtypeStruct((n, D_MODEL), F32),
        scratch_shapes=[pltpu.VMEM((tm, D_MODEL), F32)],
        compiler_params=_cparams(("parallel", "arbitrary")), name="merge_out_ln")(
            *outs, c, wb, x, wo, g, b)


def _ffn_kernel(*refs, halo, seg, alpha):
    (h_ref, wg_ref, wv_ref, cwg_ref, cwv_ref, cbg_ref, cbv_ref, wd_ref, lg_ref, lb_ref) = refs[:10]
    pg_ref, pv_ref, out_ref, hb_ref, acc_ref = refs[10:]
    j = pl.program_id(1)

    @pl.when(j == 0)
    def _():
        hb_ref[...] = h_ref[...].astype(BF16)
        acc_ref[...] = jnp.zeros_like(acc_ref)

    hb = hb_ref[...]
    tm = hb.shape[0]
    rid = lax.broadcasted_iota(jnp.int32, (tm, 1), 0)

    def conv(u, cw_ref, cb_ref, prev):
        r1 = pltpu.roll(u, 1, 0)
        r2 = pltpu.roll(u, 2, 0)
        if halo:
            p = prev[0][0]
            p6, p7 = p[SUBLANE - 2:SUBLANE - 1], p[SUBLANE - 1:SUBLANE]
            u1 = jnp.where(rid == 0, p7, r1)
            u2 = jnp.where(rid == 0, p6, jnp.where(rid == 1, p7, r2))
        else:
            t = rid % seg
            state = jnp.where(t >= seg - (CONV_W - 1), prev[0][...], u)
            u1 = jnp.where(t >= 1, r1, pltpu.roll(state, tm - (seg - 1), 0))
            u2 = jnp.where(t >= 2, r2, pltpu.roll(state, tm - (seg - 2), 0))
        cw = cw_ref[...]
        return cb_ref[...] + (cw[0:1] * u2 + cw[1:2] * u1 + cw[2:3] * u)

    ug = jnp.dot(hb, wg_ref[...], preferred_element_type=F32)
    uv = jnp.dot(hb, wv_ref[...], preferred_element_type=F32)
    gate = conv(ug, cwg_ref, cbg_ref, (pg_ref,))
    val = conv(uv, cwv_ref, cbv_ref, (pv_ref,))
    act = gate * _sigmoid(gate) * val
    acc_ref[...] += jnp.dot(act.astype(BF16), wd_ref[...], preferred_element_type=F32)

    @pl.when(j == pl.num_programs(1) - 1)
    def _():
        out_ref[...] = _layer_norm(alpha * h_ref[...] + acc_ref[...], lg_ref[...], lb_ref[...])


def _ffn(h, lw, tm, tf, alpha, prev=None, state_rows=None, seg=None):
    n = h.shape[0]
    nf = D_FF // tf
    halo = prev is not None
    in_specs = [pl.BlockSpec((tm, D_MODEL), lambda r, j: (r, 0), pipeline_mode=pl.Buffered(1)),
                pl.BlockSpec((D_MODEL, tf), lambda r, j: (0, j)),
                pl.BlockSpec((D_MODEL, tf), lambda r, j: (0, nf + j)),
                pl.BlockSpec((CONV_W, tf), lambda r, j: (0, j)),
                pl.BlockSpec((CONV_W, tf), lambda r, j: (0, nf + j)),
                pl.BlockSpec((1, tf), lambda r, j: (0, j)),
                pl.BlockSpec((1, tf), lambda r, j: (0, nf + j)),
                pl.BlockSpec((tf, D_MODEL), lambda r, j: (j, 0)),
                pl.BlockSpec((1, D_MODEL), lambda r, j: (0, 0)),
                pl.BlockSpec((1, D_MODEL), lambda r, j: (0, 0))]
    args = [h, lw['w_up'], lw['w_up'], lw['conv_w'], lw['conv_w'], lw['conv_b'], lw['conv_b'], lw['w_down'],
            lw['ln2_g'], lw['ln2_b']]
    if halo:
        in_specs += [pl.BlockSpec((1, SUBLANE, tf), lambda r, j: (r, 0, j)),
                     pl.BlockSpec((1, SUBLANE, tf), lambda r, j: (r, 0, nf + j))]
        args += [prev, prev]
    else:
        in_specs += [pl.BlockSpec((tm, tf), lambda r, j: (r, j)), pl.BlockSpec((tm, tf), lambda r, j: (r, nf + j))]
        args += [state_rows, state_rows]
    return pl.pallas_call(
        functools.partial(_ffn_kernel, halo=halo, seg=seg, alpha=alpha), grid=(n // tm, nf),
        in_specs=in_specs,
        out_specs=pl.BlockSpec((tm, D_MODEL), lambda r, j: (r, 0), pipeline_mode=pl.Buffered(1)),
        out_shape=jax.ShapeDtypeStruct((n, D_MODEL), F32),
        scratch_shapes=[pltpu.VMEM((tm, D_MODEL), BF16), pltpu.VMEM((tm, D_MODEL), F32)],
        compiler_params=_cparams(("parallel", "arbitrary")), name="conv_ffn_ln")(*args)


def _prep_layer(l, p):
    lp = {k: v[l] for k, v in p.items()}
    lw = dict(
        w_in=_pack_w_in(lp['w_in']),
        fox_b=jnp.zeros((1, LANE), F32).at[0, :N_HEADS].set(lp['fox_b_f']),
        cmp=_cmp_weights(lp['nsa_cmp_pos'], lp['nsa_cmp_wk'], lp['nsa_cmp_wv']),
        s5=_s5_weights(lp),
        w_glu=lp['s5_w_glu'].astype(BF16),
        w_branch=lp['w_branch'].astype(BF16),
        w_out=lp['w_out'].astype(BF16),
        ln1_g=lp['ln1_g'].reshape(1, -1), ln1_b=lp['ln1_b'].reshape(1, -1),
        w_up=lp['ffn_w_up'].astype(BF16), conv_w=lp['ffn_conv_w'], conv_b=lp['ffn_conv_b'].reshape(1, -1),
        w_down=lp['ffn_w_down'].astype(BF16),
        ln2_g=lp['ln2_g'].reshape(1, -1), ln2_b=lp['ln2_b'].reshape(1, -1))
    return lw


def _prompt_layer(x, b, t, lw, alpha):
    n = b * t
    kw = 2 * KV_G * HEAD_DIM
    tr = min(1024, n)
    c = _matmul(x, lw['w_in'], tr, 512)
    logf = _logf(c, lw['fox_b'], tr)
    cumr = _cumsum_prompt(logf.reshape(b, t, N_HEADS).transpose(0, 2, 1))
    o_fox = _fox_prompt(c, cumr[:, :, None, :], b, t, 512)
    o_moba = _moba_prompt(c, b, t, 2 * MOBA_BLOCK)
    kvc = _cmp_prompt(c, b, t, lw['cmp'])
    nchunk = t // CMP_STRIDE
    ov = _overlap_matrix(nchunk - CMP_LEN // CMP_STRIDE + 1, -(-t // SLC_BLOCK))
    o_nsa = _nsa_prompt(c, kvc, ov, b, t, 256)
    y_s5, hlr, hli = _s5_scan(c, lw['s5'], t, t)
    o_s5 = _glu(y_s5, lw['w_glu'], tr)
    h = _merge((o_nsa, o_s5, o_moba, o_fox), c, x, lw['w_branch'], lw['w_out'], lw['ln1_g'], lw['ln1_b'], 512, alpha)

    tm = min(1024, t)
    nt = n // tm
    edge = h.reshape(nt, tm, D_MODEL)[:, tm - (CONV_W - 1):].reshape(nt * (CONV_W - 1), D_MODEL)
    edge = _pad_rows(edge, -(-edge.shape[0] // SUBLANE) * SUBLANE)
    u_edge = _matmul(edge, lw['w_up'], edge.shape[0], 512)[:nt * (CONV_W - 1)].reshape(nt, CONV_W - 1, 2 * D_FF)
    per_seq = t // tm
    conv_state = u_edge[per_seq - 1::per_seq]
    starts_seq = (jnp.arange(nt) % per_seq == 0)[:, None, None]
    prev = jnp.where(starts_seq, 0.0, jnp.roll(u_edge, 1, axis=0))
    prev = jnp.pad(prev, ((0, 0), (SUBLANE - (CONV_W - 1), 0), (0, 0)))
    y = _ffn(h, lw, tm, 512, alpha, prev=prev)

    seg = lambda name, w: c[:, _COL[name]:_COL[name] + w]
    win_rows = min(WINDOW, t)
    states = (
        seg('kv_cmp', kw).reshape(b, t, 2, KV_G, HEAD_DIM),
        seg('kv_slc', kw).reshape(b, t, 2, KV_G, HEAD_DIM),
        c[:, _COL['moba'] + MIX:_COL['moba'] + 3 * MIX].reshape(b, t, 2, N_HEADS, HEAD_DIM),
        c[:, _COL['fox'] + MIX:_COL['fox'] + 3 * MIX].reshape(b, t, 2, N_HEADS, HEAD_DIM),
        logf.reshape(b, t, N_HEADS),
        seg('kv_win', kw).reshape(b, t, 2, KV_G, HEAD_DIM)[:, t - win_rows:],
        hlr.reshape(b, SUBLANE, S5_GROUPS, S5_STATE)[:, SUBLANE - 1],
        hli.reshape(b, SUBLANE, S5_GROUPS, S5_STATE)[:, SUBLANE - 1],
        conv_state)
    return y, states


def _sample_layer(x, bs, t_real, lw, alpha, past, page_table, past_len):
    n = bs * T_PAD
    kw = 2 * KV_G * HEAD_DIM
    c = _matmul(x, lw['w_in'], n, 512)
    c3 = c.reshape(bs, T_PAD, WP)
    logf = _logf(c, lw['fox_b'], n)
    logf3 = logf.reshape(bs, T_PAD, N_HEADS)
    tmask = (jnp.arange(T_PAD) < t_real)[None, :, None]
    new_t = jnp.pad(jnp.where(tmask, logf3, 0.0).transpose(0, 2, 1), ((0, 0), (0, 0), (0, LANE - T_PAD)))
    first = past['first_page']
    npages = page_table.shape[1]
    o_fox = _paged_mha_dec(
        _fox_dec_kernel, "fox_decode", c3, _COL['fox'], page_table, past['fox'], first,
        [new_t] + [past['fox_logf']] * npages,
        [pl.BlockSpec((1, N_HEADS, LANE), lambda b, pt: (b, 0, 0))] + _page_specs(past['fox_logf'], npages, first),
        past_len)
    o_moba = _paged_mha_dec(_moba_dec_kernel, "moba_decode", c3, _COL['moba'], page_table, past['moba'], first,
                            [], [], past_len)
    kvc = _cmp_paged(page_table, past['nsa_cmp'], first, lw['cmp'])
    nchunk = kvc.shape[1]
    ov = _overlap_matrix(nchunk - CMP_LEN // CMP_STRIDE + 1, -(-(past_len + t_real) // SLC_BLOCK))
    o_nsa = _nsa_dec(c3, kvc, ov, past['nsa_win'], past['first_seq'], page_table, past['nsa_slc'], first,
                     past_len, t_real)
    h0 = [jnp.pad(s.reshape(bs, 1, -1), ((0, 0), (0, T_PAD - 1), (0, 0))).reshape(n, -1) for s in past['s5']]
    y_s5, hlr, hli = _s5_scan(c, lw['s5'], n, T_PAD, h0=h0)
    o_s5 = _glu(y_s5, lw['w_glu'], n)
    h = _merge((o_nsa.reshape(n, MIX), o_s5, o_moba.reshape(n, MIX), o_fox.reshape(n, MIX)), c, x,
               lw['w_branch'], lw['w_out'], lw['ln1_g'], lw['ln1_b'], min(256, n), alpha)

    buf = past['ffn_conv']
    state_rows = jnp.pad(buf, ((0, 0), (T_PAD - (CONV_W - 1), 0), (0, 0))).reshape(n, 2 * D_FF)
    y = _ffn(h, lw, min(512, n), 512, alpha, state_rows=state_rows, seg=T_PAD)
    last2 = h.reshape(bs, T_PAD, D_MODEL)[:, t_real - (CONV_W - 1):t_real].reshape(bs * (CONV_W - 1), D_MODEL)
    conv_state = _matmul(last2, lw['w_up'], last2.shape[0], 512).reshape(bs, CONV_W - 1, 2 * D_FF)

    tr = lambda a: a[:, :t_real]
    kv_win_new = tr(c3[:, :, _COL['kv_win']:_COL['kv_win'] + kw]).transpose(0, 2, 1)
    old_win = past['nsa_win'][past['first_seq']:past['first_seq'] + bs]
    full_win = jnp.concatenate([old_win, kv_win_new], axis=2)
    lw_ = full_win.shape[2]
    new_win = full_win[:, :, lw_ - min(WINDOW, lw_):].transpose(0, 2, 1)
    states = (
        tr(c3[:, :, _COL['kv_cmp']:_COL['kv_cmp'] + kw]).reshape(bs, t_real, 2, KV_G, HEAD_DIM),
        tr(c3[:, :, _COL['kv_slc']:_COL['kv_slc'] + kw]).reshape(bs, t_real, 2, KV_G, HEAD_DIM),
        tr(c3[:, :, _COL['moba'] + MIX:_COL['moba'] + 3 * MIX]).reshape(bs, t_real, 2, N_HEADS, HEAD_DIM),
        tr(c3[:, :, _COL['fox'] + MIX:_COL['fox'] + 3 * MIX]).reshape(bs, t_real, 2, N_HEADS, HEAD_DIM),
        tr(logf3),
        new_win.reshape(bs, new_win.shape[1], 2, KV_G, HEAD_DIM),
        hlr.reshape(bs, T_PAD, S5_GROUPS, S5_STATE)[:, t_real - 1],
        hli.reshape(bs, T_PAD, S5_GROUPS, S5_STATE)[:, t_real - 1],
        conv_state)
    return y, states


def kernel(x_prompt, x_sample, cache_nsa_cmp_kv, cache_nsa_slc_kv, cache_moba_kv, cache_fox_kv, cache_fox_logf,
           page_table, cache_nsa_win_kv, state_s5_re, state_s5_im, state_ffn_conv, w_in, fox_b_f, nsa_cmp_pos,
           nsa_cmp_wk, nsa_cmp_wv, s5_a_re, s5_a_im, s5_b_re, s5_b_im, s5_c_re, s5_c_im, s5_d, s5_log_step,
           s5_w_glu, w_branch, w_out, ln1_g, ln1_b, ffn_w_up, ffn_conv_w, ffn_conv_b, ffn_w_down, ln2_g, ln2_b):
    depth = w_in.shape[0]
    b, t, d = x_prompt.shape
    bs, ts, _ = x_sample.shape
    n_phys, page = cache_nsa_cmp_kv.shape[1:3]
    past_len = page_table.shape[1] * page
    assert d == D_MODEL and w_in.shape[2] == IN_WIDTH and ffn_w_down.shape[1] == D_FF
    assert ts <= T_PAD - (CONV_W - 1) and past_len % MOBA_BLOCK == 0 and page == LANE and (past_len + ts) // CMP_STRIDE * CMP_STRIDE <= past_len
    alpha = float((2 * depth) ** 0.25)
    params = dict(w_in=w_in, fox_b_f=fox_b_f, nsa_cmp_pos=nsa_cmp_pos, nsa_cmp_wk=nsa_cmp_wk, nsa_cmp_wv=nsa_cmp_wv,
                  s5_a_re=s5_a_re, s5_a_im=s5_a_im, s5_b_re=s5_b_re, s5_b_im=s5_b_im, s5_c_re=s5_c_re,
                  s5_c_im=s5_c_im, s5_d=s5_d, s5_log_step=s5_log_step, s5_w_glu=s5_w_glu, w_branch=w_branch,
                  w_out=w_out, ln1_g=ln1_g, ln1_b=ln1_b, ffn_w_up=ffn_w_up, ffn_conv_w=ffn_conv_w,
                  ffn_conv_b=ffn_conv_b, ffn_w_down=ffn_w_down, ln2_g=ln2_g, ln2_b=ln2_b)
    kw = 2 * KV_G * HEAD_DIM
    yp = x_prompt.reshape(b * t, d)
    ys = jnp.pad(x_sample, ((0, 0), (0, T_PAD - ts), (0, 0))).reshape(bs * T_PAD, d)
    st_p, st_s = [], []

    def feature_major(cache):
        dd, nn, rr = cache.shape[:3]
        return cache.transpose(0, 1, 3, 4, 5, 2).reshape(dd * nn, -1, rr)

    cmp_fm, slc_fm, moba_fm, fox_fm, win_fm = (
        feature_major(a) for a in (cache_nsa_cmp_kv, cache_nsa_slc_kv, cache_moba_kv, cache_fox_kv, cache_nsa_win_kv))
    logf_hm = cache_fox_logf.transpose(0, 1, 3, 2).reshape(depth * n_phys, N_HEADS, page)
    for l in range(depth):
        lw = _prep_layer(l, params)
        past = dict(
            nsa_cmp=cmp_fm, nsa_slc=slc_fm, moba=moba_fm, fox=fox_fm, fox_logf=logf_hm, nsa_win=win_fm,
            first_page=l * n_phys, first_seq=l * bs,
            s5=(state_s5_re[l], state_s5_im[l]),
            ffn_conv=state_ffn_conv[l])
        yp, sp = _prompt_layer(yp, b, t, lw, alpha)
        ys, ss = _sample_layer(ys, bs, ts, lw, alpha, past, page_table, past_len)
        st_p.append(sp)
        st_s.append(ss)
    sp = [jnp.stack(z) for z in zip(*st_p)]
    ss = [jnp.stack(z) for z in zip(*st_s)]
    out = [yp.reshape(b, t, d), ys.reshape(bs, T_PAD, d)[:, :ts]]
    for a, c in zip(sp, ss):
        out += [a, c]
    return tuple(out)
```

```python
import functools

import numpy as np
import jax
import jax.numpy as jnp
from jax import lax
from jax.experimental import pallas as pl
from jax.experimental.pallas import tpu as pltpu

F32 = jnp.float32
BF16 = jnp.bfloat16
HI = lax.Precision.HIGHEST

LANE = 128
SUBLANE = 8
VMEM_LIMIT = 56 * 1024 * 1024

D_MODEL = 2048
HEAD_DIM = 64
N_BRANCH = 4
MIX = D_MODEL // N_BRANCH
N_HEADS = MIX // HEAD_DIM
KV_G = 2
HG = N_HEADS // KV_G
CMP_LEN = 32
CMP_STRIDE = 16
SLC_BLOCK = 64
SLC_TOPN = 16
WINDOW = 512
MOBA_BLOCK = 256
MOBA_TOPK = 3
S5_CH = 16
S5_GROUPS = MIX // S5_CH
S5_STATE = 64
D_FF = 5632
CONV_W = 3
LN_EPS = 1e-5
SCALE = HEAD_DIM ** -0.5
NEG = -1e30
FORCE = 1e4
T_PAD = 8
SLOPES = tuple(float(v) for v in np.asarray(2.0 ** (-8.0 * np.arange(1, N_HEADS + 1) / N_HEADS), np.float32))

_SPLITS = (('nsa_q', MIX), ('kv_cmp', 2 * KV_G * HEAD_DIM), ('kv_slc', 2 * KV_G * HEAD_DIM),
           ('kv_win', 2 * KV_G * HEAD_DIM), ('nsa_gate', 3 * N_HEADS), ('s5_u', MIX),
           ('moba', 3 * MIX), ('fox', 3 * MIX), ('fox_f', N_HEADS), ('merge', N_BRANCH * D_MODEL))
_SRC = {}
_o = 0
for _n, _w in _SPLITS:
    _SRC[_n] = (_o, _w)
    _o += _w
IN_WIDTH = _o
_COL = dict(merge=0, nsa_q=8192, kv_cmp=8704, kv_slc=8960, kv_win=9216, nsa_gate=9472, s5_u=9600,
            moba=10240, fox=11776, fox_f=13312)
WP = 13824


def _cparams(sem):
    return pltpu.CompilerParams(dimension_semantics=sem, vmem_limit_bytes=VMEM_LIMIT)


def _dot_nt(a, b, precision=None):
    return lax.dot_general(a, b, (((1,), (1,)), ((), ())), precision=precision, preferred_element_type=F32)


def _split_bf16(x):
    hi = x.astype(BF16)
    return hi, (x - hi.astype(F32)).astype(BF16)


def _dot3(a, b_split):
    a_hi, a_lo = _split_bf16(a)
    b_hi, b_lo = b_split
    dot = functools.partial(jnp.dot, preferred_element_type=F32)
    return dot(a_hi, b_hi) + (dot(a_hi, b_lo) + dot(a_lo, b_hi))


def _sigmoid(x):
    return 1.0 / (1.0 + jnp.exp(-x))


def _pack_w_in(w):
    lead = w.shape[:-1]
    order = ('merge', 'nsa_q', 'kv_cmp', 'kv_slc', 'kv_win', 'nsa_gate', 's5_u', 'moba', 'fox', 'fox_f')
    parts, pos = [], 0
    for name in order:
        if _COL[name] > pos:
            parts.append(jnp.zeros(lead + (_COL[name] - pos,), w.dtype))
        s, wd = _SRC[name]
        parts.append(w[..., s:s + wd])
        pos = _COL[name] + wd
    parts.append(jnp.zeros(lead + (WP - pos,), w.dtype))
    return jnp.concatenate(parts, axis=-1).astype(BF16)


def _mm_kernel(x_ref, w_ref, o_ref, xb_ref):
    @pl.when(pl.program_id(1) == 0)
    def _():
        xb_ref[...] = x_ref[...].astype(BF16)

    o_ref[...] = jnp.dot(xb_ref[...], w_ref[...], preferred_element_type=F32)


def _matmul(x, w_l, tm, tn):
    w, l = w_l
    m, k = x.shape
    n = w.shape[2]
    return pl.pallas_call(
        _mm_kernel, grid=(m // tm, n // tn),
        in_specs=[pl.BlockSpec((tm, k), lambda i, j: (i, 0)), pl.BlockSpec((None, k, tn), lambda i, j: (l, 0, j))],
        out_specs=pl.BlockSpec((tm, tn), lambda i, j: (i, j)),
        out_shape=jax.ShapeDtypeStruct((m, n), F32),
        scratch_shapes=[pltpu.VMEM((tm, k), BF16)],
        compiler_params=_cparams(("parallel", "arbitrary")), name="mm")(x, w)


def _transpose_kernel(c_ref, o_ref):
    o_ref[...] = c_ref[...].T


def _kv_state(c, b, t, col, width, t_from=0):
    tf = min(512, width)
    tt = 512
    nt, n0 = t // tt, t_from // tt
    st = pl.pallas_call(
        _transpose_kernel, grid=(b, width // tf, nt - n0),
        in_specs=[pl.BlockSpec((tt, tf), lambda i, f, q: (i * nt + n0 + q, col // tf + f))],
        out_specs=pl.BlockSpec((None, tf, tt), lambda i, f, q: (i, f, q)),
        out_shape=jax.ShapeDtypeStruct((b, width, t - t_from), F32),
        compiler_params=_cparams(("parallel", "parallel", "parallel")), name="kv_state")(c)
    return st.reshape(b, 2, width // (2 * HEAD_DIM), HEAD_DIM, t - t_from).transpose(0, 4, 1, 2, 3)


def _logf_kernel(c_ref, b_ref, o_ref):
    x = c_ref[...] + b_ref[...]
    y = jnp.minimum(x, 0.0) - jnp.log1p(jnp.exp(-jnp.abs(x)))
    o_ref[...] = y[:, :N_HEADS]


def _logf(c, b_pad, tm):
    n = c.shape[0]
    return pl.pallas_call(
        _logf_kernel, grid=(n // tm,),
        in_specs=[pl.BlockSpec((tm, LANE), lambda i: (i, _COL['fox_f'] // LANE)),
                  pl.BlockSpec((1, LANE), lambda i: (0, 0))],
        out_specs=pl.BlockSpec((tm, N_HEADS), lambda i: (i, 0)),
        out_shape=jax.ShapeDtypeStruct((n, N_HEADS), F32),
        compiler_params=_cparams(("parallel",)), name="logf")(c, b_pad)


def _cumsum_kernel(*refs, n_in):
    in_refs, o_ref = refs[-n_in - 1:-1], refs[-1]
    r = lax.broadcasted_iota(jnp.int32, (LANE, LANE), 0)
    c = lax.broadcasted_iota(jnp.int32, (LANE, LANE), 1)
    tri = jnp.where(r <= c, 1.0, 0.0).astype(F32)
    carry = jnp.zeros((N_HEADS, 1), F32)
    off = 0
    for ref in in_refs:
        for j in range(ref.shape[-1] // LANE):
            x = ref[0, :, j * LANE:(j + 1) * LANE]
            cs = jnp.dot(x, tri, precision=HI, preferred_element_type=F32) + carry
            o_ref[0, :, off:off + LANE] = cs
            carry = cs[:, LANE - 1:LANE]
            off += LANE


def _cumsum_prompt(logf_t):
    b, h, t = logf_t.shape
    return pl.pallas_call(
        functools.partial(_cumsum_kernel, n_in=1), grid=(b,),
        in_specs=[pl.BlockSpec((1, h, t), lambda i: (i, 0, 0))],
        out_specs=pl.BlockSpec((1, h, t), lambda i: (i, 0, 0)),
        out_shape=jax.ShapeDtypeStruct((b, h, t), F32),
        compiler_params=_cparams(("parallel",)), name="fox_cumsum_prompt")(logf_t)


M_FLOOR = -1e29


def _softmax_step(m, l, s):
    m_new = jnp.maximum(m, jnp.max(s, axis=1, keepdims=True))
    p = jnp.exp(s - m_new)
    alpha = jnp.exp(m - m_new)
    return m_new, alpha * l + jnp.sum(p, axis=1, keepdims=True), alpha, p.astype(BF16)


def _pipelined_attention(chains, lo, hi, score_fn, value_fn, rows, tk, dv, last_fix=None):
    def flush(c, kj, s, p_prev, a_prev, acc):
        prev = jnp.maximum(kj - 1, lo)
        return a_prev * acc + jnp.dot(p_prev, value_fn(c, prev), preferred_element_type=F32)

    def body(kj, carry):
        out = []
        for c in range(chains):
            s, p_prev, a_prev, m, l, acc = carry[c]
            acc = flush(c, kj, s, p_prev, a_prev, acc)
            m, l, alpha, p = _softmax_step(m, l, s)
            out.append((score_fn(c, kj + 1), p, alpha, m, l, acc))
        return tuple(out)

    init = tuple((score_fn(c, lo), jnp.zeros((rows, tk), BF16), jnp.ones((rows, 1), F32),
                  jnp.full((rows, 1), M_FLOOR, F32), jnp.zeros((rows, 1), F32), jnp.zeros((rows, dv), F32))
                 for c in range(chains))
    carry = lax.fori_loop(lo, hi, body, init)
    outs = []
    for c in range(chains):
        s, p_prev, a_prev, m, l, acc = carry[c]
        acc = flush(c, hi, s, p_prev, a_prev, acc)
        if last_fix is not None:
            s = last_fix(c, s)
        m, l, alpha, p = _softmax_step(m, l, s)
        acc = alpha * acc + jnp.dot(p, value_fn(c, hi), preferred_element_type=F32)
        outs.append(acc / jnp.maximum(l, 1e-30))
    return outs


def _attend_tiles(s_tiles, v_tiles):
    m = functools.reduce(jnp.maximum, [jnp.max(s, axis=1, keepdims=True) for s in s_tiles])
    l, acc = 0.0, 0.0
    for s, (v, feature_major) in zip(s_tiles, v_tiles):
        e = jnp.where(s > 0.5 * NEG, jnp.exp(s - m), 0.0)
        l = l + jnp.sum(e, axis=1, keepdims=True)
        eb = e.astype(BF16)
        acc = acc + (_dot_nt(eb, v) if feature_major else jnp.dot(eb, v, preferred_element_type=F32))
    return acc / jnp.maximum(l, 1e-30)


def _topn_mask(v, ncols, topn):
    jl = lax.broadcasted_iota(jnp.int32, v.shape, 1)
    rank = jnp.zeros(v.shape, F32)
    for j2 in range(ncols):
        col = v[:, j2:j2 + 1]
        beats = (col > v) | ((col == v) & (jl > j2))
        rank = rank + jnp.where(beats, 1.0, 0.0)
    return rank < topn


def _pad_rows(a, rows):
    return jnp.concatenate([a, jnp.zeros((rows - a.shape[0], a.shape[1]), a.dtype)], axis=0)


def _fox_prompt_kernel(q_ref, k_ref, v_ref, cr_ref, o_ref, *, tq):
    qi = pl.program_id(2)
    q0 = pl.multiple_of(qi * tq, tq)
    row = lax.broadcasted_iota(jnp.int32, (tq, tq), 0)
    col = lax.broadcasted_iota(jnp.int32, (tq, tq), 1)
    qs = [(q_ref[:, h2 * HEAD_DIM:(h2 + 1) * HEAD_DIM] * SCALE).astype(BF16) for h2 in range(2)]
    c0 = [cr_ref[0, h2, :, pl.ds(q0, LANE)][:, 0:1] for h2 in range(2)]

    def score(h2, kj):
        ks = pl.multiple_of(kj * tq, tq)
        k = k_ref[pl.ds(ks, tq), h2 * HEAD_DIM:(h2 + 1) * HEAD_DIM].astype(BF16)
        return _dot_nt(qs[h2], k) + (c0[h2] - cr_ref[0, h2, :, pl.ds(ks, tq)])

    def value(h2, kj):
        ks = pl.multiple_of(kj * tq, tq)
        return v_ref[pl.ds(ks, tq), h2 * HEAD_DIM:(h2 + 1) * HEAD_DIM].astype(BF16)

    outs = _pipelined_attention(2, 0, qi, score, value, tq, tq, HEAD_DIM,
                                last_fix=lambda h2, s: jnp.where(col <= row, s, NEG))
    o_ref[...] = jnp.concatenate(outs, axis=1)


def _fox_prompt(c, cumr, b, t, tq):
    nq = t // tq
    base = _COL['fox'] // LANE
    hp_n = N_HEADS // 2
    return pl.pallas_call(
        functools.partial(_fox_prompt_kernel, tq=tq), grid=(b, hp_n, nq),
        in_specs=[pl.BlockSpec((tq, LANE), lambda i, h, q: (i * nq + q, base + h)),
                  pl.BlockSpec((t, LANE), lambda i, h, q: (i, base + hp_n + h)),
                  pl.BlockSpec((t, LANE), lambda i, h, q: (i, base + 2 * hp_n + h)),
                  pl.BlockSpec((1, 2, 1, t), lambda i, h, q: (i, h, 0, 0))],
        out_specs=pl.BlockSpec((tq, LANE), lambda i, h, q: (i * nq + q, h)),
        out_shape=jax.ShapeDtypeStruct((b * t, MIX), F32),
        compiler_params=_cparams(("parallel", "arbitrary", "arbitrary")), name="fox_prompt")(
            c, c, c, cumr)


def _head_slope(hp, h2):
    s = jnp.float32(SLOPES[h2])
    for k in range(1, N_HEADS // 2):
        s = jnp.where(hp == k, jnp.float32(SLOPES[2 * k + h2]), s)
    return s


def _moba_prompt_kernel(q_ref, k_ref, v_ref, o_ref, kmean_ref, *, tq, nb):
    hp = pl.program_id(1)
    qi = pl.program_id(2)
    bpt = tq // MOBA_BLOCK

    @pl.when(qi == 0)
    def _():
        kmean_ref[...] = jnp.zeros_like(kmean_ref)
        for n in range(nb):
            kmean_ref[n:n + 1, :] = jnp.mean(k_ref[n * MOBA_BLOCK:(n + 1) * MOBA_BLOCK, :], axis=0, keepdims=True)

    q0 = pl.multiple_of(qi * tq, tq)
    row = lax.broadcasted_iota(jnp.int32, (tq, tq), 0)
    col = lax.broadcasted_iota(jnp.int32, (tq, tq), 1)
    colpos = lax.broadcasted_iota(jnp.int32, (1, tq), 1)
    jl = lax.broadcasted_iota(jnp.int32, (tq, LANE), 1)
    cur = qi * bpt + lax.broadcasted_iota(jnp.int32, (tq, 1), 0) // MOBA_BLOCK
    qs, slopes, blockbias = [], [], []
    for h2 in range(2):
        lo = h2 * HEAD_DIM
        qf = q_ref[:, lo:lo + HEAD_DIM]
        gate = _dot_nt(qf, kmean_ref[:, lo:lo + HEAD_DIM], precision=HI)
        gate = jnp.where(jl < cur, gate, NEG)
        picked = (_topn_mask(gate, nb, MOBA_TOPK) & (jl < cur)) | (jl == cur)
        qs.append((qf * SCALE).astype(BF16))
        slopes.append(_head_slope(hp, h2))
        blockbias.append(jnp.where(picked, 0.0, NEG))

    def score(h2, kj):
        ks = pl.multiple_of(kj * tq, tq)
        k = k_ref[pl.ds(ks, tq), h2 * HEAD_DIM:(h2 + 1) * HEAD_DIM].astype(BF16)
        bias = None
        for j in reversed(range(bpt)):
            rowbias = jnp.min(jnp.where(jl == kj * bpt + j, blockbias[h2], 0.0), axis=1, keepdims=True)
            bias = rowbias if bias is None else jnp.where(colpos < (j + 1) * MOBA_BLOCK, rowbias, bias)
        return _dot_nt(qs[h2], k) + slopes[h2] * (ks - q0 + colpos).astype(F32) + bias

    def value(h2, kj):
        ks = pl.multiple_of(kj * tq, tq)
        return v_ref[pl.ds(ks, tq), h2 * HEAD_DIM:(h2 + 1) * HEAD_DIM].astype(BF16)

    outs = _pipelined_attention(2, 0, qi, score, value, tq, tq, HEAD_DIM,
                                last_fix=lambda h2, s: jnp.where(col <= row, s, NEG))
    o_ref[...] = jnp.concatenate(outs, axis=1)


def _moba_prompt(c, b, t, tq):
    nq = t // tq
    base = _COL['moba'] // LANE
    hp_n = N_HEADS // 2
    return pl.pallas_call(
        functools.partial(_moba_prompt_kernel, tq=tq, nb=t // MOBA_BLOCK), grid=(b, hp_n, nq),
        in_specs=[pl.BlockSpec((tq, LANE), lambda i, h, q: (i * nq + q, base + h)),
                  pl.BlockSpec((t, LANE), lambda i, h, q: (i, base + hp_n + h)),
                  pl.BlockSpec((t, LANE), lambda i, h, q: (i, base + 2 * hp_n + h))],
        out_specs=pl.BlockSpec((tq, LANE), lambda i, h, q: (i * nq + q, h)),
        out_shape=jax.ShapeDtypeStruct((b * t, MIX), F32),
        scratch_shapes=[pltpu.VMEM((LANE, LANE), F32)],
        compiler_params=_cparams(("parallel", "arbitrary", "arbitrary")), name="moba_prompt")(c, c, c)


def _cmp_kernel(*refs, n_in, feature_major):
    if feature_major:
        xs_ref, refs = refs[-1], refs[:-1]
        x_refs = refs[-5 - n_in:-5]
        page = x_refs[0].shape[2]
        halves = x_refs[0].shape[1] // LANE
        for p, r in enumerate(x_refs):
            for j in range(halves):
                xs_ref[j, p * page:(p + 1) * page, :] = r[0, j * LANE:(j + 1) * LANE, :].T
        nchunk = n_in * page // CMP_STRIDE
        x = jnp.concatenate([xs_ref[j, pl.ds(l, nchunk, stride=CMP_STRIDE), :]
                             for l in range(CMP_STRIDE) for j in range(halves)], axis=1)
    else:
        x = refs[-6][0]
    pos_ref, wcat_ref, w0_ref, w1_ref, o_ref = refs[-5:]
    xb = x.astype(BF16)
    p0 = jnp.dot(xb, w0_ref[...], preferred_element_type=F32)
    p1 = jnp.dot(xb, w1_ref[...], preferred_element_type=F32)
    bias = jnp.dot(pos_ref[...].astype(BF16), wcat_ref[...], preferred_element_type=F32)[0:1]
    o_ref[0] = p0 + pltpu.roll(p1, p1.shape[0] - 1, 0) + bias


def _cmp_weights(pos, wk, wv):
    r = CMP_LEN // CMP_STRIDE
    w = jnp.stack([wk, wv]).reshape(2, r, CMP_STRIDE, HEAD_DIM, HEAD_DIM)
    e2 = jnp.eye(2, dtype=w.dtype)
    eg = jnp.eye(KV_G, dtype=w.dtype)
    big = jnp.einsum('krlde,kK,gG->rlkgdKGe', w, e2, eg)
    big = big.reshape(r, CMP_STRIDE * 2 * KV_G * HEAD_DIM, 2 * KV_G * HEAD_DIM).astype(BF16)
    wcat = jnp.concatenate([wk, wk, wv, wv], axis=1).astype(BF16)
    posb = jnp.zeros((SUBLANE, CMP_LEN * HEAD_DIM), F32).at[0].set(pos.reshape(-1))
    return posb, wcat, big[0], big[1]


def _const_specs(arrays):
    return [pl.BlockSpec(a.shape, lambda *_, nd=a.ndim: (0,) * nd) for a in arrays]


def _cmp_prompt(c, b, t, cw):
    kw = 2 * KV_G * HEAD_DIM
    nchunk = t // CMP_STRIDE
    x = c[:, _COL['kv_cmp']:_COL['kv_cmp'] + kw].reshape(b, nchunk, CMP_STRIDE * kw)
    return pl.pallas_call(
        functools.partial(_cmp_kernel, n_in=1, feature_major=False), grid=(b,),
        in_specs=[pl.BlockSpec((1, nchunk, CMP_STRIDE * kw), lambda i: (i, 0, 0))] + _const_specs(cw),
        out_specs=pl.BlockSpec((1, nchunk, kw), lambda i: (i, 0, 0)),
        out_shape=jax.ShapeDtypeStruct((b, nchunk, kw), F32),
        compiler_params=_cparams(("parallel",)), name="nsa_cmp_prompt")(x, *cw)


def _cmp_paged(page_table, pool, first, cw):
    bs, npages = page_table.shape
    kw, page = pool.shape[1:]
    nchunk = npages * page // CMP_STRIDE
    return pl.pallas_call(
        functools.partial(_cmp_kernel, n_in=npages, feature_major=True),
        grid_spec=pltpu.PrefetchScalarGridSpec(
            num_scalar_prefetch=1, grid=(bs,), in_specs=_page_specs(pool, npages, first) + _const_specs(cw),
            out_specs=pl.BlockSpec((1, nchunk, kw), lambda b, pt: (b, 0, 0)),
            scratch_shapes=[pltpu.VMEM((kw // LANE, npages * page, LANE), F32)]),
        out_shape=jax.ShapeDtypeStruct((bs, nchunk, kw), F32),
        compiler_params=_cparams(("parallel",)), name="nsa_cmp_paged")(page_table, *([pool] * npages), *cw)


def _overlap_matrix(nc, ns):
    i_c = np.arange(LANE)[:, None] * CMP_STRIDE
    j_s = np.arange(LANE)[None, :] * SLC_BLOCK
    ov = (i_c < j_s + SLC_BLOCK) & (i_c + CMP_LEN > j_s)
    ov &= (np.arange(LANE)[:, None] < nc) & (np.arange(LANE)[None, :] < ns)
    return jnp.asarray(ov, F32)


def _nsa_prompt_kernel(q_ref, slc_ref, win_ref, cmp_ref, g_ref, ov_ref, o_ref, *, tq, nc, ns):
    qi = pl.program_id(1)
    q0 = pl.multiple_of(qi * tq, tq)
    rows = HG * tq
    rl = lax.broadcasted_iota(jnp.int32, (tq, 1), 0)
    t1 = q0 + rl
    t4 = jnp.concatenate([t1] * HG, axis=0)
    lane = lax.broadcasted_iota(jnp.int32, (1, LANE), 1)
    colpos = lax.broadcasted_iota(jnp.int32, (1, tq), 1)
    dloc = lax.broadcasted_iota(jnp.int32, (tq, tq), 0) - lax.broadcasted_iota(jnp.int32, (tq, tq), 1)
    sg = _sigmoid(g_ref[...])
    jl = lax.broadcasted_iota(jnp.int32, (tq, LANE), 1)
    cur = t1 // SLC_BLOCK
    erow = lax.broadcasted_iota(jnp.int32, (LANE, tq), 0)
    ecol = lax.broadcasted_iota(jnp.int32, (LANE, tq), 1)
    wtiles = WINDOW // tq
    qs, slope, o_cmp, blockbias = [], [], [], []
    for g in range(KV_G):
        kl = g * HEAD_DIM
        vl = KV_G * HEAD_DIM + g * HEAD_DIM
        qg = jnp.concatenate([q_ref[:, (g * HG + h) * HEAD_DIM:(g * HG + h + 1) * HEAD_DIM] for h in range(HG)],
                             axis=0)
        qs.append((qg * SCALE).astype(BF16))
        slope.append(jnp.concatenate([jnp.full((tq, 1), SLOPES[g * HG + h], F32) for h in range(HG)], axis=0))

        kc = cmp_ref[0, :, kl:kl + HEAD_DIM].astype(BF16)
        vc = cmp_ref[0, :, vl:vl + HEAD_DIM].astype(BF16)
        dist = (t4 - (lane * CMP_STRIDE + CMP_LEN - 1)).astype(F32)
        ok_c = (dist >= 0) & (lane < nc)
        s = jnp.where(ok_c, _dot_nt(qs[g], kc) - slope[g] * dist, NEG)
        m = jnp.max(s, axis=1, keepdims=True)
        e = jnp.where(ok_c, jnp.exp(s - m), 0.0)
        p_c = e / jnp.maximum(jnp.sum(e, axis=1, keepdims=True), 1e-30)
        o_cmp.append(jnp.dot(p_c.astype(BF16), vc, preferred_element_type=F32))

        psum = p_c[0:tq]
        for h in range(1, HG):
            psum = psum + p_c[h * tq:(h + 1) * tq]
        imp = jnp.dot(psum, ov_ref[...], precision=HI, preferred_element_type=F32)
        forced = (jl == 0) | (jl == cur) | (jl == cur - 1)
        imp = jnp.where(forced, FORCE, imp)
        imp = jnp.where(jl <= cur, imp, NEG)
        picked = _topn_mask(imp, ns, min(SLC_TOPN, ns)) & (jl <= cur)
        blockbias.append(jnp.where(picked, 0.0, NEG).astype(BF16))

    def scores(ref, g, ks, bias):
        k = ref[pl.ds(ks, tq), g * HEAD_DIM:(g + 1) * HEAD_DIM].astype(BF16)
        cpos = (ks - q0 + colpos).astype(F32)
        return _dot_nt(qs[g], k) + slope[g] * cpos + jnp.concatenate([bias] * HG, axis=0)

    def values(ref, g, kj):
        ks = pl.multiple_of(kj * tq, tq)
        vl = KV_G * HEAD_DIM + g * HEAD_DIM
        return ref[pl.ds(ks, tq), vl:vl + HEAD_DIM].astype(BF16)

    def slc_score(g, kj):
        ks = pl.multiple_of(kj * tq, tq)
        expand = jnp.where((ks + ecol) // SLC_BLOCK == erow, 1.0, 0.0).astype(BF16)
        return scores(slc_ref, g, ks, jnp.dot(blockbias[g], expand, preferred_element_type=F32))

    def win_score(g, kj):
        ks = pl.multiple_of(kj * tq, tq)
        d = dloc + (q0 - ks)
        return scores(win_ref, g, ks, jnp.where((d >= 0) & (d < WINDOW), 0.0, NEG))

    causal = jnp.concatenate([dloc] * HG, axis=0) >= 0
    o_slc = _pipelined_attention(KV_G, 0, qi, slc_score, functools.partial(values, slc_ref), rows, tq, HEAD_DIM,
                                 last_fix=lambda g, s: jnp.where(causal, s, NEG))
    o_win = _pipelined_attention(KV_G, jnp.maximum(qi - wtiles, 0), qi, win_score,
                                 functools.partial(values, win_ref), rows, tq, HEAD_DIM)

    for g in range(KV_G):
        for h in range(HG):
            hh = g * HG + h
            r0 = h * tq
            o = (sg[:, hh:hh + 1] * o_cmp[g][r0:r0 + tq]
                 + sg[:, N_HEADS + hh:N_HEADS + hh + 1] * o_slc[g][r0:r0 + tq]
                 + sg[:, 2 * N_HEADS + hh:2 * N_HEADS + hh + 1] * o_win[g][r0:r0 + tq])
            o_ref[:, hh * HEAD_DIM:(hh + 1) * HEAD_DIM] = o


def _nsa_prompt(c, kvc, ov, b, t, tq):
    nq = t // tq
    kw = 2 * KV_G * HEAD_DIM
    nchunk = t // CMP_STRIDE
    nc = nchunk - CMP_LEN // CMP_STRIDE + 1
    ns = -(-t // SLC_BLOCK)
    return pl.pallas_call(
        functools.partial(_nsa_prompt_kernel, tq=tq, nc=nc, ns=ns), grid=(b, nq),
        in_specs=[pl.BlockSpec((tq, MIX), lambda i, q: (i * nq + q, _COL['nsa_q'] // MIX)),
                  pl.BlockSpec((t, kw), lambda i, q: (i, _COL['kv_slc'] // kw)),
                  pl.BlockSpec((t, kw), lambda i, q: (i, _COL['kv_win'] // kw)),
                  pl.BlockSpec((1, nchunk, kw), lambda i, q: (i, 0, 0)),
                  pl.BlockSpec((tq, LANE), lambda i, q: (i * nq + q, _COL['nsa_gate'] // LANE)),
                  pl.BlockSpec((LANE, LANE), lambda i, q: (0, 0))],
        out_specs=pl.BlockSpec((tq, MIX), lambda i, q: (i * nq + q, 0)),
        out_shape=jax.ShapeDtypeStruct((b * t, MIX), F32),
        compiler_params=_cparams(("parallel", "arbitrary")), name="nsa_prompt")(c, c, c, kvc, c, ov)


def _dec_rows():
    rid = lax.broadcasted_iota(jnp.int32, (N_HEADS * T_PAD, 1), 0)
    t8 = rid % T_PAD
    slope = jnp.concatenate([jnp.full((T_PAD, 1), SLOPES[h], F32) for h in range(N_HEADS)], axis=0)
    return t8, slope


def _expand_heads(q8):
    lane = lax.broadcasted_iota(jnp.int32, q8.shape, 1)
    return jnp.concatenate([jnp.where(lane // HEAD_DIM == h, q8, 0.0) for h in range(N_HEADS)], axis=0)


def _collapse_heads(res):
    lane = lax.broadcasted_iota(jnp.int32, (T_PAD, res.shape[1]), 1)
    out = jnp.zeros((T_PAD, res.shape[1]), F32)
    for h in range(N_HEADS):
        out = out + jnp.where(lane // HEAD_DIM == h, res[h * T_PAD:(h + 1) * T_PAD], 0.0)
    return out


def _moba_dec_kernel(pt_ref, q_ref, kn_ref, vn_ref, *rest, npages, past):
    page_refs, o_ref = rest[:npages], rest[npages]
    page = page_refs[0].shape[2]
    t8, slope = _dec_rows()
    qpos = past + t8
    lane = lax.broadcasted_iota(jnp.int32, (1, LANE), 1)
    q8 = q_ref[0]
    qx_f = _expand_heads(q8)
    qx = (qx_f * SCALE).astype(BF16)
    per_blk = MOBA_BLOCK // page
    nb_past = past // MOBA_BLOCK
    lanei = lax.broadcasted_iota(jnp.int32, (MIX, LANE), 1)
    kmean = jnp.zeros((MIX, LANE), F32)
    for n in range(nb_past):
        tot = page_refs[n * per_blk][0, 0:MIX, :]
        for p in range(n * per_blk + 1, (n + 1) * per_blk):
            tot = tot + page_refs[p][0, 0:MIX, :]
        col = jnp.sum(tot, axis=1, keepdims=True) * (1.0 / MOBA_BLOCK)
        kmean = kmean + jnp.where(lanei == n, col, 0.0)
    gate = jnp.dot(qx_f, kmean, precision=HI, preferred_element_type=F32)
    jl = lax.broadcasted_iota(jnp.int32, gate.shape, 1)
    cur = qpos // MOBA_BLOCK
    gate = jnp.where(jl < cur, gate, NEG)
    nb = -(-(past + T_PAD) // MOBA_BLOCK)
    sel = jnp.where(_topn_mask(gate, nb, max(1, min(MOBA_TOPK, nb - 1))) & (jl < cur), 1.0, 0.0)
    s_tiles, v_tiles = [], []
    for p in range(npages):
        kt = page_refs[p][0, 0:MIX, :].astype(BF16)
        v_tiles.append((page_refs[p][0, MIX:2 * MIX, :].astype(BF16), True))
        n = (p * page) // MOBA_BLOCK
        d = (qpos - (p * page + lane)).astype(F32)
        valid = (sel[:, n:n + 1] > 0.5) & (d >= 0)
        s_tiles.append(jnp.where(valid, jnp.dot(qx, kt, preferred_element_type=F32) - slope * d, NEG))
    k = _pad_rows(kn_ref[0], LANE).astype(BF16)
    v_tiles.append((_pad_rows(vn_ref[0], LANE).astype(BF16), False))
    d = (t8 - lane).astype(F32)
    valid = (d >= 0) & (lane < T_PAD)
    s_tiles.append(jnp.where(valid, _dot_nt(qx, k) - slope * d, NEG))
    o_ref[0] = _collapse_heads(_attend_tiles(s_tiles, v_tiles))


def _fox_dec_kernel(pt_ref, q_ref, kn_ref, vn_ref, ln_ref, *rest, npages, past):
    logf_refs, page_refs, o_ref = rest[:npages], rest[npages:2 * npages], rest[2 * npages]
    t8, _ = _dec_rows()
    lane = lax.broadcasted_iota(jnp.int32, (1, LANE), 1)
    qx = (_expand_heads(q_ref[0]) * SCALE).astype(BF16)
    r = lax.broadcasted_iota(jnp.int32, (LANE, LANE), 0)
    c = lax.broadcasted_iota(jnp.int32, (LANE, LANE), 1)
    tri = jnp.where(r <= c, 1.0, 0.0).astype(F32)

    def head_rows(x):
        return jnp.concatenate([jnp.broadcast_to(x[h:h + 1], (T_PAD, x.shape[1])) for h in range(N_HEADS)], axis=0)

    carry = jnp.zeros((N_HEADS, 1), F32)
    cum = []
    for ref in list(logf_refs) + [ln_ref]:
        cs = jnp.dot(ref[0], tri, precision=HI, preferred_element_type=F32) + carry
        cum.append(cs)
        carry = cs[:, LANE - 1:LANE]
    c_ref = head_rows(cum[npages - 1][:, LANE - 1:LANE])

    s_tiles, v_tiles = [], []
    for p in range(npages):
        kt = page_refs[p][0, 0:MIX, :].astype(BF16)
        v_tiles.append((page_refs[p][0, MIX:2 * MIX, :].astype(BF16), True))
        s_tiles.append(jnp.dot(qx, kt, preferred_element_type=F32) + (c_ref - head_rows(cum[p])))
    k = _pad_rows(kn_ref[0], LANE).astype(BF16)
    v_tiles.append((_pad_rows(vn_ref[0], LANE).astype(BF16), False))
    valid = (lane <= t8) & (lane < T_PAD)
    s_tiles.append(jnp.where(valid, _dot_nt(qx, k) + (c_ref - head_rows(cum[npages])), NEG))
    o_ref[0] = _collapse_heads(_attend_tiles(s_tiles, v_tiles))


def _page_specs(pool, npages, first):
    blk = (1,) + pool.shape[1:]
    return [pl.BlockSpec(blk, lambda b, pt, p=p: (first + pt[b, p], 0, 0)) for p in range(npages)]


def _paged_mha_dec(kernel, name, c3, col, page_table, pool, first, extra_args, extra_specs, past):
    bs, npages = page_table.shape
    qb = col // MIX
    in_specs = [pl.BlockSpec((1, T_PAD, MIX), lambda b, pt: (b, 0, qb)),
                pl.BlockSpec((1, T_PAD, MIX), lambda b, pt: (b, 0, qb + 1)),
                pl.BlockSpec((1, T_PAD, MIX), lambda b, pt: (b, 0, qb + 2))]
    in_specs += extra_specs
    in_specs += _page_specs(pool, npages, first)
    return pl.pallas_call(
        functools.partial(kernel, npages=npages, past=past),
        grid_spec=pltpu.PrefetchScalarGridSpec(
            num_scalar_prefetch=1, grid=(bs,), in_specs=in_specs,
            out_specs=pl.BlockSpec((1, T_PAD, MIX), lambda b, pt: (b, 0, 0))),
        out_shape=jax.ShapeDtypeStruct((bs, T_PAD, MIX), F32),
        compiler_params=_cparams(("parallel",)), name=name)(
            page_table, c3, c3, c3, *extra_args, *([pool] * npages))


def _nsa_dec_kernel(pt_ref, q_ref, ns_ref, nw_ref, g_ref, cmp_ref, ov_ref, wc_ref, *rest, npages, past, nc, ns):
    page_refs, o_ref = rest[:npages], rest[npages]
    page = page_refs[0].shape[2]
    kw = KV_G * HEAD_DIM
    t8, slope = _dec_rows()
    qpos = past + t8
    lane = lax.broadcasted_iota(jnp.int32, (1, LANE), 1)
    lane8 = lax.broadcasted_iota(jnp.int32, (T_PAD, LANE), 1)
    q8 = q_ref[0] * SCALE
    rows = []
    for h in range(N_HEADS):
        x = q8[:, (h // 2) * LANE:(h // 2 + 1) * LANE]
        dst = h // HG
        if h % 2 != dst:
            x = pltpu.roll(x, HEAD_DIM, 1)
        rows.append(jnp.where((lane8 // HEAD_DIM) == dst, x, 0.0))
    qx = jnp.concatenate(rows, axis=0).astype(BF16)

    cm = cmp_ref[0]
    dist = (qpos - (lane * CMP_STRIDE + CMP_LEN - 1)).astype(F32)
    ok_c = (dist >= 0) & (lane < nc)
    s = jnp.where(ok_c, _dot_nt(qx, cm[:, 0:kw].astype(BF16)) - slope * dist, NEG)
    m = jnp.max(s, axis=1, keepdims=True)
    e = jnp.where(ok_c, jnp.exp(s - m), 0.0)
    p_c = e / jnp.maximum(jnp.sum(e, axis=1, keepdims=True), 1e-30)
    o_cmp = jnp.dot(p_c.astype(BF16), cm[:, kw:2 * kw].astype(BF16), preferred_element_type=F32)

    psum = []
    for g in range(KV_G):
        acc = p_c[g * HG * T_PAD:g * HG * T_PAD + T_PAD]
        for h in range(1, HG):
            r0 = (g * HG + h) * T_PAD
            acc = acc + p_c[r0:r0 + T_PAD]
        psum.append(acc)
    imp = jnp.dot(jnp.concatenate(psum, axis=0), ov_ref[...], precision=HI, preferred_element_type=F32)
    jl = lax.broadcasted_iota(jnp.int32, imp.shape, 1)
    tg = lax.broadcasted_iota(jnp.int32, (KV_G * T_PAD, 1), 0) % T_PAD
    cur = (past + tg) // SLC_BLOCK
    forced = (jl == 0) | (jl == cur) | (jl == cur - 1)
    imp = jnp.where(forced, FORCE, imp)
    imp = jnp.where(jl <= cur, imp, NEG)
    sel = jnp.where(_topn_mask(imp, ns, min(SLC_TOPN, ns)) & (jl <= cur), 1.0, 0.0)
    sel_rows = jnp.concatenate([sel[(h // HG) * T_PAD:(h // HG + 1) * T_PAD] for h in range(N_HEADS)], axis=0)

    def new_tile(ref, extra_valid):
        k = _pad_rows(ref[0, :, 0:kw], LANE).astype(BF16)
        v = _pad_rows(ref[0, :, kw:2 * kw], LANE).astype(BF16)
        d = (t8 - lane).astype(F32)
        valid = (d >= 0) & (lane < T_PAD) & extra_valid
        return jnp.where(valid, _dot_nt(qx, k) - slope * d, NEG), (v, False)

    s_tiles, v_tiles = [], []
    per = page // SLC_BLOCK
    for p in range(npages):
        kt = page_refs[p][0, 0:kw, :].astype(BF16)
        v_tiles.append((page_refs[p][0, kw:2 * kw, :].astype(BF16), True))
        picked = jnp.zeros((N_HEADS * T_PAD, LANE), jnp.bool_)
        for j in range(per):
            blk = p * per + j
            picked = picked | ((lane // SLC_BLOCK == j) & (sel_rows[:, blk:blk + 1] > 0.5))
        d = (qpos - (p * page + lane)).astype(F32)
        s_tiles.append(jnp.where(picked & (d >= 0), jnp.dot(qx, kt, preferred_element_type=F32) - slope * d, NEG))
    blk_new = past // SLC_BLOCK
    s_new, v_new = new_tile(ns_ref, sel_rows[:, blk_new:blk_new + 1] > 0.5)
    o_slc = _attend_tiles(s_tiles + [s_new], v_tiles + [v_new])

    s_tiles, v_tiles = [], []
    wb = wc_ref.shape[2]
    w_off = past - wb
    for j in range(wb // LANE):
        kt = wc_ref[0, 0:kw, j * LANE:(j + 1) * LANE].astype(BF16)
        v_tiles.append((wc_ref[0, kw:2 * kw, j * LANE:(j + 1) * LANE].astype(BF16), True))
        d = qpos - (w_off + j * LANE + lane)
        valid = (d >= 0) & (d < WINDOW)
        s_tiles.append(jnp.where(valid, jnp.dot(qx, kt, preferred_element_type=F32) - slope * d.astype(F32), NEG))
    s_new, v_new = new_tile(nw_ref, True)
    o_win = _attend_tiles(s_tiles + [s_new], v_tiles + [v_new])

    sg = _sigmoid(g_ref[0])
    for h in range(N_HEADS):
        r0, l0 = h * T_PAD, (h // HG) * HEAD_DIM
        o = (sg[:, h:h + 1] * o_cmp[r0:r0 + T_PAD, l0:l0 + HEAD_DIM]
             + sg[:, N_HEADS + h:N_HEADS + h + 1] * o_slc[r0:r0 + T_PAD, l0:l0 + HEAD_DIM]
             + sg[:, 2 * N_HEADS + h:2 * N_HEADS + h + 1] * o_win[r0:r0 + T_PAD, l0:l0 + HEAD_DIM])
        o_ref[0, :, h * HEAD_DIM:(h + 1) * HEAD_DIM] = o


def _nsa_dec(c3, kvc, ov, win_cache, win_first, page_table, pool, first, past, t_real):
    bs, npages = page_table.shape
    kw = 2 * KV_G * HEAD_DIM
    nchunk = kvc.shape[1]
    nc = nchunk - CMP_LEN // CMP_STRIDE + 1
    ns = -(-(past + t_real) // SLC_BLOCK)
    wb = win_cache.shape[2]
    in_specs = [pl.BlockSpec((1, T_PAD, MIX), lambda b, pt: (b, 0, _COL['nsa_q'] // MIX)),
                pl.BlockSpec((1, T_PAD, kw), lambda b, pt: (b, 0, _COL['kv_slc'] // kw)),
                pl.BlockSpec((1, T_PAD, kw), lambda b, pt: (b, 0, _COL['kv_win'] // kw)),
                pl.BlockSpec((1, T_PAD, LANE), lambda b, pt: (b, 0, _COL['nsa_gate'] // LANE)),
                pl.BlockSpec((1, nchunk, kw), lambda b, pt: (b, 0, 0)),
                pl.BlockSpec((LANE, LANE), lambda b, pt: (0, 0)),
                pl.BlockSpec((1, kw, wb), lambda b, pt: (win_first + b, 0, 0))]
    in_specs += _page_specs(pool, npages, first)
    return pl.pallas_call(
        functools.partial(_nsa_dec_kernel, npages=npages, past=past, nc=nc, ns=ns),
        grid_spec=pltpu.PrefetchScalarGridSpec(
            num_scalar_prefetch=1, grid=(bs,), in_specs=in_specs,
            out_specs=pl.BlockSpec((1, T_PAD, MIX), lambda b, pt: (b, 0, 0))),
        out_shape=jax.ShapeDtypeStruct((bs, T_PAD, MIX), F32),
        compiler_params=_cparams(("parallel",)), name="nsa_decode")(
            page_table, c3, c3, c3, c3, kvc, ov, win_cache, *([pool] * npages))


def _s5_disc_kernel(ar_ref, ai_ref, ls_ref, btr_ref, bti_ref, abr_ref, abi_ref, bbr_ref, bbi_ref):
    ar, ai = ar_ref[...], ai_ref[...]
    step = jnp.exp(ls_ref[...])
    mag = jnp.exp(ar * step)
    abr = mag * jnp.cos(ai * step)
    abi = mag * jnp.sin(ai * step)
    den = ar * ar + ai * ai
    zr = (ar * (abr - 1.0) + ai * abi) / den
    zi = (ar * abi - ai * (abr - 1.0)) / den
    abr_ref[...] = abr
    abi_ref[...] = abi
    btr, bti = btr_ref[...], bti_ref[...]
    bbr_ref[...] = zr * btr - zi * bti
    bbi_ref[...] = zr * bti + zi * btr


def _s5_disc(a_re, a_im, log_step, b_re, b_im):
    rep = lambda a: jnp.repeat(a, S5_CH, axis=0)
    n = S5_GROUPS * S5_CH
    args = (rep(a_re), rep(a_im), rep(jnp.broadcast_to(log_step[:, None], (S5_GROUPS, S5_STATE))),
            b_re.transpose(0, 2, 1).reshape(n, S5_STATE), b_im.transpose(0, 2, 1).reshape(n, S5_STATE))
    shp = jax.ShapeDtypeStruct((n, S5_STATE), F32)
    return pl.pallas_call(_s5_disc_kernel, out_shape=(shp, shp, shp, shp), name="s5_discretise")(*args)


_S5_CH_ROWS = 128


def _s5_scan_kernel(*refs, seg, has_h0):
    if has_h0:
        (u_ref, bre_ref, bim_ref, ar_ref, ai_ref, cre_ref, cim_ref, d_ref, h0r_ref, h0i_ref,
         y_ref, hlr_ref, hli_ref, hr_s, hi_s) = refs
    else:
        (u_ref, bre_ref, bim_ref, ar_ref, ai_ref, cre_ref, cim_ref, d_ref,
         y_ref, hlr_ref, hli_ref, hr_s, hi_s) = refs
    rows = u_ref.shape[0]
    ch = min(_S5_CH_ROWS, rows)
    pad = ch
    nch = rows // ch
    ar, ai = ar_ref[0], ai_ref[0]
    bre, bim, cre, cim = (_split_bf16(r[0]) for r in (bre_ref, bim_ref, cre_ref, cim_ref))
    hr_s[0:pad, :] = jnp.zeros((pad, hr_s.shape[1]), F32)
    hi_s[0:pad, :] = jnp.zeros((pad, hi_s.shape[1]), F32)

    def init_body(i, _):
        r0 = pl.multiple_of(i * ch, ch)
        u = u_ref[pl.ds(r0, ch), :]
        br = _dot3(u, bre)
        bi = _dot3(u, bim)
        if has_h0:
            h0r, h0i = h0r_ref[pl.ds(r0, ch), :], h0i_ref[pl.ds(r0, ch), :]
            br = br + (ar * h0r - ai * h0i)
            bi = bi + (ar * h0i + ai * h0r)
        hr_s[pl.ds(pad + r0, ch), :] = br
        hi_s[pl.ds(pad + r0, ch), :] = bi
        return 0

    lax.fori_loop(0, nch, init_body, 0)

    rl = lax.broadcasted_iota(jnp.int32, (ch, 1), 0)
    pr, pi = ar, ai
    d = 1
    while d < seg:
        first = d // ch

        def pass_body(i, _, d=d, pr=pr, pi=pi):
            r0 = pl.multiple_of((nch - 1 - i) * ch, ch)
            cr = hr_s[pl.ds(pad + r0, ch), :]
            ci = hi_s[pl.ds(pad + r0, ch), :]
            if d < SUBLANE:
                lo = pad - SUBLANE
                sr = pltpu.roll(hr_s[pl.ds(lo + r0, ch + SUBLANE), :], d, 0)[SUBLANE:]
                si = pltpu.roll(hi_s[pl.ds(lo + r0, ch + SUBLANE), :], d, 0)[SUBLANE:]
            else:
                sr = hr_s[pl.ds(pad + r0 - d, ch), :]
                si = hi_s[pl.ds(pad + r0 - d, ch), :]
            if d < ch or seg < rows:
                keep = ((r0 + rl) % seg) >= d
                sr = jnp.where(keep, sr, 0.0)
                si = jnp.where(keep, si, 0.0)
            hr_s[pl.ds(pad + r0, ch), :] = cr + (pr * sr - pi * si)
            hi_s[pl.ds(pad + r0, ch), :] = ci + (pr * si + pi * sr)
            return 0

        lax.fori_loop(0, nch - first, pass_body, 0)
        pr, pi = pr * pr - pi * pi, 2.0 * pr * pi
        d *= 2

    def out_body(i, _):
        r0 = pl.multiple_of(i * ch, ch)
        hr = hr_s[pl.ds(pad + r0, ch), :]
        hi = hi_s[pl.ds(pad + r0, ch), :]
        y = _dot3(hr, cre) - _dot3(hi, cim)
        y_ref[pl.ds(r0, ch), :] = y + d_ref[0] * u_ref[pl.ds(r0, ch), :]
        return 0

    lax.fori_loop(0, nch, out_body, 0)
    nl = hlr_ref.shape[0]
    hlr_ref[...] = hr_s[pad + rows - nl:pad + rows, :]
    hli_ref[...] = hi_s[pad + rows - nl:pad + rows, :]


def _s5_scan(c, sw, rows, seg, h0=None):
    n = c.shape[0]
    nt = n // rows
    gl = LANE // S5_CH
    lt = S5_GROUPS // gl
    w = gl * S5_STATE
    nl = (rows // seg) * SUBLANE if seg == SUBLANE else SUBLANE
    ub = _COL['s5_u'] // LANE
    in_specs = [pl.BlockSpec((rows, LANE), lambda i, j: (i, ub + j)),
                pl.BlockSpec((1, LANE, w), lambda i, j: (j, 0, 0)),
                pl.BlockSpec((1, LANE, w), lambda i, j: (j, 0, 0)),
                pl.BlockSpec((1, 1, w), lambda i, j: (j, 0, 0)),
                pl.BlockSpec((1, 1, w), lambda i, j: (j, 0, 0)),
                pl.BlockSpec((1, w, LANE), lambda i, j: (j, 0, 0)),
                pl.BlockSpec((1, w, LANE), lambda i, j: (j, 0, 0)),
                pl.BlockSpec((1, 1, LANE), lambda i, j: (j, 0, 0))]
    args = [c, sw['bre'], sw['bim'], sw['ar'], sw['ai'], sw['cre'], sw['cim'], sw['d']]
    if h0 is not None:
        in_specs += [pl.BlockSpec((rows, w), lambda i, j: (i, j))] * 2
        args += list(h0)
    hshape = jax.ShapeDtypeStruct((nt * nl, S5_GROUPS * S5_STATE), F32)
    return pl.pallas_call(
        functools.partial(_s5_scan_kernel, seg=seg, has_h0=h0 is not None), grid=(nt, lt),
        in_specs=in_specs,
        out_specs=(pl.BlockSpec((rows, LANE), lambda i, j: (i, j)),
                   pl.BlockSpec((nl, w), lambda i, j: (i, j)), pl.BlockSpec((nl, w), lambda i, j: (i, j))),
        out_shape=(jax.ShapeDtypeStruct((n, MIX), F32), hshape, hshape),
        scratch_shapes=[pltpu.VMEM((min(_S5_CH_ROWS, rows) + rows, w), F32)] * 2,
        compiler_params=_cparams(("parallel", "arbitrary")), name="s5_scan")(*args)


def _s5_weights(lp):
    abr, abi, bbr, bbi = _s5_disc(lp['s5_a_re'], lp['s5_a_im'], lp['s5_log_step'], lp['s5_b_re'], lp['s5_b_im'])
    gl = LANE // S5_CH
    lt = S5_GROUPS // gl
    eye = jnp.eye(gl, dtype=F32)

    def bdiag(bb):
        return jnp.einsum('jgcn,gh->jgchn', bb.reshape(lt, gl, S5_CH, S5_STATE), eye).reshape(
            lt, gl * S5_CH, gl * S5_STATE)

    def cdiag(cc):
        return jnp.einsum('jgcn,gh->jgnhc', cc.reshape(lt, gl, S5_CH, S5_STATE), eye).reshape(
            lt, gl * S5_STATE, gl * S5_CH)

    return dict(bre=bdiag(bbr), bim=bdiag(bbi),
                ar=abr[::S5_CH].reshape(lt, 1, gl * S5_STATE), ai=abi[::S5_CH].reshape(lt, 1, gl * S5_STATE),
                cre=cdiag(lp['s5_c_re']), cim=cdiag(lp['s5_c_im']), d=lp['s5_d'].reshape(lt, 1, LANE))


def _glu_kernel(y_ref, w_ref, o_ref):
    y = y_ref[...]
    g = 0.5 * y * (1.0 + jnp.tanh(np.float32(np.sqrt(2.0 / np.pi)) * (y + np.float32(0.044715) * (y * y * y))))
    z = jnp.dot(g.astype(BF16), w_ref[...], preferred_element_type=F32)
    o_ref[...] = z[:, :MIX] * _sigmoid(z[:, MIX:])


def _glu(y, w_l, tm):
    w, l = w_l
    n = y.shape[0]
    return pl.pallas_call(
        _glu_kernel, grid=(n // tm,),
        in_specs=[pl.BlockSpec((tm, MIX), lambda i: (i, 0)),
                  pl.BlockSpec((None, MIX, 2 * MIX), lambda i: (l, 0, 0))],
        out_specs=pl.BlockSpec((tm, MIX), lambda i: (i, 0)),
        out_shape=jax.ShapeDtypeStruct((n, MIX), F32),
        compiler_params=_cparams(("parallel",)), name="s5_glu")(y, w)


def _layer_norm(x, g, b):
    mu = jnp.mean(x, axis=-1, keepdims=True)
    xc = x - mu
    var = jnp.mean(xc * xc, axis=-1, keepdims=True)
    return xc * lax.rsqrt(var + LN_EPS) * g + b


def _merge_kernel(o0_ref, o1_ref, o2_ref, o3_ref, mg_ref, wb_ref, x_ref, wo_ref, g_ref, b_ref, out_ref, acc_ref,
                  *, alpha):
    i = pl.program_id(1)

    @pl.when(i == 0)
    def _():
        acc_ref[...] = jnp.zeros_like(acc_ref)

    for k, o_ref in enumerate((o0_ref, o1_ref, o2_ref, o3_ref)):
        @pl.when(i == k)
        def _(o_ref=o_ref):
            proj = jnp.dot(o_ref[...].astype(BF16), wb_ref[0], preferred_element_type=F32)
            acc_ref[...] += _sigmoid(mg_ref[...]) * proj

    @pl.when(i == N_BRANCH - 1)
    def _():
        mixed = jnp.dot(acc_ref[...].astype(BF16), wo_ref[...], preferred_element_type=F32)
        out_ref[...] = _layer_norm(alpha * x_ref[...] + mixed, g_ref[...], b_ref[...])


def _merge(outs, c, x, wb_l, wo_l, g, b, tm, alpha):
    (wb, l), wo = wb_l, wo_l[0]
    n = x.shape[0]
    o_spec = pl.BlockSpec((tm, MIX), lambda r, i: (r, 0))
    return pl.pallas_call(
        functools.partial(_merge_kernel, alpha=alpha), grid=(n // tm, N_BRANCH),
        in_specs=[o_spec, o_spec, o_spec, o_spec,
                  pl.BlockSpec((tm, D_MODEL), lambda r, i: (r, i)),
                  pl.BlockSpec((None, 1, MIX, D_MODEL), lambda r, i: (l, i, 0, 0)),
                  pl.BlockSpec((tm, D_MODEL), lambda r, i: (r, 0)),
                  pl.BlockSpec((None, D_MODEL, D_MODEL), lambda r, i: (l, 0, 0), pipeline_mode=pl.Buffered(1)),
                  pl.BlockSpec((1, D_MODEL), lambda r, i: (0, 0)),
                  pl.BlockSpec((1, D_MODEL), lambda r, i: (0, 0))],
        out_specs=pl.BlockSpec((tm, D_MODEL), lambda r, i: (r, 0)),
        out_shape=jax.ShapeDtypeStruct((n, D_MODEL), F32),
        scratch_shapes=[pltpu.VMEM((tm, D_MODEL), F32)],
        compiler_params=_cparams(("parallel", "arbitrary")), name="merge_out_ln")(
            *outs, c, wb, x, wo, g, b)


def _ffn_kernel(*refs, halo, seg, alpha):
    (h_ref, wg_ref, wv_ref, cwg_ref, cwv_ref, cbg_ref, cbv_ref, wd_ref, lg_ref, lb_ref) = refs[:10]
    pg_ref, pv_ref, out_ref, hb_ref, acc_ref = refs[10:]
    j = pl.program_id(1)

    @pl.when(j == 0)
    def _():
        hb_ref[...] = h_ref[...].astype(BF16)
        acc_ref[...] = jnp.zeros_like(acc_ref)

    hb = hb_ref[...]
    tm = hb.shape[0]
    rid = lax.broadcasted_iota(jnp.int32, (tm, 1), 0)

    def conv(u, cw_ref, cb_ref, prev):
        r1 = pltpu.roll(u, 1, 0)
        r2 = pltpu.roll(u, 2, 0)
        if halo:
            p = prev[0][0]
            p6, p7 = p[SUBLANE - 2:SUBLANE - 1], p[SUBLANE - 1:SUBLANE]
            u1 = jnp.where(rid == 0, p7, r1)
            u2 = jnp.where(rid == 0, p6, jnp.where(rid == 1, p7, r2))
        else:
            t = rid % seg
            state = jnp.where(t >= seg - (CONV_W - 1), prev[0][...], u)
            u1 = jnp.where(t >= 1, r1, pltpu.roll(state, tm - (seg - 1), 0))
            u2 = jnp.where(t >= 2, r2, pltpu.roll(state, tm - (seg - 2), 0))
        cw = cw_ref[...]
        return cb_ref[...] + (cw[0:1] * u2 + cw[1:2] * u1 + cw[2:3] * u)

    ug = jnp.dot(hb, wg_ref[...], preferred_element_type=F32)
    uv = jnp.dot(hb, wv_ref[...], preferred_element_type=F32)
    gate = conv(ug, cwg_ref, cbg_ref, (pg_ref,))
    val = conv(uv, cwv_ref, cbv_ref, (pv_ref,))
    act = gate * _sigmoid(gate) * val
    acc_ref[...] += jnp.dot(act.astype(BF16), wd_ref[...], preferred_element_type=F32)

    @pl.when(j == pl.num_programs(1) - 1)
    def _():
        out_ref[...] = _layer_norm(alpha * h_ref[...] + acc_ref[...], lg_ref[...], lb_ref[...])


def _ffn(h, lw, tm, tf, alpha, prev=None, state_rows=None, seg=None):
    n = h.shape[0]
    nf = D_FF // tf
    halo = prev is not None
    (w_up, l), w_down = lw['w_up'], lw['w_down'][0]
    in_specs = [pl.BlockSpec((tm, D_MODEL), lambda r, j: (r, 0), pipeline_mode=pl.Buffered(1)),
                pl.BlockSpec((None, D_MODEL, tf), lambda r, j: (l, 0, j)),
                pl.BlockSpec((None, D_MODEL, tf), lambda r, j: (l, 0, nf + j)),
                pl.BlockSpec((CONV_W, tf), lambda r, j: (0, j)),
                pl.BlockSpec((CONV_W, tf), lambda r, j: (0, nf + j)),
                pl.BlockSpec((1, tf), lambda r, j: (0, j)),
                pl.BlockSpec((1, tf), lambda r, j: (0, nf + j)),
                pl.BlockSpec((None, tf, D_MODEL), lambda r, j: (l, j, 0)),
                pl.BlockSpec((1, D_MODEL), lambda r, j: (0, 0)),
                pl.BlockSpec((1, D_MODEL), lambda r, j: (0, 0))]
    args = [h, w_up, w_up, lw['conv_w'], lw['conv_w'], lw['conv_b'], lw['conv_b'], w_down,
            lw['ln2_g'], lw['ln2_b']]
    if halo:
        in_specs += [pl.BlockSpec((1, SUBLANE, tf), lambda r, j: (r, 0, j)),
                     pl.BlockSpec((1, SUBLANE, tf), lambda r, j: (r, 0, nf + j))]
        args += [prev, prev]
    else:
        in_specs += [pl.BlockSpec((tm, tf), lambda r, j: (r, j)), pl.BlockSpec((tm, tf), lambda r, j: (r, nf + j))]
        args += [state_rows, state_rows]
    return pl.pallas_call(
        functools.partial(_ffn_kernel, halo=halo, seg=seg, alpha=alpha), grid=(n // tm, nf),
        in_specs=in_specs,
        out_specs=pl.BlockSpec((tm, D_MODEL), lambda r, j: (r, 0), pipeline_mode=pl.Buffered(1)),
        out_shape=jax.ShapeDtypeStruct((n, D_MODEL), F32),
        scratch_shapes=[pltpu.VMEM((tm, D_MODEL), BF16), pltpu.VMEM((tm, D_MODEL), F32)],
        compiler_params=_cparams(("parallel", "arbitrary")), name="conv_ffn_ln")(*args)


def _prep_shared(p):
    return dict(w_in=_pack_w_in(p['w_in']), w_glu=p['s5_w_glu'].astype(BF16), w_branch=p['w_branch'].astype(BF16),
                w_out=p['w_out'].astype(BF16), w_up=p['ffn_w_up'].astype(BF16), w_down=p['ffn_w_down'].astype(BF16))


def _prep_layer(l, p, shared):
    lp = {k: v[l] for k, v in p.items() if k not in ('w_in', 's5_w_glu', 'w_branch', 'w_out', 'ffn_w_up', 'ffn_w_down')}
    lw = {k: (v, l) for k, v in shared.items()}
    lw.update(
        fox_b=jnp.zeros((1, LANE), F32).at[0, :N_HEADS].set(lp['fox_b_f']),
        cmp=_cmp_weights(lp['nsa_cmp_pos'], lp['nsa_cmp_wk'], lp['nsa_cmp_wv']),
        s5=_s5_weights(lp),
        ln1_g=lp['ln1_g'].reshape(1, -1), ln1_b=lp['ln1_b'].reshape(1, -1),
        conv_w=lp['ffn_conv_w'], conv_b=lp['ffn_conv_b'].reshape(1, -1),
        ln2_g=lp['ln2_g'].reshape(1, -1), ln2_b=lp['ln2_b'].reshape(1, -1))
    return lw


def _prompt_layer(x, b, t, lw, alpha):
    n = b * t
    kw = 2 * KV_G * HEAD_DIM
    tr = min(1024, n)
    c = _matmul(x, lw['w_in'], tr, 512)
    logf = _logf(c, lw['fox_b'], tr)
    cumr = _cumsum_prompt(logf.reshape(b, t, N_HEADS).transpose(0, 2, 1))
    o_fox = _fox_prompt(c, cumr[:, :, None, :], b, t, 512)
    o_moba = _moba_prompt(c, b, t, 2 * MOBA_BLOCK)
    kvc = _cmp_prompt(c, b, t, lw['cmp'])
    nchunk = t // CMP_STRIDE
    ov = _overlap_matrix(nchunk - CMP_LEN // CMP_STRIDE + 1, -(-t // SLC_BLOCK))
    o_nsa = _nsa_prompt(c, kvc, ov, b, t, 256)
    y_s5, hlr, hli = _s5_scan(c, lw['s5'], t, t)
    o_s5 = _glu(y_s5, lw['w_glu'], tr)
    h = _merge((o_nsa, o_s5, o_moba, o_fox), c, x, lw['w_branch'], lw['w_out'], lw['ln1_g'], lw['ln1_b'], 512, alpha)

    tm = min(1024, t)
    nt = n // tm
    edge = h.reshape(nt, tm, D_MODEL)[:, tm - (CONV_W - 1):].reshape(nt * (CONV_W - 1), D_MODEL)
    edge = _pad_rows(edge, -(-edge.shape[0] // SUBLANE) * SUBLANE)
    u_edge = _matmul(edge, lw['w_up'], edge.shape[0], 512)[:nt * (CONV_W - 1)].reshape(nt, CONV_W - 1, 2 * D_FF)
    per_seq = t // tm
    conv_state = u_edge[per_seq - 1::per_seq]
    starts_seq = (jnp.arange(nt) % per_seq == 0)[:, None, None]
    prev = jnp.where(starts_seq, 0.0, jnp.roll(u_edge, 1, axis=0))
    prev = jnp.pad(prev, ((0, 0), (SUBLANE - (CONV_W - 1), 0), (0, 0)))
    y = _ffn(h, lw, tm, 512, alpha, prev=prev)

    win_rows = min(WINDOW, t)
    states = (
        _kv_state(c, b, t, _COL['kv_cmp'], kw),
        _kv_state(c, b, t, _COL['kv_slc'], kw),
        _kv_state(c, b, t, _COL['moba'] + MIX, 2 * MIX),
        _kv_state(c, b, t, _COL['fox'] + MIX, 2 * MIX),
        logf.reshape(b, t, N_HEADS),
        _kv_state(c, b, t, _COL['kv_win'], kw, t_from=t - win_rows),
        hlr.reshape(b, SUBLANE, S5_GROUPS, S5_STATE)[:, SUBLANE - 1],
        hli.reshape(b, SUBLANE, S5_GROUPS, S5_STATE)[:, SUBLANE - 1],
        conv_state)
    return y, states


def _sample_layer(x, bs, t_real, lw, alpha, past, page_table, past_len):
    n = bs * T_PAD
    kw = 2 * KV_G * HEAD_DIM
    c = _matmul(x, lw['w_in'], n, 512)
    c3 = c.reshape(bs, T_PAD, WP)
    logf = _logf(c, lw['fox_b'], n)
    logf3 = logf.reshape(bs, T_PAD, N_HEADS)
    tmask = (jnp.arange(T_PAD) < t_real)[None, :, None]
    new_t = jnp.pad(jnp.where(tmask, logf3, 0.0).transpose(0, 2, 1), ((0, 0), (0, 0), (0, LANE - T_PAD)))
    first = past['first_page']
    npages = page_table.shape[1]
    o_fox = _paged_mha_dec(
        _fox_dec_kernel, "fox_decode", c3, _COL['fox'], page_table, past['fox'], first,
        [new_t] + [past['fox_logf']] * npages,
        [pl.BlockSpec((1, N_HEADS, LANE), lambda b, pt: (b, 0, 0))] + _page_specs(past['fox_logf'], npages, first),
        past_len)
    o_moba = _paged_mha_dec(_moba_dec_kernel, "moba_decode", c3, _COL['moba'], page_table, past['moba'], first,
                            [], [], past_len)
    kvc = _cmp_paged(page_table, past['nsa_cmp'], first, lw['cmp'])
    nchunk = kvc.shape[1]
    ov = _overlap_matrix(nchunk - CMP_LEN // CMP_STRIDE + 1, -(-(past_len + t_real) // SLC_BLOCK))
    o_nsa = _nsa_dec(c3, kvc, ov, past['nsa_win'], past['first_seq'], page_table, past['nsa_slc'], first,
                     past_len, t_real)
    h0 = [jnp.pad(s.reshape(bs, 1, -1), ((0, 0), (0, T_PAD - 1), (0, 0))).reshape(n, -1) for s in past['s5']]
    y_s5, hlr, hli = _s5_scan(c, lw['s5'], n, T_PAD, h0=h0)
    o_s5 = _glu(y_s5, lw['w_glu'], n)
    h = _merge((o_nsa.reshape(n, MIX), o_s5, o_moba.reshape(n, MIX), o_fox.reshape(n, MIX)), c, x,
               lw['w_branch'], lw['w_out'], lw['ln1_g'], lw['ln1_b'], min(256, n), alpha)

    buf = past['ffn_conv']
    state_rows = jnp.pad(buf, ((0, 0), (T_PAD - (CONV_W - 1), 0), (0, 0))).reshape(n, 2 * D_FF)
    y = _ffn(h, lw, min(512, n), 512, alpha, state_rows=state_rows, seg=T_PAD)
    last2 = h.reshape(bs, T_PAD, D_MODEL)[:, t_real - (CONV_W - 1):t_real].reshape(bs * (CONV_W - 1), D_MODEL)
    conv_state = _matmul(last2, lw['w_up'], last2.shape[0], 512).reshape(bs, CONV_W - 1, 2 * D_FF)

    tr = lambda a: a[:, :t_real]
    kv_win_new = tr(c3[:, :, _COL['kv_win']:_COL['kv_win'] + kw])
    states = (
        tr(c3[:, :, _COL['kv_cmp']:_COL['kv_cmp'] + kw]).reshape(bs, t_real, 2, KV_G, HEAD_DIM),
        tr(c3[:, :, _COL['kv_slc']:_COL['kv_slc'] + kw]).reshape(bs, t_real, 2, KV_G, HEAD_DIM),
        tr(c3[:, :, _COL['moba'] + MIX:_COL['moba'] + 3 * MIX]).reshape(bs, t_real, 2, N_HEADS, HEAD_DIM),
        tr(c3[:, :, _COL['fox'] + MIX:_COL['fox'] + 3 * MIX]).reshape(bs, t_real, 2, N_HEADS, HEAD_DIM),
        tr(logf3),
        kv_win_new,
        hlr.reshape(bs, T_PAD, S5_GROUPS, S5_STATE)[:, t_real - 1],
        hli.reshape(bs, T_PAD, S5_GROUPS, S5_STATE)[:, t_real - 1],
        conv_state)
    return y, states


def kernel(x_prompt, x_sample, cache_nsa_cmp_kv, cache_nsa_slc_kv, cache_moba_kv, cache_fox_kv, cache_fox_logf,
           page_table, cache_nsa_win_kv, state_s5_re, state_s5_im, state_ffn_conv, w_in, fox_b_f, nsa_cmp_pos,
           nsa_cmp_wk, nsa_cmp_wv, s5_a_re, s5_a_im, s5_b_re, s5_b_im, s5_c_re, s5_c_im, s5_d, s5_log_step,
           s5_w_glu, w_branch, w_out, ln1_g, ln1_b, ffn_w_up, ffn_conv_w, ffn_conv_b, ffn_w_down, ln2_g, ln2_b):
    depth = w_in.shape[0]
    b, t, d = x_prompt.shape
    bs, ts, _ = x_sample.shape
    n_phys, page = cache_nsa_cmp_kv.shape[1:3]
    past_len = page_table.shape[1] * page
    assert d == D_MODEL and w_in.shape[2] == IN_WIDTH and ffn_w_down.shape[1] == D_FF
    assert ts <= T_PAD - (CONV_W - 1) and past_len % MOBA_BLOCK == 0 and page == LANE and (past_len + ts) // CMP_STRIDE * CMP_STRIDE <= past_len
    alpha = float((2 * depth) ** 0.25)
    params = dict(w_in=w_in, fox_b_f=fox_b_f, nsa_cmp_pos=nsa_cmp_pos, nsa_cmp_wk=nsa_cmp_wk, nsa_cmp_wv=nsa_cmp_wv,
                  s5_a_re=s5_a_re, s5_a_im=s5_a_im, s5_b_re=s5_b_re, s5_b_im=s5_b_im, s5_c_re=s5_c_re,
                  s5_c_im=s5_c_im, s5_d=s5_d, s5_log_step=s5_log_step, s5_w_glu=s5_w_glu, w_branch=w_branch,
                  w_out=w_out, ln1_g=ln1_g, ln1_b=ln1_b, ffn_w_up=ffn_w_up, ffn_conv_w=ffn_conv_w,
                  ffn_conv_b=ffn_conv_b, ffn_w_down=ffn_w_down, ln2_g=ln2_g, ln2_b=ln2_b)
    kw = 2 * KV_G * HEAD_DIM
    yp = x_prompt.reshape(b * t, d)
    ys = jnp.pad(x_sample, ((0, 0), (0, T_PAD - ts), (0, 0))).reshape(bs * T_PAD, d)
    st_p, st_s = [], []

    def feature_major(cache):
        dd, nn, rr = cache.shape[:3]
        return cache.transpose(0, 1, 3, 4, 5, 2).reshape(dd * nn, -1, rr)

    cmp_fm, slc_fm, moba_fm, fox_fm, win_fm = (
        feature_major(a) for a in (cache_nsa_cmp_kv, cache_nsa_slc_kv, cache_moba_kv, cache_fox_kv, cache_nsa_win_kv))
    logf_hm = cache_fox_logf.transpose(0, 1, 3, 2).reshape(depth * n_phys, N_HEADS, page)
    shared = _prep_shared(params)
    for l in range(depth):
        lw = _prep_layer(l, params, shared)
        past = dict(
            nsa_cmp=cmp_fm, nsa_slc=slc_fm, moba=moba_fm, fox=fox_fm, fox_logf=logf_hm, nsa_win=win_fm,
            first_page=l * n_phys, first_seq=l * bs,
            s5=(state_s5_re[l], state_s5_im[l]),
            ffn_conv=state_ffn_conv[l])
        yp, sp = _prompt_layer(yp, b, t, lw, alpha)
        ys, ss = _sample_layer(ys, bs, ts, lw, alpha, past, page_table, past_len)
        st_p.append(sp)
        st_s.append(ss)
    sp = [jnp.stack(z) for z in zip(*st_p)]
    ss = [jnp.stack(z) for z in zip(*st_s)]
    wb = win_fm.shape[2]
    full_win = jnp.concatenate([win_fm.reshape(depth, bs, kw, wb), ss[5].transpose(0, 1, 3, 2)], axis=3)
    keep = min(WINDOW, wb + ts)
    ss[5] = full_win[..., wb + ts - keep:].transpose(0, 1, 3, 2).reshape(depth, bs, keep, 2, KV_G, HEAD_DIM)
    out = [yp.reshape(b, t, d), ys.reshape(bs, T_PAD, d)[:, :ts]]
    for a, c in zip(sp, ss):
        out += [a, c]
    return tuple(out)
```

```python
import functools

import numpy as np
import jax
import jax.numpy as jnp
from jax import lax
from jax.experimental import pallas as pl
from jax.experimental.pallas import tpu as pltpu

F32 = jnp.float32
BF16 = jnp.bfloat16
HI = lax.Precision.HIGHEST

LANE = 128
SUBLANE = 8
VMEM_LIMIT = 56 * 1024 * 1024

D_MODEL = 2048
HEAD_DIM = 64
N_BRANCH = 4
MIX = D_MODEL // N_BRANCH
N_HEADS = MIX // HEAD_DIM
KV_G = 2
HG = N_HEADS // KV_G
CMP_LEN = 32
CMP_STRIDE = 16
SLC_BLOCK = 64
SLC_TOPN = 16
WINDOW = 512
MOBA_BLOCK = 256
MOBA_TOPK = 3
S5_CH = 16
S5_GROUPS = MIX // S5_CH
S5_STATE = 64
D_FF = 5632
CONV_W = 3
LN_EPS = 1e-5
SCALE = HEAD_DIM ** -0.5
NEG = -1e30
FORCE = 1e4
T_PAD = 8
SLOPES = tuple(float(v) for v in np.asarray(2.0 ** (-8.0 * np.arange(1, N_HEADS + 1) / N_HEADS), np.float32))

_SPLITS = (('nsa_q', MIX), ('kv_cmp', 2 * KV_G * HEAD_DIM), ('kv_slc', 2 * KV_G * HEAD_DIM),
           ('kv_win', 2 * KV_G * HEAD_DIM), ('nsa_gate', 3 * N_HEADS), ('s5_u', MIX),
           ('moba', 3 * MIX), ('fox', 3 * MIX), ('fox_f', N_HEADS), ('merge', N_BRANCH * D_MODEL))
_SRC = {}
_o = 0
for _n, _w in _SPLITS:
    _SRC[_n] = (_o, _w)
    _o += _w
IN_WIDTH = _o
_COL = dict(merge=0, nsa_q=8192, kv_cmp=8704, kv_slc=8960, kv_win=9216, nsa_gate=9472, s5_u=9600,
            moba=10240, fox=11776, fox_f=13312)
WP = 13824


def _cparams(sem):
    return pltpu.CompilerParams(dimension_semantics=sem, vmem_limit_bytes=VMEM_LIMIT)


def _dot_nt(a, b, precision=None):
    return lax.dot_general(a, b, (((1,), (1,)), ((), ())), precision=precision, preferred_element_type=F32)


def _split_bf16(x):
    hi = x.astype(BF16)
    return hi, (x - hi.astype(F32)).astype(BF16)


def _dot3(a, b_split):
    a_hi, a_lo = _split_bf16(a)
    b_hi, b_lo = b_split
    dot = functools.partial(jnp.dot, preferred_element_type=F32)
    return dot(a_hi, b_hi) + (dot(a_hi, b_lo) + dot(a_lo, b_hi))


def _sigmoid(x):
    return 1.0 / (1.0 + jnp.exp(-x))


def _pack_w_in(w):
    lead = w.shape[:-1]
    order = ('merge', 'nsa_q', 'kv_cmp', 'kv_slc', 'kv_win', 'nsa_gate', 's5_u', 'moba', 'fox', 'fox_f')
    parts, pos = [], 0
    for name in order:
        if _COL[name] > pos:
            parts.append(jnp.zeros(lead + (_COL[name] - pos,), w.dtype))
        s, wd = _SRC[name]
        parts.append(w[..., s:s + wd])
        pos = _COL[name] + wd
    parts.append(jnp.zeros(lead + (WP - pos,), w.dtype))
    return jnp.concatenate(parts, axis=-1).astype(BF16)


def _mm_kernel(x_ref, w_ref, o_ref, xb_ref):
    @pl.when(pl.program_id(1) == 0)
    def _():
        xb_ref[...] = x_ref[...].astype(BF16)

    o_ref[...] = jnp.dot(xb_ref[...], w_ref[...], preferred_element_type=F32)


def _matmul(x, w_l, tm, tn):
    w, l = w_l
    m, k = x.shape
    n = w.shape[2]
    return pl.pallas_call(
        _mm_kernel, grid=(m // tm, n // tn),
        in_specs=[pl.BlockSpec((tm, k), lambda i, j: (i, 0)), pl.BlockSpec((None, k, tn), lambda i, j: (l, 0, j))],
        out_specs=pl.BlockSpec((tm, tn), lambda i, j: (i, j)),
        out_shape=jax.ShapeDtypeStruct((m, n), F32),
        scratch_shapes=[pltpu.VMEM((tm, k), BF16)],
        compiler_params=_cparams(("parallel", "arbitrary")), name="mm")(x, w)


def _transpose_kernel(c_ref, o_ref):
    o_ref[...] = c_ref[...].T


def _kv_state(c, b, t, col, width, t_from=0):
    tf = min(512, width)
    tt = 512
    nt, n0 = t // tt, t_from // tt
    st = pl.pallas_call(
        _transpose_kernel, grid=(b, width // tf, nt - n0),
        in_specs=[pl.BlockSpec((tt, tf), lambda i, f, q: (i * nt + n0 + q, col // tf + f))],
        out_specs=pl.BlockSpec((None, tf, tt), lambda i, f, q: (i, f, q)),
        out_shape=jax.ShapeDtypeStruct((b, width, t - t_from), F32),
        compiler_params=_cparams(("parallel", "parallel", "parallel")), name="kv_state")(c)
    return st.reshape(b, 2, width // (2 * HEAD_DIM), HEAD_DIM, t - t_from).transpose(0, 4, 1, 2, 3)


def _logf_kernel(c_ref, b_ref, o_ref):
    x = c_ref[...] + b_ref[...]
    y = jnp.minimum(x, 0.0) - jnp.log1p(jnp.exp(-jnp.abs(x)))
    o_ref[...] = y[:, :N_HEADS]


def _logf(c, b_pad, tm):
    n = c.shape[0]
    return pl.pallas_call(
        _logf_kernel, grid=(n // tm,),
        in_specs=[pl.BlockSpec((tm, LANE), lambda i: (i, _COL['fox_f'] // LANE)),
                  pl.BlockSpec((1, LANE), lambda i: (0, 0))],
        out_specs=pl.BlockSpec((tm, N_HEADS), lambda i: (i, 0)),
        out_shape=jax.ShapeDtypeStruct((n, N_HEADS), F32),
        compiler_params=_cparams(("parallel",)), name="logf")(c, b_pad)


def _cumsum_kernel(*refs, n_in):
    in_refs, o_ref = refs[-n_in - 1:-1], refs[-1]
    r = lax.broadcasted_iota(jnp.int32, (LANE, LANE), 0)
    c = lax.broadcasted_iota(jnp.int32, (LANE, LANE), 1)
    tri = jnp.where(r <= c, 1.0, 0.0).astype(F32)
    carry = jnp.zeros((N_HEADS, 1), F32)
    off = 0
    for ref in in_refs:
        for j in range(ref.shape[-1] // LANE):
            x = ref[0, :, j * LANE:(j + 1) * LANE]
            cs = jnp.dot(x, tri, precision=HI, preferred_element_type=F32) + carry
            o_ref[0, :, off:off + LANE] = cs
            carry = cs[:, LANE - 1:LANE]
            off += LANE


def _cumsum_prompt(logf_t):
    b, h, t = logf_t.shape
    return pl.pallas_call(
        functools.partial(_cumsum_kernel, n_in=1), grid=(b,),
        in_specs=[pl.BlockSpec((1, h, t), lambda i: (i, 0, 0))],
        out_specs=pl.BlockSpec((1, h, t), lambda i: (i, 0, 0)),
        out_shape=jax.ShapeDtypeStruct((b, h, t), F32),
        compiler_params=_cparams(("parallel",)), name="fox_cumsum_prompt")(logf_t)


M_FLOOR = -1e29


def _softmax_step(m, l, s):
    m_new = jnp.maximum(m, jnp.max(s, axis=1, keepdims=True))
    p = jnp.exp(s - m_new)
    alpha = jnp.exp(m - m_new)
    return m_new, alpha * l + jnp.sum(p, axis=1, keepdims=True), alpha, p.astype(BF16)


def _pipelined_attention(chains, lo, hi, score_fn, value_fn, rows, tk, dv, last_fix=None):
    def flush(c, kj, s, p_prev, a_prev, acc):
        prev = jnp.maximum(kj - 1, lo)
        return a_prev * acc + jnp.dot(p_prev, value_fn(c, prev), preferred_element_type=F32)

    def body(kj, carry):
        out = []
        for c in range(chains):
            s, p_prev, a_prev, m, l, acc = carry[c]
            acc = flush(c, kj, s, p_prev, a_prev, acc)
            m, l, alpha, p = _softmax_step(m, l, s)
            out.append((score_fn(c, kj + 1), p, alpha, m, l, acc))
        return tuple(out)

    init = tuple((score_fn(c, lo), jnp.zeros((rows, tk), BF16), jnp.ones((rows, 1), F32),
                  jnp.full((rows, 1), M_FLOOR, F32), jnp.zeros((rows, 1), F32), jnp.zeros((rows, dv), F32))
                 for c in range(chains))
    carry = lax.fori_loop(lo, hi, body, init)
    outs = []
    for c in range(chains):
        s, p_prev, a_prev, m, l, acc = carry[c]
        acc = flush(c, hi, s, p_prev, a_prev, acc)
        if last_fix is not None:
            s = last_fix(c, s)
        m, l, alpha, p = _softmax_step(m, l, s)
        acc = alpha * acc + jnp.dot(p, value_fn(c, hi), preferred_element_type=F32)
        outs.append(acc / jnp.maximum(l, 1e-30))
    return outs


def _attend_tiles(s_tiles, v_tiles):
    m = functools.reduce(jnp.maximum, [jnp.max(s, axis=1, keepdims=True) for s in s_tiles])
    l, acc = 0.0, 0.0
    for s, (v, feature_major) in zip(s_tiles, v_tiles):
        e = jnp.where(s > 0.5 * NEG, jnp.exp(s - m), 0.0)
        l = l + jnp.sum(e, axis=1, keepdims=True)
        eb = e.astype(BF16)
        acc = acc + (_dot_nt(eb, v) if feature_major else jnp.dot(eb, v, preferred_element_type=F32))
    return acc / jnp.maximum(l, 1e-30)


def _topn_mask_wide(v, ncols, topn):
    rows = v.shape[0]
    nc8 = -(-ncols // SUBLANE) * SUBLANE
    vt = v.T[:nc8, :]
    jr = lax.broadcasted_iota(jnp.int32, vt.shape, 0)
    rank = jnp.zeros(vt.shape, F32)
    for j2 in range(ncols):
        cand = vt[j2:j2 + 1, :]
        beats = (cand > vt) | ((cand == vt) & (jr > j2))
        rank = rank + jnp.where(beats, 1.0, 0.0)
    top = jnp.where(rank < topn, 1.0, 0.0)
    return _pad_rows(top, v.shape[1]).T > 0.5


def _topn_mask(v, ncols, topn):
    jl = lax.broadcasted_iota(jnp.int32, v.shape, 1)
    rank = jnp.zeros(v.shape, F32)
    for j2 in range(ncols):
        col = v[:, j2:j2 + 1]
        beats = (col > v) | ((col == v) & (jl > j2))
        rank = rank + jnp.where(beats, 1.0, 0.0)
    return rank < topn


def _pad_rows(a, rows):
    return jnp.concatenate([a, jnp.zeros((rows - a.shape[0], a.shape[1]), a.dtype)], axis=0)


def _fox_prompt_kernel(q_ref, k_ref, v_ref, cr_ref, o_ref, *, tq):
    qi = pl.program_id(2)
    q0 = pl.multiple_of(qi * tq, tq)
    row = lax.broadcasted_iota(jnp.int32, (tq, tq), 0)
    col = lax.broadcasted_iota(jnp.int32, (tq, tq), 1)
    qs = [(q_ref[:, h2 * HEAD_DIM:(h2 + 1) * HEAD_DIM] * SCALE).astype(BF16) for h2 in range(2)]
    c0 = [cr_ref[0, h2, :, pl.ds(q0, LANE)][:, 0:1] for h2 in range(2)]

    def score(h2, kj):
        ks = pl.multiple_of(kj * tq, tq)
        k = k_ref[pl.ds(ks, tq), h2 * HEAD_DIM:(h2 + 1) * HEAD_DIM].astype(BF16)
        return _dot_nt(qs[h2], k) + (c0[h2] - cr_ref[0, h2, :, pl.ds(ks, tq)])

    def value(h2, kj):
        ks = pl.multiple_of(kj * tq, tq)
        return v_ref[pl.ds(ks, tq), h2 * HEAD_DIM:(h2 + 1) * HEAD_DIM].astype(BF16)

    outs = _pipelined_attention(2, 0, qi, score, value, tq, tq, HEAD_DIM,
                                last_fix=lambda h2, s: jnp.where(col <= row, s, NEG))
    o_ref[...] = jnp.concatenate(outs, axis=1)


def _fox_prompt(c, cumr, b, t, tq):
    nq = t // tq
    base = _COL['fox'] // LANE
    hp_n = N_HEADS // 2
    return pl.pallas_call(
        functools.partial(_fox_prompt_kernel, tq=tq), grid=(b, hp_n, nq),
        in_specs=[pl.BlockSpec((tq, LANE), lambda i, h, q: (i * nq + q, base + h)),
                  pl.BlockSpec((t, LANE), lambda i, h, q: (i, base + hp_n + h)),
                  pl.BlockSpec((t, LANE), lambda i, h, q: (i, base + 2 * hp_n + h)),
                  pl.BlockSpec((1, 2, 1, t), lambda i, h, q: (i, h, 0, 0))],
        out_specs=pl.BlockSpec((tq, LANE), lambda i, h, q: (i * nq + q, h)),
        out_shape=jax.ShapeDtypeStruct((b * t, MIX), F32),
        compiler_params=_cparams(("parallel", "arbitrary", "arbitrary")), name="fox_prompt")(
            c, c, c, cumr)


def _head_slope(hp, h2):
    s = jnp.float32(SLOPES[h2])
    for k in range(1, N_HEADS // 2):
        s = jnp.where(hp == k, jnp.float32(SLOPES[2 * k + h2]), s)
    return s


def _moba_prompt_kernel(q_ref, k_ref, v_ref, o_ref, kmean_ref, *, tq, nb):
    hp = pl.program_id(1)
    qi = pl.program_id(2)
    bpt = tq // MOBA_BLOCK

    @pl.when(qi == 0)
    def _():
        kmean_ref[...] = jnp.zeros_like(kmean_ref)
        for n in range(nb):
            kmean_ref[n:n + 1, :] = jnp.mean(k_ref[n * MOBA_BLOCK:(n + 1) * MOBA_BLOCK, :], axis=0, keepdims=True)

    q0 = pl.multiple_of(qi * tq, tq)
    row = lax.broadcasted_iota(jnp.int32, (tq, tq), 0)
    col = lax.broadcasted_iota(jnp.int32, (tq, tq), 1)
    colpos = lax.broadcasted_iota(jnp.int32, (1, tq), 1)
    jl = lax.broadcasted_iota(jnp.int32, (tq, LANE), 1)
    cur = qi * bpt + lax.broadcasted_iota(jnp.int32, (tq, 1), 0) // MOBA_BLOCK
    qs, slopes, blockbias = [], [], []
    for h2 in range(2):
        lo = h2 * HEAD_DIM
        qf = q_ref[:, lo:lo + HEAD_DIM]
        gate = _dot_nt(qf, kmean_ref[:, lo:lo + HEAD_DIM], precision=HI)
        gate = jnp.where(jl < cur, gate, NEG)
        picked = (_topn_mask_wide(gate, nb, MOBA_TOPK) & (jl < cur)) | (jl == cur)
        qs.append((qf * SCALE).astype(BF16))
        slopes.append(_head_slope(hp, h2))
        blockbias.append(jnp.where(picked, 0.0, NEG))

    def score(h2, kj):
        ks = pl.multiple_of(kj * tq, tq)
        k = k_ref[pl.ds(ks, tq), h2 * HEAD_DIM:(h2 + 1) * HEAD_DIM].astype(BF16)
        bias = None
        for j in reversed(range(bpt)):
            rowbias = jnp.min(jnp.where(jl == kj * bpt + j, blockbias[h2], 0.0), axis=1, keepdims=True)
            bias = rowbias if bias is None else jnp.where(colpos < (j + 1) * MOBA_BLOCK, rowbias, bias)
        return _dot_nt(qs[h2], k) + slopes[h2] * (ks - q0 + colpos).astype(F32) + bias

    def value(h2, kj):
        ks = pl.multiple_of(kj * tq, tq)
        return v_ref[pl.ds(ks, tq), h2 * HEAD_DIM:(h2 + 1) * HEAD_DIM].astype(BF16)

    outs = _pipelined_attention(2, 0, qi, score, value, tq, tq, HEAD_DIM,
                                last_fix=lambda h2, s: jnp.where(col <= row, s, NEG))
    o_ref[...] = jnp.concatenate(outs, axis=1)


def _moba_prompt(c, b, t, tq):
    nq = t // tq
    base = _COL['moba'] // LANE
    hp_n = N_HEADS // 2
    return pl.pallas_call(
        functools.partial(_moba_prompt_kernel, tq=tq, nb=t // MOBA_BLOCK), grid=(b, hp_n, nq),
        in_specs=[pl.BlockSpec((tq, LANE), lambda i, h, q: (i * nq + q, base + h)),
                  pl.BlockSpec((t, LANE), lambda i, h, q: (i, base + hp_n + h)),
                  pl.BlockSpec((t, LANE), lambda i, h, q: (i, base + 2 * hp_n + h))],
        out_specs=pl.BlockSpec((tq, LANE), lambda i, h, q: (i * nq + q, h)),
        out_shape=jax.ShapeDtypeStruct((b * t, MIX), F32),
        scratch_shapes=[pltpu.VMEM((LANE, LANE), F32)],
        compiler_params=_cparams(("parallel", "arbitrary", "arbitrary")), name="moba_prompt")(c, c, c)


def _cmp_kernel(*refs, n_in, feature_major):
    if feature_major:
        xs_ref, refs = refs[-1], refs[:-1]
        x_refs = refs[-5 - n_in:-5]
        page = x_refs[0].shape[2]
        halves = x_refs[0].shape[1] // LANE
        for p, r in enumerate(x_refs):
            for j in range(halves):
                xs_ref[j, p * page:(p + 1) * page, :] = r[0, j * LANE:(j + 1) * LANE, :].T
        nchunk = n_in * page // CMP_STRIDE
        x = jnp.concatenate([xs_ref[j, pl.ds(l, nchunk, stride=CMP_STRIDE), :]
                             for l in range(CMP_STRIDE) for j in range(halves)], axis=1)
    else:
        x = refs[-6][0]
    pos_ref, wcat_ref, w0_ref, w1_ref, o_ref = refs[-5:]
    xb = x.astype(BF16)
    p0 = jnp.dot(xb, w0_ref[...], preferred_element_type=F32)
    p1 = jnp.dot(xb, w1_ref[...], preferred_element_type=F32)
    bias = jnp.dot(pos_ref[...].astype(BF16), wcat_ref[...], preferred_element_type=F32)[0:1]
    o_ref[0] = p0 + pltpu.roll(p1, p1.shape[0] - 1, 0) + bias


def _cmp_weights(pos, wk, wv):
    r = CMP_LEN // CMP_STRIDE
    w = jnp.stack([wk, wv]).reshape(2, r, CMP_STRIDE, HEAD_DIM, HEAD_DIM)
    e2 = jnp.eye(2, dtype=w.dtype)
    eg = jnp.eye(KV_G, dtype=w.dtype)
    big = jnp.einsum('krlde,kK,gG->rlkgdKGe', w, e2, eg)
    big = big.reshape(r, CMP_STRIDE * 2 * KV_G * HEAD_DIM, 2 * KV_G * HEAD_DIM).astype(BF16)
    wcat = jnp.concatenate([wk, wk, wv, wv], axis=1).astype(BF16)
    posb = jnp.zeros((SUBLANE, CMP_LEN * HEAD_DIM), F32).at[0].set(pos.reshape(-1))
    return posb, wcat, big[0], big[1]


def _const_specs(arrays):
    return [pl.BlockSpec(a.shape, lambda *_, nd=a.ndim: (0,) * nd) for a in arrays]


def _cmp_prompt(c, b, t, cw):
    kw = 2 * KV_G * HEAD_DIM
    nchunk = t // CMP_STRIDE
    x = c[:, _COL['kv_cmp']:_COL['kv_cmp'] + kw].reshape(b, nchunk, CMP_STRIDE * kw)
    return pl.pallas_call(
        functools.partial(_cmp_kernel, n_in=1, feature_major=False), grid=(b,),
        in_specs=[pl.BlockSpec((1, nchunk, CMP_STRIDE * kw), lambda i: (i, 0, 0))] + _const_specs(cw),
        out_specs=pl.BlockSpec((1, nchunk, kw), lambda i: (i, 0, 0)),
        out_shape=jax.ShapeDtypeStruct((b, nchunk, kw), F32),
        compiler_params=_cparams(("parallel",)), name="nsa_cmp_prompt")(x, *cw)


def _cmp_paged(page_table, pool, first, cw):
    bs, npages = page_table.shape
    kw, page = pool.shape[1:]
    nchunk = npages * page // CMP_STRIDE
    return pl.pallas_call(
        functools.partial(_cmp_kernel, n_in=npages, feature_major=True),
        grid_spec=pltpu.PrefetchScalarGridSpec(
            num_scalar_prefetch=1, grid=(bs,), in_specs=_page_specs(pool, npages, first) + _const_specs(cw),
            out_specs=pl.BlockSpec((1, nchunk, kw), lambda b, pt: (b, 0, 0)),
            scratch_shapes=[pltpu.VMEM((kw // LANE, npages * page, LANE), F32)]),
        out_shape=jax.ShapeDtypeStruct((bs, nchunk, kw), F32),
        compiler_params=_cparams(("parallel",)), name="nsa_cmp_paged")(page_table, *([pool] * npages), *cw)


def _overlap_matrix(nc, ns):
    i_c = np.arange(LANE)[:, None] * CMP_STRIDE
    j_s = np.arange(LANE)[None, :] * SLC_BLOCK
    ov = (i_c < j_s + SLC_BLOCK) & (i_c + CMP_LEN > j_s)
    ov &= (np.arange(LANE)[:, None] < nc) & (np.arange(LANE)[None, :] < ns)
    return jnp.asarray(ov, F32)


def _nsa_prompt_kernel(q_ref, slc_ref, win_ref, cmp_ref, g_ref, ov_ref, o_ref, *, tq, nc, ns):
    qi = pl.program_id(1)
    q0 = pl.multiple_of(qi * tq, tq)
    rows = HG * tq
    rl = lax.broadcasted_iota(jnp.int32, (tq, 1), 0)
    t1 = q0 + rl
    t4 = jnp.concatenate([t1] * HG, axis=0)
    lane = lax.broadcasted_iota(jnp.int32, (1, LANE), 1)
    colpos = lax.broadcasted_iota(jnp.int32, (1, tq), 1)
    dloc = lax.broadcasted_iota(jnp.int32, (tq, tq), 0) - lax.broadcasted_iota(jnp.int32, (tq, tq), 1)
    sg = _sigmoid(g_ref[...])
    jl = lax.broadcasted_iota(jnp.int32, (tq, LANE), 1)
    cur = t1 // SLC_BLOCK
    erow = lax.broadcasted_iota(jnp.int32, (LANE, tq), 0)
    ecol = lax.broadcasted_iota(jnp.int32, (LANE, tq), 1)
    wtiles = WINDOW // tq
    qs, slope, o_cmp, blockbias = [], [], [], []
    for g in range(KV_G):
        kl = g * HEAD_DIM
        vl = KV_G * HEAD_DIM + g * HEAD_DIM
        qg = jnp.concatenate([q_ref[:, (g * HG + h) * HEAD_DIM:(g * HG + h + 1) * HEAD_DIM] for h in range(HG)],
                             axis=0)
        qs.append((qg * SCALE).astype(BF16))
        slope.append(jnp.concatenate([jnp.full((tq, 1), SLOPES[g * HG + h], F32) for h in range(HG)], axis=0))

        kc = cmp_ref[0, :, kl:kl + HEAD_DIM].astype(BF16)
        vc = cmp_ref[0, :, vl:vl + HEAD_DIM].astype(BF16)
        dist = (t4 - (lane * CMP_STRIDE + CMP_LEN - 1)).astype(F32)
        ok_c = (dist >= 0) & (lane < nc)
        s = jnp.where(ok_c, _dot_nt(qs[g], kc) - slope[g] * dist, NEG)
        m = jnp.max(s, axis=1, keepdims=True)
        e = jnp.where(ok_c, jnp.exp(s - m), 0.0)
        p_c = e / jnp.maximum(jnp.sum(e, axis=1, keepdims=True), 1e-30)
        o_cmp.append(jnp.dot(p_c.astype(BF16), vc, preferred_element_type=F32))

        psum = p_c[0:tq]
        for h in range(1, HG):
            psum = psum + p_c[h * tq:(h + 1) * tq]
        imp = jnp.dot(psum, ov_ref[...], precision=HI, preferred_element_type=F32)
        forced = (jl == 0) | (jl == cur) | (jl == cur - 1)
        imp = jnp.where(forced, FORCE, imp)
        imp = jnp.where(jl <= cur, imp, NEG)
        picked = _topn_mask_wide(imp, ns, min(SLC_TOPN, ns)) & (jl <= cur)
        blockbias.append(jnp.where(picked, 0.0, NEG).astype(BF16))

    def scores(ref, g, ks, bias):
        k = ref[pl.ds(ks, tq), g * HEAD_DIM:(g + 1) * HEAD_DIM].astype(BF16)
        cpos = (ks - q0 + colpos).astype(F32)
        return _dot_nt(qs[g], k) + slope[g] * cpos + jnp.concatenate([bias] * HG, axis=0)

    def values(ref, g, kj):
        ks = pl.multiple_of(kj * tq, tq)
        vl = KV_G * HEAD_DIM + g * HEAD_DIM
        return ref[pl.ds(ks, tq), vl:vl + HEAD_DIM].astype(BF16)

    def slc_score(g, kj):
        ks = pl.multiple_of(kj * tq, tq)
        expand = jnp.where((ks + ecol) // SLC_BLOCK == erow, 1.0, 0.0).astype(BF16)
        return scores(slc_ref, g, ks, jnp.dot(blockbias[g], expand, preferred_element_type=F32))

    def win_score(g, kj):
        ks = pl.multiple_of(kj * tq, tq)
        d = dloc + (q0 - ks)
        return scores(win_ref, g, ks, jnp.where((d >= 0) & (d < WINDOW), 0.0, NEG))

    causal = jnp.concatenate([dloc] * HG, axis=0) >= 0
    o_slc = _pipelined_attention(KV_G, 0, qi, slc_score, functools.partial(values, slc_ref), rows, tq, HEAD_DIM,
                                 last_fix=lambda g, s: jnp.where(causal, s, NEG))
    o_win = _pipelined_attention(KV_G, jnp.maximum(qi - wtiles, 0), qi, win_score,
                                 functools.partial(values, win_ref), rows, tq, HEAD_DIM)

    for g in range(KV_G):
        for h in range(HG):
            hh = g * HG + h
            r0 = h * tq
            o = (sg[:, hh:hh + 1] * o_cmp[g][r0:r0 + tq]
                 + sg[:, N_HEADS + hh:N_HEADS + hh + 1] * o_slc[g][r0:r0 + tq]
                 + sg[:, 2 * N_HEADS + hh:2 * N_HEADS + hh + 1] * o_win[g][r0:r0 + tq])
            o_ref[:, hh * HEAD_DIM:(hh + 1) * HEAD_DIM] = o


def _nsa_prompt(c, kvc, ov, b, t, tq):
    nq = t // tq
    kw = 2 * KV_G * HEAD_DIM
    nchunk = t // CMP_STRIDE
    nc = nchunk - CMP_LEN // CMP_STRIDE + 1
    ns = -(-t // SLC_BLOCK)
    return pl.pallas_call(
        functools.partial(_nsa_prompt_kernel, tq=tq, nc=nc, ns=ns), grid=(b, nq),
        in_specs=[pl.BlockSpec((tq, MIX), lambda i, q: (i * nq + q, _COL['nsa_q'] // MIX)),
                  pl.BlockSpec((t, kw), lambda i, q: (i, _COL['kv_slc'] // kw)),
                  pl.BlockSpec((t, kw), lambda i, q: (i, _COL['kv_win'] // kw)),
                  pl.BlockSpec((1, nchunk, kw), lambda i, q: (i, 0, 0)),
                  pl.BlockSpec((tq, LANE), lambda i, q: (i * nq + q, _COL['nsa_gate'] // LANE)),
                  pl.BlockSpec((LANE, LANE), lambda i, q: (0, 0))],
        out_specs=pl.BlockSpec((tq, MIX), lambda i, q: (i * nq + q, 0)),
        out_shape=jax.ShapeDtypeStruct((b * t, MIX), F32),
        compiler_params=_cparams(("parallel", "arbitrary")), name="nsa_prompt")(c, c, c, kvc, c, ov)


def _dec_rows():
    rid = lax.broadcasted_iota(jnp.int32, (N_HEADS * T_PAD, 1), 0)
    t8 = rid % T_PAD
    slope = jnp.concatenate([jnp.full((T_PAD, 1), SLOPES[h], F32) for h in range(N_HEADS)], axis=0)
    return t8, slope


def _expand_heads(q8):
    lane = lax.broadcasted_iota(jnp.int32, q8.shape, 1)
    return jnp.concatenate([jnp.where(lane // HEAD_DIM == h, q8, 0.0) for h in range(N_HEADS)], axis=0)


def _collapse_heads(res):
    lane = lax.broadcasted_iota(jnp.int32, (T_PAD, res.shape[1]), 1)
    out = jnp.zeros((T_PAD, res.shape[1]), F32)
    for h in range(N_HEADS):
        out = out + jnp.where(lane // HEAD_DIM == h, res[h * T_PAD:(h + 1) * T_PAD], 0.0)
    return out


def _moba_dec_kernel(pt_ref, q_ref, kn_ref, vn_ref, *rest, npages, past):
    page_refs, o_ref = rest[:npages], rest[npages]
    page = page_refs[0].shape[2]
    t8, slope = _dec_rows()
    qpos = past + t8
    lane = lax.broadcasted_iota(jnp.int32, (1, LANE), 1)
    q8 = q_ref[0]
    qx_f = _expand_heads(q8)
    qx = (qx_f * SCALE).astype(BF16)
    per_blk = MOBA_BLOCK // page
    nb_past = past // MOBA_BLOCK
    lanei = lax.broadcasted_iota(jnp.int32, (MIX, LANE), 1)
    kmean = jnp.zeros((MIX, LANE), F32)
    for n in range(nb_past):
        tot = page_refs[n * per_blk][0, 0:MIX, :]
        for p in range(n * per_blk + 1, (n + 1) * per_blk):
            tot = tot + page_refs[p][0, 0:MIX, :]
        col = jnp.sum(tot, axis=1, keepdims=True) * (1.0 / MOBA_BLOCK)
        kmean = kmean + jnp.where(lanei == n, col, 0.0)
    gate = jnp.dot(qx_f, kmean, precision=HI, preferred_element_type=F32)
    jl = lax.broadcasted_iota(jnp.int32, gate.shape, 1)
    cur = qpos // MOBA_BLOCK
    gate = jnp.where(jl < cur, gate, NEG)
    nb = -(-(past + T_PAD) // MOBA_BLOCK)
    sel = jnp.where(_topn_mask(gate, nb, max(1, min(MOBA_TOPK, nb - 1))) & (jl < cur), 1.0, 0.0)
    s_tiles, v_tiles = [], []
    for p in range(npages):
        kt = page_refs[p][0, 0:MIX, :].astype(BF16)
        v_tiles.append((page_refs[p][0, MIX:2 * MIX, :].astype(BF16), True))
        n = (p * page) // MOBA_BLOCK
        d = (qpos - (p * page + lane)).astype(F32)
        valid = (sel[:, n:n + 1] > 0.5) & (d >= 0)
        s_tiles.append(jnp.where(valid, jnp.dot(qx, kt, preferred_element_type=F32) - slope * d, NEG))
    k = _pad_rows(kn_ref[0], LANE).astype(BF16)
    v_tiles.append((_pad_rows(vn_ref[0], LANE).astype(BF16), False))
    d = (t8 - lane).astype(F32)
    valid = (d >= 0) & (lane < T_PAD)
    s_tiles.append(jnp.where(valid, _dot_nt(qx, k) - slope * d, NEG))
    o_ref[0] = _collapse_heads(_attend_tiles(s_tiles, v_tiles))


def _fox_dec_kernel(pt_ref, q_ref, kn_ref, vn_ref, ln_ref, *rest, npages, past):
    logf_refs, page_refs, o_ref = rest[:npages], rest[npages:2 * npages], rest[2 * npages]
    t8, _ = _dec_rows()
    lane = lax.broadcasted_iota(jnp.int32, (1, LANE), 1)
    qx = (_expand_heads(q_ref[0]) * SCALE).astype(BF16)
    r = lax.broadcasted_iota(jnp.int32, (LANE, LANE), 0)
    c = lax.broadcasted_iota(jnp.int32, (LANE, LANE), 1)
    tri = jnp.where(r <= c, 1.0, 0.0).astype(F32)

    def head_rows(x):
        return jnp.concatenate([jnp.broadcast_to(x[h:h + 1], (T_PAD, x.shape[1])) for h in range(N_HEADS)], axis=0)

    carry = jnp.zeros((N_HEADS, 1), F32)
    cum = []
    for ref in list(logf_refs) + [ln_ref]:
        cs = jnp.dot(ref[0], tri, precision=HI, preferred_element_type=F32) + carry
        cum.append(cs)
        carry = cs[:, LANE - 1:LANE]
    c_ref = head_rows(cum[npages - 1][:, LANE - 1:LANE])

    s_tiles, v_tiles = [], []
    for p in range(npages):
        kt = page_refs[p][0, 0:MIX, :].astype(BF16)
        v_tiles.append((page_refs[p][0, MIX:2 * MIX, :].astype(BF16), True))
        s_tiles.append(jnp.dot(qx, kt, preferred_element_type=F32) + (c_ref - head_rows(cum[p])))
    k = _pad_rows(kn_ref[0], LANE).astype(BF16)
    v_tiles.append((_pad_rows(vn_ref[0], LANE).astype(BF16), False))
    valid = (lane <= t8) & (lane < T_PAD)
    s_tiles.append(jnp.where(valid, _dot_nt(qx, k) + (c_ref - head_rows(cum[npages])), NEG))
    o_ref[0] = _collapse_heads(_attend_tiles(s_tiles, v_tiles))


def _page_specs(pool, npages, first):
    blk = (1,) + pool.shape[1:]
    return [pl.BlockSpec(blk, lambda b, pt, p=p: (first + pt[b, p], 0, 0)) for p in range(npages)]


def _paged_mha_dec(kernel, name, c3, col, page_table, pool, first, extra_args, extra_specs, past):
    bs, npages = page_table.shape
    qb = col // MIX
    in_specs = [pl.BlockSpec((1, T_PAD, MIX), lambda b, pt: (b, 0, qb)),
                pl.BlockSpec((1, T_PAD, MIX), lambda b, pt: (b, 0, qb + 1)),
                pl.BlockSpec((1, T_PAD, MIX), lambda b, pt: (b, 0, qb + 2))]
    in_specs += extra_specs
    in_specs += _page_specs(pool, npages, first)
    return pl.pallas_call(
        functools.partial(kernel, npages=npages, past=past),
        grid_spec=pltpu.PrefetchScalarGridSpec(
            num_scalar_prefetch=1, grid=(bs,), in_specs=in_specs,
            out_specs=pl.BlockSpec((1, T_PAD, MIX), lambda b, pt: (b, 0, 0))),
        out_shape=jax.ShapeDtypeStruct((bs, T_PAD, MIX), F32),
        compiler_params=_cparams(("parallel",)), name=name)(
            page_table, c3, c3, c3, *extra_args, *([pool] * npages))


def _nsa_dec_kernel(pt_ref, q_ref, ns_ref, nw_ref, g_ref, cmp_ref, ov_ref, wc_ref, *rest, npages, past, nc, ns):
    page_refs, o_ref = rest[:npages], rest[npages]
    page = page_refs[0].shape[2]
    kw = KV_G * HEAD_DIM
    t8, slope = _dec_rows()
    qpos = past + t8
    lane = lax.broadcasted_iota(jnp.int32, (1, LANE), 1)
    lane8 = lax.broadcasted_iota(jnp.int32, (T_PAD, LANE), 1)
    q8 = q_ref[0] * SCALE
    rows = []
    for h in range(N_HEADS):
        x = q8[:, (h // 2) * LANE:(h // 2 + 1) * LANE]
        dst = h // HG
        if h % 2 != dst:
            x = pltpu.roll(x, HEAD_DIM, 1)
        rows.append(jnp.where((lane8 // HEAD_DIM) == dst, x, 0.0))
    qx = jnp.concatenate(rows, axis=0).astype(BF16)

    cm = cmp_ref[0]
    dist = (qpos - (lane * CMP_STRIDE + CMP_LEN - 1)).astype(F32)
    ok_c = (dist >= 0) & (lane < nc)
    s = jnp.where(ok_c, _dot_nt(qx, cm[:, 0:kw].astype(BF16)) - slope * dist, NEG)
    m = jnp.max(s, axis=1, keepdims=True)
    e = jnp.where(ok_c, jnp.exp(s - m), 0.0)
    p_c = e / jnp.maximum(jnp.sum(e, axis=1, keepdims=True), 1e-30)
    o_cmp = jnp.dot(p_c.astype(BF16), cm[:, kw:2 * kw].astype(BF16), preferred_element_type=F32)

    psum = []
    for g in range(KV_G):
        acc = p_c[g * HG * T_PAD:g * HG * T_PAD + T_PAD]
        for h in range(1, HG):
            r0 = (g * HG + h) * T_PAD
            acc = acc + p_c[r0:r0 + T_PAD]
        psum.append(acc)
    imp = jnp.dot(jnp.concatenate(psum, axis=0), ov_ref[...], precision=HI, preferred_element_type=F32)
    jl = lax.broadcasted_iota(jnp.int32, imp.shape, 1)
    tg = lax.broadcasted_iota(jnp.int32, (KV_G * T_PAD, 1), 0) % T_PAD
    cur = (past + tg) // SLC_BLOCK
    forced = (jl == 0) | (jl == cur) | (jl == cur - 1)
    imp = jnp.where(forced, FORCE, imp)
    imp = jnp.where(jl <= cur, imp, NEG)
    sel = jnp.where(_topn_mask(imp, ns, min(SLC_TOPN, ns)) & (jl <= cur), 1.0, 0.0)
    sel_rows = jnp.concatenate([sel[(h // HG) * T_PAD:(h // HG + 1) * T_PAD] for h in range(N_HEADS)], axis=0)

    def new_tile(ref, extra_valid):
        k = _pad_rows(ref[0, :, 0:kw], LANE).astype(BF16)
        v = _pad_rows(ref[0, :, kw:2 * kw], LANE).astype(BF16)
        d = (t8 - lane).astype(F32)
        valid = (d >= 0) & (lane < T_PAD) & extra_valid
        return jnp.where(valid, _dot_nt(qx, k) - slope * d, NEG), (v, False)

    s_tiles, v_tiles = [], []
    per = page // SLC_BLOCK
    for p in range(npages):
        kt = page_refs[p][0, 0:kw, :].astype(BF16)
        v_tiles.append((page_refs[p][0, kw:2 * kw, :].astype(BF16), True))
        picked = jnp.zeros((N_HEADS * T_PAD, LANE), jnp.bool_)
        for j in range(per):
            blk = p * per + j
            picked = picked | ((lane // SLC_BLOCK == j) & (sel_rows[:, blk:blk + 1] > 0.5))
        d = (qpos - (p * page + lane)).astype(F32)
        s_tiles.append(jnp.where(picked & (d >= 0), jnp.dot(qx, kt, preferred_element_type=F32) - slope * d, NEG))
    blk_new = past // SLC_BLOCK
    s_new, v_new = new_tile(ns_ref, sel_rows[:, blk_new:blk_new + 1] > 0.5)
    o_slc = _attend_tiles(s_tiles + [s_new], v_tiles + [v_new])

    s_tiles, v_tiles = [], []
    wb = wc_ref.shape[2]
    w_off = past - wb
    for j in range(wb // LANE):
        kt = wc_ref[0, 0:kw, j * LANE:(j + 1) * LANE].astype(BF16)
        v_tiles.append((wc_ref[0, kw:2 * kw, j * LANE:(j + 1) * LANE].astype(BF16), True))
        d = qpos - (w_off + j * LANE + lane)
        valid = (d >= 0) & (d < WINDOW)
        s_tiles.append(jnp.where(valid, jnp.dot(qx, kt, preferred_element_type=F32) - slope * d.astype(F32), NEG))
    s_new, v_new = new_tile(nw_ref, True)
    o_win = _attend_tiles(s_tiles + [s_new], v_tiles + [v_new])

    sg = _sigmoid(g_ref[0])
    for h in range(N_HEADS):
        r0, l0 = h * T_PAD, (h // HG) * HEAD_DIM
        o = (sg[:, h:h + 1] * o_cmp[r0:r0 + T_PAD, l0:l0 + HEAD_DIM]
             + sg[:, N_HEADS + h:N_HEADS + h + 1] * o_slc[r0:r0 + T_PAD, l0:l0 + HEAD_DIM]
             + sg[:, 2 * N_HEADS + h:2 * N_HEADS + h + 1] * o_win[r0:r0 + T_PAD, l0:l0 + HEAD_DIM])
        o_ref[0, :, h * HEAD_DIM:(h + 1) * HEAD_DIM] = o


def _nsa_dec(c3, kvc, ov, win_cache, win_first, page_table, pool, first, past, t_real):
    bs, npages = page_table.shape
    kw = 2 * KV_G * HEAD_DIM
    nchunk = kvc.shape[1]
    nc = nchunk - CMP_LEN // CMP_STRIDE + 1
    ns = -(-(past + t_real) // SLC_BLOCK)
    wb = win_cache.shape[2]
    in_specs = [pl.BlockSpec((1, T_PAD, MIX), lambda b, pt: (b, 0, _COL['nsa_q'] // MIX)),
                pl.BlockSpec((1, T_PAD, kw), lambda b, pt: (b, 0, _COL['kv_slc'] // kw)),
                pl.BlockSpec((1, T_PAD, kw), lambda b, pt: (b, 0, _COL['kv_win'] // kw)),
                pl.BlockSpec((1, T_PAD, LANE), lambda b, pt: (b, 0, _COL['nsa_gate'] // LANE)),
                pl.BlockSpec((1, nchunk, kw), lambda b, pt: (b, 0, 0)),
                pl.BlockSpec((LANE, LANE), lambda b, pt: (0, 0)),
                pl.BlockSpec((1, kw, wb), lambda b, pt: (win_first + b, 0, 0))]
    in_specs += _page_specs(pool, npages, first)
    return pl.pallas_call(
        functools.partial(_nsa_dec_kernel, npages=npages, past=past, nc=nc, ns=ns),
        grid_spec=pltpu.PrefetchScalarGridSpec(
            num_scalar_prefetch=1, grid=(bs,), in_specs=in_specs,
            out_specs=pl.BlockSpec((1, T_PAD, MIX), lambda b, pt: (b, 0, 0))),
        out_shape=jax.ShapeDtypeStruct((bs, T_PAD, MIX), F32),
        compiler_params=_cparams(("parallel",)), name="nsa_decode")(
            page_table, c3, c3, c3, c3, kvc, ov, win_cache, *([pool] * npages))


def _s5_disc_kernel(ar_ref, ai_ref, ls_ref, btr_ref, bti_ref, abr_ref, abi_ref, bbr_ref, bbi_ref):
    ar, ai = ar_ref[...], ai_ref[...]
    step = jnp.exp(ls_ref[...])
    mag = jnp.exp(ar * step)
    abr = mag * jnp.cos(ai * step)
    abi = mag * jnp.sin(ai * step)
    den = ar * ar + ai * ai
    zr = (ar * (abr - 1.0) + ai * abi) / den
    zi = (ar * abi - ai * (abr - 1.0)) / den
    abr_ref[...] = abr
    abi_ref[...] = abi
    btr, bti = btr_ref[...], bti_ref[...]
    bbr_ref[...] = zr * btr - zi * bti
    bbi_ref[...] = zr * bti + zi * btr


def _s5_disc(a_re, a_im, log_step, b_re, b_im):
    rep = lambda a: jnp.repeat(a, S5_CH, axis=0)
    n = S5_GROUPS * S5_CH
    args = (rep(a_re), rep(a_im), rep(jnp.broadcast_to(log_step[:, None], (S5_GROUPS, S5_STATE))),
            b_re.transpose(0, 2, 1).reshape(n, S5_STATE), b_im.transpose(0, 2, 1).reshape(n, S5_STATE))
    shp = jax.ShapeDtypeStruct((n, S5_STATE), F32)
    return pl.pallas_call(_s5_disc_kernel, out_shape=(shp, shp, shp, shp), name="s5_discretise")(*args)


_S5_CH_ROWS = 128


def _s5_scan_kernel(*refs, seg, has_h0):
    if has_h0:
        (u_ref, bre_ref, bim_ref, ar_ref, ai_ref, cre_ref, cim_ref, d_ref, h0r_ref, h0i_ref,
         y_ref, hlr_ref, hli_ref, hr_s, hi_s) = refs
    else:
        (u_ref, bre_ref, bim_ref, ar_ref, ai_ref, cre_ref, cim_ref, d_ref,
         y_ref, hlr_ref, hli_ref, hr_s, hi_s) = refs
    rows = u_ref.shape[0]
    ch = min(_S5_CH_ROWS, rows)
    pad = ch
    nch = rows // ch
    ar, ai = ar_ref[0], ai_ref[0]
    bre, bim, cre, cim = (_split_bf16(r[0]) for r in (bre_ref, bim_ref, cre_ref, cim_ref))
    hr_s[0:pad, :] = jnp.zeros((pad, hr_s.shape[1]), F32)
    hi_s[0:pad, :] = jnp.zeros((pad, hi_s.shape[1]), F32)

    def init_body(i, _):
        r0 = pl.multiple_of(i * ch, ch)
        u = u_ref[pl.ds(r0, ch), :]
        br = _dot3(u, bre)
        bi = _dot3(u, bim)
        if has_h0:
            h0r, h0i = h0r_ref[pl.ds(r0, ch), :], h0i_ref[pl.ds(r0, ch), :]
            br = br + (ar * h0r - ai * h0i)
            bi = bi + (ar * h0i + ai * h0r)
        hr_s[pl.ds(pad + r0, ch), :] = br
        hi_s[pl.ds(pad + r0, ch), :] = bi
        return 0

    lax.fori_loop(0, nch, init_body, 0)

    rl = lax.broadcasted_iota(jnp.int32, (ch, 1), 0)
    pr, pi = ar, ai
    d = 1
    two_level = seg == rows and rows > ch
    span = ch if two_level else seg
    while d < span:
        first = d // ch

        def pass_body(i, _, d=d, pr=pr, pi=pi):
            r0 = pl.multiple_of((nch - 1 - i) * ch, ch)
            cr = hr_s[pl.ds(pad + r0, ch), :]
            ci = hi_s[pl.ds(pad + r0, ch), :]
            if d < SUBLANE:
                lo = pad - SUBLANE
                sr = pltpu.roll(hr_s[pl.ds(lo + r0, ch + SUBLANE), :], d, 0)[SUBLANE:]
                si = pltpu.roll(hi_s[pl.ds(lo + r0, ch + SUBLANE), :], d, 0)[SUBLANE:]
            else:
                sr = hr_s[pl.ds(pad + r0 - d, ch), :]
                si = hi_s[pl.ds(pad + r0 - d, ch), :]
            if d < ch or seg < rows:
                keep = ((r0 + rl) % span) >= d
                sr = jnp.where(keep, sr, 0.0)
                si = jnp.where(keep, si, 0.0)
            hr_s[pl.ds(pad + r0, ch), :] = cr + (pr * sr - pi * si)
            hi_s[pl.ds(pad + r0, ch), :] = ci + (pr * si + pi * sr)
            return 0

        lax.fori_loop(0, nch - first, pass_body, 0)
        pr, pi = pr * pr - pi * pi, 2.0 * pr * pi
        d *= 2

    if two_level:
        tr, ti = jnp.broadcast_to(ar, (ch, ar.shape[1])), jnp.broadcast_to(ai, (ch, ai.shape[1]))
        d = 1
        while d < ch:
            sr, si = pltpu.roll(tr, d, 0), pltpu.roll(ti, d, 0)
            keep = rl >= d
            tr, ti = jnp.where(keep, tr * sr - ti * si, tr), jnp.where(keep, tr * si + ti * sr, ti)
            d *= 2
        lasts = [(hr_s[pad + (c + 1) * ch - 1:pad + (c + 1) * ch, :], hi_s[pad + (c + 1) * ch - 1:pad + (c + 1) * ch, :])
                 for c in range(nch - 1)]
        kr, ki = lasts[0]
        for c in range(1, nch):
            if c > 1:
                lr, li = lasts[c - 1]
                kr, ki = lr + (pr * kr - pi * ki), li + (pr * ki + pi * kr)
            rows_c = slice(pad + c * ch, pad + (c + 1) * ch)
            hr_s[rows_c, :] = hr_s[rows_c, :] + (tr * kr - ti * ki)
            hi_s[rows_c, :] = hi_s[rows_c, :] + (tr * ki + ti * kr)

    def out_body(i, _):
        r0 = pl.multiple_of(i * ch, ch)
        hr = hr_s[pl.ds(pad + r0, ch), :]
        hi = hi_s[pl.ds(pad + r0, ch), :]
        y = _dot3(hr, cre) - _dot3(hi, cim)
        y_ref[pl.ds(r0, ch), :] = y + d_ref[0] * u_ref[pl.ds(r0, ch), :]
        return 0

    lax.fori_loop(0, nch, out_body, 0)
    nl = hlr_ref.shape[0]
    hlr_ref[...] = hr_s[pad + rows - nl:pad + rows, :]
    hli_ref[...] = hi_s[pad + rows - nl:pad + rows, :]


def _s5_scan(c, sw, rows, seg, h0=None):
    n = c.shape[0]
    nt = n // rows
    gl = LANE // S5_CH
    lt = S5_GROUPS // gl
    w = gl * S5_STATE
    nl = (rows // seg) * SUBLANE if seg == SUBLANE else SUBLANE
    ub = _COL['s5_u'] // LANE
    in_specs = [pl.BlockSpec((rows, LANE), lambda i, j: (i, ub + j)),
                pl.BlockSpec((1, LANE, w), lambda i, j: (j, 0, 0)),
                pl.BlockSpec((1, LANE, w), lambda i, j: (j, 0, 0)),
                pl.BlockSpec((1, 1, w), lambda i, j: (j, 0, 0)),
                pl.BlockSpec((1, 1, w), lambda i, j: (j, 0, 0)),
                pl.BlockSpec((1, w, LANE), lambda i, j: (j, 0, 0)),
                pl.BlockSpec((1, w, LANE), lambda i, j: (j, 0, 0)),
                pl.BlockSpec((1, 1, LANE), lambda i, j: (j, 0, 0))]
    args = [c, sw['bre'], sw['bim'], sw['ar'], sw['ai'], sw['cre'], sw['cim'], sw['d']]
    if h0 is not None:
        in_specs += [pl.BlockSpec((rows, w), lambda i, j: (i, j))] * 2
        args += list(h0)
    hshape = jax.ShapeDtypeStruct((nt * nl, S5_GROUPS * S5_STATE), F32)
    return pl.pallas_call(
        functools.partial(_s5_scan_kernel, seg=seg, has_h0=h0 is not None), grid=(nt, lt),
        in_specs=in_specs,
        out_specs=(pl.BlockSpec((rows, LANE), lambda i, j: (i, j)),
                   pl.BlockSpec((nl, w), lambda i, j: (i, j)), pl.BlockSpec((nl, w), lambda i, j: (i, j))),
        out_shape=(jax.ShapeDtypeStruct((n, MIX), F32), hshape, hshape),
        scratch_shapes=[pltpu.VMEM((min(_S5_CH_ROWS, rows) + rows, w), F32)] * 2,
        compiler_params=_cparams(("parallel", "arbitrary")), name="s5_scan")(*args)


def _s5_weights(lp):
    abr, abi, bbr, bbi = _s5_disc(lp['s5_a_re'], lp['s5_a_im'], lp['s5_log_step'], lp['s5_b_re'], lp['s5_b_im'])
    gl = LANE // S5_CH
    lt = S5_GROUPS // gl
    eye = jnp.eye(gl, dtype=F32)

    def bdiag(bb):
        return jnp.einsum('jgcn,gh->jgchn', bb.reshape(lt, gl, S5_CH, S5_STATE), eye).reshape(
            lt, gl * S5_CH, gl * S5_STATE)

    def cdiag(cc):
        return jnp.einsum('jgcn,gh->jgnhc', cc.reshape(lt, gl, S5_CH, S5_STATE), eye).reshape(
            lt, gl * S5_STATE, gl * S5_CH)

    return dict(bre=bdiag(bbr), bim=bdiag(bbi),
                ar=abr[::S5_CH].reshape(lt, 1, gl * S5_STATE), ai=abi[::S5_CH].reshape(lt, 1, gl * S5_STATE),
                cre=cdiag(lp['s5_c_re']), cim=cdiag(lp['s5_c_im']), d=lp['s5_d'].reshape(lt, 1, LANE))


def _glu_kernel(y_ref, w_ref, o_ref):
    y = y_ref[...]
    g = 0.5 * y * (1.0 + jnp.tanh(np.float32(np.sqrt(2.0 / np.pi)) * (y + np.float32(0.044715) * (y * y * y))))
    z = jnp.dot(g.astype(BF16), w_ref[...], preferred_element_type=F32)
    o_ref[...] = z[:, :MIX] * _sigmoid(z[:, MIX:])


def _glu(y, w_l, tm):
    w, l = w_l
    n = y.shape[0]
    return pl.pallas_call(
        _glu_kernel, grid=(n // tm,),
        in_specs=[pl.BlockSpec((tm, MIX), lambda i: (i, 0)),
                  pl.BlockSpec((None, MIX, 2 * MIX), lambda i: (l, 0, 0))],
        out_specs=pl.BlockSpec((tm, MIX), lambda i: (i, 0)),
        out_shape=jax.ShapeDtypeStruct((n, MIX), F32),
        compiler_params=_cparams(("parallel",)), name="s5_glu")(y, w)


def _layer_norm(x, g, b):
    mu = jnp.mean(x, axis=-1, keepdims=True)
    xc = x - mu
    var = jnp.mean(xc * xc, axis=-1, keepdims=True)
    return xc * lax.rsqrt(var + LN_EPS) * g + b


def _merge_kernel(o0_ref, o1_ref, o2_ref, o3_ref, mg_ref, wb_ref, x_ref, wo_ref, g_ref, b_ref, out_ref, acc_ref,
                  *, alpha):
    i = pl.program_id(1)

    @pl.when(i == 0)
    def _():
        acc_ref[...] = jnp.zeros_like(acc_ref)

    for k, o_ref in enumerate((o0_ref, o1_ref, o2_ref, o3_ref)):
        @pl.when(i == k)
        def _(o_ref=o_ref):
            proj = jnp.dot(o_ref[...].astype(BF16), wb_ref[0], preferred_element_type=F32)
            acc_ref[...] += _sigmoid(mg_ref[...]) * proj

    @pl.when(i == N_BRANCH - 1)
    def _():
        mixed = jnp.dot(acc_ref[...].astype(BF16), wo_ref[...], preferred_element_type=F32)
        out_ref[...] = _layer_norm(alpha * x_ref[...] + mixed, g_ref[...], b_ref[...])


def _merge(outs, c, x, wb_l, wo_l, g, b, tm, alpha):
    (wb, l), wo = wb_l, wo_l[0]
    n = x.shape[0]
    o_spec = pl.BlockSpec((tm, MIX), lambda r, i: (r, 0))
    return pl.pallas_call(
        functools.partial(_merge_kernel, alpha=alpha), grid=(n // tm, N_BRANCH),
        in_specs=[o_spec, o_spec, o_spec, o_spec,
                  pl.BlockSpec((tm, D_MODEL), lambda r, i: (r, i)),
                  pl.BlockSpec((None, 1, MIX, D_MODEL), lambda r, i: (l, i, 0, 0)),
                  pl.BlockSpec((tm, D_MODEL), lambda r, i: (r, 0)),
                  pl.BlockSpec((None, D_MODEL, D_MODEL), lambda r, i: (l, 0, 0), pipeline_mode=pl.Buffered(1)),
                  pl.BlockSpec((1, D_MODEL), lambda r, i: (0, 0)),
                  pl.BlockSpec((1, D_MODEL), lambda r, i: (0, 0))],
        out_specs=pl.BlockSpec((tm, D_MODEL), lambda r, i: (r, 0)),
        out_shape=jax.ShapeDtypeStruct((n, D_MODEL), F32),
        scratch_shapes=[pltpu.VMEM((tm, D_MODEL), F32)],
        compiler_params=_cparams(("parallel", "arbitrary")), name="merge_out_ln")(
            *outs, c, wb, x, wo, g, b)


def _ffn_kernel(*refs, halo, seg, alpha):
    (h_ref, wg_ref, wv_ref, cwg_ref, cwv_ref, cbg_ref, cbv_ref, wd_ref, lg_ref, lb_ref) = refs[:10]
    pg_ref, pv_ref, out_ref, hb_ref, acc_ref = refs[10:]
    j = pl.program_id(1)

    @pl.when(j == 0)
    def _():
        hb_ref[...] = h_ref[...].astype(BF16)
        acc_ref[...] = jnp.zeros_like(acc_ref)

    hb = hb_ref[...]
    tm = hb.shape[0]
    rid = lax.broadcasted_iota(jnp.int32, (tm, 1), 0)

    def conv(u, cw_ref, cb_ref, prev):
        r1 = pltpu.roll(u, 1, 0)
        r2 = pltpu.roll(u, 2, 0)
        if halo:
            p = prev[0][0]
            p6, p7 = p[SUBLANE - 2:SUBLANE - 1], p[SUBLANE - 1:SUBLANE]
            u1 = jnp.where(rid == 0, p7, r1)
            u2 = jnp.where(rid == 0, p6, jnp.where(rid == 1, p7, r2))
        else:
            t = rid % seg
            state = jnp.where(t >= seg - (CONV_W - 1), prev[0][...], u)
            u1 = jnp.where(t >= 1, r1, pltpu.roll(state, tm - (seg - 1), 0))
            u2 = jnp.where(t >= 2, r2, pltpu.roll(state, tm - (seg - 2), 0))
        cw = cw_ref[...]
        return cb_ref[...] + (cw[0:1] * u2 + cw[1:2] * u1 + cw[2:3] * u)

    ug = jnp.dot(hb, wg_ref[...], preferred_element_type=F32)
    uv = jnp.dot(hb, wv_ref[...], preferred_element_type=F32)
    gate = conv(ug, cwg_ref, cbg_ref, (pg_ref,))
    val = conv(uv, cwv_ref, cbv_ref, (pv_ref,))
    act = gate * _sigmoid(gate) * val
    acc_ref[...] += jnp.dot(act.astype(BF16), wd_ref[...], preferred_element_type=F32)

    @pl.when(j == pl.num_programs(1) - 1)
    def _():
        out_ref[...] = _layer_norm(alpha * h_ref[...] + acc_ref[...], lg_ref[...], lb_ref[...])


def _ffn(h, lw, tm, tf, alpha, prev=None, state_rows=None, seg=None):
    n = h.shape[0]
    nf = D_FF // tf
    halo = prev is not None
    (w_up, l), w_down = lw['w_up'], lw['w_down'][0]
    in_specs = [pl.BlockSpec((tm, D_MODEL), lambda r, j: (r, 0), pipeline_mode=pl.Buffered(1)),
                pl.BlockSpec((None, D_MODEL, tf), lambda r, j: (l, 0, j)),
                pl.BlockSpec((None, D_MODEL, tf), lambda r, j: (l, 0, nf + j)),
                pl.BlockSpec((CONV_W, tf), lambda r, j: (0, j)),
                pl.BlockSpec((CONV_W, tf), lambda r, j: (0, nf + j)),
                pl.BlockSpec((1, tf), lambda r, j: (0, j)),
                pl.BlockSpec((1, tf), lambda r, j: (0, nf + j)),
                pl.BlockSpec((None, tf, D_MODEL), lambda r, j: (l, j, 0)),
                pl.BlockSpec((1, D_MODEL), lambda r, j: (0, 0)),
                pl.BlockSpec((1, D_MODEL), lambda r, j: (0, 0))]
    args = [h, w_up, w_up, lw['conv_w'], lw['conv_w'], lw['conv_b'], lw['conv_b'], w_down,
            lw['ln2_g'], lw['ln2_b']]
    if halo:
        in_specs += [pl.BlockSpec((1, SUBLANE, tf), lambda r, j: (r, 0, j)),
                     pl.BlockSpec((1, SUBLANE, tf), lambda r, j: (r, 0, nf + j))]
        args += [prev, prev]
    else:
        in_specs += [pl.BlockSpec((tm, tf), lambda r, j: (r, j)), pl.BlockSpec((tm, tf), lambda r, j: (r, nf + j))]
        args += [state_rows, state_rows]
    return pl.pallas_call(
        functools.partial(_ffn_kernel, halo=halo, seg=seg, alpha=alpha), grid=(n // tm, nf),
        in_specs=in_specs,
        out_specs=pl.BlockSpec((tm, D_MODEL), lambda r, j: (r, 0), pipeline_mode=pl.Buffered(1)),
        out_shape=jax.ShapeDtypeStruct((n, D_MODEL), F32),
        scratch_shapes=[pltpu.VMEM((tm, D_MODEL), BF16), pltpu.VMEM((tm, D_MODEL), F32)],
        compiler_params=_cparams(("parallel", "arbitrary")), name="conv_ffn_ln")(*args)


def _prep_shared(p):
    return dict(w_in=_pack_w_in(p['w_in']), w_glu=p['s5_w_glu'].astype(BF16), w_branch=p['w_branch'].astype(BF16),
                w_out=p['w_out'].astype(BF16), w_up=p['ffn_w_up'].astype(BF16), w_down=p['ffn_w_down'].astype(BF16))


def _prep_layer(l, p, shared):
    lp = {k: v[l] for k, v in p.items() if k not in ('w_in', 's5_w_glu', 'w_branch', 'w_out', 'ffn_w_up', 'ffn_w_down')}
    lw = {k: (v, l) for k, v in shared.items()}
    lw.update(
        fox_b=jnp.zeros((1, LANE), F32).at[0, :N_HEADS].set(lp['fox_b_f']),
        cmp=_cmp_weights(lp['nsa_cmp_pos'], lp['nsa_cmp_wk'], lp['nsa_cmp_wv']),
        s5=_s5_weights(lp),
        ln1_g=lp['ln1_g'].reshape(1, -1), ln1_b=lp['ln1_b'].reshape(1, -1),
        conv_w=lp['ffn_conv_w'], conv_b=lp['ffn_conv_b'].reshape(1, -1),
        ln2_g=lp['ln2_g'].reshape(1, -1), ln2_b=lp['ln2_b'].reshape(1, -1))
    return lw


def _prompt_layer(x, b, t, lw, alpha):
    n = b * t
    kw = 2 * KV_G * HEAD_DIM
    tr = min(1024, n)
    c = _matmul(x, lw['w_in'], tr, 1536)
    logf = _logf(c, lw['fox_b'], tr)
    cumr = _cumsum_prompt(logf.reshape(b, t, N_HEADS).transpose(0, 2, 1))
    o_fox = _fox_prompt(c, cumr[:, :, None, :], b, t, 512)
    o_moba = _moba_prompt(c, b, t, 2 * MOBA_BLOCK)
    kvc = _cmp_prompt(c, b, t, lw['cmp'])
    nchunk = t // CMP_STRIDE
    ov = _overlap_matrix(nchunk - CMP_LEN // CMP_STRIDE + 1, -(-t // SLC_BLOCK))
    o_nsa = _nsa_prompt(c, kvc, ov, b, t, 256)
    y_s5, hlr, hli = _s5_scan(c, lw['s5'], t, t)
    o_s5 = _glu(y_s5, lw['w_glu'], tr)
    h = _merge((o_nsa, o_s5, o_moba, o_fox), c, x, lw['w_branch'], lw['w_out'], lw['ln1_g'], lw['ln1_b'], 512, alpha)

    tm = min(1024, t)
    nt = n // tm
    edge = h.reshape(nt, tm, D_MODEL)[:, tm - (CONV_W - 1):].reshape(nt * (CONV_W - 1), D_MODEL)
    edge = _pad_rows(edge, -(-edge.shape[0] // SUBLANE) * SUBLANE)
    u_edge = _matmul(edge, lw['w_up'], edge.shape[0], 512)[:nt * (CONV_W - 1)].reshape(nt, CONV_W - 1, 2 * D_FF)
    per_seq = t // tm
    conv_state = u_edge[per_seq - 1::per_seq]
    starts_seq = (jnp.arange(nt) % per_seq == 0)[:, None, None]
    prev = jnp.where(starts_seq, 0.0, jnp.roll(u_edge, 1, axis=0))
    prev = jnp.pad(prev, ((0, 0), (SUBLANE - (CONV_W - 1), 0), (0, 0)))
    y = _ffn(h, lw, tm, 512, alpha, prev=prev)

    win_rows = min(WINDOW, t)
    states = (
        _kv_state(c, b, t, _COL['kv_cmp'], kw),
        _kv_state(c, b, t, _COL['kv_slc'], kw),
        _kv_state(c, b, t, _COL['moba'] + MIX, 2 * MIX),
        _kv_state(c, b, t, _COL['fox'] + MIX, 2 * MIX),
        logf.reshape(b, t, N_HEADS),
        _kv_state(c, b, t, _COL['kv_win'], kw, t_from=t - win_rows),
        hlr.reshape(b, SUBLANE, S5_GROUPS, S5_STATE)[:, SUBLANE - 1],
        hli.reshape(b, SUBLANE, S5_GROUPS, S5_STATE)[:, SUBLANE - 1],
        conv_state)
    return y, states


def _sample_layer(x, bs, t_real, lw, alpha, past, page_table, past_len):
    n = bs * T_PAD
    kw = 2 * KV_G * HEAD_DIM
    c = _matmul(x, lw['w_in'], n, 1536)
    c3 = c.reshape(bs, T_PAD, WP)
    logf = _logf(c, lw['fox_b'], n)
    logf3 = logf.reshape(bs, T_PAD, N_HEADS)
    tmask = (jnp.arange(T_PAD) < t_real)[None, :, None]
    new_t = jnp.pad(jnp.where(tmask, logf3, 0.0).transpose(0, 2, 1), ((0, 0), (0, 0), (0, LANE - T_PAD)))
    first = past['first_page']
    npages = page_table.shape[1]
    o_fox = _paged_mha_dec(
        _fox_dec_kernel, "fox_decode", c3, _COL['fox'], page_table, past['fox'], first,
        [new_t] + [past['fox_logf']] * npages,
        [pl.BlockSpec((1, N_HEADS, LANE), lambda b, pt: (b, 0, 0))] + _page_specs(past['fox_logf'], npages, first),
        past_len)
    o_moba = _paged_mha_dec(_moba_dec_kernel, "moba_decode", c3, _COL['moba'], page_table, past['moba'], first,
                            [], [], past_len)
    kvc = _cmp_paged(page_table, past['nsa_cmp'], first, lw['cmp'])
    nchunk = kvc.shape[1]
    ov = _overlap_matrix(nchunk - CMP_LEN // CMP_STRIDE + 1, -(-(past_len + t_real) // SLC_BLOCK))
    o_nsa = _nsa_dec(c3, kvc, ov, past['nsa_win'], past['first_seq'], page_table, past['nsa_slc'], first,
                     past_len, t_real)
    h0 = [jnp.pad(s.reshape(bs, 1, -1), ((0, 0), (0, T_PAD - 1), (0, 0))).reshape(n, -1) for s in past['s5']]
    y_s5, hlr, hli = _s5_scan(c, lw['s5'], n, T_PAD, h0=h0)
    o_s5 = _glu(y_s5, lw['w_glu'], n)
    h = _merge((o_nsa.reshape(n, MIX), o_s5, o_moba.reshape(n, MIX), o_fox.reshape(n, MIX)), c, x,
               lw['w_branch'], lw['w_out'], lw['ln1_g'], lw['ln1_b'], min(256, n), alpha)

    buf = past['ffn_conv']
    state_rows = jnp.pad(buf, ((0, 0), (T_PAD - (CONV_W - 1), 0), (0, 0))).reshape(n, 2 * D_FF)
    y = _ffn(h, lw, min(512, n), 512, alpha, state_rows=state_rows, seg=T_PAD)
    last2 = h.reshape(bs, T_PAD, D_MODEL)[:, t_real - (CONV_W - 1):t_real].reshape(bs * (CONV_W - 1), D_MODEL)
    conv_state = _matmul(last2, lw['w_up'], last2.shape[0], 512).reshape(bs, CONV_W - 1, 2 * D_FF)

    tr = lambda a: a[:, :t_real]
    kv_win_new = tr(c3[:, :, _COL['kv_win']:_COL['kv_win'] + kw])
    states = (
        tr(c3[:, :, _COL['kv_cmp']:_COL['kv_cmp'] + kw]).reshape(bs, t_real, 2, KV_G, HEAD_DIM),
        tr(c3[:, :, _COL['kv_slc']:_COL['kv_slc'] + kw]).reshape(bs, t_real, 2, KV_G, HEAD_DIM),
        tr(c3[:, :, _COL['moba'] + MIX:_COL['moba'] + 3 * MIX]).reshape(bs, t_real, 2, N_HEADS, HEAD_DIM),
        tr(c3[:, :, _COL['fox'] + MIX:_COL['fox'] + 3 * MIX]).reshape(bs, t_real, 2, N_HEADS, HEAD_DIM),
        tr(logf3),
        kv_win_new,
        hlr.reshape(bs, T_PAD, S5_GROUPS, S5_STATE)[:, t_real - 1],
        hli.reshape(bs, T_PAD, S5_GROUPS, S5_STATE)[:, t_real - 1],
        conv_state)
    return y, states


def kernel(x_prompt, x_sample, cache_nsa_cmp_kv, cache_nsa_slc_kv, cache_moba_kv, cache_fox_kv, cache_fox_logf,
           page_table, cache_nsa_win_kv, state_s5_re, state_s5_im, state_ffn_conv, w_in, fox_b_f, nsa_cmp_pos,
           nsa_cmp_wk, nsa_cmp_wv, s5_a_re, s5_a_im, s5_b_re, s5_b_im, s5_c_re, s5_c_im, s5_d, s5_log_step,
           s5_w_glu, w_branch, w_out, ln1_g, ln1_b, ffn_w_up, ffn_conv_w, ffn_conv_b, ffn_w_down, ln2_g, ln2_b):
    depth = w_in.shape[0]
    b, t, d = x_prompt.shape
    bs, ts, _ = x_sample.shape
    n_phys, page = cache_nsa_cmp_kv.shape[1:3]
    past_len = page_table.shape[1] * page
    assert d == D_MODEL and w_in.shape[2] == IN_WIDTH and ffn_w_down.shape[1] == D_FF
    assert ts <= T_PAD - (CONV_W - 1) and past_len % MOBA_BLOCK == 0 and page == LANE and (past_len + ts) // CMP_STRIDE * CMP_STRIDE <= past_len
    alpha = float((2 * depth) ** 0.25)
    params = dict(w_in=w_in, fox_b_f=fox_b_f, nsa_cmp_pos=nsa_cmp_pos, nsa_cmp_wk=nsa_cmp_wk, nsa_cmp_wv=nsa_cmp_wv,
                  s5_a_re=s5_a_re, s5_a_im=s5_a_im, s5_b_re=s5_b_re, s5_b_im=s5_b_im, s5_c_re=s5_c_re,
                  s5_c_im=s5_c_im, s5_d=s5_d, s5_log_step=s5_log_step, s5_w_glu=s5_w_glu, w_branch=w_branch,
                  w_out=w_out, ln1_g=ln1_g, ln1_b=ln1_b, ffn_w_up=ffn_w_up, ffn_conv_w=ffn_conv_w,
                  ffn_conv_b=ffn_conv_b, ffn_w_down=ffn_w_down, ln2_g=ln2_g, ln2_b=ln2_b)
    kw = 2 * KV_G * HEAD_DIM
    yp = x_prompt.reshape(b * t, d)
    ys = jnp.pad(x_sample, ((0, 0), (0, T_PAD - ts), (0, 0))).reshape(bs * T_PAD, d)
    st_p, st_s = [], []

    def feature_major(cache):
        dd, nn, rr = cache.shape[:3]
        return cache.transpose(0, 1, 3, 4, 5, 2).reshape(dd * nn, -1, rr)

    cmp_fm, slc_fm, moba_fm, fox_fm, win_fm = (
        feature_major(a) for a in (cache_nsa_cmp_kv, cache_nsa_slc_kv, cache_moba_kv, cache_fox_kv, cache_nsa_win_kv))
    logf_hm = cache_fox_logf.transpose(0, 1, 3, 2).reshape(depth * n_phys, N_HEADS, page)
    shared = _prep_shared(params)
    for l in range(depth):
        lw = _prep_layer(l, params, shared)
        past = dict(
            nsa_cmp=cmp_fm, nsa_slc=slc_fm, moba=moba_fm, fox=fox_fm, fox_logf=logf_hm, nsa_win=win_fm,
            first_page=l * n_phys, first_seq=l * bs,
            s5=(state_s5_re[l], state_s5_im[l]),
            ffn_conv=state_ffn_conv[l])
        yp, sp = _prompt_layer(yp, b, t, lw, alpha)
        ys, ss = _sample_layer(ys, bs, ts, lw, alpha, past, page_table, past_len)
        st_p.append(sp)
        st_s.append(ss)
    sp = [jnp.stack(z) for z in zip(*st_p)]
    ss = [jnp.stack(z) for z in zip(*st_s)]
    wb = win_fm.shape[2]
    full_win = jnp.concatenate([win_fm.reshape(depth, bs, kw, wb), ss[5].transpose(0, 1, 3, 2)], axis=3)
    keep = min(WINDOW, wb + ts)
    ss[5] = full_win[..., wb + ts - keep:].transpose(0, 1, 3, 2).reshape(depth, bs, keep, 2, KV_G, HEAD_DIM)
    out = [yp.reshape(b, t, d), ys.reshape(bs, T_PAD, d)[:, :ts]]
    for a, c in zip(sp, ss):
        out += [a, c]
    return tuple(out)
```

```python
import functools

import numpy as np
import jax
import jax.numpy as jnp
from jax import lax
from jax.experimental import pallas as pl
from jax.experimental.pallas import tpu as pltpu

F32 = jnp.float32
BF16 = jnp.bfloat16
HI = lax.Precision.HIGHEST

LANE = 128
SUBLANE = 8
VMEM_LIMIT = 56 * 1024 * 1024

D_MODEL = 2048
HEAD_DIM = 64
N_BRANCH = 4
MIX = D_MODEL // N_BRANCH
N_HEADS = MIX // HEAD_DIM
KV_G = 2
HG = N_HEADS // KV_G
CMP_LEN = 32
CMP_STRIDE = 16
SLC_BLOCK = 64
SLC_TOPN = 16
WINDOW = 512
MOBA_BLOCK = 256
MOBA_TOPK = 3
S5_CH = 16
S5_GROUPS = MIX // S5_CH
S5_STATE = 64
D_FF = 5632
CONV_W = 3
LN_EPS = 1e-5
SCALE = HEAD_DIM ** -0.5
NEG = -1e30
FORCE = 1e4
T_PAD = 8
SLOPES = tuple(float(v) for v in np.asarray(2.0 ** (-8.0 * np.arange(1, N_HEADS + 1) / N_HEADS), np.float32))

_SPLITS = (('nsa_q', MIX), ('kv_cmp', 2 * KV_G * HEAD_DIM), ('kv_slc', 2 * KV_G * HEAD_DIM),
           ('kv_win', 2 * KV_G * HEAD_DIM), ('nsa_gate', 3 * N_HEADS), ('s5_u', MIX),
           ('moba', 3 * MIX), ('fox', 3 * MIX), ('fox_f', N_HEADS), ('merge', N_BRANCH * D_MODEL))
_SRC = {}
_o = 0
for _n, _w in _SPLITS:
    _SRC[_n] = (_o, _w)
    _o += _w
IN_WIDTH = _o
_COL = dict(merge=0, nsa_q=8192, kv_cmp=8704, kv_slc=8960, kv_win=9216, nsa_gate=9472, s5_u=9600,
            moba=10240, fox=11776, fox_f=13312)
WP = 13824


def _cparams(sem):
    return pltpu.CompilerParams(dimension_semantics=sem, vmem_limit_bytes=VMEM_LIMIT)


def _dot_nt(a, b, precision=None):
    return lax.dot_general(a, b, (((1,), (1,)), ((), ())), precision=precision, preferred_element_type=F32)


def _split_bf16(x):
    hi = x.astype(BF16)
    return hi, (x - hi.astype(F32)).astype(BF16)


def _dot3(a, b_split):
    a_hi, a_lo = _split_bf16(a)
    b_hi, b_lo = b_split
    dot = functools.partial(jnp.dot, preferred_element_type=F32)
    return dot(a_hi, b_hi) + (dot(a_hi, b_lo) + dot(a_lo, b_hi))


def _sigmoid(x):
    return 1.0 / (1.0 + jnp.exp(-x))


def _pack_w_in(w):
    lead = w.shape[:-1]
    order = ('merge', 'nsa_q', 'kv_cmp', 'kv_slc', 'kv_win', 'nsa_gate', 's5_u', 'moba', 'fox', 'fox_f')
    parts, pos = [], 0
    for name in order:
        if _COL[name] > pos:
            parts.append(jnp.zeros(lead + (_COL[name] - pos,), w.dtype))
        s, wd = _SRC[name]
        parts.append(w[..., s:s + wd])
        pos = _COL[name] + wd
    parts.append(jnp.zeros(lead + (WP - pos,), w.dtype))
    return jnp.concatenate(parts, axis=-1).astype(BF16)


def _mm_kernel(x_ref, w_ref, o_ref, xb_ref):
    @pl.when(pl.program_id(1) == 0)
    def _():
        xb_ref[...] = x_ref[...].astype(BF16)

    o_ref[...] = jnp.dot(xb_ref[...], w_ref[...], preferred_element_type=F32)


def _matmul(x, w_l, tm, tn):
    w, l = w_l
    m, k = x.shape
    n = w.shape[2]
    return pl.pallas_call(
        _mm_kernel, grid=(m // tm, n // tn),
        in_specs=[pl.BlockSpec((tm, k), lambda i, j: (i, 0)), pl.BlockSpec((None, k, tn), lambda i, j: (l, 0, j))],
        out_specs=pl.BlockSpec((tm, tn), lambda i, j: (i, j)),
        out_shape=jax.ShapeDtypeStruct((m, n), F32),
        scratch_shapes=[pltpu.VMEM((tm, k), BF16)],
        compiler_params=_cparams(("parallel", "arbitrary")), name="mm")(x, w)


def _transpose_kernel(c_ref, o_ref):
    o_ref[...] = c_ref[...].T


def _kv_state(c, b, t, col, width, t_from=0):
    tf = min(512, width)
    tt = 512
    nt, n0 = t // tt, t_from // tt
    st = pl.pallas_call(
        _transpose_kernel, grid=(b, width // tf, nt - n0),
        in_specs=[pl.BlockSpec((tt, tf), lambda i, f, q: (i * nt + n0 + q, col // tf + f))],
        out_specs=pl.BlockSpec((None, tf, tt), lambda i, f, q: (i, f, q)),
        out_shape=jax.ShapeDtypeStruct((b, width, t - t_from), F32),
        compiler_params=_cparams(("parallel", "parallel", "parallel")), name="kv_state")(c)
    return st.reshape(b, 2, width // (2 * HEAD_DIM), HEAD_DIM, t - t_from).transpose(0, 4, 1, 2, 3)


def _logf_kernel(c_ref, b_ref, o_ref):
    x = c_ref[...] + b_ref[...]
    y = jnp.minimum(x, 0.0) - jnp.log1p(jnp.exp(-jnp.abs(x)))
    o_ref[...] = y[:, :N_HEADS]


def _logf(c, b_pad, tm):
    n = c.shape[0]
    return pl.pallas_call(
        _logf_kernel, grid=(n // tm,),
        in_specs=[pl.BlockSpec((tm, LANE), lambda i: (i, _COL['fox_f'] // LANE)),
                  pl.BlockSpec((1, LANE), lambda i: (0, 0))],
        out_specs=pl.BlockSpec((tm, N_HEADS), lambda i: (i, 0)),
        out_shape=jax.ShapeDtypeStruct((n, N_HEADS), F32),
        compiler_params=_cparams(("parallel",)), name="logf")(c, b_pad)


def _cumsum_kernel(*refs, n_in):
    in_refs, o_ref = refs[-n_in - 1:-1], refs[-1]
    r = lax.broadcasted_iota(jnp.int32, (LANE, LANE), 0)
    c = lax.broadcasted_iota(jnp.int32, (LANE, LANE), 1)
    tri = jnp.where(r <= c, 1.0, 0.0).astype(F32)
    carry = jnp.zeros((N_HEADS, 1), F32)
    off = 0
    for ref in in_refs:
        for j in range(ref.shape[-1] // LANE):
            x = ref[0, :, j * LANE:(j + 1) * LANE]
            cs = jnp.dot(x, tri, precision=HI, preferred_element_type=F32) + carry
            o_ref[0, :, off:off + LANE] = cs
            carry = cs[:, LANE - 1:LANE]
            off += LANE


def _cumsum_prompt(logf_t):
    b, h, t = logf_t.shape
    return pl.pallas_call(
        functools.partial(_cumsum_kernel, n_in=1), grid=(b,),
        in_specs=[pl.BlockSpec((1, h, t), lambda i: (i, 0, 0))],
        out_specs=pl.BlockSpec((1, h, t), lambda i: (i, 0, 0)),
        out_shape=jax.ShapeDtypeStruct((b, h, t), F32),
        compiler_params=_cparams(("parallel",)), name="fox_cumsum_prompt")(logf_t)


M_FLOOR = -1e29


def _softmax_step(m, l, s):
    m_new = jnp.maximum(m, jnp.max(s, axis=1, keepdims=True))
    p = jnp.exp(s - m_new)
    alpha = jnp.exp(m - m_new)
    return m_new, alpha * l + jnp.sum(p, axis=1, keepdims=True), alpha, p.astype(BF16)


def _pipelined_attention(chains, lo, hi, score_fn, value_fn, rows, tk, dv, last_fix=None):
    def flush(c, kj, s, p_prev, a_prev, acc):
        prev = jnp.maximum(kj - 1, lo)
        return a_prev * acc + jnp.dot(p_prev, value_fn(c, prev), preferred_element_type=F32)

    def body(kj, carry):
        out = []
        for c in range(chains):
            s, p_prev, a_prev, m, l, acc = carry[c]
            acc = flush(c, kj, s, p_prev, a_prev, acc)
            m, l, alpha, p = _softmax_step(m, l, s)
            out.append((score_fn(c, kj + 1), p, alpha, m, l, acc))
        return tuple(out)

    init = tuple((score_fn(c, lo), jnp.zeros((rows, tk), BF16), jnp.ones((rows, 1), F32),
                  jnp.full((rows, 1), M_FLOOR, F32), jnp.zeros((rows, 1), F32), jnp.zeros((rows, dv), F32))
                 for c in range(chains))
    carry = lax.fori_loop(lo, hi, body, init)
    outs = []
    for c in range(chains):
        s, p_prev, a_prev, m, l, acc = carry[c]
        acc = flush(c, hi, s, p_prev, a_prev, acc)
        if last_fix is not None:
            s = last_fix(c, s)
        m, l, alpha, p = _softmax_step(m, l, s)
        acc = alpha * acc + jnp.dot(p, value_fn(c, hi), preferred_element_type=F32)
        outs.append(acc / jnp.maximum(l, 1e-30))
    return outs


def _attend_tiles(s_tiles, v_tiles):
    m = functools.reduce(jnp.maximum, [jnp.max(s, axis=1, keepdims=True) for s in s_tiles])
    l, acc = 0.0, 0.0
    for s, (v, feature_major) in zip(s_tiles, v_tiles):
        e = jnp.where(s > 0.5 * NEG, jnp.exp(s - m), 0.0)
        l = l + jnp.sum(e, axis=1, keepdims=True)
        eb = e.astype(BF16)
        acc = acc + (_dot_nt(eb, v) if feature_major else jnp.dot(eb, v, preferred_element_type=F32))
    return acc / jnp.maximum(l, 1e-30)


def _topn_mask_wide(v, ncols, topn):
    rows = v.shape[0]
    nc8 = -(-ncols // SUBLANE) * SUBLANE
    vt = v.T[:nc8, :]
    jr = lax.broadcasted_iota(jnp.int32, vt.shape, 0)
    rank = jnp.zeros(vt.shape, F32)
    for j2 in range(ncols):
        cand = vt[j2:j2 + 1, :]
        beats = (cand > vt) | ((cand == vt) & (jr > j2))
        rank = rank + jnp.where(beats, 1.0, 0.0)
    top = jnp.where(rank < topn, 1.0, 0.0)
    return _pad_rows(top, v.shape[1]).T > 0.5


def _topn_mask(v, ncols, topn):
    jl = lax.broadcasted_iota(jnp.int32, v.shape, 1)
    rank = jnp.zeros(v.shape, F32)
    for j2 in range(ncols):
        col = v[:, j2:j2 + 1]
        beats = (col > v) | ((col == v) & (jl > j2))
        rank = rank + jnp.where(beats, 1.0, 0.0)
    return rank < topn


def _pad_rows(a, rows):
    return jnp.concatenate([a, jnp.zeros((rows - a.shape[0], a.shape[1]), a.dtype)], axis=0)


def _fox_prompt_kernel(q_ref, k_ref, v_ref, cr_ref, o_ref, *, tq):
    qi = pl.program_id(2)
    q0 = pl.multiple_of(qi * tq, tq)
    row = lax.broadcasted_iota(jnp.int32, (tq, tq), 0)
    col = lax.broadcasted_iota(jnp.int32, (tq, tq), 1)
    qs = [(q_ref[:, h2 * HEAD_DIM:(h2 + 1) * HEAD_DIM] * SCALE).astype(BF16) for h2 in range(2)]
    c0 = [cr_ref[0, h2, :, pl.ds(q0, LANE)][:, 0:1] for h2 in range(2)]

    def score(h2, kj):
        ks = pl.multiple_of(kj * tq, tq)
        k = k_ref[pl.ds(ks, tq), h2 * HEAD_DIM:(h2 + 1) * HEAD_DIM].astype(BF16)
        return _dot_nt(qs[h2], k) + (c0[h2] - cr_ref[0, h2, :, pl.ds(ks, tq)])

    def value(h2, kj):
        ks = pl.multiple_of(kj * tq, tq)
        return v_ref[pl.ds(ks, tq), h2 * HEAD_DIM:(h2 + 1) * HEAD_DIM].astype(BF16)

    outs = _pipelined_attention(2, 0, qi, score, value, tq, tq, HEAD_DIM,
                                last_fix=lambda h2, s: jnp.where(col <= row, s, NEG))
    o_ref[...] = jnp.concatenate(outs, axis=1)


def _fox_prompt(c, cumr, b, t, tq):
    nq = t // tq
    base = _COL['fox'] // LANE
    hp_n = N_HEADS // 2
    return pl.pallas_call(
        functools.partial(_fox_prompt_kernel, tq=tq), grid=(b, hp_n, nq),
        in_specs=[pl.BlockSpec((tq, LANE), lambda i, h, q: (i * nq + q, base + h)),
                  pl.BlockSpec((t, LANE), lambda i, h, q: (i, base + hp_n + h)),
                  pl.BlockSpec((t, LANE), lambda i, h, q: (i, base + 2 * hp_n + h)),
                  pl.BlockSpec((1, 2, 1, t), lambda i, h, q: (i, h, 0, 0))],
        out_specs=pl.BlockSpec((tq, LANE), lambda i, h, q: (i * nq + q, h)),
        out_shape=jax.ShapeDtypeStruct((b * t, MIX), F32),
        compiler_params=_cparams(("parallel", "arbitrary", "arbitrary")), name="fox_prompt")(
            c, c, c, cumr)


def _head_slope(hp, h2):
    s = jnp.float32(SLOPES[h2])
    for k in range(1, N_HEADS // 2):
        s = jnp.where(hp == k, jnp.float32(SLOPES[2 * k + h2]), s)
    return s


def _moba_prompt_kernel(q_ref, k_ref, v_ref, o_ref, kmean_ref, *, tq, nb):
    hp = pl.program_id(1)
    qi = pl.program_id(2)
    bpt = tq // MOBA_BLOCK

    @pl.when(qi == 0)
    def _():
        kmean_ref[...] = jnp.zeros_like(kmean_ref)
        for n in range(nb):
            kmean_ref[n:n + 1, :] = jnp.mean(k_ref[n * MOBA_BLOCK:(n + 1) * MOBA_BLOCK, :], axis=0, keepdims=True)

    q0 = pl.multiple_of(qi * tq, tq)
    row = lax.broadcasted_iota(jnp.int32, (tq, tq), 0)
    col = lax.broadcasted_iota(jnp.int32, (tq, tq), 1)
    colpos = lax.broadcasted_iota(jnp.int32, (1, tq), 1)
    jl = lax.broadcasted_iota(jnp.int32, (tq, LANE), 1)
    cur = qi * bpt + lax.broadcasted_iota(jnp.int32, (tq, 1), 0) // MOBA_BLOCK
    qs, slopes, blockbias = [], [], []
    for h2 in range(2):
        lo = h2 * HEAD_DIM
        qf = q_ref[:, lo:lo + HEAD_DIM]
        gate = _dot_nt(qf, kmean_ref[:, lo:lo + HEAD_DIM], precision=HI)
        gate = jnp.where(jl < cur, gate, NEG)
        picked = (_topn_mask_wide(gate, nb, MOBA_TOPK) & (jl < cur)) | (jl == cur)
        qs.append((qf * SCALE).astype(BF16))
        slopes.append(_head_slope(hp, h2))
        blockbias.append(jnp.where(picked, 0.0, NEG))

    def score(h2, kj):
        ks = pl.multiple_of(kj * tq, tq)
        k = k_ref[pl.ds(ks, tq), h2 * HEAD_DIM:(h2 + 1) * HEAD_DIM].astype(BF16)
        bias = None
        for j in reversed(range(bpt)):
            rowbias = jnp.min(jnp.where(jl == kj * bpt + j, blockbias[h2], 0.0), axis=1, keepdims=True)
            bias = rowbias if bias is None else jnp.where(colpos < (j + 1) * MOBA_BLOCK, rowbias, bias)
        return _dot_nt(qs[h2], k) + slopes[h2] * (ks - q0 + colpos).astype(F32) + bias

    def value(h2, kj):
        ks = pl.multiple_of(kj * tq, tq)
        return v_ref[pl.ds(ks, tq), h2 * HEAD_DIM:(h2 + 1) * HEAD_DIM].astype(BF16)

    outs = _pipelined_attention(2, 0, qi, score, value, tq, tq, HEAD_DIM,
                                last_fix=lambda h2, s: jnp.where(col <= row, s, NEG))
    o_ref[...] = jnp.concatenate(outs, axis=1)


def _moba_prompt(c, b, t, tq):
    nq = t // tq
    base = _COL['moba'] // LANE
    hp_n = N_HEADS // 2
    return pl.pallas_call(
        functools.partial(_moba_prompt_kernel, tq=tq, nb=t // MOBA_BLOCK), grid=(b, hp_n, nq),
        in_specs=[pl.BlockSpec((tq, LANE), lambda i, h, q: (i * nq + q, base + h)),
                  pl.BlockSpec((t, LANE), lambda i, h, q: (i, base + hp_n + h)),
                  pl.BlockSpec((t, LANE), lambda i, h, q: (i, base + 2 * hp_n + h))],
        out_specs=pl.BlockSpec((tq, LANE), lambda i, h, q: (i * nq + q, h)),
        out_shape=jax.ShapeDtypeStruct((b * t, MIX), F32),
        scratch_shapes=[pltpu.VMEM((LANE, LANE), F32)],
        compiler_params=_cparams(("parallel", "arbitrary", "arbitrary")), name="moba_prompt")(c, c, c)


def _cmp_kernel(*refs, n_in, feature_major):
    if feature_major:
        xs_ref, refs = refs[-1], refs[:-1]
        x_refs = refs[-5 - n_in:-5]
        page = x_refs[0].shape[2]
        halves = x_refs[0].shape[1] // LANE
        for p, r in enumerate(x_refs):
            for j in range(halves):
                xs_ref[j, p * page:(p + 1) * page, :] = r[0, j * LANE:(j + 1) * LANE, :].T
        nchunk = n_in * page // CMP_STRIDE
        x = jnp.concatenate([xs_ref[j, pl.ds(l, nchunk, stride=CMP_STRIDE), :]
                             for l in range(CMP_STRIDE) for j in range(halves)], axis=1)
    else:
        x = refs[-6][0]
    pos_ref, wcat_ref, w0_ref, w1_ref, o_ref = refs[-5:]
    xb = x.astype(BF16)
    p0 = jnp.dot(xb, w0_ref[...], preferred_element_type=F32)
    p1 = jnp.dot(xb, w1_ref[...], preferred_element_type=F32)
    bias = jnp.dot(pos_ref[...].astype(BF16), wcat_ref[...], preferred_element_type=F32)[0:1]
    o_ref[0] = p0 + pltpu.roll(p1, p1.shape[0] - 1, 0) + bias


def _cmp_weights(pos, wk, wv):
    r = CMP_LEN // CMP_STRIDE
    w = jnp.stack([wk, wv]).reshape(2, r, CMP_STRIDE, HEAD_DIM, HEAD_DIM)
    e2 = jnp.eye(2, dtype=w.dtype)
    eg = jnp.eye(KV_G, dtype=w.dtype)
    big = jnp.einsum('krlde,kK,gG->rlkgdKGe', w, e2, eg)
    big = big.reshape(r, CMP_STRIDE * 2 * KV_G * HEAD_DIM, 2 * KV_G * HEAD_DIM).astype(BF16)
    wcat = jnp.concatenate([wk, wk, wv, wv], axis=1).astype(BF16)
    posb = jnp.zeros((SUBLANE, CMP_LEN * HEAD_DIM), F32).at[0].set(pos.reshape(-1))
    return posb, wcat, big[0], big[1]


def _const_specs(arrays):
    return [pl.BlockSpec(a.shape, lambda *_, nd=a.ndim: (0,) * nd) for a in arrays]


def _cmp_prompt(c, b, t, cw):
    kw = 2 * KV_G * HEAD_DIM
    nchunk = t // CMP_STRIDE
    x = c[:, _COL['kv_cmp']:_COL['kv_cmp'] + kw].reshape(b, nchunk, CMP_STRIDE * kw)
    return pl.pallas_call(
        functools.partial(_cmp_kernel, n_in=1, feature_major=False), grid=(b,),
        in_specs=[pl.BlockSpec((1, nchunk, CMP_STRIDE * kw), lambda i: (i, 0, 0))] + _const_specs(cw),
        out_specs=pl.BlockSpec((1, nchunk, kw), lambda i: (i, 0, 0)),
        out_shape=jax.ShapeDtypeStruct((b, nchunk, kw), F32),
        compiler_params=_cparams(("parallel",)), name="nsa_cmp_prompt")(x, *cw)


def _cmp_paged(page_table, pool, first, cw):
    bs, npages = page_table.shape
    kw, page = pool.shape[1:]
    nchunk = npages * page // CMP_STRIDE
    return pl.pallas_call(
        functools.partial(_cmp_kernel, n_in=npages, feature_major=True),
        grid_spec=pltpu.PrefetchScalarGridSpec(
            num_scalar_prefetch=1, grid=(bs,), in_specs=_page_specs(pool, npages, first) + _const_specs(cw),
            out_specs=pl.BlockSpec((1, nchunk, kw), lambda b, pt: (b, 0, 0)),
            scratch_shapes=[pltpu.VMEM((kw // LANE, npages * page, LANE), F32)]),
        out_shape=jax.ShapeDtypeStruct((bs, nchunk, kw), F32),
        compiler_params=_cparams(("parallel",)), name="nsa_cmp_paged")(page_table, *([pool] * npages), *cw)


def _overlap_matrix(nc, ns):
    i_c = np.arange(LANE)[:, None] * CMP_STRIDE
    j_s = np.arange(LANE)[None, :] * SLC_BLOCK
    ov = (i_c < j_s + SLC_BLOCK) & (i_c + CMP_LEN > j_s)
    ov &= (np.arange(LANE)[:, None] < nc) & (np.arange(LANE)[None, :] < ns)
    return jnp.asarray(ov, F32)


def _nsa_prompt_kernel(q_ref, slc_ref, win_ref, cmp_ref, g_ref, ov_ref, o_ref, *, tq, nc, ns):
    qi = pl.program_id(1)
    q0 = pl.multiple_of(qi * tq, tq)
    rows = HG * tq
    rl = lax.broadcasted_iota(jnp.int32, (tq, 1), 0)
    t1 = q0 + rl
    t4 = jnp.concatenate([t1] * HG, axis=0)
    lane = lax.broadcasted_iota(jnp.int32, (1, LANE), 1)
    colpos = lax.broadcasted_iota(jnp.int32, (1, tq), 1)
    dloc = lax.broadcasted_iota(jnp.int32, (tq, tq), 0) - lax.broadcasted_iota(jnp.int32, (tq, tq), 1)
    sg = _sigmoid(g_ref[...])
    jl = lax.broadcasted_iota(jnp.int32, (tq, LANE), 1)
    cur = t1 // SLC_BLOCK
    erow = lax.broadcasted_iota(jnp.int32, (LANE, tq), 0)
    ecol = lax.broadcasted_iota(jnp.int32, (LANE, tq), 1)
    wtiles = WINDOW // tq
    qs, slope, o_cmp, blockbias = [], [], [], []
    for g in range(KV_G):
        kl = g * HEAD_DIM
        vl = KV_G * HEAD_DIM + g * HEAD_DIM
        qg = jnp.concatenate([q_ref[:, (g * HG + h) * HEAD_DIM:(g * HG + h + 1) * HEAD_DIM] for h in range(HG)],
                             axis=0)
        qs.append((qg * SCALE).astype(BF16))
        slope.append(jnp.concatenate([jnp.full((tq, 1), SLOPES[g * HG + h], F32) for h in range(HG)], axis=0))

        kc = cmp_ref[0, :, kl:kl + HEAD_DIM].astype(BF16)
        vc = cmp_ref[0, :, vl:vl + HEAD_DIM].astype(BF16)
        dist = (t4 - (lane * CMP_STRIDE + CMP_LEN - 1)).astype(F32)
        ok_c = (dist >= 0) & (lane < nc)
        s = jnp.where(ok_c, _dot_nt(qs[g], kc) - slope[g] * dist, NEG)
        m = jnp.max(s, axis=1, keepdims=True)
        e = jnp.where(ok_c, jnp.exp(s - m), 0.0)
        p_c = e / jnp.maximum(jnp.sum(e, axis=1, keepdims=True), 1e-30)
        o_cmp.append(jnp.dot(p_c.astype(BF16), vc, preferred_element_type=F32))

        psum = p_c[0:tq]
        for h in range(1, HG):
            psum = psum + p_c[h * tq:(h + 1) * tq]
        imp = jnp.dot(psum, ov_ref[...], precision=HI, preferred_element_type=F32)
        forced = (jl == 0) | (jl == cur) | (jl == cur - 1)
        imp = jnp.where(forced, FORCE, imp)
        imp = jnp.where(jl <= cur, imp, NEG)
        picked = _topn_mask_wide(imp, ns, min(SLC_TOPN, ns)) & (jl <= cur)
        blockbias.append(jnp.where(picked, 0.0, NEG).astype(BF16))

    def scores(ref, g, ks, bias):
        k = ref[pl.ds(ks, tq), g * HEAD_DIM:(g + 1) * HEAD_DIM].astype(BF16)
        cpos = (ks - q0 + colpos).astype(F32)
        return _dot_nt(qs[g], k) + slope[g] * cpos + jnp.concatenate([bias] * HG, axis=0)

    def values(ref, g, kj):
        ks = pl.multiple_of(kj * tq, tq)
        vl = KV_G * HEAD_DIM + g * HEAD_DIM
        return ref[pl.ds(ks, tq), vl:vl + HEAD_DIM].astype(BF16)

    def slc_score(g, kj):
        ks = pl.multiple_of(kj * tq, tq)
        expand = jnp.where((ks + ecol) // SLC_BLOCK == erow, 1.0, 0.0).astype(BF16)
        return scores(slc_ref, g, ks, jnp.dot(blockbias[g], expand, preferred_element_type=F32))

    def win_score(g, kj):
        ks = pl.multiple_of(kj * tq, tq)
        d = dloc + (q0 - ks)
        return scores(win_ref, g, ks, jnp.where((d >= 0) & (d < WINDOW), 0.0, NEG))

    causal = jnp.concatenate([dloc] * HG, axis=0) >= 0
    o_slc = _pipelined_attention(KV_G, 0, qi, slc_score, functools.partial(values, slc_ref), rows, tq, HEAD_DIM,
                                 last_fix=lambda g, s: jnp.where(causal, s, NEG))
    o_win = _pipelined_attention(KV_G, jnp.maximum(qi - wtiles, 0), qi, win_score,
                                 functools.partial(values, win_ref), rows, tq, HEAD_DIM)

    for g in range(KV_G):
        for h in range(HG):
            hh = g * HG + h
            r0 = h * tq
            o = (sg[:, hh:hh + 1] * o_cmp[g][r0:r0 + tq]
                 + sg[:, N_HEADS + hh:N_HEADS + hh + 1] * o_slc[g][r0:r0 + tq]
                 + sg[:, 2 * N_HEADS + hh:2 * N_HEADS + hh + 1] * o_win[g][r0:r0 + tq])
            o_ref[:, hh * HEAD_DIM:(hh + 1) * HEAD_DIM] = o


def _nsa_prompt(c, kvc, ov, b, t, tq):
    nq = t // tq
    kw = 2 * KV_G * HEAD_DIM
    nchunk = t // CMP_STRIDE
    nc = nchunk - CMP_LEN // CMP_STRIDE + 1
    ns = -(-t // SLC_BLOCK)
    return pl.pallas_call(
        functools.partial(_nsa_prompt_kernel, tq=tq, nc=nc, ns=ns), grid=(b, nq),
        in_specs=[pl.BlockSpec((tq, MIX), lambda i, q: (i * nq + q, _COL['nsa_q'] // MIX)),
                  pl.BlockSpec((t, kw), lambda i, q: (i, _COL['kv_slc'] // kw)),
                  pl.BlockSpec((t, kw), lambda i, q: (i, _COL['kv_win'] // kw)),
                  pl.BlockSpec((1, nchunk, kw), lambda i, q: (i, 0, 0)),
                  pl.BlockSpec((tq, LANE), lambda i, q: (i * nq + q, _COL['nsa_gate'] // LANE)),
                  pl.BlockSpec((LANE, LANE), lambda i, q: (0, 0))],
        out_specs=pl.BlockSpec((tq, MIX), lambda i, q: (i * nq + q, 0)),
        out_shape=jax.ShapeDtypeStruct((b * t, MIX), F32),
        compiler_params=_cparams(("parallel", "arbitrary")), name="nsa_prompt")(c, c, c, kvc, c, ov)


def _dec_rows():
    rid = lax.broadcasted_iota(jnp.int32, (N_HEADS * T_PAD, 1), 0)
    t8 = rid % T_PAD
    slope = jnp.concatenate([jnp.full((T_PAD, 1), SLOPES[h], F32) for h in range(N_HEADS)], axis=0)
    return t8, slope


def _expand_heads(q8):
    lane = lax.broadcasted_iota(jnp.int32, q8.shape, 1)
    return jnp.concatenate([jnp.where(lane // HEAD_DIM == h, q8, 0.0) for h in range(N_HEADS)], axis=0)


def _collapse_heads(res):
    lane = lax.broadcasted_iota(jnp.int32, (T_PAD, res.shape[1]), 1)
    out = jnp.zeros((T_PAD, res.shape[1]), F32)
    for h in range(N_HEADS):
        out = out + jnp.where(lane // HEAD_DIM == h, res[h * T_PAD:(h + 1) * T_PAD], 0.0)
    return out


def _moba_dec_kernel(pt_ref, q_ref, kn_ref, vn_ref, *rest, npages, past):
    page_refs, o_ref = rest[:npages], rest[npages]
    page = page_refs[0].shape[2]
    t8, slope = _dec_rows()
    qpos = past + t8
    lane = lax.broadcasted_iota(jnp.int32, (1, LANE), 1)
    q8 = q_ref[0]
    qx_f = _expand_heads(q8)
    qx = (qx_f * SCALE).astype(BF16)
    per_blk = MOBA_BLOCK // page
    nb_past = past // MOBA_BLOCK
    lanei = lax.broadcasted_iota(jnp.int32, (MIX, LANE), 1)
    kmean = jnp.zeros((MIX, LANE), F32)
    for n in range(nb_past):
        tot = page_refs[n * per_blk][0, 0:MIX, :]
        for p in range(n * per_blk + 1, (n + 1) * per_blk):
            tot = tot + page_refs[p][0, 0:MIX, :]
        col = jnp.sum(tot, axis=1, keepdims=True) * (1.0 / MOBA_BLOCK)
        kmean = kmean + jnp.where(lanei == n, col, 0.0)
    gate = jnp.dot(qx_f, kmean, precision=HI, preferred_element_type=F32)
    jl = lax.broadcasted_iota(jnp.int32, gate.shape, 1)
    cur = qpos // MOBA_BLOCK
    gate = jnp.where(jl < cur, gate, NEG)
    nb = -(-(past + T_PAD) // MOBA_BLOCK)
    sel = jnp.where(_topn_mask(gate, nb, max(1, min(MOBA_TOPK, nb - 1))) & (jl < cur), 1.0, 0.0)
    s_tiles, v_tiles = [], []
    for p in range(npages):
        kt = page_refs[p][0, 0:MIX, :].astype(BF16)
        v_tiles.append((page_refs[p][0, MIX:2 * MIX, :].astype(BF16), True))
        n = (p * page) // MOBA_BLOCK
        d = (qpos - (p * page + lane)).astype(F32)
        valid = (sel[:, n:n + 1] > 0.5) & (d >= 0)
        s_tiles.append(jnp.where(valid, jnp.dot(qx, kt, preferred_element_type=F32) - slope * d, NEG))
    k = _pad_rows(kn_ref[0], LANE).astype(BF16)
    v_tiles.append((_pad_rows(vn_ref[0], LANE).astype(BF16), False))
    d = (t8 - lane).astype(F32)
    valid = (d >= 0) & (lane < T_PAD)
    s_tiles.append(jnp.where(valid, _dot_nt(qx, k) - slope * d, NEG))
    o_ref[0] = _collapse_heads(_attend_tiles(s_tiles, v_tiles))


def _fox_dec_kernel(pt_ref, q_ref, kn_ref, vn_ref, ln_ref, *rest, npages, past):
    logf_refs, page_refs, o_ref = rest[:npages], rest[npages:2 * npages], rest[2 * npages]
    t8, _ = _dec_rows()
    lane = lax.broadcasted_iota(jnp.int32, (1, LANE), 1)
    qx = (_expand_heads(q_ref[0]) * SCALE).astype(BF16)
    r = lax.broadcasted_iota(jnp.int32, (LANE, LANE), 0)
    c = lax.broadcasted_iota(jnp.int32, (LANE, LANE), 1)
    tri = jnp.where(r <= c, 1.0, 0.0).astype(F32)

    def head_rows(x):
        return jnp.concatenate([jnp.broadcast_to(x[h:h + 1], (T_PAD, x.shape[1])) for h in range(N_HEADS)], axis=0)

    carry = jnp.zeros((N_HEADS, 1), F32)
    cum = []
    for ref in list(logf_refs) + [ln_ref]:
        cs = jnp.dot(ref[0], tri, precision=HI, preferred_element_type=F32) + carry
        cum.append(cs)
        carry = cs[:, LANE - 1:LANE]
    c_ref = head_rows(cum[npages - 1][:, LANE - 1:LANE])

    s_tiles, v_tiles = [], []
    for p in range(npages):
        kt = page_refs[p][0, 0:MIX, :].astype(BF16)
        v_tiles.append((page_refs[p][0, MIX:2 * MIX, :].astype(BF16), True))
        s_tiles.append(jnp.dot(qx, kt, preferred_element_type=F32) + (c_ref - head_rows(cum[p])))
    k = _pad_rows(kn_ref[0], LANE).astype(BF16)
    v_tiles.append((_pad_rows(vn_ref[0], LANE).astype(BF16), False))
    valid = (lane <= t8) & (lane < T_PAD)
    s_tiles.append(jnp.where(valid, _dot_nt(qx, k) + (c_ref - head_rows(cum[npages])), NEG))
    o_ref[0] = _collapse_heads(_attend_tiles(s_tiles, v_tiles))


def _page_specs(pool, npages, first):
    blk = (1,) + pool.shape[1:]
    return [pl.BlockSpec(blk, lambda b, pt, p=p: (first + pt[b, p], 0, 0)) for p in range(npages)]


def _paged_mha_dec(kernel, name, c3, col, page_table, pool, first, extra_args, extra_specs, past):
    bs, npages = page_table.shape
    qb = col // MIX
    in_specs = [pl.BlockSpec((1, T_PAD, MIX), lambda b, pt: (b, 0, qb)),
                pl.BlockSpec((1, T_PAD, MIX), lambda b, pt: (b, 0, qb + 1)),
                pl.BlockSpec((1, T_PAD, MIX), lambda b, pt: (b, 0, qb + 2))]
    in_specs += extra_specs
    in_specs += _page_specs(pool, npages, first)
    return pl.pallas_call(
        functools.partial(kernel, npages=npages, past=past),
        grid_spec=pltpu.PrefetchScalarGridSpec(
            num_scalar_prefetch=1, grid=(bs,), in_specs=in_specs,
            out_specs=pl.BlockSpec((1, T_PAD, MIX), lambda b, pt: (b, 0, 0))),
        out_shape=jax.ShapeDtypeStruct((bs, T_PAD, MIX), F32),
        compiler_params=_cparams(("parallel",)), name=name)(
            page_table, c3, c3, c3, *extra_args, *([pool] * npages))


def _nsa_dec_kernel(pt_ref, q_ref, ns_ref, nw_ref, g_ref, cmp_ref, ov_ref, wc_ref, *rest, npages, past, nc, ns):
    page_refs, o_ref = rest[:npages], rest[npages]
    page = page_refs[0].shape[2]
    kw = KV_G * HEAD_DIM
    t8, slope = _dec_rows()
    qpos = past + t8
    lane = lax.broadcasted_iota(jnp.int32, (1, LANE), 1)
    lane8 = lax.broadcasted_iota(jnp.int32, (T_PAD, LANE), 1)
    q8 = q_ref[0] * SCALE
    rows = []
    for h in range(N_HEADS):
        x = q8[:, (h // 2) * LANE:(h // 2 + 1) * LANE]
        dst = h // HG
        if h % 2 != dst:
            x = pltpu.roll(x, HEAD_DIM, 1)
        rows.append(jnp.where((lane8 // HEAD_DIM) == dst, x, 0.0))
    qx = jnp.concatenate(rows, axis=0).astype(BF16)

    cm = cmp_ref[0]
    dist = (qpos - (lane * CMP_STRIDE + CMP_LEN - 1)).astype(F32)
    ok_c = (dist >= 0) & (lane < nc)
    s = jnp.where(ok_c, _dot_nt(qx, cm[:, 0:kw].astype(BF16)) - slope * dist, NEG)
    m = jnp.max(s, axis=1, keepdims=True)
    e = jnp.where(ok_c, jnp.exp(s - m), 0.0)
    p_c = e / jnp.maximum(jnp.sum(e, axis=1, keepdims=True), 1e-30)
    o_cmp = jnp.dot(p_c.astype(BF16), cm[:, kw:2 * kw].astype(BF16), preferred_element_type=F32)

    psum = []
    for g in range(KV_G):
        acc = p_c[g * HG * T_PAD:g * HG * T_PAD + T_PAD]
        for h in range(1, HG):
            r0 = (g * HG + h) * T_PAD
            acc = acc + p_c[r0:r0 + T_PAD]
        psum.append(acc)
    imp = jnp.dot(jnp.concatenate(psum, axis=0), ov_ref[...], precision=HI, preferred_element_type=F32)
    jl = lax.broadcasted_iota(jnp.int32, imp.shape, 1)
    tg = lax.broadcasted_iota(jnp.int32, (KV_G * T_PAD, 1), 0) % T_PAD
    cur = (past + tg) // SLC_BLOCK
    forced = (jl == 0) | (jl == cur) | (jl == cur - 1)
    imp = jnp.where(forced, FORCE, imp)
    imp = jnp.where(jl <= cur, imp, NEG)
    sel = jnp.where(_topn_mask(imp, ns, min(SLC_TOPN, ns)) & (jl <= cur), 1.0, 0.0)
    sel_rows = jnp.concatenate([sel[(h // HG) * T_PAD:(h // HG + 1) * T_PAD] for h in range(N_HEADS)], axis=0)

    def new_tile(ref, extra_valid):
        k = _pad_rows(ref[0, :, 0:kw], LANE).astype(BF16)
        v = _pad_rows(ref[0, :, kw:2 * kw], LANE).astype(BF16)
        d = (t8 - lane).astype(F32)
        valid = (d >= 0) & (lane < T_PAD) & extra_valid
        return jnp.where(valid, _dot_nt(qx, k) - slope * d, NEG), (v, False)

    s_tiles, v_tiles = [], []
    per = page // SLC_BLOCK
    for p in range(npages):
        kt = page_refs[p][0, 0:kw, :].astype(BF16)
        v_tiles.append((page_refs[p][0, kw:2 * kw, :].astype(BF16), True))
        picked = jnp.zeros((N_HEADS * T_PAD, LANE), jnp.bool_)
        for j in range(per):
            blk = p * per + j
            picked = picked | ((lane // SLC_BLOCK == j) & (sel_rows[:, blk:blk + 1] > 0.5))
        d = (qpos - (p * page + lane)).astype(F32)
        s_tiles.append(jnp.where(picked & (d >= 0), jnp.dot(qx, kt, preferred_element_type=F32) - slope * d, NEG))
    blk_new = past // SLC_BLOCK
    s_new, v_new = new_tile(ns_ref, sel_rows[:, blk_new:blk_new + 1] > 0.5)
    o_slc = _attend_tiles(s_tiles + [s_new], v_tiles + [v_new])

    s_tiles, v_tiles = [], []
    wb = wc_ref.shape[2]
    w_off = past - wb
    for j in range(wb // LANE):
        kt = wc_ref[0, 0:kw, j * LANE:(j + 1) * LANE].astype(BF16)
        v_tiles.append((wc_ref[0, kw:2 * kw, j * LANE:(j + 1) * LANE].astype(BF16), True))
        d = qpos - (w_off + j * LANE + lane)
        valid = (d >= 0) & (d < WINDOW)
        s_tiles.append(jnp.where(valid, jnp.dot(qx, kt, preferred_element_type=F32) - slope * d.astype(F32), NEG))
    s_new, v_new = new_tile(nw_ref, True)
    o_win = _attend_tiles(s_tiles + [s_new], v_tiles + [v_new])

    sg = _sigmoid(g_ref[0])
    for h in range(N_HEADS):
        r0, l0 = h * T_PAD, (h // HG) * HEAD_DIM
        o = (sg[:, h:h + 1] * o_cmp[r0:r0 + T_PAD, l0:l0 + HEAD_DIM]
             + sg[:, N_HEADS + h:N_HEADS + h + 1] * o_slc[r0:r0 + T_PAD, l0:l0 + HEAD_DIM]
             + sg[:, 2 * N_HEADS + h:2 * N_HEADS + h + 1] * o_win[r0:r0 + T_PAD, l0:l0 + HEAD_DIM])
        o_ref[0, :, h * HEAD_DIM:(h + 1) * HEAD_DIM] = o


def _nsa_dec(c3, kvc, ov, win_cache, win_first, page_table, pool, first, past, t_real):
    bs, npages = page_table.shape
    kw = 2 * KV_G * HEAD_DIM
    nchunk = kvc.shape[1]
    nc = nchunk - CMP_LEN // CMP_STRIDE + 1
    ns = -(-(past + t_real) // SLC_BLOCK)
    wb = win_cache.shape[2]
    in_specs = [pl.BlockSpec((1, T_PAD, MIX), lambda b, pt: (b, 0, _COL['nsa_q'] // MIX)),
                pl.BlockSpec((1, T_PAD, kw), lambda b, pt: (b, 0, _COL['kv_slc'] // kw)),
                pl.BlockSpec((1, T_PAD, kw), lambda b, pt: (b, 0, _COL['kv_win'] // kw)),
                pl.BlockSpec((1, T_PAD, LANE), lambda b, pt: (b, 0, _COL['nsa_gate'] // LANE)),
                pl.BlockSpec((1, nchunk, kw), lambda b, pt: (b, 0, 0)),
                pl.BlockSpec((LANE, LANE), lambda b, pt: (0, 0)),
                pl.BlockSpec((1, kw, wb), lambda b, pt: (win_first + b, 0, 0))]
    in_specs += _page_specs(pool, npages, first)
    return pl.pallas_call(
        functools.partial(_nsa_dec_kernel, npages=npages, past=past, nc=nc, ns=ns),
        grid_spec=pltpu.PrefetchScalarGridSpec(
            num_scalar_prefetch=1, grid=(bs,), in_specs=in_specs,
            out_specs=pl.BlockSpec((1, T_PAD, MIX), lambda b, pt: (b, 0, 0))),
        out_shape=jax.ShapeDtypeStruct((bs, T_PAD, MIX), F32),
        compiler_params=_cparams(("parallel",)), name="nsa_decode")(
            page_table, c3, c3, c3, c3, kvc, ov, win_cache, *([pool] * npages))


def _s5_disc_kernel(ar_ref, ai_ref, ls_ref, btr_ref, bti_ref, abr_ref, abi_ref, bbr_ref, bbi_ref):
    ar, ai = ar_ref[...], ai_ref[...]
    step = jnp.exp(ls_ref[...])
    mag = jnp.exp(ar * step)
    abr = mag * jnp.cos(ai * step)
    abi = mag * jnp.sin(ai * step)
    den = ar * ar + ai * ai
    zr = (ar * (abr - 1.0) + ai * abi) / den
    zi = (ar * abi - ai * (abr - 1.0)) / den
    abr_ref[...] = abr
    abi_ref[...] = abi
    btr, bti = btr_ref[...], bti_ref[...]
    bbr_ref[...] = zr * btr - zi * bti
    bbi_ref[...] = zr * bti + zi * btr


def _s5_disc(a_re, a_im, log_step, b_re, b_im):
    rep = lambda a: jnp.repeat(a, S5_CH, axis=0)
    n = S5_GROUPS * S5_CH
    args = (rep(a_re), rep(a_im), rep(jnp.broadcast_to(log_step[:, None], (S5_GROUPS, S5_STATE))),
            b_re.transpose(0, 2, 1).reshape(n, S5_STATE), b_im.transpose(0, 2, 1).reshape(n, S5_STATE))
    shp = jax.ShapeDtypeStruct((n, S5_STATE), F32)
    return pl.pallas_call(_s5_disc_kernel, out_shape=(shp, shp, shp, shp), name="s5_discretise")(*args)


_S5_CH_ROWS = 128


def _s5_scan_kernel(*refs, seg, has_h0):
    if has_h0:
        (u_ref, bre_ref, bim_ref, ar_ref, ai_ref, cre_ref, cim_ref, d_ref, h0r_ref, h0i_ref,
         y_ref, hlr_ref, hli_ref, hr_s, hi_s) = refs
    else:
        (u_ref, bre_ref, bim_ref, ar_ref, ai_ref, cre_ref, cim_ref, d_ref,
         y_ref, hlr_ref, hli_ref, hr_s, hi_s) = refs
    rows = u_ref.shape[0]
    ch = min(_S5_CH_ROWS, rows)
    pad = ch
    nch = rows // ch
    ar, ai = ar_ref[0], ai_ref[0]
    bre, bim, cre, cim = (_split_bf16(r[0]) for r in (bre_ref, bim_ref, cre_ref, cim_ref))
    hr_s[0:pad, :] = jnp.zeros((pad, hr_s.shape[1]), F32)
    hi_s[0:pad, :] = jnp.zeros((pad, hi_s.shape[1]), F32)

    def init_body(i, _):
        r0 = pl.multiple_of(i * ch, ch)
        u = u_ref[pl.ds(r0, ch), :]
        br = _dot3(u, bre)
        bi = _dot3(u, bim)
        if has_h0:
            h0r, h0i = h0r_ref[pl.ds(r0, ch), :], h0i_ref[pl.ds(r0, ch), :]
            br = br + (ar * h0r - ai * h0i)
            bi = bi + (ar * h0i + ai * h0r)
        hr_s[pl.ds(pad + r0, ch), :] = br
        hi_s[pl.ds(pad + r0, ch), :] = bi
        return 0

    lax.fori_loop(0, nch, init_body, 0)

    rl = lax.broadcasted_iota(jnp.int32, (ch, 1), 0)
    pr, pi = ar, ai
    d = 1
    two_level = seg == rows and rows > ch
    span = ch if two_level else seg
    while d < span:
        first = d // ch

        def pass_body(i, _, d=d, pr=pr, pi=pi):
            r0 = pl.multiple_of((nch - 1 - i) * ch, ch)
            cr = hr_s[pl.ds(pad + r0, ch), :]
            ci = hi_s[pl.ds(pad + r0, ch), :]
            if d < SUBLANE:
                lo = pad - SUBLANE
                sr = pltpu.roll(hr_s[pl.ds(lo + r0, ch + SUBLANE), :], d, 0)[SUBLANE:]
                si = pltpu.roll(hi_s[pl.ds(lo + r0, ch + SUBLANE), :], d, 0)[SUBLANE:]
            else:
                sr = hr_s[pl.ds(pad + r0 - d, ch), :]
                si = hi_s[pl.ds(pad + r0 - d, ch), :]
            if d < ch or seg < rows:
                keep = ((r0 + rl) % span) >= d
                sr = jnp.where(keep, sr, 0.0)
                si = jnp.where(keep, si, 0.0)
            hr_s[pl.ds(pad + r0, ch), :] = cr + (pr * sr - pi * si)
            hi_s[pl.ds(pad + r0, ch), :] = ci + (pr * si + pi * sr)
            return 0

        lax.fori_loop(0, nch - first, pass_body, 0)
        pr, pi = pr * pr - pi * pi, 2.0 * pr * pi
        d *= 2

    if two_level:
        tr, ti = jnp.broadcast_to(ar, (ch, ar.shape[1])), jnp.broadcast_to(ai, (ch, ai.shape[1]))
        d = 1
        while d < ch:
            sr, si = pltpu.roll(tr, d, 0), pltpu.roll(ti, d, 0)
            keep = rl >= d
            tr, ti = jnp.where(keep, tr * sr - ti * si, tr), jnp.where(keep, tr * si + ti * sr, ti)
            d *= 2
        lasts = [(hr_s[pad + (c + 1) * ch - 1:pad + (c + 1) * ch, :], hi_s[pad + (c + 1) * ch - 1:pad + (c + 1) * ch, :])
                 for c in range(nch - 1)]
        kr, ki = lasts[0]
        for c in range(1, nch):
            if c > 1:
                lr, li = lasts[c - 1]
                kr, ki = lr + (pr * kr - pi * ki), li + (pr * ki + pi * kr)
            rows_c = slice(pad + c * ch, pad + (c + 1) * ch)
            hr_s[rows_c, :] = hr_s[rows_c, :] + (tr * kr - ti * ki)
            hi_s[rows_c, :] = hi_s[rows_c, :] + (tr * ki + ti * kr)

    def out_body(i, _):
        r0 = pl.multiple_of(i * ch, ch)
        hr = hr_s[pl.ds(pad + r0, ch), :]
        hi = hi_s[pl.ds(pad + r0, ch), :]
        y = (jnp.dot(hr.astype(BF16), cre[0], preferred_element_type=F32)
             - jnp.dot(hi.astype(BF16), cim[0], preferred_element_type=F32))
        y_ref[pl.ds(r0, ch), :] = y + d_ref[0] * u_ref[pl.ds(r0, ch), :]
        return 0

    lax.fori_loop(0, nch, out_body, 0)
    nl = hlr_ref.shape[0]
    hlr_ref[...] = hr_s[pad + rows - nl:pad + rows, :]
    hli_ref[...] = hi_s[pad + rows - nl:pad + rows, :]


def _s5_scan(c, sw, rows, seg, h0=None):
    n = c.shape[0]
    nt = n // rows
    gl = LANE // S5_CH
    lt = S5_GROUPS // gl
    w = gl * S5_STATE
    nl = (rows // seg) * SUBLANE if seg == SUBLANE else SUBLANE
    ub = _COL['s5_u'] // LANE
    in_specs = [pl.BlockSpec((rows, LANE), lambda i, j: (i, ub + j)),
                pl.BlockSpec((1, LANE, w), lambda i, j: (j, 0, 0)),
                pl.BlockSpec((1, LANE, w), lambda i, j: (j, 0, 0)),
                pl.BlockSpec((1, 1, w), lambda i, j: (j, 0, 0)),
                pl.BlockSpec((1, 1, w), lambda i, j: (j, 0, 0)),
                pl.BlockSpec((1, w, LANE), lambda i, j: (j, 0, 0)),
                pl.BlockSpec((1, w, LANE), lambda i, j: (j, 0, 0)),
                pl.BlockSpec((1, 1, LANE), lambda i, j: (j, 0, 0))]
    args = [c, sw['bre'], sw['bim'], sw['ar'], sw['ai'], sw['cre'], sw['cim'], sw['d']]
    if h0 is not None:
        in_specs += [pl.BlockSpec((rows, w), lambda i, j: (i, j))] * 2
        args += list(h0)
    hshape = jax.ShapeDtypeStruct((nt * nl, S5_GROUPS * S5_STATE), F32)
    return pl.pallas_call(
        functools.partial(_s5_scan_kernel, seg=seg, has_h0=h0 is not None), grid=(nt, lt),
        in_specs=in_specs,
        out_specs=(pl.BlockSpec((rows, LANE), lambda i, j: (i, j)),
                   pl.BlockSpec((nl, w), lambda i, j: (i, j)), pl.BlockSpec((nl, w), lambda i, j: (i, j))),
        out_shape=(jax.ShapeDtypeStruct((n, MIX), F32), hshape, hshape),
        scratch_shapes=[pltpu.VMEM((min(_S5_CH_ROWS, rows) + rows, w), F32)] * 2,
        compiler_params=_cparams(("parallel", "arbitrary")), name="s5_scan")(*args)


def _s5_weights(lp):
    abr, abi, bbr, bbi = _s5_disc(lp['s5_a_re'], lp['s5_a_im'], lp['s5_log_step'], lp['s5_b_re'], lp['s5_b_im'])
    gl = LANE // S5_CH
    lt = S5_GROUPS // gl
    eye = jnp.eye(gl, dtype=F32)

    def bdiag(bb):
        return jnp.einsum('jgcn,gh->jgchn', bb.reshape(lt, gl, S5_CH, S5_STATE), eye).reshape(
            lt, gl * S5_CH, gl * S5_STATE)

    def cdiag(cc):
        return jnp.einsum('jgcn,gh->jgnhc', cc.reshape(lt, gl, S5_CH, S5_STATE), eye).reshape(
            lt, gl * S5_STATE, gl * S5_CH)

    return dict(bre=bdiag(bbr), bim=bdiag(bbi),
                ar=abr[::S5_CH].reshape(lt, 1, gl * S5_STATE), ai=abi[::S5_CH].reshape(lt, 1, gl * S5_STATE),
                cre=cdiag(lp['s5_c_re']), cim=cdiag(lp['s5_c_im']), d=lp['s5_d'].reshape(lt, 1, LANE))


def _glu_kernel(y_ref, w_ref, o_ref):
    y = y_ref[...]
    g = 0.5 * y * (1.0 + jnp.tanh(np.float32(np.sqrt(2.0 / np.pi)) * (y + np.float32(0.044715) * (y * y * y))))
    z = jnp.dot(g.astype(BF16), w_ref[...], preferred_element_type=F32)
    o_ref[...] = z[:, :MIX] * _sigmoid(z[:, MIX:])


def _glu(y, w_l, tm):
    w, l = w_l
    n = y.shape[0]
    return pl.pallas_call(
        _glu_kernel, grid=(n // tm,),
        in_specs=[pl.BlockSpec((tm, MIX), lambda i: (i, 0)),
                  pl.BlockSpec((None, MIX, 2 * MIX), lambda i: (l, 0, 0))],
        out_specs=pl.BlockSpec((tm, MIX), lambda i: (i, 0)),
        out_shape=jax.ShapeDtypeStruct((n, MIX), F32),
        compiler_params=_cparams(("parallel",)), name="s5_glu")(y, w)


def _layer_norm(x, g, b):
    mu = jnp.mean(x, axis=-1, keepdims=True)
    xc = x - mu
    var = jnp.mean(xc * xc, axis=-1, keepdims=True)
    return xc * lax.rsqrt(var + LN_EPS) * g + b


def _merge_kernel(o0_ref, o1_ref, o2_ref, o3_ref, mg_ref, wb_ref, x_ref, wo_ref, g_ref, b_ref, out_ref, acc_ref,
                  *, alpha):
    i = pl.program_id(1)

    @pl.when(i == 0)
    def _():
        acc_ref[...] = jnp.zeros_like(acc_ref)

    for k, o_ref in enumerate((o0_ref, o1_ref, o2_ref, o3_ref)):
        @pl.when(i == k)
        def _(o_ref=o_ref):
            proj = jnp.dot(o_ref[...].astype(BF16), wb_ref[0], preferred_element_type=F32)
            acc_ref[...] += _sigmoid(mg_ref[...]) * proj

    @pl.when(i == N_BRANCH - 1)
    def _():
        mixed = jnp.dot(acc_ref[...].astype(BF16), wo_ref[...], preferred_element_type=F32)
        out_ref[...] = _layer_norm(alpha * x_ref[...] + mixed, g_ref[...], b_ref[...])


def _merge(outs, c, x, wb_l, wo_l, g, b, tm, alpha):
    (wb, l), wo = wb_l, wo_l[0]
    n = x.shape[0]
    o_spec = pl.BlockSpec((tm, MIX), lambda r, i: (r, 0))
    return pl.pallas_call(
        functools.partial(_merge_kernel, alpha=alpha), grid=(n // tm, N_BRANCH),
        in_specs=[o_spec, o_spec, o_spec, o_spec,
                  pl.BlockSpec((tm, D_MODEL), lambda r, i: (r, i)),
                  pl.BlockSpec((None, 1, MIX, D_MODEL), lambda r, i: (l, i, 0, 0)),
                  pl.BlockSpec((tm, D_MODEL), lambda r, i: (r, 0)),
                  pl.BlockSpec((None, D_MODEL, D_MODEL), lambda r, i: (l, 0, 0), pipeline_mode=pl.Buffered(1)),
                  pl.BlockSpec((1, D_MODEL), lambda r, i: (0, 0)),
                  pl.BlockSpec((1, D_MODEL), lambda r, i: (0, 0))],
        out_specs=pl.BlockSpec((tm, D_MODEL), lambda r, i: (r, 0)),
        out_shape=jax.ShapeDtypeStruct((n, D_MODEL), F32),
        scratch_shapes=[pltpu.VMEM((tm, D_MODEL), F32)],
        compiler_params=_cparams(("parallel", "arbitrary")), name="merge_out_ln")(
            *outs, c, wb, x, wo, g, b)


FFN_SUBTILES = 1


def _ffn_kernel(*refs, halo, seg, alpha):
    (h_ref, wg_ref, wv_ref, cwg_ref, cwv_ref, cbg_ref, cbv_ref, wd_ref, lg_ref, lb_ref) = refs[:10]
    pg_ref, pv_ref, out_ref, hb_ref, acc_ref = refs[10:]
    j = pl.program_id(1)

    @pl.when(j == 0)
    def _():
        hb_ref[...] = h_ref[...].astype(BF16)
        acc_ref[...] = jnp.zeros_like(acc_ref)

    hb = hb_ref[...]
    tm = hb.shape[0]
    tf = wg_ref.shape[1]
    rid = lax.broadcasted_iota(jnp.int32, (tm, 1), 0)
    rid8 = lax.broadcasted_iota(jnp.int32, (SUBLANE, 1), 0)

    def conv(u, cw, cb, prev):
        r1 = pltpu.roll(u, 1, 0)
        r2 = pltpu.roll(u, 2, 0)
        if halo:
            p6, p7 = prev[SUBLANE - 2:SUBLANE - 1], prev[SUBLANE - 1:SUBLANE]
            top1 = jnp.where(rid8 == 0, p7, r1[:SUBLANE])
            top2 = jnp.where(rid8 == 0, p6, jnp.where(rid8 == 1, p7, r2[:SUBLANE]))
            u1 = jnp.concatenate([top1, r1[SUBLANE:]], axis=0)
            u2 = jnp.concatenate([top2, r2[SUBLANE:]], axis=0)
        else:
            t = rid % seg
            state = jnp.where(t >= seg - (CONV_W - 1), prev, u)
            u1 = jnp.where(t >= 1, r1, pltpu.roll(state, tm - (seg - 1), 0))
            u2 = jnp.where(t >= 2, r2, pltpu.roll(state, tm - (seg - 2), 0))
        return cb + (cw[0:1] * u2 + cw[1:2] * u1 + cw[2:3] * u)

    sub = tf // FFN_SUBTILES
    ups = []
    for s in range(FFN_SUBTILES):
        cols = slice(s * sub, (s + 1) * sub)
        ups.append((jnp.dot(hb, wg_ref[:, cols], preferred_element_type=F32),
                    jnp.dot(hb, wv_ref[:, cols], preferred_element_type=F32)))
    down = None
    for s, (ug, uv) in enumerate(ups):
        cols = slice(s * sub, (s + 1) * sub)
        pg = pg_ref[0][:, cols] if halo else pg_ref[:, cols]
        pv = pv_ref[0][:, cols] if halo else pv_ref[:, cols]
        gate = conv(ug, cwg_ref[:, cols], cbg_ref[:, cols], pg)
        val = conv(uv, cwv_ref[:, cols], cbv_ref[:, cols], pv)
        act = (gate * _sigmoid(gate) * val).astype(BF16)
        part = jnp.dot(act, wd_ref[cols, :], preferred_element_type=F32)
        down = part if down is None else down + part
    acc_ref[...] += down

    @pl.when(j == pl.num_programs(1) - 1)
    def _():
        out_ref[...] = _layer_norm(alpha * h_ref[...] + acc_ref[...], lg_ref[...], lb_ref[...])


def _ffn(h, lw, tm, tf, alpha, prev=None, state_rows=None, seg=None):
    n = h.shape[0]
    nf = D_FF // tf
    halo = prev is not None
    (w_up, l), w_down = lw['w_up'], lw['w_down'][0]
    in_specs = [pl.BlockSpec((tm, D_MODEL), lambda r, j: (r, 0), pipeline_mode=pl.Buffered(1)),
                pl.BlockSpec((None, D_MODEL, tf), lambda r, j: (l, 0, j)),
                pl.BlockSpec((None, D_MODEL, tf), lambda r, j: (l, 0, nf + j)),
                pl.BlockSpec((CONV_W, tf), lambda r, j: (0, j)),
                pl.BlockSpec((CONV_W, tf), lambda r, j: (0, nf + j)),
                pl.BlockSpec((1, tf), lambda r, j: (0, j)),
                pl.BlockSpec((1, tf), lambda r, j: (0, nf + j)),
                pl.BlockSpec((None, tf, D_MODEL), lambda r, j: (l, j, 0)),
                pl.BlockSpec((1, D_MODEL), lambda r, j: (0, 0)),
                pl.BlockSpec((1, D_MODEL), lambda r, j: (0, 0))]
    args = [h, w_up, w_up, lw['conv_w'], lw['conv_w'], lw['conv_b'], lw['conv_b'], w_down,
            lw['ln2_g'], lw['ln2_b']]
    if halo:
        in_specs += [pl.BlockSpec((1, SUBLANE, tf), lambda r, j: (r, 0, j)),
                     pl.BlockSpec((1, SUBLANE, tf), lambda r, j: (r, 0, nf + j))]
        args += [prev, prev]
    else:
        in_specs += [pl.BlockSpec((tm, tf), lambda r, j: (r, j)), pl.BlockSpec((tm, tf), lambda r, j: (r, nf + j))]
        args += [state_rows, state_rows]
    return pl.pallas_call(
        functools.partial(_ffn_kernel, halo=halo, seg=seg, alpha=alpha), grid=(n // tm, nf),
        in_specs=in_specs,
        out_specs=pl.BlockSpec((tm, D_MODEL), lambda r, j: (r, 0), pipeline_mode=pl.Buffered(1)),
        out_shape=jax.ShapeDtypeStruct((n, D_MODEL), F32),
        scratch_shapes=[pltpu.VMEM((tm, D_MODEL), BF16), pltpu.VMEM((tm, D_MODEL), F32)],
        compiler_params=_cparams(("parallel", "arbitrary")), name="conv_ffn_ln")(*args)


def _prep_shared(p):
    return dict(w_in=_pack_w_in(p['w_in']), w_glu=p['s5_w_glu'].astype(BF16), w_branch=p['w_branch'].astype(BF16),
                w_out=p['w_out'].astype(BF16), w_up=p['ffn_w_up'].astype(BF16), w_down=p['ffn_w_down'].astype(BF16))


def _prep_layer(l, p, shared):
    lp = {k: v[l] for k, v in p.items() if k not in ('w_in', 's5_w_glu', 'w_branch', 'w_out', 'ffn_w_up', 'ffn_w_down')}
    lw = {k: (v, l) for k, v in shared.items()}
    lw.update(
        fox_b=jnp.zeros((1, LANE), F32).at[0, :N_HEADS].set(lp['fox_b_f']),
        cmp=_cmp_weights(lp['nsa_cmp_pos'], lp['nsa_cmp_wk'], lp['nsa_cmp_wv']),
        s5=_s5_weights(lp),
        ln1_g=lp['ln1_g'].reshape(1, -1), ln1_b=lp['ln1_b'].reshape(1, -1),
        conv_w=lp['ffn_conv_w'], conv_b=lp['ffn_conv_b'].reshape(1, -1),
        ln2_g=lp['ln2_g'].reshape(1, -1), ln2_b=lp['ln2_b'].reshape(1, -1))
    return lw


def _prompt_layer(x, b, t, lw, alpha):
    n = b * t
    kw = 2 * KV_G * HEAD_DIM
    tr = min(1024, n)
    c = _matmul(x, lw['w_in'], tr, 1536)
    logf = _logf(c, lw['fox_b'], tr)
    cumr = _cumsum_prompt(logf.reshape(b, t, N_HEADS).transpose(0, 2, 1))
    o_fox = _fox_prompt(c, cumr[:, :, None, :], b, t, 512)
    o_moba = _moba_prompt(c, b, t, 2 * MOBA_BLOCK)
    kvc = _cmp_prompt(c, b, t, lw['cmp'])
    nchunk = t // CMP_STRIDE
    ov = _overlap_matrix(nchunk - CMP_LEN // CMP_STRIDE + 1, -(-t // SLC_BLOCK))
    o_nsa = _nsa_prompt(c, kvc, ov, b, t, 256)
    y_s5, hlr, hli = _s5_scan(c, lw['s5'], t, t)
    o_s5 = _glu(y_s5, lw['w_glu'], tr)
    h = _merge((o_nsa, o_s5, o_moba, o_fox), c, x, lw['w_branch'], lw['w_out'], lw['ln1_g'], lw['ln1_b'], 512, alpha)

    tm = min(1024, t)
    nt = n // tm
    edge = h.reshape(nt, tm, D_MODEL)[:, tm - (CONV_W - 1):].reshape(nt * (CONV_W - 1), D_MODEL)
    edge = _pad_rows(edge, -(-edge.shape[0] // SUBLANE) * SUBLANE)
    u_edge = _matmul(edge, lw['w_up'], edge.shape[0], 2 * D_FF // 8)[:nt * (CONV_W - 1)].reshape(nt, CONV_W - 1, 2 * D_FF)
    per_seq = t // tm
    conv_state = u_edge[per_seq - 1::per_seq]
    starts_seq = (jnp.arange(nt) % per_seq == 0)[:, None, None]
    prev = jnp.where(starts_seq, 0.0, jnp.roll(u_edge, 1, axis=0))
    prev = jnp.pad(prev, ((0, 0), (SUBLANE - (CONV_W - 1), 0), (0, 0)))
    y = _ffn(h, lw, tm, 512, alpha, prev=prev)

    win_rows = min(WINDOW, t)
    states = (
        _kv_state(c, b, t, _COL['kv_cmp'], kw),
        _kv_state(c, b, t, _COL['kv_slc'], kw),
        _kv_state(c, b, t, _COL['moba'] + MIX, 2 * MIX),
        _kv_state(c, b, t, _COL['fox'] + MIX, 2 * MIX),
        logf.reshape(b, t, N_HEADS),
        _kv_state(c, b, t, _COL['kv_win'], kw, t_from=t - win_rows),
        hlr.reshape(b, SUBLANE, S5_GROUPS, S5_STATE)[:, SUBLANE - 1],
        hli.reshape(b, SUBLANE, S5_GROUPS, S5_STATE)[:, SUBLANE - 1],
        conv_state)
    return y, states


def _sample_layer(x, bs, t_real, lw, alpha, past, page_table, past_len):
    n = bs * T_PAD
    kw = 2 * KV_G * HEAD_DIM
    c = _matmul(x, lw['w_in'], n, 1536)
    c3 = c.reshape(bs, T_PAD, WP)
    logf = _logf(c, lw['fox_b'], n)
    logf3 = logf.reshape(bs, T_PAD, N_HEADS)
    tmask = (jnp.arange(T_PAD) < t_real)[None, :, None]
    new_t = jnp.pad(jnp.where(tmask, logf3, 0.0).transpose(0, 2, 1), ((0, 0), (0, 0), (0, LANE - T_PAD)))
    first = past['first_page']
    npages = page_table.shape[1]
    o_fox = _paged_mha_dec(
        _fox_dec_kernel, "fox_decode", c3, _COL['fox'], page_table, past['fox'], first,
        [new_t] + [past['fox_logf']] * npages,
        [pl.BlockSpec((1, N_HEADS, LANE), lambda b, pt: (b, 0, 0))] + _page_specs(past['fox_logf'], npages, first),
        past_len)
    o_moba = _paged_mha_dec(_moba_dec_kernel, "moba_decode", c3, _COL['moba'], page_table, past['moba'], first,
                            [], [], past_len)
    kvc = _cmp_paged(page_table, past['nsa_cmp'], first, lw['cmp'])
    nchunk = kvc.shape[1]
    ov = _overlap_matrix(nchunk - CMP_LEN // CMP_STRIDE + 1, -(-(past_len + t_real) // SLC_BLOCK))
    o_nsa = _nsa_dec(c3, kvc, ov, past['nsa_win'], past['first_seq'], page_table, past['nsa_slc'], first,
                     past_len, t_real)
    h0 = [jnp.pad(s.reshape(bs, 1, -1), ((0, 0), (0, T_PAD - 1), (0, 0))).reshape(n, -1) for s in past['s5']]
    y_s5, hlr, hli = _s5_scan(c, lw['s5'], n, T_PAD, h0=h0)
    o_s5 = _glu(y_s5, lw['w_glu'], n)
    h = _merge((o_nsa.reshape(n, MIX), o_s5, o_moba.reshape(n, MIX), o_fox.reshape(n, MIX)), c, x,
               lw['w_branch'], lw['w_out'], lw['ln1_g'], lw['ln1_b'], min(256, n), alpha)

    buf = past['ffn_conv']
    state_rows = jnp.pad(buf, ((0, 0), (T_PAD - (CONV_W - 1), 0), (0, 0))).reshape(n, 2 * D_FF)
    y = _ffn(h, lw, min(512, n), 512, alpha, state_rows=state_rows, seg=T_PAD)
    last2 = h.reshape(bs, T_PAD, D_MODEL)[:, t_real - (CONV_W - 1):t_real].reshape(bs * (CONV_W - 1), D_MODEL)
    conv_state = _matmul(last2, lw['w_up'], last2.shape[0], 2 * D_FF // 8).reshape(bs, CONV_W - 1, 2 * D_FF)

    tr = lambda a: a[:, :t_real]
    kv_win_new = tr(c3[:, :, _COL['kv_win']:_COL['kv_win'] + kw])
    states = (
        tr(c3[:, :, _COL['kv_cmp']:_COL['kv_cmp'] + kw]).reshape(bs, t_real, 2, KV_G, HEAD_DIM),
        tr(c3[:, :, _COL['kv_slc']:_COL['kv_slc'] + kw]).reshape(bs, t_real, 2, KV_G, HEAD_DIM),
        tr(c3[:, :, _COL['moba'] + MIX:_COL['moba'] + 3 * MIX]).reshape(bs, t_real, 2, N_HEADS, HEAD_DIM),
        tr(c3[:, :, _COL['fox'] + MIX:_COL['fox'] + 3 * MIX]).reshape(bs, t_real, 2, N_HEADS, HEAD_DIM),
        tr(logf3),
        kv_win_new,
        hlr.reshape(bs, T_PAD, S5_GROUPS, S5_STATE)[:, t_real - 1],
        hli.reshape(bs, T_PAD, S5_GROUPS, S5_STATE)[:, t_real - 1],
        conv_state)
    return y, states


def kernel(x_prompt, x_sample, cache_nsa_cmp_kv, cache_nsa_slc_kv, cache_moba_kv, cache_fox_kv, cache_fox_logf,
           page_table, cache_nsa_win_kv, state_s5_re, state_s5_im, state_ffn_conv, w_in, fox_b_f, nsa_cmp_pos,
           nsa_cmp_wk, nsa_cmp_wv, s5_a_re, s5_a_im, s5_b_re, s5_b_im, s5_c_re, s5_c_im, s5_d, s5_log_step,
           s5_w_glu, w_branch, w_out, ln1_g, ln1_b, ffn_w_up, ffn_conv_w, ffn_conv_b, ffn_w_down, ln2_g, ln2_b):
    depth = w_in.shape[0]
    b, t, d = x_prompt.shape
    bs, ts, _ = x_sample.shape
    n_phys, page = cache_nsa_cmp_kv.shape[1:3]
    past_len = page_table.shape[1] * page
    assert d == D_MODEL and w_in.shape[2] == IN_WIDTH and ffn_w_down.shape[1] == D_FF
    assert ts <= T_PAD - (CONV_W - 1) and past_len % MOBA_BLOCK == 0 and page == LANE and (past_len + ts) // CMP_STRIDE * CMP_STRIDE <= past_len
    alpha = float((2 * depth) ** 0.25)
    params = dict(w_in=w_in, fox_b_f=fox_b_f, nsa_cmp_pos=nsa_cmp_pos, nsa_cmp_wk=nsa_cmp_wk, nsa_cmp_wv=nsa_cmp_wv,
                  s5_a_re=s5_a_re, s5_a_im=s5_a_im, s5_b_re=s5_b_re, s5_b_im=s5_b_im, s5_c_re=s5_c_re,
                  s5_c_im=s5_c_im, s5_d=s5_d, s5_log_step=s5_log_step, s5_w_glu=s5_w_glu, w_branch=w_branch,
                  w_out=w_out, ln1_g=ln1_g, ln1_b=ln1_b, ffn_w_up=ffn_w_up, ffn_conv_w=ffn_conv_w,
                  ffn_conv_b=ffn_conv_b, ffn_w_down=ffn_w_down, ln2_g=ln2_g, ln2_b=ln2_b)
    kw = 2 * KV_G * HEAD_DIM
    yp = x_prompt.reshape(b * t, d)
    ys = jnp.pad(x_sample, ((0, 0), (0, T_PAD - ts), (0, 0))).reshape(bs * T_PAD, d)
    st_p, st_s = [], []

    def feature_major(cache):
        dd, nn, rr = cache.shape[:3]
        return cache.transpose(0, 1, 3, 4, 5, 2).reshape(dd * nn, -1, rr)

    cmp_fm, slc_fm, moba_fm, fox_fm, win_fm = (
        feature_major(a) for a in (cache_nsa_cmp_kv, cache_nsa_slc_kv, cache_moba_kv, cache_fox_kv, cache_nsa_win_kv))
    logf_hm = cache_fox_logf.transpose(0, 1, 3, 2).reshape(depth * n_phys, N_HEADS, page)
    shared = _prep_shared(params)
    for l in range(depth):
        lw = _prep_layer(l, params, shared)
        past = dict(
            nsa_cmp=cmp_fm, nsa_slc=slc_fm, moba=moba_fm, fox=fox_fm, fox_logf=logf_hm, nsa_win=win_fm,
            first_page=l * n_phys, first_seq=l * bs,
            s5=(state_s5_re[l], state_s5_im[l]),
            ffn_conv=state_ffn_conv[l])
        yp, sp = _prompt_layer(yp, b, t, lw, alpha)
        ys, ss = _sample_layer(ys, bs, ts, lw, alpha, past, page_table, past_len)
        st_p.append(sp)
        st_s.append(ss)
    sp = [jnp.stack(z) for z in zip(*st_p)]
    ss = [jnp.stack(z) for z in zip(*st_s)]
    wb = win_fm.shape[2]
    full_win = jnp.concatenate([win_fm.reshape(depth, bs, kw, wb), ss[5].transpose(0, 1, 3, 2)], axis=3)
    keep = min(WINDOW, wb + ts)
    ss[5] = full_win[..., wb + ts - keep:].transpose(0, 1, 3, 2).reshape(depth, bs, keep, 2, KV_G, HEAD_DIM)
    out = [yp.reshape(b, t, d), ys.reshape(bs, T_PAD, d)[:, :ts]]
    for a, c in zip(sp, ss):
        out += [a, c]
    return tuple(out)
```

```python
import functools

import numpy as np
import jax
import jax.numpy as jnp
from jax import lax
from jax.experimental import pallas as pl
from jax.experimental.pallas import tpu as pltpu

F32 = jnp.float32
BF16 = jnp.bfloat16
HI = lax.Precision.HIGHEST

LANE = 128
SUBLANE = 8
VMEM_LIMIT = 56 * 1024 * 1024

D_MODEL = 2048
HEAD_DIM = 64
N_BRANCH = 4
MIX = D_MODEL // N_BRANCH
N_HEADS = MIX // HEAD_DIM
KV_G = 2
HG = N_HEADS // KV_G
CMP_LEN = 32
CMP_STRIDE = 16
SLC_BLOCK = 64
SLC_TOPN = 16
WINDOW = 512
MOBA_BLOCK = 256
MOBA_TOPK = 3
S5_CH = 16
S5_GROUPS = MIX // S5_CH
S5_STATE = 64
D_FF = 5632
CONV_W = 3
LN_EPS = 1e-5
SCALE = HEAD_DIM ** -0.5
NEG = -1e30
FORCE = 1e4
T_PAD = 8
SLOPES = tuple(float(v) for v in np.asarray(2.0 ** (-8.0 * np.arange(1, N_HEADS + 1) / N_HEADS), np.float32))

_SPLITS = (('nsa_q', MIX), ('kv_cmp', 2 * KV_G * HEAD_DIM), ('kv_slc', 2 * KV_G * HEAD_DIM),
           ('kv_win', 2 * KV_G * HEAD_DIM), ('nsa_gate', 3 * N_HEADS), ('s5_u', MIX),
           ('moba', 3 * MIX), ('fox', 3 * MIX), ('fox_f', N_HEADS), ('merge', N_BRANCH * D_MODEL))
_SRC = {}
_o = 0
for _n, _w in _SPLITS:
    _SRC[_n] = (_o, _w)
    _o += _w
IN_WIDTH = _o
_COL = dict(merge=0, nsa_q=8192, kv_cmp=8704, kv_slc=8960, kv_win=9216, nsa_gate=9472, s5_u=9600,
            moba=10240, fox=11776, fox_f=13312)
WP = 13824


def _cparams(sem):
    return pltpu.CompilerParams(dimension_semantics=sem, vmem_limit_bytes=VMEM_LIMIT)


def _dot_nt(a, b, precision=None):
    return lax.dot_general(a, b, (((1,), (1,)), ((), ())), precision=precision, preferred_element_type=F32)


def _split_bf16(x):
    hi = x.astype(BF16)
    return hi, (x - hi.astype(F32)).astype(BF16)


def _dot3(a, b_split):
    a_hi, a_lo = _split_bf16(a)
    b_hi, b_lo = b_split
    dot = functools.partial(jnp.dot, preferred_element_type=F32)
    return dot(a_hi, b_hi) + (dot(a_hi, b_lo) + dot(a_lo, b_hi))


def _sigmoid(x):
    return 1.0 / (1.0 + jnp.exp(-x))


def _pack_w_in(w):
    wt = w.transpose(0, 2, 1)
    depth, _, d = wt.shape
    order = ('merge', 'nsa_q', 'kv_cmp', 'kv_slc', 'kv_win', 'nsa_gate', 's5_u', 'moba', 'fox', 'fox_f')
    parts, pos = [], 0
    for name in order:
        if _COL[name] > pos:
            parts.append(jnp.zeros((depth, _COL[name] - pos, d), w.dtype))
        s, wd = _SRC[name]
        parts.append(wt[:, s:s + wd])
        pos = _COL[name] + wd
    parts.append(jnp.zeros((depth, WP - pos, d), w.dtype))
    return jnp.concatenate(parts, axis=1).astype(BF16)


def _mm_kernel(x_ref, w_ref, o_ref, xb_ref, *, w_rows_are_outputs):
    @pl.when(pl.program_id(1) == 0)
    def _():
        xb_ref[...] = x_ref[...].astype(BF16)

    if w_rows_are_outputs:
        o_ref[...] = _dot_nt(xb_ref[...], w_ref[...])
    else:
        o_ref[...] = jnp.dot(xb_ref[...], w_ref[...], preferred_element_type=F32)


def _matmul(x, w_l, tm, tn, w_rows_are_outputs=False):
    w, l = w_l
    m, k = x.shape
    if w_rows_are_outputs:
        n = w.shape[1]
        w_spec = pl.BlockSpec((None, tn, k), lambda i, j: (l, j, 0))
    else:
        n = w.shape[2]
        w_spec = pl.BlockSpec((None, k, tn), lambda i, j: (l, 0, j))
    return pl.pallas_call(
        functools.partial(_mm_kernel, w_rows_are_outputs=w_rows_are_outputs), grid=(m // tm, n // tn),
        in_specs=[pl.BlockSpec((tm, k), lambda i, j: (i, 0)), w_spec],
        out_specs=pl.BlockSpec((tm, tn), lambda i, j: (i, j)),
        out_shape=jax.ShapeDtypeStruct((m, n), F32),
        scratch_shapes=[pltpu.VMEM((tm, k), BF16)],
        compiler_params=_cparams(("parallel", "arbitrary")), name="mm")(x, w)


def _transpose_kernel(c_ref, o_ref):
    o_ref[...] = c_ref[...].T


def _kv_state(c, b, t, col, width, t_from=0):
    tf = min(512, width)
    tt = 512
    nt, n0 = t // tt, t_from // tt
    st = pl.pallas_call(
        _transpose_kernel, grid=(b, width // tf, nt - n0),
        in_specs=[pl.BlockSpec((tt, tf), lambda i, f, q: (i * nt + n0 + q, col // tf + f))],
        out_specs=pl.BlockSpec((None, tf, tt), lambda i, f, q: (i, f, q)),
        out_shape=jax.ShapeDtypeStruct((b, width, t - t_from), F32),
        compiler_params=_cparams(("parallel", "parallel", "parallel")), name="kv_state")(c)
    return st.reshape(b, 2, width // (2 * HEAD_DIM), HEAD_DIM, t - t_from).transpose(0, 4, 1, 2, 3)


def _logf_kernel(c_ref, b_ref, o_ref):
    x = c_ref[...] + b_ref[...]
    y = jnp.minimum(x, 0.0) - jnp.log1p(jnp.exp(-jnp.abs(x)))
    o_ref[...] = y[:, :N_HEADS]


def _logf(c, b_pad, tm):
    n = c.shape[0]
    return pl.pallas_call(
        _logf_kernel, grid=(n // tm,),
        in_specs=[pl.BlockSpec((tm, LANE), lambda i: (i, _COL['fox_f'] // LANE)),
                  pl.BlockSpec((1, LANE), lambda i: (0, 0))],
        out_specs=pl.BlockSpec((tm, N_HEADS), lambda i: (i, 0)),
        out_shape=jax.ShapeDtypeStruct((n, N_HEADS), F32),
        compiler_params=_cparams(("parallel",)), name="logf")(c, b_pad)


def _cumsum_kernel(*refs, n_in):
    in_refs, o_ref = refs[-n_in - 1:-1], refs[-1]
    r = lax.broadcasted_iota(jnp.int32, (LANE, LANE), 0)
    c = lax.broadcasted_iota(jnp.int32, (LANE, LANE), 1)
    tri = jnp.where(r <= c, 1.0, 0.0).astype(F32)
    carry = jnp.zeros((N_HEADS, 1), F32)
    off = 0
    for ref in in_refs:
        for j in range(ref.shape[-1] // LANE):
            x = ref[0, :, j * LANE:(j + 1) * LANE]
            cs = jnp.dot(x, tri, precision=HI, preferred_element_type=F32) + carry
            o_ref[0, :, off:off + LANE] = cs
            carry = cs[:, LANE - 1:LANE]
            off += LANE


def _cumsum_prompt(logf_t):
    b, h, t = logf_t.shape
    return pl.pallas_call(
        functools.partial(_cumsum_kernel, n_in=1), grid=(b,),
        in_specs=[pl.BlockSpec((1, h, t), lambda i: (i, 0, 0))],
        out_specs=pl.BlockSpec((1, h, t), lambda i: (i, 0, 0)),
        out_shape=jax.ShapeDtypeStruct((b, h, t), F32),
        compiler_params=_cparams(("parallel",)), name="fox_cumsum_prompt")(logf_t)


M_FLOOR = -1e29


def _softmax_step(m, l, s):
    m_new = jnp.maximum(m, jnp.max(s, axis=1, keepdims=True))
    p = jnp.exp(s - m_new)
    alpha = jnp.exp(m - m_new)
    return m_new, alpha * l + jnp.sum(p, axis=1, keepdims=True), alpha, p.astype(BF16)


def _pipelined_attention(chains, lo, hi, score_fn, value_fn, rows, tk, dv, last_fix=None):
    def flush(c, kj, s, p_prev, a_prev, acc):
        prev = jnp.maximum(kj - 1, lo)
        return a_prev * acc + jnp.dot(p_prev, value_fn(c, prev), preferred_element_type=F32)

    def body(kj, carry):
        out = []
        for c in range(chains):
            s, p_prev, a_prev, m, l, acc = carry[c]
            acc = flush(c, kj, s, p_prev, a_prev, acc)
            m, l, alpha, p = _softmax_step(m, l, s)
            out.append((score_fn(c, kj + 1), p, alpha, m, l, acc))
        return tuple(out)

    init = tuple((score_fn(c, lo), jnp.zeros((rows, tk), BF16), jnp.ones((rows, 1), F32),
                  jnp.full((rows, 1), M_FLOOR, F32), jnp.zeros((rows, 1), F32), jnp.zeros((rows, dv), F32))
                 for c in range(chains))
    carry = lax.fori_loop(lo, hi, body, init)
    outs = []
    for c in range(chains):
        s, p_prev, a_prev, m, l, acc = carry[c]
        acc = flush(c, hi, s, p_prev, a_prev, acc)
        if last_fix is not None:
            s = last_fix(c, s)
        m, l, alpha, p = _softmax_step(m, l, s)
        acc = alpha * acc + jnp.dot(p, value_fn(c, hi), preferred_element_type=F32)
        outs.append(acc / jnp.maximum(l, 1e-30))
    return outs


def _attend_tiles(s_tiles, v_tiles):
    m = functools.reduce(jnp.maximum, [jnp.max(s, axis=1, keepdims=True) for s in s_tiles])
    l, acc = 0.0, 0.0
    for s, (v, feature_major) in zip(s_tiles, v_tiles):
        e = jnp.where(s > 0.5 * NEG, jnp.exp(s - m), 0.0)
        l = l + jnp.sum(e, axis=1, keepdims=True)
        eb = e.astype(BF16)
        acc = acc + (_dot_nt(eb, v) if feature_major else jnp.dot(eb, v, preferred_element_type=F32))
    return acc / jnp.maximum(l, 1e-30)


def _topn_mask_wide(v, ncols, topn):
    rows = v.shape[0]
    nc8 = -(-ncols // SUBLANE) * SUBLANE
    vt = v.T[:nc8, :]
    jr = lax.broadcasted_iota(jnp.int32, vt.shape, 0)
    rank = jnp.zeros(vt.shape, F32)
    for j2 in range(ncols):
        cand = vt[j2:j2 + 1, :]
        beats = (cand > vt) | ((cand == vt) & (jr > j2))
        rank = rank + jnp.where(beats, 1.0, 0.0)
    top = jnp.where(rank < topn, 1.0, 0.0)
    return _pad_rows(top, v.shape[1]).T > 0.5


def _topn_mask(v, ncols, topn):
    jl = lax.broadcasted_iota(jnp.int32, v.shape, 1)
    rank = jnp.zeros(v.shape, F32)
    for j2 in range(ncols):
        col = v[:, j2:j2 + 1]
        beats = (col > v) | ((col == v) & (jl > j2))
        rank = rank + jnp.where(beats, 1.0, 0.0)
    return rank < topn


def _pad_rows(a, rows):
    return jnp.concatenate([a, jnp.zeros((rows - a.shape[0], a.shape[1]), a.dtype)], axis=0)


def _fox_prompt_kernel(q_ref, k_ref, v_ref, cr_ref, o_ref, *, tq):
    qi = pl.program_id(2)
    q0 = pl.multiple_of(qi * tq, tq)
    row = lax.broadcasted_iota(jnp.int32, (tq, tq), 0)
    col = lax.broadcasted_iota(jnp.int32, (tq, tq), 1)
    qs = [(q_ref[:, h2 * HEAD_DIM:(h2 + 1) * HEAD_DIM] * SCALE).astype(BF16) for h2 in range(2)]
    c0 = [cr_ref[0, h2, :, pl.ds(q0, LANE)][:, 0:1] for h2 in range(2)]

    def score(h2, kj):
        ks = pl.multiple_of(kj * tq, tq)
        k = k_ref[pl.ds(ks, tq), h2 * HEAD_DIM:(h2 + 1) * HEAD_DIM].astype(BF16)
        return _dot_nt(qs[h2], k) + (c0[h2] - cr_ref[0, h2, :, pl.ds(ks, tq)])

    def value(h2, kj):
        ks = pl.multiple_of(kj * tq, tq)
        return v_ref[pl.ds(ks, tq), h2 * HEAD_DIM:(h2 + 1) * HEAD_DIM].astype(BF16)

    outs = _pipelined_attention(2, 0, qi, score, value, tq, tq, HEAD_DIM,
                                last_fix=lambda h2, s: jnp.where(col <= row, s, NEG))
    o_ref[...] = jnp.concatenate(outs, axis=1)


def _fox_prompt(c, cumr, b, t, tq):
    nq = t // tq
    base = _COL['fox'] // LANE
    hp_n = N_HEADS // 2
    return pl.pallas_call(
        functools.partial(_fox_prompt_kernel, tq=tq), grid=(b, hp_n, nq),
        in_specs=[pl.BlockSpec((tq, LANE), lambda i, h, q: (i * nq + q, base + h)),
                  pl.BlockSpec((t, LANE), lambda i, h, q: (i, base + hp_n + h)),
                  pl.BlockSpec((t, LANE), lambda i, h, q: (i, base + 2 * hp_n + h)),
                  pl.BlockSpec((1, 2, 1, t), lambda i, h, q: (i, h, 0, 0))],
        out_specs=pl.BlockSpec((tq, LANE), lambda i, h, q: (i * nq + q, h)),
        out_shape=jax.ShapeDtypeStruct((b * t, MIX), F32),
        compiler_params=_cparams(("parallel", "arbitrary", "arbitrary")), name="fox_prompt")(
            c, c, c, cumr)


def _head_slope(hp, h2):
    s = jnp.float32(SLOPES[h2])
    for k in range(1, N_HEADS // 2):
        s = jnp.where(hp == k, jnp.float32(SLOPES[2 * k + h2]), s)
    return s


def _moba_prompt_kernel(q_ref, k_ref, v_ref, o_ref, kmean_ref, *, tq, nb):
    hp = pl.program_id(1)
    qi = pl.program_id(2)
    bpt = tq // MOBA_BLOCK

    @pl.when(qi == 0)
    def _():
        kmean_ref[...] = jnp.zeros_like(kmean_ref)
        for n in range(nb):
            kmean_ref[n:n + 1, :] = jnp.mean(k_ref[n * MOBA_BLOCK:(n + 1) * MOBA_BLOCK, :], axis=0, keepdims=True)

    q0 = pl.multiple_of(qi * tq, tq)
    row = lax.broadcasted_iota(jnp.int32, (tq, tq), 0)
    col = lax.broadcasted_iota(jnp.int32, (tq, tq), 1)
    colpos = lax.broadcasted_iota(jnp.int32, (1, tq), 1)
    jl = lax.broadcasted_iota(jnp.int32, (tq, LANE), 1)
    cur = qi * bpt + lax.broadcasted_iota(jnp.int32, (tq, 1), 0) // MOBA_BLOCK
    qs, slopes, blockbias = [], [], []
    for h2 in range(2):
        lo = h2 * HEAD_DIM
        qf = q_ref[:, lo:lo + HEAD_DIM]
        gate = _dot_nt(qf, kmean_ref[:, lo:lo + HEAD_DIM], precision=HI)
        gate = jnp.where(jl < cur, gate, NEG)
        picked = (_topn_mask_wide(gate, nb, MOBA_TOPK) & (jl < cur)) | (jl == cur)
        qs.append((qf * SCALE).astype(BF16))
        slopes.append(_head_slope(hp, h2))
        blockbias.append(jnp.where(picked, 0.0, NEG))

    def score(h2, kj):
        ks = pl.multiple_of(kj * tq, tq)
        k = k_ref[pl.ds(ks, tq), h2 * HEAD_DIM:(h2 + 1) * HEAD_DIM].astype(BF16)
        bias = None
        for j in reversed(range(bpt)):
            rowbias = jnp.min(jnp.where(jl == kj * bpt + j, blockbias[h2], 0.0), axis=1, keepdims=True)
            bias = rowbias if bias is None else jnp.where(colpos < (j + 1) * MOBA_BLOCK, rowbias, bias)
        return _dot_nt(qs[h2], k) + slopes[h2] * (ks - q0 + colpos).astype(F32) + bias

    def value(h2, kj):
        ks = pl.multiple_of(kj * tq, tq)
        return v_ref[pl.ds(ks, tq), h2 * HEAD_DIM:(h2 + 1) * HEAD_DIM].astype(BF16)

    outs = _pipelined_attention(2, 0, qi, score, value, tq, tq, HEAD_DIM,
                                last_fix=lambda h2, s: jnp.where(col <= row, s, NEG))
    o_ref[...] = jnp.concatenate(outs, axis=1)


def _moba_prompt(c, b, t, tq):
    nq = t // tq
    base = _COL['moba'] // LANE
    hp_n = N_HEADS // 2
    return pl.pallas_call(
        functools.partial(_moba_prompt_kernel, tq=tq, nb=t // MOBA_BLOCK), grid=(b, hp_n, nq),
        in_specs=[pl.BlockSpec((tq, LANE), lambda i, h, q: (i * nq + q, base + h)),
                  pl.BlockSpec((t, LANE), lambda i, h, q: (i, base + hp_n + h)),
                  pl.BlockSpec((t, LANE), lambda i, h, q: (i, base + 2 * hp_n + h))],
        out_specs=pl.BlockSpec((tq, LANE), lambda i, h, q: (i * nq + q, h)),
        out_shape=jax.ShapeDtypeStruct((b * t, MIX), F32),
        scratch_shapes=[pltpu.VMEM((LANE, LANE), F32)],
        compiler_params=_cparams(("parallel", "arbitrary", "arbitrary")), name="moba_prompt")(c, c, c)


def _cmp_kernel(*refs, n_in, feature_major):
    if feature_major:
        xs_ref, refs = refs[-1], refs[:-1]
        x_refs = refs[-5 - n_in:-5]
        page = x_refs[0].shape[2]
        halves = x_refs[0].shape[1] // LANE
        for p, r in enumerate(x_refs):
            for j in range(halves):
                xs_ref[j, p * page:(p + 1) * page, :] = r[0, j * LANE:(j + 1) * LANE, :].T
        nchunk = n_in * page // CMP_STRIDE
        x = jnp.concatenate([xs_ref[j, pl.ds(l, nchunk, stride=CMP_STRIDE), :]
                             for l in range(CMP_STRIDE) for j in range(halves)], axis=1)
    else:
        x = refs[-6][0]
    pos_ref, wcat_ref, w0_ref, w1_ref, o_ref = refs[-5:]
    xb = x.astype(BF16)
    p0 = jnp.dot(xb, w0_ref[...], preferred_element_type=F32)
    p1 = jnp.dot(xb, w1_ref[...], preferred_element_type=F32)
    bias = jnp.dot(pos_ref[...].astype(BF16), wcat_ref[...], preferred_element_type=F32)[0:1]
    o_ref[0] = p0 + pltpu.roll(p1, p1.shape[0] - 1, 0) + bias


def _cmp_weights(pos, wk, wv):
    r = CMP_LEN // CMP_STRIDE
    w = jnp.stack([wk, wv]).reshape(2, r, CMP_STRIDE, HEAD_DIM, HEAD_DIM)
    e2 = jnp.eye(2, dtype=w.dtype)
    eg = jnp.eye(KV_G, dtype=w.dtype)
    big = jnp.einsum('krlde,kK,gG->rlkgdKGe', w, e2, eg)
    big = big.reshape(r, CMP_STRIDE * 2 * KV_G * HEAD_DIM, 2 * KV_G * HEAD_DIM).astype(BF16)
    wcat = jnp.concatenate([wk, wk, wv, wv], axis=1).astype(BF16)
    posb = jnp.zeros((SUBLANE, CMP_LEN * HEAD_DIM), F32).at[0].set(pos.reshape(-1))
    return posb, wcat, big[0], big[1]


def _const_specs(arrays):
    return [pl.BlockSpec(a.shape, lambda *_, nd=a.ndim: (0,) * nd) for a in arrays]


def _cmp_prompt(c, b, t, cw):
    kw = 2 * KV_G * HEAD_DIM
    nchunk = t // CMP_STRIDE
    x = c[:, _COL['kv_cmp']:_COL['kv_cmp'] + kw].reshape(b, nchunk, CMP_STRIDE * kw)
    return pl.pallas_call(
        functools.partial(_cmp_kernel, n_in=1, feature_major=False), grid=(b,),
        in_specs=[pl.BlockSpec((1, nchunk, CMP_STRIDE * kw), lambda i: (i, 0, 0))] + _const_specs(cw),
        out_specs=pl.BlockSpec((1, nchunk, kw), lambda i: (i, 0, 0)),
        out_shape=jax.ShapeDtypeStruct((b, nchunk, kw), F32),
        compiler_params=_cparams(("parallel",)), name="nsa_cmp_prompt")(x, *cw)


def _cmp_paged(page_table, pool, first, cw):
    bs, npages = page_table.shape
    kw, page = pool.shape[1:]
    nchunk = npages * page // CMP_STRIDE
    return pl.pallas_call(
        functools.partial(_cmp_kernel, n_in=npages, feature_major=True),
        grid_spec=pltpu.PrefetchScalarGridSpec(
            num_scalar_prefetch=1, grid=(bs,), in_specs=_page_specs(pool, npages, first) + _const_specs(cw),
            out_specs=pl.BlockSpec((1, nchunk, kw), lambda b, pt: (b, 0, 0)),
            scratch_shapes=[pltpu.VMEM((kw // LANE, npages * page, LANE), F32)]),
        out_shape=jax.ShapeDtypeStruct((bs, nchunk, kw), F32),
        compiler_params=_cparams(("parallel",)), name="nsa_cmp_paged")(page_table, *([pool] * npages), *cw)


def _overlap_matrix(nc, ns):
    i_c = np.arange(LANE)[:, None] * CMP_STRIDE
    j_s = np.arange(LANE)[None, :] * SLC_BLOCK
    ov = (i_c < j_s + SLC_BLOCK) & (i_c + CMP_LEN > j_s)
    ov &= (np.arange(LANE)[:, None] < nc) & (np.arange(LANE)[None, :] < ns)
    return jnp.asarray(ov, F32)


def _nsa_prompt_kernel(q_ref, slc_ref, win_ref, cmp_ref, g_ref, ov_ref, o_ref, *, tq, nc, ns):
    qi = pl.program_id(1)
    q0 = pl.multiple_of(qi * tq, tq)
    rows = HG * tq
    rl = lax.broadcasted_iota(jnp.int32, (tq, 1), 0)
    t1 = q0 + rl
    t4 = jnp.concatenate([t1] * HG, axis=0)
    lane = lax.broadcasted_iota(jnp.int32, (1, LANE), 1)
    colpos = lax.broadcasted_iota(jnp.int32, (1, tq), 1)
    dloc = lax.broadcasted_iota(jnp.int32, (tq, tq), 0) - lax.broadcasted_iota(jnp.int32, (tq, tq), 1)
    sg = _sigmoid(g_ref[...])
    jl = lax.broadcasted_iota(jnp.int32, (tq, LANE), 1)
    cur = t1 // SLC_BLOCK
    erow = lax.broadcasted_iota(jnp.int32, (LANE, tq), 0)
    ecol = lax.broadcasted_iota(jnp.int32, (LANE, tq), 1)
    wtiles = WINDOW // tq
    qs, slope, o_cmp, blockbias = [], [], [], []
    for g in range(KV_G):
        kl = g * HEAD_DIM
        vl = KV_G * HEAD_DIM + g * HEAD_DIM
        qg = jnp.concatenate([q_ref[:, (g * HG + h) * HEAD_DIM:(g * HG + h + 1) * HEAD_DIM] for h in range(HG)],
                             axis=0)
        qs.append((qg * SCALE).astype(BF16))
        slope.append(jnp.concatenate([jnp.full((tq, 1), SLOPES[g * HG + h], F32) for h in range(HG)], axis=0))

        kc = cmp_ref[0, :, kl:kl + HEAD_DIM].astype(BF16)
        vc = cmp_ref[0, :, vl:vl + HEAD_DIM].astype(BF16)
        dist = (t4 - (lane * CMP_STRIDE + CMP_LEN - 1)).astype(F32)
        ok_c = (dist >= 0) & (lane < nc)
        s = jnp.where(ok_c, _dot_nt(qs[g], kc) - slope[g] * dist, NEG)
        m = jnp.max(s, axis=1, keepdims=True)
        e = jnp.where(ok_c, jnp.exp(s - m), 0.0)
        p_c = e / jnp.maximum(jnp.sum(e, axis=1, keepdims=True), 1e-30)
        o_cmp.append(jnp.dot(p_c.astype(BF16), vc, preferred_element_type=F32))

        psum = p_c[0:tq]
        for h in range(1, HG):
            psum = psum + p_c[h * tq:(h + 1) * tq]
        imp = jnp.dot(psum, ov_ref[...], precision=HI, preferred_element_type=F32)
        forced = (jl == 0) | (jl == cur) | (jl == cur - 1)
        imp = jnp.where(forced, FORCE, imp)
        imp = jnp.where(jl <= cur, imp, NEG)
        picked = _topn_mask_wide(imp, ns, min(SLC_TOPN, ns)) & (jl <= cur)
        blockbias.append(jnp.where(picked, 0.0, NEG).astype(BF16))

    def scores(ref, g, ks, bias):
        k = ref[pl.ds(ks, tq), g * HEAD_DIM:(g + 1) * HEAD_DIM].astype(BF16)
        cpos = (ks - q0 + colpos).astype(F32)
        return _dot_nt(qs[g], k) + slope[g] * cpos + jnp.concatenate([bias] * HG, axis=0)

    def values(ref, g, kj):
        ks = pl.multiple_of(kj * tq, tq)
        vl = KV_G * HEAD_DIM + g * HEAD_DIM
        return ref[pl.ds(ks, tq), vl:vl + HEAD_DIM].astype(BF16)

    def slc_score(g, kj):
        ks = pl.multiple_of(kj * tq, tq)
        expand = jnp.where((ks + ecol) // SLC_BLOCK == erow, 1.0, 0.0).astype(BF16)
        return scores(slc_ref, g, ks, jnp.dot(blockbias[g], expand, preferred_element_type=F32))

    def win_score(g, kj):
        ks = pl.multiple_of(kj * tq, tq)
        d = dloc + (q0 - ks)
        return scores(win_ref, g, ks, jnp.where((d >= 0) & (d < WINDOW), 0.0, NEG))

    causal = jnp.concatenate([dloc] * HG, axis=0) >= 0
    o_slc = _pipelined_attention(KV_G, 0, qi, slc_score, functools.partial(values, slc_ref), rows, tq, HEAD_DIM,
                                 last_fix=lambda g, s: jnp.where(causal, s, NEG))
    o_win = _pipelined_attention(KV_G, jnp.maximum(qi - wtiles, 0), qi, win_score,
                                 functools.partial(values, win_ref), rows, tq, HEAD_DIM)

    for g in range(KV_G):
        for h in range(HG):
            hh = g * HG + h
            r0 = h * tq
            o = (sg[:, hh:hh + 1] * o_cmp[g][r0:r0 + tq]
                 + sg[:, N_HEADS + hh:N_HEADS + hh + 1] * o_slc[g][r0:r0 + tq]
                 + sg[:, 2 * N_HEADS + hh:2 * N_HEADS + hh + 1] * o_win[g][r0:r0 + tq])
            o_ref[:, hh * HEAD_DIM:(hh + 1) * HEAD_DIM] = o


def _nsa_prompt(c, kvc, ov, b, t, tq):
    nq = t // tq
    kw = 2 * KV_G * HEAD_DIM
    nchunk = t // CMP_STRIDE
    nc = nchunk - CMP_LEN // CMP_STRIDE + 1
    ns = -(-t // SLC_BLOCK)
    return pl.pallas_call(
        functools.partial(_nsa_prompt_kernel, tq=tq, nc=nc, ns=ns), grid=(b, nq),
        in_specs=[pl.BlockSpec((tq, MIX), lambda i, q: (i * nq + q, _COL['nsa_q'] // MIX)),
                  pl.BlockSpec((t, kw), lambda i, q: (i, _COL['kv_slc'] // kw)),
                  pl.BlockSpec((t, kw), lambda i, q: (i, _COL['kv_win'] // kw)),
                  pl.BlockSpec((1, nchunk, kw), lambda i, q: (i, 0, 0)),
                  pl.BlockSpec((tq, LANE), lambda i, q: (i * nq + q, _COL['nsa_gate'] // LANE)),
                  pl.BlockSpec((LANE, LANE), lambda i, q: (0, 0))],
        out_specs=pl.BlockSpec((tq, MIX), lambda i, q: (i * nq + q, 0)),
        out_shape=jax.ShapeDtypeStruct((b * t, MIX), F32),
        compiler_params=_cparams(("parallel", "arbitrary")), name="nsa_prompt")(c, c, c, kvc, c, ov)


def _dec_rows():
    rid = lax.broadcasted_iota(jnp.int32, (N_HEADS * T_PAD, 1), 0)
    t8 = rid % T_PAD
    slope = jnp.concatenate([jnp.full((T_PAD, 1), SLOPES[h], F32) for h in range(N_HEADS)], axis=0)
    return t8, slope


def _expand_heads(q8):
    lane = lax.broadcasted_iota(jnp.int32, q8.shape, 1)
    return jnp.concatenate([jnp.where(lane // HEAD_DIM == h, q8, 0.0) for h in range(N_HEADS)], axis=0)


def _collapse_heads(res):
    lane = lax.broadcasted_iota(jnp.int32, (T_PAD, res.shape[1]), 1)
    out = jnp.zeros((T_PAD, res.shape[1]), F32)
    for h in range(N_HEADS):
        out = out + jnp.where(lane // HEAD_DIM == h, res[h * T_PAD:(h + 1) * T_PAD], 0.0)
    return out


def _moba_dec_kernel(pt_ref, q_ref, kn_ref, vn_ref, *rest, npages, past):
    page_refs, o_ref = rest[:npages], rest[npages]
    page = page_refs[0].shape[2]
    t8, slope = _dec_rows()
    qpos = past + t8
    lane = lax.broadcasted_iota(jnp.int32, (1, LANE), 1)
    q8 = q_ref[0]
    qx_f = _expand_heads(q8)
    qx = (qx_f * SCALE).astype(BF16)
    per_blk = MOBA_BLOCK // page
    nb_past = past // MOBA_BLOCK
    lanei = lax.broadcasted_iota(jnp.int32, (MIX, LANE), 1)
    kmean = jnp.zeros((MIX, LANE), F32)
    for n in range(nb_past):
        tot = page_refs[n * per_blk][0, 0:MIX, :]
        for p in range(n * per_blk + 1, (n + 1) * per_blk):
            tot = tot + page_refs[p][0, 0:MIX, :]
        col = jnp.sum(tot, axis=1, keepdims=True) * (1.0 / MOBA_BLOCK)
        kmean = kmean + jnp.where(lanei == n, col, 0.0)
    gate = jnp.dot(qx_f, kmean, precision=HI, preferred_element_type=F32)
    jl = lax.broadcasted_iota(jnp.int32, gate.shape, 1)
    cur = qpos // MOBA_BLOCK
    gate = jnp.where(jl < cur, gate, NEG)
    nb = -(-(past + T_PAD) // MOBA_BLOCK)
    sel = jnp.where(_topn_mask(gate, nb, max(1, min(MOBA_TOPK, nb - 1))) & (jl < cur), 1.0, 0.0)
    s_tiles, v_tiles = [], []
    for p in range(npages):
        kt = page_refs[p][0, 0:MIX, :].astype(BF16)
        v_tiles.append((page_refs[p][0, MIX:2 * MIX, :].astype(BF16), True))
        n = (p * page) // MOBA_BLOCK
        d = (qpos - (p * page + lane)).astype(F32)
        valid = (sel[:, n:n + 1] > 0.5) & (d >= 0)
        s_tiles.append(jnp.where(valid, jnp.dot(qx, kt, preferred_element_type=F32) - slope * d, NEG))
    k = _pad_rows(kn_ref[0], LANE).astype(BF16)
    v_tiles.append((_pad_rows(vn_ref[0], LANE).astype(BF16), False))
    d = (t8 - lane).astype(F32)
    valid = (d >= 0) & (lane < T_PAD)
    s_tiles.append(jnp.where(valid, _dot_nt(qx, k) - slope * d, NEG))
    o_ref[0] = _collapse_heads(_attend_tiles(s_tiles, v_tiles))


def _fox_dec_kernel(pt_ref, q_ref, kn_ref, vn_ref, ln_ref, *rest, npages, past):
    logf_refs, page_refs, o_ref = rest[:npages], rest[npages:2 * npages], rest[2 * npages]
    t8, _ = _dec_rows()
    lane = lax.broadcasted_iota(jnp.int32, (1, LANE), 1)
    qx = (_expand_heads(q_ref[0]) * SCALE).astype(BF16)
    r = lax.broadcasted_iota(jnp.int32, (LANE, LANE), 0)
    c = lax.broadcasted_iota(jnp.int32, (LANE, LANE), 1)
    tri = jnp.where(r <= c, 1.0, 0.0).astype(F32)

    def head_rows(x):
        return jnp.concatenate([jnp.broadcast_to(x[h:h + 1], (T_PAD, x.shape[1])) for h in range(N_HEADS)], axis=0)

    carry = jnp.zeros((N_HEADS, 1), F32)
    cum = []
    for ref in list(logf_refs) + [ln_ref]:
        cs = jnp.dot(ref[0], tri, precision=HI, preferred_element_type=F32) + carry
        cum.append(cs)
        carry = cs[:, LANE - 1:LANE]
    c_ref = head_rows(cum[npages - 1][:, LANE - 1:LANE])

    s_tiles, v_tiles = [], []
    for p in range(npages):
        kt = page_refs[p][0, 0:MIX, :].astype(BF16)
        v_tiles.append((page_refs[p][0, MIX:2 * MIX, :].astype(BF16), True))
        s_tiles.append(jnp.dot(qx, kt, preferred_element_type=F32) + (c_ref - head_rows(cum[p])))
    k = _pad_rows(kn_ref[0], LANE).astype(BF16)
    v_tiles.append((_pad_rows(vn_ref[0], LANE).astype(BF16), False))
    valid = (lane <= t8) & (lane < T_PAD)
    s_tiles.append(jnp.where(valid, _dot_nt(qx, k) + (c_ref - head_rows(cum[npages])), NEG))
    o_ref[0] = _collapse_heads(_attend_tiles(s_tiles, v_tiles))


def _page_specs(pool, npages, first):
    blk = (1,) + pool.shape[1:]
    return [pl.BlockSpec(blk, lambda b, pt, p=p: (first + pt[b, p], 0, 0)) for p in range(npages)]


def _paged_mha_dec(kernel, name, c3, col, page_table, pool, first, extra_args, extra_specs, past):
    bs, npages = page_table.shape
    qb = col // MIX
    in_specs = [pl.BlockSpec((1, T_PAD, MIX), lambda b, pt: (b, 0, qb)),
                pl.BlockSpec((1, T_PAD, MIX), lambda b, pt: (b, 0, qb + 1)),
                pl.BlockSpec((1, T_PAD, MIX), lambda b, pt: (b, 0, qb + 2))]
    in_specs += extra_specs
    in_specs += _page_specs(pool, npages, first)
    return pl.pallas_call(
        functools.partial(kernel, npages=npages, past=past),
        grid_spec=pltpu.PrefetchScalarGridSpec(
            num_scalar_prefetch=1, grid=(bs,), in_specs=in_specs,
            out_specs=pl.BlockSpec((1, T_PAD, MIX), lambda b, pt: (b, 0, 0))),
        out_shape=jax.ShapeDtypeStruct((bs, T_PAD, MIX), F32),
        compiler_params=_cparams(("parallel",)), name=name)(
            page_table, c3, c3, c3, *extra_args, *([pool] * npages))


def _nsa_dec_kernel(pt_ref, q_ref, ns_ref, nw_ref, g_ref, cmp_ref, ov_ref, wc_ref, *rest, npages, past, nc, ns):
    page_refs, o_ref = rest[:npages], rest[npages]
    page = page_refs[0].shape[2]
    kw = KV_G * HEAD_DIM
    t8, slope = _dec_rows()
    qpos = past + t8
    lane = lax.broadcasted_iota(jnp.int32, (1, LANE), 1)
    lane8 = lax.broadcasted_iota(jnp.int32, (T_PAD, LANE), 1)
    q8 = q_ref[0] * SCALE
    rows = []
    for h in range(N_HEADS):
        x = q8[:, (h // 2) * LANE:(h // 2 + 1) * LANE]
        dst = h // HG
        if h % 2 != dst:
            x = pltpu.roll(x, HEAD_DIM, 1)
        rows.append(jnp.where((lane8 // HEAD_DIM) == dst, x, 0.0))
    qx = jnp.concatenate(rows, axis=0).astype(BF16)

    cm = cmp_ref[0]
    dist = (qpos - (lane * CMP_STRIDE + CMP_LEN - 1)).astype(F32)
    ok_c = (dist >= 0) & (lane < nc)
    s = jnp.where(ok_c, _dot_nt(qx, cm[:, 0:kw].astype(BF16)) - slope * dist, NEG)
    m = jnp.max(s, axis=1, keepdims=True)
    e = jnp.where(ok_c, jnp.exp(s - m), 0.0)
    p_c = e / jnp.maximum(jnp.sum(e, axis=1, keepdims=True), 1e-30)
    o_cmp = jnp.dot(p_c.astype(BF16), cm[:, kw:2 * kw].astype(BF16), preferred_element_type=F32)

    psum = []
    for g in range(KV_G):
        acc = p_c[g * HG * T_PAD:g * HG * T_PAD + T_PAD]
        for h in range(1, HG):
            r0 = (g * HG + h) * T_PAD
            acc = acc + p_c[r0:r0 + T_PAD]
        psum.append(acc)
    imp = jnp.dot(jnp.concatenate(psum, axis=0), ov_ref[...], precision=HI, preferred_element_type=F32)
    jl = lax.broadcasted_iota(jnp.int32, imp.shape, 1)
    tg = lax.broadcasted_iota(jnp.int32, (KV_G * T_PAD, 1), 0) % T_PAD
    cur = (past + tg) // SLC_BLOCK
    forced = (jl == 0) | (jl == cur) | (jl == cur - 1)
    imp = jnp.where(forced, FORCE, imp)
    imp = jnp.where(jl <= cur, imp, NEG)
    sel = jnp.where(_topn_mask(imp, ns, min(SLC_TOPN, ns)) & (jl <= cur), 1.0, 0.0)
    sel_rows = jnp.concatenate([sel[(h // HG) * T_PAD:(h // HG + 1) * T_PAD] for h in range(N_HEADS)], axis=0)

    def new_tile(ref, extra_valid):
        k = _pad_rows(ref[0, :, 0:kw], LANE).astype(BF16)
        v = _pad_rows(ref[0, :, kw:2 * kw], LANE).astype(BF16)
        d = (t8 - lane).astype(F32)
        valid = (d >= 0) & (lane < T_PAD) & extra_valid
        return jnp.where(valid, _dot_nt(qx, k) - slope * d, NEG), (v, False)

    s_tiles, v_tiles = [], []
    per = page // SLC_BLOCK
    for p in range(npages):
        kt = page_refs[p][0, 0:kw, :].astype(BF16)
        v_tiles.append((page_refs[p][0, kw:2 * kw, :].astype(BF16), True))
        picked = jnp.zeros((N_HEADS * T_PAD, LANE), jnp.bool_)
        for j in range(per):
            blk = p * per + j
            picked = picked | ((lane // SLC_BLOCK == j) & (sel_rows[:, blk:blk + 1] > 0.5))
        d = (qpos - (p * page + lane)).astype(F32)
        s_tiles.append(jnp.where(picked & (d >= 0), jnp.dot(qx, kt, preferred_element_type=F32) - slope * d, NEG))
    blk_new = past // SLC_BLOCK
    s_new, v_new = new_tile(ns_ref, sel_rows[:, blk_new:blk_new + 1] > 0.5)
    o_slc = _attend_tiles(s_tiles + [s_new], v_tiles + [v_new])

    s_tiles, v_tiles = [], []
    wb = wc_ref.shape[2]
    w_off = past - wb
    for j in range(wb // LANE):
        kt = wc_ref[0, 0:kw, j * LANE:(j + 1) * LANE].astype(BF16)
        v_tiles.append((wc_ref[0, kw:2 * kw, j * LANE:(j + 1) * LANE].astype(BF16), True))
        d = qpos - (w_off + j * LANE + lane)
        valid = (d >= 0) & (d < WINDOW)
        s_tiles.append(jnp.where(valid, jnp.dot(qx, kt, preferred_element_type=F32) - slope * d.astype(F32), NEG))
    s_new, v_new = new_tile(nw_ref, True)
    o_win = _attend_tiles(s_tiles + [s_new], v_tiles + [v_new])

    sg = _sigmoid(g_ref[0])
    for h in range(N_HEADS):
        r0, l0 = h * T_PAD, (h // HG) * HEAD_DIM
        o = (sg[:, h:h + 1] * o_cmp[r0:r0 + T_PAD, l0:l0 + HEAD_DIM]
             + sg[:, N_HEADS + h:N_HEADS + h + 1] * o_slc[r0:r0 + T_PAD, l0:l0 + HEAD_DIM]
             + sg[:, 2 * N_HEADS + h:2 * N_HEADS + h + 1] * o_win[r0:r0 + T_PAD, l0:l0 + HEAD_DIM])
        o_ref[0, :, h * HEAD_DIM:(h + 1) * HEAD_DIM] = o


def _nsa_dec(c3, kvc, ov, win_cache, win_first, page_table, pool, first, past, t_real):
    bs, npages = page_table.shape
    kw = 2 * KV_G * HEAD_DIM
    nchunk = kvc.shape[1]
    nc = nchunk - CMP_LEN // CMP_STRIDE + 1
    ns = -(-(past + t_real) // SLC_BLOCK)
    wb = win_cache.shape[2]
    in_specs = [pl.BlockSpec((1, T_PAD, MIX), lambda b, pt: (b, 0, _COL['nsa_q'] // MIX)),
                pl.BlockSpec((1, T_PAD, kw), lambda b, pt: (b, 0, _COL['kv_slc'] // kw)),
                pl.BlockSpec((1, T_PAD, kw), lambda b, pt: (b, 0, _COL['kv_win'] // kw)),
                pl.BlockSpec((1, T_PAD, LANE), lambda b, pt: (b, 0, _COL['nsa_gate'] // LANE)),
                pl.BlockSpec((1, nchunk, kw), lambda b, pt: (b, 0, 0)),
                pl.BlockSpec((LANE, LANE), lambda b, pt: (0, 0)),
                pl.BlockSpec((1, kw, wb), lambda b, pt: (win_first + b, 0, 0))]
    in_specs += _page_specs(pool, npages, first)
    return pl.pallas_call(
        functools.partial(_nsa_dec_kernel, npages=npages, past=past, nc=nc, ns=ns),
        grid_spec=pltpu.PrefetchScalarGridSpec(
            num_scalar_prefetch=1, grid=(bs,), in_specs=in_specs,
            out_specs=pl.BlockSpec((1, T_PAD, MIX), lambda b, pt: (b, 0, 0))),
        out_shape=jax.ShapeDtypeStruct((bs, T_PAD, MIX), F32),
        compiler_params=_cparams(("parallel",)), name="nsa_decode")(
            page_table, c3, c3, c3, c3, kvc, ov, win_cache, *([pool] * npages))


def _s5_disc_kernel(ar_ref, ai_ref, ls_ref, btr_ref, bti_ref, abr_ref, abi_ref, bbr_ref, bbi_ref):
    ar, ai = ar_ref[...], ai_ref[...]
    step = jnp.exp(ls_ref[...])
    mag = jnp.exp(ar * step)
    abr = mag * jnp.cos(ai * step)
    abi = mag * jnp.sin(ai * step)
    den = ar * ar + ai * ai
    zr = (ar * (abr - 1.0) + ai * abi) / den
    zi = (ar * abi - ai * (abr - 1.0)) / den
    abr_ref[...] = abr
    abi_ref[...] = abi
    btr, bti = btr_ref[...], bti_ref[...]
    bbr_ref[...] = zr * btr - zi * bti
    bbi_ref[...] = zr * bti + zi * btr


def _s5_disc(a_re, a_im, log_step, b_re, b_im):
    rep = lambda a: jnp.repeat(a, S5_CH, axis=0)
    n = S5_GROUPS * S5_CH
    args = (rep(a_re), rep(a_im), rep(jnp.broadcast_to(log_step[:, None], (S5_GROUPS, S5_STATE))),
            b_re.transpose(0, 2, 1).reshape(n, S5_STATE), b_im.transpose(0, 2, 1).reshape(n, S5_STATE))
    shp = jax.ShapeDtypeStruct((n, S5_STATE), F32)
    return pl.pallas_call(_s5_disc_kernel, out_shape=(shp, shp, shp, shp), name="s5_discretise")(*args)


_S5_CH_ROWS = 128


def _s5_scan_kernel(*refs, seg, has_h0):
    if has_h0:
        (u_ref, bre_ref, bim_ref, ar_ref, ai_ref, cre_ref, cim_ref, d_ref, h0r_ref, h0i_ref,
         y_ref, hlr_ref, hli_ref, hr_s, hi_s) = refs
    else:
        (u_ref, bre_ref, bim_ref, ar_ref, ai_ref, cre_ref, cim_ref, d_ref,
         y_ref, hlr_ref, hli_ref, hr_s, hi_s) = refs
    rows = u_ref.shape[0]
    ch = min(_S5_CH_ROWS, rows)
    pad = ch
    nch = rows // ch
    ar, ai = ar_ref[0], ai_ref[0]
    bre, bim, cre, cim = (_split_bf16(r[0]) for r in (bre_ref, bim_ref, cre_ref, cim_ref))
    hr_s[0:pad, :] = jnp.zeros((pad, hr_s.shape[1]), F32)
    hi_s[0:pad, :] = jnp.zeros((pad, hi_s.shape[1]), F32)

    def init_body(i, _):
        r0 = pl.multiple_of(i * ch, ch)
        u = u_ref[pl.ds(r0, ch), :]
        br = _dot3(u, bre)
        bi = _dot3(u, bim)
        if has_h0:
            h0r, h0i = h0r_ref[pl.ds(r0, ch), :], h0i_ref[pl.ds(r0, ch), :]
            br = br + (ar * h0r - ai * h0i)
            bi = bi + (ar * h0i + ai * h0r)
        hr_s[pl.ds(pad + r0, ch), :] = br
        hi_s[pl.ds(pad + r0, ch), :] = bi
        return 0

    lax.fori_loop(0, nch, init_body, 0)

    rl = lax.broadcasted_iota(jnp.int32, (ch, 1), 0)
    pr, pi = ar, ai
    d = 1
    two_level = seg == rows and rows > ch
    span = ch if two_level else seg
    while d < span:
        first = d // ch

        def pass_body(i, _, d=d, pr=pr, pi=pi):
            r0 = pl.multiple_of((nch - 1 - i) * ch, ch)
            cr = hr_s[pl.ds(pad + r0, ch), :]
            ci = hi_s[pl.ds(pad + r0, ch), :]
            if d < SUBLANE:
                lo = pad - SUBLANE
                sr = pltpu.roll(hr_s[pl.ds(lo + r0, ch + SUBLANE), :], d, 0)[SUBLANE:]
                si = pltpu.roll(hi_s[pl.ds(lo + r0, ch + SUBLANE), :], d, 0)[SUBLANE:]
            else:
                sr = hr_s[pl.ds(pad + r0 - d, ch), :]
                si = hi_s[pl.ds(pad + r0 - d, ch), :]
            if d < ch or seg < rows:
                keep = ((r0 + rl) % span) >= d
                sr = jnp.where(keep, sr, 0.0)
                si = jnp.where(keep, si, 0.0)
            hr_s[pl.ds(pad + r0, ch), :] = cr + (pr * sr - pi * si)
            hi_s[pl.ds(pad + r0, ch), :] = ci + (pr * si + pi * sr)
            return 0

        lax.fori_loop(0, nch - first, pass_body, 0)
        pr, pi = pr * pr - pi * pi, 2.0 * pr * pi
        d *= 2

    if two_level:
        tr, ti = jnp.broadcast_to(ar, (ch, ar.shape[1])), jnp.broadcast_to(ai, (ch, ai.shape[1]))
        d = 1
        while d < ch:
            sr, si = pltpu.roll(tr, d, 0), pltpu.roll(ti, d, 0)
            keep = rl >= d
            tr, ti = jnp.where(keep, tr * sr - ti * si, tr), jnp.where(keep, tr * si + ti * sr, ti)
            d *= 2
        lasts = [(hr_s[pad + (c + 1) * ch - 1:pad + (c + 1) * ch, :], hi_s[pad + (c + 1) * ch - 1:pad + (c + 1) * ch, :])
                 for c in range(nch - 1)]
        kr, ki = lasts[0]
        for c in range(1, nch):
            if c > 1:
                lr, li = lasts[c - 1]
                kr, ki = lr + (pr * kr - pi * ki), li + (pr * ki + pi * kr)
            rows_c = slice(pad + c * ch, pad + (c + 1) * ch)
            hr_s[rows_c, :] = hr_s[rows_c, :] + (tr * kr - ti * ki)
            hi_s[rows_c, :] = hi_s[rows_c, :] + (tr * ki + ti * kr)

    def out_body(i, _):
        r0 = pl.multiple_of(i * ch, ch)
        hr = hr_s[pl.ds(pad + r0, ch), :]
        hi = hi_s[pl.ds(pad + r0, ch), :]
        y = (jnp.dot(hr.astype(BF16), cre[0], preferred_element_type=F32)
             - jnp.dot(hi.astype(BF16), cim[0], preferred_element_type=F32))
        y_ref[pl.ds(r0, ch), :] = y + d_ref[0] * u_ref[pl.ds(r0, ch), :]
        return 0

    lax.fori_loop(0, nch, out_body, 0)
    nl = hlr_ref.shape[0]
    hlr_ref[...] = hr_s[pad + rows - nl:pad + rows, :]
    hli_ref[...] = hi_s[pad + rows - nl:pad + rows, :]


def _s5_scan(c, sw, rows, seg, h0=None):
    n = c.shape[0]
    nt = n // rows
    gl = LANE // S5_CH
    lt = S5_GROUPS // gl
    w = gl * S5_STATE
    nl = (rows // seg) * SUBLANE if seg == SUBLANE else SUBLANE
    ub = _COL['s5_u'] // LANE
    in_specs = [pl.BlockSpec((rows, LANE), lambda i, j: (i, ub + j)),
                pl.BlockSpec((1, LANE, w), lambda i, j: (j, 0, 0)),
                pl.BlockSpec((1, LANE, w), lambda i, j: (j, 0, 0)),
                pl.BlockSpec((1, 1, w), lambda i, j: (j, 0, 0)),
                pl.BlockSpec((1, 1, w), lambda i, j: (j, 0, 0)),
                pl.BlockSpec((1, w, LANE), lambda i, j: (j, 0, 0)),
                pl.BlockSpec((1, w, LANE), lambda i, j: (j, 0, 0)),
                pl.BlockSpec((1, 1, LANE), lambda i, j: (j, 0, 0))]
    args = [c, sw['bre'], sw['bim'], sw['ar'], sw['ai'], sw['cre'], sw['cim'], sw['d']]
    if h0 is not None:
        in_specs += [pl.BlockSpec((rows, w), lambda i, j: (i, j))] * 2
        args += list(h0)
    hshape = jax.ShapeDtypeStruct((nt * nl, S5_GROUPS * S5_STATE), F32)
    return pl.pallas_call(
        functools.partial(_s5_scan_kernel, seg=seg, has_h0=h0 is not None), grid=(nt, lt),
        in_specs=in_specs,
        out_specs=(pl.BlockSpec((rows, LANE), lambda i, j: (i, j)),
                   pl.BlockSpec((nl, w), lambda i, j: (i, j)), pl.BlockSpec((nl, w), lambda i, j: (i, j))),
        out_shape=(jax.ShapeDtypeStruct((n, MIX), F32), hshape, hshape),
        scratch_shapes=[pltpu.VMEM((min(_S5_CH_ROWS, rows) + rows, w), F32)] * 2,
        compiler_params=_cparams(("parallel", "arbitrary")), name="s5_scan")(*args)


def _s5_weights(lp):
    abr, abi, bbr, bbi = _s5_disc(lp['s5_a_re'], lp['s5_a_im'], lp['s5_log_step'], lp['s5_b_re'], lp['s5_b_im'])
    gl = LANE // S5_CH
    lt = S5_GROUPS // gl
    eye = jnp.eye(gl, dtype=F32)

    def bdiag(bb):
        return jnp.einsum('jgcn,gh->jgchn', bb.reshape(lt, gl, S5_CH, S5_STATE), eye).reshape(
            lt, gl * S5_CH, gl * S5_STATE)

    def cdiag(cc):
        return jnp.einsum('jgcn,gh->jgnhc', cc.reshape(lt, gl, S5_CH, S5_STATE), eye).reshape(
            lt, gl * S5_STATE, gl * S5_CH)

    return dict(bre=bdiag(bbr), bim=bdiag(bbi),
                ar=abr[::S5_CH].reshape(lt, 1, gl * S5_STATE), ai=abi[::S5_CH].reshape(lt, 1, gl * S5_STATE),
                cre=cdiag(lp['s5_c_re']), cim=cdiag(lp['s5_c_im']), d=lp['s5_d'].reshape(lt, 1, LANE))


def _glu_kernel(y_ref, w_ref, o_ref):
    y = y_ref[...]
    g = 0.5 * y * (1.0 + jnp.tanh(np.float32(np.sqrt(2.0 / np.pi)) * (y + np.float32(0.044715) * (y * y * y))))
    z = jnp.dot(g.astype(BF16), w_ref[...], preferred_element_type=F32)
    o_ref[...] = z[:, :MIX] * _sigmoid(z[:, MIX:])


def _glu(y, w_l, tm):
    w, l = w_l
    n = y.shape[0]
    return pl.pallas_call(
        _glu_kernel, grid=(n // tm,),
        in_specs=[pl.BlockSpec((tm, MIX), lambda i: (i, 0)),
                  pl.BlockSpec((None, MIX, 2 * MIX), lambda i: (l, 0, 0))],
        out_specs=pl.BlockSpec((tm, MIX), lambda i: (i, 0)),
        out_shape=jax.ShapeDtypeStruct((n, MIX), F32),
        compiler_params=_cparams(("parallel",)), name="s5_glu")(y, w)


def _layer_norm(x, g, b):
    mu = jnp.mean(x, axis=-1, keepdims=True)
    xc = x - mu
    var = jnp.mean(xc * xc, axis=-1, keepdims=True)
    return xc * lax.rsqrt(var + LN_EPS) * g + b


def _merge_kernel(o0_ref, o1_ref, o2_ref, o3_ref, mg_ref, wb_ref, x_ref, wo_ref, g_ref, b_ref, out_ref, acc_ref,
                  *, alpha):
    i = pl.program_id(1)

    @pl.when(i == 0)
    def _():
        acc_ref[...] = jnp.zeros_like(acc_ref)

    for k, o_ref in enumerate((o0_ref, o1_ref, o2_ref, o3_ref)):
        @pl.when(i == k)
        def _(o_ref=o_ref):
            proj = jnp.dot(o_ref[...].astype(BF16), wb_ref[0], preferred_element_type=F32)
            acc_ref[...] += _sigmoid(mg_ref[...]) * proj

    @pl.when(i == N_BRANCH - 1)
    def _():
        mixed = jnp.dot(acc_ref[...].astype(BF16), wo_ref[...], preferred_element_type=F32)
        out_ref[...] = _layer_norm(alpha * x_ref[...] + mixed, g_ref[...], b_ref[...])


def _merge(outs, c, x, wb_l, wo_l, g, b, tm, alpha):
    (wb, l), wo = wb_l, wo_l[0]
    n = x.shape[0]
    o_spec = pl.BlockSpec((tm, MIX), lambda r, i: (r, 0))
    return pl.pallas_call(
        functools.partial(_merge_kernel, alpha=alpha), grid=(n // tm, N_BRANCH),
        in_specs=[o_spec, o_spec, o_spec, o_spec,
                  pl.BlockSpec((tm, D_MODEL), lambda r, i: (r, i)),
                  pl.BlockSpec((None, 1, MIX, D_MODEL), lambda r, i: (l, i, 0, 0)),
                  pl.BlockSpec((tm, D_MODEL), lambda r, i: (r, 0)),
                  pl.BlockSpec((None, D_MODEL, D_MODEL), lambda r, i: (l, 0, 0), pipeline_mode=pl.Buffered(1)),
                  pl.BlockSpec((1, D_MODEL), lambda r, i: (0, 0)),
                  pl.BlockSpec((1, D_MODEL), lambda r, i: (0, 0))],
        out_specs=pl.BlockSpec((tm, D_MODEL), lambda r, i: (r, 0)),
        out_shape=jax.ShapeDtypeStruct((n, D_MODEL), F32),
        scratch_shapes=[pltpu.VMEM((tm, D_MODEL), F32)],
        compiler_params=_cparams(("parallel", "arbitrary")), name="merge_out_ln")(
            *outs, c, wb, x, wo, g, b)


def _ffn_kernel(*refs, halo, seg, alpha):
    (h_ref, wg_ref, wv_ref, cwg_ref, cwv_ref, cbg_ref, cbv_ref, wd_ref, lg_ref, lb_ref) = refs[:10]
    pg_ref, pv_ref, out_ref, hb_ref, acc_ref = refs[10:]
    j = pl.program_id(1)

    @pl.when(j == 0)
    def _():
        hb_ref[...] = h_ref[...].astype(BF16)
        acc_ref[...] = jnp.zeros_like(acc_ref)

    hb = hb_ref[...]
    tm = hb.shape[0]
    tf = wg_ref.shape[1]
    rid = lax.broadcasted_iota(jnp.int32, (tm, 1), 0)
    rid8 = lax.broadcasted_iota(jnp.int32, (SUBLANE, 1), 0)

    def conv(u, cw, cb, prev):
        r1 = pltpu.roll(u, 1, 0)
        r2 = pltpu.roll(u, 2, 0)
        if halo:
            p6, p7 = prev[SUBLANE - 2:SUBLANE - 1], prev[SUBLANE - 1:SUBLANE]
            top1 = jnp.where(rid8 == 0, p7, r1[:SUBLANE])
            top2 = jnp.where(rid8 == 0, p6, jnp.where(rid8 == 1, p7, r2[:SUBLANE]))
            u1 = jnp.concatenate([top1, r1[SUBLANE:]], axis=0)
            u2 = jnp.concatenate([top2, r2[SUBLANE:]], axis=0)
        else:
            t = rid % seg
            state = jnp.where(t >= seg - (CONV_W - 1), prev, u)
            u1 = jnp.where(t >= 1, r1, pltpu.roll(state, tm - (seg - 1), 0))
            u2 = jnp.where(t >= 2, r2, pltpu.roll(state, tm - (seg - 2), 0))
        return cb + (cw[0:1] * u2 + cw[1:2] * u1 + cw[2:3] * u)

    ug = jnp.dot(hb, wg_ref[...], preferred_element_type=F32)
    uv = jnp.dot(hb, wv_ref[...], preferred_element_type=F32)
    gate = conv(ug, cwg_ref[...], cbg_ref[...], pg_ref[0] if halo else pg_ref[...])
    val = conv(uv, cwv_ref[...], cbv_ref[...], pv_ref[0] if halo else pv_ref[...])
    act = (gate * _sigmoid(gate) * val).astype(BF16)
    acc_ref[...] += jnp.dot(act, wd_ref[...], preferred_element_type=F32)

    @pl.when(j == pl.num_programs(1) - 1)
    def _():
        out_ref[...] = _layer_norm(alpha * h_ref[...] + acc_ref[...], lg_ref[...], lb_ref[...])


def _ffn(h, lw, tm, tf, alpha, prev=None, state_rows=None, seg=None):
    n = h.shape[0]
    nf = D_FF // tf
    halo = prev is not None
    (w_up, l), w_down = lw['w_up'], lw['w_down'][0]
    in_specs = [pl.BlockSpec((tm, D_MODEL), lambda r, j: (r, 0), pipeline_mode=pl.Buffered(1)),
                pl.BlockSpec((None, D_MODEL, tf), lambda r, j: (l, 0, j)),
                pl.BlockSpec((None, D_MODEL, tf), lambda r, j: (l, 0, nf + j)),
                pl.BlockSpec((CONV_W, tf), lambda r, j: (0, j)),
                pl.BlockSpec((CONV_W, tf), lambda r, j: (0, nf + j)),
                pl.BlockSpec((1, tf), lambda r, j: (0, j)),
                pl.BlockSpec((1, tf), lambda r, j: (0, nf + j)),
                pl.BlockSpec((None, tf, D_MODEL), lambda r, j: (l, j, 0)),
                pl.BlockSpec((1, D_MODEL), lambda r, j: (0, 0)),
                pl.BlockSpec((1, D_MODEL), lambda r, j: (0, 0))]
    args = [h, w_up, w_up, lw['conv_w'], lw['conv_w'], lw['conv_b'], lw['conv_b'], w_down,
            lw['ln2_g'], lw['ln2_b']]
    if halo:
        in_specs += [pl.BlockSpec((1, SUBLANE, tf), lambda r, j: (r, 0, j)),
                     pl.BlockSpec((1, SUBLANE, tf), lambda r, j: (r, 0, nf + j))]
        args += [prev, prev]
    else:
        in_specs += [pl.BlockSpec((tm, tf), lambda r, j: (r, j)), pl.BlockSpec((tm, tf), lambda r, j: (r, nf + j))]
        args += [state_rows, state_rows]
    return pl.pallas_call(
        functools.partial(_ffn_kernel, halo=halo, seg=seg, alpha=alpha), grid=(n // tm, nf),
        in_specs=in_specs,
        out_specs=pl.BlockSpec((tm, D_MODEL), lambda r, j: (r, 0), pipeline_mode=pl.Buffered(1)),
        out_shape=jax.ShapeDtypeStruct((n, D_MODEL), F32),
        scratch_shapes=[pltpu.VMEM((tm, D_MODEL), BF16), pltpu.VMEM((tm, D_MODEL), F32)],
        compiler_params=_cparams(("parallel", "arbitrary")), name="conv_ffn_ln")(*args)


def _prep_shared(p):
    return dict(w_in=_pack_w_in(p['w_in']), w_glu=p['s5_w_glu'].astype(BF16), w_branch=p['w_branch'].astype(BF16),
                w_out=p['w_out'].astype(BF16), w_up=p['ffn_w_up'].astype(BF16), w_down=p['ffn_w_down'].astype(BF16))


def _prep_layer(l, p, shared):
    lp = {k: v[l] for k, v in p.items() if k not in ('w_in', 's5_w_glu', 'w_branch', 'w_out', 'ffn_w_up', 'ffn_w_down')}
    lw = {k: (v, l) for k, v in shared.items()}
    lw.update(
        fox_b=jnp.zeros((1, LANE), F32).at[0, :N_HEADS].set(lp['fox_b_f']),
        cmp=_cmp_weights(lp['nsa_cmp_pos'], lp['nsa_cmp_wk'], lp['nsa_cmp_wv']),
        s5=_s5_weights(lp),
        ln1_g=lp['ln1_g'].reshape(1, -1), ln1_b=lp['ln1_b'].reshape(1, -1),
        conv_w=lp['ffn_conv_w'], conv_b=lp['ffn_conv_b'].reshape(1, -1),
        ln2_g=lp['ln2_g'].reshape(1, -1), ln2_b=lp['ln2_b'].reshape(1, -1))
    return lw


def _prompt_layer(x, b, t, lw, alpha):
    n = b * t
    kw = 2 * KV_G * HEAD_DIM
    tr = min(1024, n)
    c = _matmul(x, lw['w_in'], tr, 1536, w_rows_are_outputs=True)
    logf = _logf(c, lw['fox_b'], tr)
    cumr = _cumsum_prompt(logf.reshape(b, t, N_HEADS).transpose(0, 2, 1))
    o_fox = _fox_prompt(c, cumr[:, :, None, :], b, t, 512)
    o_moba = _moba_prompt(c, b, t, 2 * MOBA_BLOCK)
    kvc = _cmp_prompt(c, b, t, lw['cmp'])
    nchunk = t // CMP_STRIDE
    ov = _overlap_matrix(nchunk - CMP_LEN // CMP_STRIDE + 1, -(-t // SLC_BLOCK))
    o_nsa = _nsa_prompt(c, kvc, ov, b, t, 256)
    y_s5, hlr, hli = _s5_scan(c, lw['s5'], t, t)
    o_s5 = _glu(y_s5, lw['w_glu'], tr)
    h = _merge((o_nsa, o_s5, o_moba, o_fox), c, x, lw['w_branch'], lw['w_out'], lw['ln1_g'], lw['ln1_b'], 512, alpha)

    tm = min(1024, t)
    nt = n // tm
    edge = h.reshape(nt, tm, D_MODEL)[:, tm - (CONV_W - 1):].reshape(nt * (CONV_W - 1), D_MODEL)
    edge = _pad_rows(edge, -(-edge.shape[0] // SUBLANE) * SUBLANE)
    u_edge = _matmul(edge, lw['w_up'], edge.shape[0], 2 * D_FF // 8)[:nt * (CONV_W - 1)].reshape(nt, CONV_W - 1, 2 * D_FF)
    per_seq = t // tm
    conv_state = u_edge[per_seq - 1::per_seq]
    starts_seq = (jnp.arange(nt) % per_seq == 0)[:, None, None]
    prev = jnp.where(starts_seq, 0.0, jnp.roll(u_edge, 1, axis=0))
    prev = jnp.pad(prev, ((0, 0), (SUBLANE - (CONV_W - 1), 0), (0, 0)))
    y = _ffn(h, lw, tm, 512, alpha, prev=prev)

    win_rows = min(WINDOW, t)
    states = (
        _kv_state(c, b, t, _COL['kv_cmp'], kw),
        _kv_state(c, b, t, _COL['kv_slc'], kw),
        _kv_state(c, b, t, _COL['moba'] + MIX, 2 * MIX),
        _kv_state(c, b, t, _COL['fox'] + MIX, 2 * MIX),
        logf.reshape(b, t, N_HEADS),
        _kv_state(c, b, t, _COL['kv_win'], kw, t_from=t - win_rows),
        hlr.reshape(b, SUBLANE, S5_GROUPS, S5_STATE)[:, SUBLANE - 1],
        hli.reshape(b, SUBLANE, S5_GROUPS, S5_STATE)[:, SUBLANE - 1],
        conv_state)
    return y, states


def _sample_layer(x, bs, t_real, lw, alpha, past, page_table, past_len):
    n = bs * T_PAD
    kw = 2 * KV_G * HEAD_DIM
    c = _matmul(x, lw['w_in'], n, 1536, w_rows_are_outputs=True)
    c3 = c.reshape(bs, T_PAD, WP)
    logf = _logf(c, lw['fox_b'], n)
    logf3 = logf.reshape(bs, T_PAD, N_HEADS)
    tmask = (jnp.arange(T_PAD) < t_real)[None, :, None]
    new_t = jnp.pad(jnp.where(tmask, logf3, 0.0).transpose(0, 2, 1), ((0, 0), (0, 0), (0, LANE - T_PAD)))
    first = past['first_page']
    npages = page_table.shape[1]
    o_fox = _paged_mha_dec(
        _fox_dec_kernel, "fox_decode", c3, _COL['fox'], page_table, past['fox'], first,
        [new_t] + [past['fox_logf']] * npages,
        [pl.BlockSpec((1, N_HEADS, LANE), lambda b, pt: (b, 0, 0))] + _page_specs(past['fox_logf'], npages, first),
        past_len)
    o_moba = _paged_mha_dec(_moba_dec_kernel, "moba_decode", c3, _COL['moba'], page_table, past['moba'], first,
                            [], [], past_len)
    kvc = _cmp_paged(page_table, past['nsa_cmp'], first, lw['cmp'])
    nchunk = kvc.shape[1]
    ov = _overlap_matrix(nchunk - CMP_LEN // CMP_STRIDE + 1, -(-(past_len + t_real) // SLC_BLOCK))
    o_nsa = _nsa_dec(c3, kvc, ov, past['nsa_win'], past['first_seq'], page_table, past['nsa_slc'], first,
                     past_len, t_real)
    h0 = [jnp.pad(s.reshape(bs, 1, -1), ((0, 0), (0, T_PAD - 1), (0, 0))).reshape(n, -1) for s in past['s5']]
    y_s5, hlr, hli = _s5_scan(c, lw['s5'], n, T_PAD, h0=h0)
    o_s5 = _glu(y_s5, lw['w_glu'], n)
    h = _merge((o_nsa.reshape(n, MIX), o_s5, o_moba.reshape(n, MIX), o_fox.reshape(n, MIX)), c, x,
               lw['w_branch'], lw['w_out'], lw['ln1_g'], lw['ln1_b'], min(256, n), alpha)

    buf = past['ffn_conv']
    state_rows = jnp.pad(buf, ((0, 0), (T_PAD - (CONV_W - 1), 0), (0, 0))).reshape(n, 2 * D_FF)
    y = _ffn(h, lw, min(512, n), 512, alpha, state_rows=state_rows, seg=T_PAD)
    last2 = h.reshape(bs, T_PAD, D_MODEL)[:, t_real - (CONV_W - 1):t_real].reshape(bs * (CONV_W - 1), D_MODEL)
    conv_state = _matmul(last2, lw['w_up'], last2.shape[0], 2 * D_FF // 8).reshape(bs, CONV_W - 1, 2 * D_FF)

    tr = lambda a: a[:, :t_real]
    kv_win_new = tr(c3[:, :, _COL['kv_win']:_COL['kv_win'] + kw])
    states = (
        tr(c3[:, :, _COL['kv_cmp']:_COL['kv_cmp'] + kw]).reshape(bs, t_real, 2, KV_G, HEAD_DIM),
        tr(c3[:, :, _COL['kv_slc']:_COL['kv_slc'] + kw]).reshape(bs, t_real, 2, KV_G, HEAD_DIM),
        tr(c3[:, :, _COL['moba'] + MIX:_COL['moba'] + 3 * MIX]).reshape(bs, t_real, 2, N_HEADS, HEAD_DIM),
        tr(c3[:, :, _COL['fox'] + MIX:_COL['fox'] + 3 * MIX]).reshape(bs, t_real, 2, N_HEADS, HEAD_DIM),
        tr(logf3),
        kv_win_new,
        hlr.reshape(bs, T_PAD, S5_GROUPS, S5_STATE)[:, t_real - 1],
        hli.reshape(bs, T_PAD, S5_GROUPS, S5_STATE)[:, t_real - 1],
        conv_state)
    return y, states


def kernel(x_prompt, x_sample, cache_nsa_cmp_kv, cache_nsa_slc_kv, cache_moba_kv, cache_fox_kv, cache_fox_logf,
           page_table, cache_nsa_win_kv, state_s5_re, state_s5_im, state_ffn_conv, w_in, fox_b_f, nsa_cmp_pos,
           nsa_cmp_wk, nsa_cmp_wv, s5_a_re, s5_a_im, s5_b_re, s5_b_im, s5_c_re, s5_c_im, s5_d, s5_log_step,
           s5_w_glu, w_branch, w_out, ln1_g, ln1_b, ffn_w_up, ffn_conv_w, ffn_conv_b, ffn_w_down, ln2_g, ln2_b):
    depth = w_in.shape[0]
    b, t, d = x_prompt.shape
    bs, ts, _ = x_sample.shape
    n_phys, page = cache_nsa_cmp_kv.shape[1:3]
    past_len = page_table.shape[1] * page
    assert d == D_MODEL and w_in.shape[2] == IN_WIDTH and ffn_w_down.shape[1] == D_FF
    assert ts <= T_PAD - (CONV_W - 1) and past_len % MOBA_BLOCK == 0 and page == LANE and (past_len + ts) // CMP_STRIDE * CMP_STRIDE <= past_len
    alpha = float((2 * depth) ** 0.25)
    params = dict(w_in=w_in, fox_b_f=fox_b_f, nsa_cmp_pos=nsa_cmp_pos, nsa_cmp_wk=nsa_cmp_wk, nsa_cmp_wv=nsa_cmp_wv,
                  s5_a_re=s5_a_re, s5_a_im=s5_a_im, s5_b_re=s5_b_re, s5_b_im=s5_b_im, s5_c_re=s5_c_re,
                  s5_c_im=s5_c_im, s5_d=s5_d, s5_log_step=s5_log_step, s5_w_glu=s5_w_glu, w_branch=w_branch,
                  w_out=w_out, ln1_g=ln1_g, ln1_b=ln1_b, ffn_w_up=ffn_w_up, ffn_conv_w=ffn_conv_w,
                  ffn_conv_b=ffn_conv_b, ffn_w_down=ffn_w_down, ln2_g=ln2_g, ln2_b=ln2_b)
    kw = 2 * KV_G * HEAD_DIM
    yp = x_prompt.reshape(b * t, d)
    ys = jnp.pad(x_sample, ((0, 0), (0, T_PAD - ts), (0, 0))).reshape(bs * T_PAD, d)
    st_p, st_s = [], []

    def feature_major(cache):
        dd, nn, rr = cache.shape[:3]
        return cache.transpose(0, 1, 3, 4, 5, 2).reshape(dd * nn, -1, rr)

    cmp_fm, slc_fm, moba_fm, fox_fm, win_fm = (
        feature_major(a) for a in (cache_nsa_cmp_kv, cache_nsa_slc_kv, cache_moba_kv, cache_fox_kv, cache_nsa_win_kv))
    logf_hm = cache_fox_logf.transpose(0, 1, 3, 2).reshape(depth * n_phys, N_HEADS, page)
    shared = _prep_shared(params)
    for l in range(depth):
        lw = _prep_layer(l, params, shared)
        past = dict(
            nsa_cmp=cmp_fm, nsa_slc=slc_fm, moba=moba_fm, fox=fox_fm, fox_logf=logf_hm, nsa_win=win_fm,
            first_page=l * n_phys, first_seq=l * bs,
            s5=(state_s5_re[l], state_s5_im[l]),
            ffn_conv=state_ffn_conv[l])
        yp, sp = _prompt_layer(yp, b, t, lw, alpha)
        ys, ss = _sample_layer(ys, bs, ts, lw, alpha, past, page_table, past_len)
        st_p.append(sp)
        st_s.append(ss)
    sp = [jnp.stack(z) for z in zip(*st_p)]
    ss = [jnp.stack(z) for z in zip(*st_s)]
    wb = win_fm.shape[2]
    full_win = jnp.concatenate([win_fm.reshape(depth, bs, kw, wb), ss[5].transpose(0, 1, 3, 2)], axis=3)
    keep = min(WINDOW, wb + ts)
    ss[5] = full_win[..., wb + ts - keep:].transpose(0, 1, 3, 2).reshape(depth, bs, keep, 2, KV_G, HEAD_DIM)
    out = [yp.reshape(b, t, d), ys.reshape(bs, T_PAD, d)[:, :ts]]
    for a, c in zip(sp, ss):
        out += [a, c]
    return tuple(out)
```

```python
import functools

import numpy as np
import jax
import jax.numpy as jnp
from jax import lax
from jax.experimental import pallas as pl
from jax.experimental.pallas import tpu as pltpu

F32 = jnp.float32
BF16 = jnp.bfloat16
HI = lax.Precision.HIGHEST

LANE = 128
SUBLANE = 8
VMEM_LIMIT = 56 * 1024 * 1024

D_MODEL = 2048
HEAD_DIM = 64
N_BRANCH = 4
MIX = D_MODEL // N_BRANCH
N_HEADS = MIX // HEAD_DIM
KV_G = 2
HG = N_HEADS // KV_G
CMP_LEN = 32
CMP_STRIDE = 16
SLC_BLOCK = 64
SLC_TOPN = 16
WINDOW = 512
MOBA_BLOCK = 256
MOBA_TOPK = 3
S5_CH = 16
S5_GROUPS = MIX // S5_CH
S5_STATE = 64
D_FF = 5632
CONV_W = 3
LN_EPS = 1e-5
SCALE = HEAD_DIM ** -0.5
NEG = -1e30
FORCE = 1e4
T_PAD = 8
SLOPES = tuple(float(v) for v in np.asarray(2.0 ** (-8.0 * np.arange(1, N_HEADS + 1) / N_HEADS), np.float32))

_SPLITS = (('nsa_q', MIX), ('kv_cmp', 2 * KV_G * HEAD_DIM), ('kv_slc', 2 * KV_G * HEAD_DIM),
           ('kv_win', 2 * KV_G * HEAD_DIM), ('nsa_gate', 3 * N_HEADS), ('s5_u', MIX),
           ('moba', 3 * MIX), ('fox', 3 * MIX), ('fox_f', N_HEADS), ('merge', N_BRANCH * D_MODEL))
_SRC = {}
_o = 0
for _n, _w in _SPLITS:
    _SRC[_n] = (_o, _w)
    _o += _w
IN_WIDTH = _o
_COL = dict(merge=0, nsa_q=8192, kv_cmp=8704, kv_slc=8960, kv_win=9216, nsa_gate=9472, s5_u=9600,
            moba=10240, fox=11776, fox_f=13312)
WP = 13824


def _cparams(sem):
    return pltpu.CompilerParams(dimension_semantics=sem, vmem_limit_bytes=VMEM_LIMIT)


def _dot_nt(a, b, precision=None):
    return lax.dot_general(a, b, (((1,), (1,)), ((), ())), precision=precision, preferred_element_type=F32)


def _sigmoid(x):
    return 1.0 / (1.0 + jnp.exp(-x))


def _pack_w_in(w):
    wt = w.transpose(0, 2, 1)
    depth, _, d = wt.shape
    order = ('merge', 'nsa_q', 'kv_cmp', 'kv_slc', 'kv_win', 'nsa_gate', 's5_u', 'moba', 'fox', 'fox_f')
    parts, pos = [], 0
    for name in order:
        if _COL[name] > pos:
            parts.append(jnp.zeros((depth, _COL[name] - pos, d), w.dtype))
        s, wd = _SRC[name]
        parts.append(wt[:, s:s + wd])
        pos = _COL[name] + wd
    parts.append(jnp.zeros((depth, WP - pos, d), w.dtype))
    return jnp.concatenate(parts, axis=1).astype(BF16)


def _mm_kernel(x_ref, w_ref, o_ref, xb_ref, *, w_rows_are_outputs):
    @pl.when(pl.program_id(1) == 0)
    def _():
        xb_ref[...] = x_ref[...].astype(BF16)

    if w_rows_are_outputs:
        o_ref[...] = _dot_nt(xb_ref[...], w_ref[...])
    else:
        o_ref[...] = jnp.dot(xb_ref[...], w_ref[...], preferred_element_type=F32)


def _matmul(x, w_l, tm, tn, w_rows_are_outputs=False):
    w, l = w_l
    m, k = x.shape
    if w_rows_are_outputs:
        n = w.shape[1]
        w_spec = pl.BlockSpec((None, tn, k), lambda i, j: (l, j, 0))
    else:
        n = w.shape[2]
        w_spec = pl.BlockSpec((None, k, tn), lambda i, j: (l, 0, j))
    return pl.pallas_call(
        functools.partial(_mm_kernel, w_rows_are_outputs=w_rows_are_outputs), grid=(m // tm, n // tn),
        in_specs=[pl.BlockSpec((tm, k), lambda i, j: (i, 0)), w_spec],
        out_specs=pl.BlockSpec((tm, tn), lambda i, j: (i, j)),
        out_shape=jax.ShapeDtypeStruct((m, n), F32),
        scratch_shapes=[pltpu.VMEM((tm, k), BF16)],
        compiler_params=_cparams(("parallel", "arbitrary")), name="mm")(x, w)


def _transpose_kernel(c_ref, o_ref):
    o_ref[...] = c_ref[...].T


def _kv_state(c, b, t, col, width, t_from=0):
    tf = min(512, width)
    tt = 512
    nt, n0 = t // tt, t_from // tt
    st = pl.pallas_call(
        _transpose_kernel, grid=(b, width // tf, nt - n0),
        in_specs=[pl.BlockSpec((tt, tf), lambda i, f, q: (i * nt + n0 + q, col // tf + f))],
        out_specs=pl.BlockSpec((None, tf, tt), lambda i, f, q: (i, f, q)),
        out_shape=jax.ShapeDtypeStruct((b, width, t - t_from), F32),
        compiler_params=_cparams(("parallel", "parallel", "parallel")), name="kv_state")(c)
    return st.reshape(b, 2, width // (2 * HEAD_DIM), HEAD_DIM, t - t_from).transpose(0, 4, 1, 2, 3)


def _logf_kernel(c_ref, b_ref, o_ref):
    x = c_ref[...] + b_ref[...]
    y = jnp.minimum(x, 0.0) - jnp.log1p(jnp.exp(-jnp.abs(x)))
    o_ref[...] = y[:, :N_HEADS]


def _logf(c, b_pad, tm):
    n = c.shape[0]
    return pl.pallas_call(
        _logf_kernel, grid=(n // tm,),
        in_specs=[pl.BlockSpec((tm, LANE), lambda i: (i, _COL['fox_f'] // LANE)),
                  pl.BlockSpec((1, LANE), lambda i: (0, 0))],
        out_specs=pl.BlockSpec((tm, N_HEADS), lambda i: (i, 0)),
        out_shape=jax.ShapeDtypeStruct((n, N_HEADS), F32),
        compiler_params=_cparams(("parallel",)), name="logf")(c, b_pad)


def _cumsum_kernel(*refs, n_in):
    in_refs, o_ref = refs[-n_in - 1:-1], refs[-1]
    r = lax.broadcasted_iota(jnp.int32, (LANE, LANE), 0)
    c = lax.broadcasted_iota(jnp.int32, (LANE, LANE), 1)
    tri = jnp.where(r <= c, 1.0, 0.0).astype(F32)
    carry = jnp.zeros((N_HEADS, 1), F32)
    off = 0
    for ref in in_refs:
        for j in range(ref.shape[-1] // LANE):
            x = ref[0, :, j * LANE:(j + 1) * LANE]
            cs = jnp.dot(x, tri, precision=HI, preferred_element_type=F32) + carry
            o_ref[0, :, off:off + LANE] = cs
            carry = cs[:, LANE - 1:LANE]
            off += LANE


def _cumsum_prompt(logf_t):
    b, h, t = logf_t.shape
    return pl.pallas_call(
        functools.partial(_cumsum_kernel, n_in=1), grid=(b,),
        in_specs=[pl.BlockSpec((1, h, t), lambda i: (i, 0, 0))],
        out_specs=pl.BlockSpec((1, h, t), lambda i: (i, 0, 0)),
        out_shape=jax.ShapeDtypeStruct((b, h, t), F32),
        compiler_params=_cparams(("parallel",)), name="fox_cumsum_prompt")(logf_t)


M_FLOOR = -1e29


def _softmax_step(m, l, s):
    m_new = jnp.maximum(m, jnp.max(s, axis=1, keepdims=True))
    p = jnp.exp(s - m_new)
    alpha = jnp.exp(m - m_new)
    return m_new, alpha * l + jnp.sum(p, axis=1, keepdims=True), alpha, p.astype(BF16)


def _pipelined_attention(chains, lo, hi, score_fn, value_fn, rows, tk, dv, last_fix=None):
    def flush(c, kj, s, p_prev, a_prev, acc):
        prev = jnp.maximum(kj - 1, lo)
        return a_prev * acc + jnp.dot(p_prev, value_fn(c, prev), preferred_element_type=F32)

    def body(kj, carry):
        out = []
        for c in range(chains):
            s, p_prev, a_prev, m, l, acc = carry[c]
            acc = flush(c, kj, s, p_prev, a_prev, acc)
            m, l, alpha, p = _softmax_step(m, l, s)
            out.append((score_fn(c, kj + 1), p, alpha, m, l, acc))
        return tuple(out)

    init = tuple((score_fn(c, lo), jnp.zeros((rows, tk), BF16), jnp.ones((rows, 1), F32),
                  jnp.full((rows, 1), M_FLOOR, F32), jnp.zeros((rows, 1), F32), jnp.zeros((rows, dv), F32))
                 for c in range(chains))
    carry = lax.fori_loop(lo, hi, body, init)
    outs = []
    for c in range(chains):
        s, p_prev, a_prev, m, l, acc = carry[c]
        acc = flush(c, hi, s, p_prev, a_prev, acc)
        if last_fix is not None:
            s = last_fix(c, s)
        m, l, alpha, p = _softmax_step(m, l, s)
        acc = alpha * acc + jnp.dot(p, value_fn(c, hi), preferred_element_type=F32)
        outs.append(acc / jnp.maximum(l, 1e-30))
    return outs


def _attend_tiles(s_tiles, v_tiles):
    m = functools.reduce(jnp.maximum, [jnp.max(s, axis=1, keepdims=True) for s in s_tiles])
    l, acc = 0.0, 0.0
    for s, (v, feature_major) in zip(s_tiles, v_tiles):
        e = jnp.where(s > 0.5 * NEG, jnp.exp(s - m), 0.0)
        l = l + jnp.sum(e, axis=1, keepdims=True)
        eb = e.astype(BF16)
        acc = acc + (_dot_nt(eb, v) if feature_major else jnp.dot(eb, v, preferred_element_type=F32))
    return acc / jnp.maximum(l, 1e-30)


def _topn_mask_wide(v, ncols, topn):
    rows = v.shape[0]
    nc8 = -(-ncols // SUBLANE) * SUBLANE
    vt = v.T[:nc8, :]
    jr = lax.broadcasted_iota(jnp.int32, vt.shape, 0)
    rank = jnp.zeros(vt.shape, F32)
    for j2 in range(ncols):
        cand = vt[j2:j2 + 1, :]
        beats = (cand > vt) | ((cand == vt) & (jr > j2))
        rank = rank + jnp.where(beats, 1.0, 0.0)
    top = jnp.where(rank < topn, 1.0, 0.0)
    return _pad_rows(top, v.shape[1]).T > 0.5


def _topn_mask(v, ncols, topn):
    jl = lax.broadcasted_iota(jnp.int32, v.shape, 1)
    rank = jnp.zeros(v.shape, F32)
    for j2 in range(ncols):
        col = v[:, j2:j2 + 1]
        beats = (col > v) | ((col == v) & (jl > j2))
        rank = rank + jnp.where(beats, 1.0, 0.0)
    return rank < topn


def _pad_rows(a, rows):
    return jnp.concatenate([a, jnp.zeros((rows - a.shape[0], a.shape[1]), a.dtype)], axis=0)


def _fox_prompt_kernel(q_ref, k_ref, v_ref, cr_ref, o_ref, *, tq):
    qi = pl.program_id(2)
    q0 = pl.multiple_of(qi * tq, tq)
    row = lax.broadcasted_iota(jnp.int32, (tq, tq), 0)
    col = lax.broadcasted_iota(jnp.int32, (tq, tq), 1)
    qs = [(q_ref[:, h2 * HEAD_DIM:(h2 + 1) * HEAD_DIM] * SCALE).astype(BF16) for h2 in range(2)]
    c0 = [cr_ref[0, h2, :, pl.ds(q0, LANE)][:, 0:1] for h2 in range(2)]

    def score(h2, kj):
        ks = pl.multiple_of(kj * tq, tq)
        k = k_ref[pl.ds(ks, tq), h2 * HEAD_DIM:(h2 + 1) * HEAD_DIM].astype(BF16)
        return _dot_nt(qs[h2], k) + (c0[h2] - cr_ref[0, h2, :, pl.ds(ks, tq)])

    def value(h2, kj):
        ks = pl.multiple_of(kj * tq, tq)
        return v_ref[pl.ds(ks, tq), h2 * HEAD_DIM:(h2 + 1) * HEAD_DIM].astype(BF16)

    outs = _pipelined_attention(2, 0, qi, score, value, tq, tq, HEAD_DIM,
                                last_fix=lambda h2, s: jnp.where(col <= row, s, NEG))
    o_ref[...] = jnp.concatenate(outs, axis=1)


def _fox_prompt(c, cumr, b, t, tq):
    nq = t // tq
    base = _COL['fox'] // LANE
    hp_n = N_HEADS // 2
    return pl.pallas_call(
        functools.partial(_fox_prompt_kernel, tq=tq), grid=(b, hp_n, nq),
        in_specs=[pl.BlockSpec((tq, LANE), lambda i, h, q: (i * nq + q, base + h)),
                  pl.BlockSpec((t, LANE), lambda i, h, q: (i, base + hp_n + h)),
                  pl.BlockSpec((t, LANE), lambda i, h, q: (i, base + 2 * hp_n + h)),
                  pl.BlockSpec((1, 2, 1, t), lambda i, h, q: (i, h, 0, 0))],
        out_specs=pl.BlockSpec((tq, LANE), lambda i, h, q: (i * nq + q, h)),
        out_shape=jax.ShapeDtypeStruct((b * t, MIX), F32),
        compiler_params=_cparams(("parallel", "arbitrary", "arbitrary")), name="fox_prompt")(
            c, c, c, cumr)


def _head_slope(hp, h2):
    s = jnp.float32(SLOPES[h2])
    for k in range(1, N_HEADS // 2):
        s = jnp.where(hp == k, jnp.float32(SLOPES[2 * k + h2]), s)
    return s


def _moba_prompt_kernel(q_ref, k_ref, v_ref, o_ref, kmean_ref, *, tq, nb):
    hp = pl.program_id(1)
    qi = pl.program_id(2)
    bpt = tq // MOBA_BLOCK

    @pl.when(qi == 0)
    def _():
        kmean_ref[...] = jnp.zeros_like(kmean_ref)
        for n in range(nb):
            kmean_ref[n:n + 1, :] = jnp.mean(k_ref[n * MOBA_BLOCK:(n + 1) * MOBA_BLOCK, :], axis=0, keepdims=True)

    q0 = pl.multiple_of(qi * tq, tq)
    row = lax.broadcasted_iota(jnp.int32, (tq, tq), 0)
    col = lax.broadcasted_iota(jnp.int32, (tq, tq), 1)
    colpos = lax.broadcasted_iota(jnp.int32, (1, tq), 1)
    jl = lax.broadcasted_iota(jnp.int32, (tq, LANE), 1)
    cur = qi * bpt + lax.broadcasted_iota(jnp.int32, (tq, 1), 0) // MOBA_BLOCK
    qs, slopes, blockbias = [], [], []
    for h2 in range(2):
        lo = h2 * HEAD_DIM
        qf = q_ref[:, lo:lo + HEAD_DIM]
        gate = _dot_nt(qf, kmean_ref[:, lo:lo + HEAD_DIM], precision=HI)
        gate = jnp.where(jl < cur, gate, NEG)
        picked = (_topn_mask_wide(gate, nb, MOBA_TOPK) & (jl < cur)) | (jl == cur)
        qs.append((qf * SCALE).astype(BF16))
        slopes.append(_head_slope(hp, h2))
        blockbias.append(jnp.where(picked, 0.0, NEG))

    def score(h2, kj):
        ks = pl.multiple_of(kj * tq, tq)
        k = k_ref[pl.ds(ks, tq), h2 * HEAD_DIM:(h2 + 1) * HEAD_DIM].astype(BF16)
        bias = None
        for j in reversed(range(bpt)):
            rowbias = jnp.min(jnp.where(jl == kj * bpt + j, blockbias[h2], 0.0), axis=1, keepdims=True)
            bias = rowbias if bias is None else jnp.where(colpos < (j + 1) * MOBA_BLOCK, rowbias, bias)
        return _dot_nt(qs[h2], k) + slopes[h2] * (ks - q0 + colpos).astype(F32) + bias

    def value(h2, kj):
        ks = pl.multiple_of(kj * tq, tq)
        return v_ref[pl.ds(ks, tq), h2 * HEAD_DIM:(h2 + 1) * HEAD_DIM].astype(BF16)

    outs = _pipelined_attention(2, 0, qi, score, value, tq, tq, HEAD_DIM,
                                last_fix=lambda h2, s: jnp.where(col <= row, s, NEG))
    o_ref[...] = jnp.concatenate(outs, axis=1)


def _moba_prompt(c, b, t, tq):
    nq = t // tq
    base = _COL['moba'] // LANE
    hp_n = N_HEADS // 2
    return pl.pallas_call(
        functools.partial(_moba_prompt_kernel, tq=tq, nb=t // MOBA_BLOCK), grid=(b, hp_n, nq),
        in_specs=[pl.BlockSpec((tq, LANE), lambda i, h, q: (i * nq + q, base + h)),
                  pl.BlockSpec((t, LANE), lambda i, h, q: (i, base + hp_n + h)),
                  pl.BlockSpec((t, LANE), lambda i, h, q: (i, base + 2 * hp_n + h))],
        out_specs=pl.BlockSpec((tq, LANE), lambda i, h, q: (i * nq + q, h)),
        out_shape=jax.ShapeDtypeStruct((b * t, MIX), F32),
        scratch_shapes=[pltpu.VMEM((LANE, LANE), F32)],
        compiler_params=_cparams(("parallel", "arbitrary", "arbitrary")), name="moba_prompt")(c, c, c)


def _cmp_kernel(*refs, n_in, feature_major):
    if feature_major:
        xs_ref, refs = refs[-1], refs[:-1]
        x_refs = refs[-5 - n_in:-5]
        page = x_refs[0].shape[2]
        halves = x_refs[0].shape[1] // LANE
        for p, r in enumerate(x_refs):
            for j in range(halves):
                xs_ref[j, p * page:(p + 1) * page, :] = r[0, j * LANE:(j + 1) * LANE, :].T
        nchunk = n_in * page // CMP_STRIDE
        x = jnp.concatenate([xs_ref[j, pl.ds(l, nchunk, stride=CMP_STRIDE), :]
                             for l in range(CMP_STRIDE) for j in range(halves)], axis=1)
    else:
        x = refs[-6][0]
    pos_ref, wcat_ref, w0_ref, w1_ref, o_ref = refs[-5:]
    xb = x.astype(BF16)
    p0 = jnp.dot(xb, w0_ref[...], preferred_element_type=F32)
    p1 = jnp.dot(xb, w1_ref[...], preferred_element_type=F32)
    bias = jnp.dot(pos_ref[...].astype(BF16), wcat_ref[...], preferred_element_type=F32)[0:1]
    o_ref[0] = p0 + pltpu.roll(p1, p1.shape[0] - 1, 0) + bias


def _cmp_weights(pos, wk, wv):
    r = CMP_LEN // CMP_STRIDE
    w = jnp.stack([wk, wv]).reshape(2, r, CMP_STRIDE, HEAD_DIM, HEAD_DIM)
    e2 = jnp.eye(2, dtype=w.dtype)
    eg = jnp.eye(KV_G, dtype=w.dtype)
    big = jnp.einsum('krlde,kK,gG->rlkgdKGe', w, e2, eg)
    big = big.reshape(r, CMP_STRIDE * 2 * KV_G * HEAD_DIM, 2 * KV_G * HEAD_DIM).astype(BF16)
    wcat = jnp.concatenate([wk, wk, wv, wv], axis=1).astype(BF16)
    posb = jnp.zeros((SUBLANE, CMP_LEN * HEAD_DIM), F32).at[0].set(pos.reshape(-1))
    return posb, wcat, big[0], big[1]


def _const_specs(arrays):
    return [pl.BlockSpec(a.shape, lambda *_, nd=a.ndim: (0,) * nd) for a in arrays]


def _cmp_prompt(c, b, t, cw):
    kw = 2 * KV_G * HEAD_DIM
    nchunk = t // CMP_STRIDE
    x = c[:, _COL['kv_cmp']:_COL['kv_cmp'] + kw].reshape(b, nchunk, CMP_STRIDE * kw)
    return pl.pallas_call(
        functools.partial(_cmp_kernel, n_in=1, feature_major=False), grid=(b,),
        in_specs=[pl.BlockSpec((1, nchunk, CMP_STRIDE * kw), lambda i: (i, 0, 0))] + _const_specs(cw),
        out_specs=pl.BlockSpec((1, nchunk, kw), lambda i: (i, 0, 0)),
        out_shape=jax.ShapeDtypeStruct((b, nchunk, kw), F32),
        compiler_params=_cparams(("parallel",)), name="nsa_cmp_prompt")(x, *cw)


def _cmp_paged(page_table, pool, first, cw):
    bs, npages = page_table.shape
    kw, page = pool.shape[1:]
    nchunk = npages * page // CMP_STRIDE
    return pl.pallas_call(
        functools.partial(_cmp_kernel, n_in=npages, feature_major=True),
        grid_spec=pltpu.PrefetchScalarGridSpec(
            num_scalar_prefetch=1, grid=(bs,), in_specs=_page_specs(pool, npages, first) + _const_specs(cw),
            out_specs=pl.BlockSpec((1, nchunk, kw), lambda b, pt: (b, 0, 0)),
            scratch_shapes=[pltpu.VMEM((kw // LANE, npages * page, LANE), F32)]),
        out_shape=jax.ShapeDtypeStruct((bs, nchunk, kw), F32),
        compiler_params=_cparams(("parallel",)), name="nsa_cmp_paged")(page_table, *([pool] * npages), *cw)


def _overlap_matrix(nc, ns):
    i_c = np.arange(LANE)[:, None] * CMP_STRIDE
    j_s = np.arange(LANE)[None, :] * SLC_BLOCK
    ov = (i_c < j_s + SLC_BLOCK) & (i_c + CMP_LEN > j_s)
    ov &= (np.arange(LANE)[:, None] < nc) & (np.arange(LANE)[None, :] < ns)
    return jnp.asarray(ov, F32)


def _nsa_prompt_kernel(q_ref, slc_ref, win_ref, cmp_ref, g_ref, ov_ref, o_ref, *, tq, nc, ns):
    qi = pl.program_id(1)
    q0 = pl.multiple_of(qi * tq, tq)
    rows = HG * tq
    rl = lax.broadcasted_iota(jnp.int32, (tq, 1), 0)
    t1 = q0 + rl
    t4 = jnp.concatenate([t1] * HG, axis=0)
    lane = lax.broadcasted_iota(jnp.int32, (1, LANE), 1)
    colpos = lax.broadcasted_iota(jnp.int32, (1, tq), 1)
    dloc = lax.broadcasted_iota(jnp.int32, (tq, tq), 0) - lax.broadcasted_iota(jnp.int32, (tq, tq), 1)
    sg = _sigmoid(g_ref[...])
    jl = lax.broadcasted_iota(jnp.int32, (tq, LANE), 1)
    cur = t1 // SLC_BLOCK
    erow = lax.broadcasted_iota(jnp.int32, (LANE, tq), 0)
    ecol = lax.broadcasted_iota(jnp.int32, (LANE, tq), 1)
    wtiles = WINDOW // tq
    qs, slope, o_cmp, blockbias = [], [], [], []
    for g in range(KV_G):
        kl = g * HEAD_DIM
        vl = KV_G * HEAD_DIM + g * HEAD_DIM
        qg = jnp.concatenate([q_ref[:, (g * HG + h) * HEAD_DIM:(g * HG + h + 1) * HEAD_DIM] for h in range(HG)],
                             axis=0)
        qs.append((qg * SCALE).astype(BF16))
        slope.append(jnp.concatenate([jnp.full((tq, 1), SLOPES[g * HG + h], F32) for h in range(HG)], axis=0))

        kc = cmp_ref[0, :, kl:kl + HEAD_DIM].astype(BF16)
        vc = cmp_ref[0, :, vl:vl + HEAD_DIM].astype(BF16)
        dist = (t4 - (lane * CMP_STRIDE + CMP_LEN - 1)).astype(F32)
        ok_c = (dist >= 0) & (lane < nc)
        s = jnp.where(ok_c, _dot_nt(qs[g], kc) - slope[g] * dist, NEG)
        m = jnp.max(s, axis=1, keepdims=True)
        e = jnp.where(ok_c, jnp.exp(s - m), 0.0)
        p_c = e / jnp.maximum(jnp.sum(e, axis=1, keepdims=True), 1e-30)
        o_cmp.append(jnp.dot(p_c.astype(BF16), vc, preferred_element_type=F32))

        psum = p_c[0:tq]
        for h in range(1, HG):
            psum = psum + p_c[h * tq:(h + 1) * tq]
        imp = jnp.dot(psum, ov_ref[...], precision=HI, preferred_element_type=F32)
        forced = (jl == 0) | (jl == cur) | (jl == cur - 1)
        imp = jnp.where(forced, FORCE, imp)
        imp = jnp.where(jl <= cur, imp, NEG)
        picked = _topn_mask_wide(imp, ns, min(SLC_TOPN, ns)) & (jl <= cur)
        blockbias.append(jnp.where(picked, 0.0, NEG).astype(BF16))

    def scores(ref, g, ks, bias):
        k = ref[pl.ds(ks, tq), g * HEAD_DIM:(g + 1) * HEAD_DIM].astype(BF16)
        cpos = (ks - q0 + colpos).astype(F32)
        return _dot_nt(qs[g], k) + slope[g] * cpos + jnp.concatenate([bias] * HG, axis=0)

    def values(ref, g, kj):
        ks = pl.multiple_of(kj * tq, tq)
        vl = KV_G * HEAD_DIM + g * HEAD_DIM
        return ref[pl.ds(ks, tq), vl:vl + HEAD_DIM].astype(BF16)

    def slc_score(g, kj):
        ks = pl.multiple_of(kj * tq, tq)
        expand = jnp.where((ks + ecol) // SLC_BLOCK == erow, 1.0, 0.0).astype(BF16)
        return scores(slc_ref, g, ks, jnp.dot(blockbias[g], expand, preferred_element_type=F32))

    def win_score(g, kj):
        ks = pl.multiple_of(kj * tq, tq)
        d = dloc + (q0 - ks)
        return scores(win_ref, g, ks, jnp.where((d >= 0) & (d < WINDOW), 0.0, NEG))

    causal = jnp.concatenate([dloc] * HG, axis=0) >= 0
    o_slc = _pipelined_attention(KV_G, 0, qi, slc_score, functools.partial(values, slc_ref), rows, tq, HEAD_DIM,
                                 last_fix=lambda g, s: jnp.where(causal, s, NEG))
    o_win = _pipelined_attention(KV_G, jnp.maximum(qi - wtiles, 0), qi, win_score,
                                 functools.partial(values, win_ref), rows, tq, HEAD_DIM)

    for g in range(KV_G):
        for h in range(HG):
            hh = g * HG + h
            r0 = h * tq
            o = (sg[:, hh:hh + 1] * o_cmp[g][r0:r0 + tq]
                 + sg[:, N_HEADS + hh:N_HEADS + hh + 1] * o_slc[g][r0:r0 + tq]
                 + sg[:, 2 * N_HEADS + hh:2 * N_HEADS + hh + 1] * o_win[g][r0:r0 + tq])
            o_ref[:, hh * HEAD_DIM:(hh + 1) * HEAD_DIM] = o


def _nsa_prompt(c, kvc, ov, b, t, tq):
    nq = t // tq
    kw = 2 * KV_G * HEAD_DIM
    nchunk = t // CMP_STRIDE
    nc = nchunk - CMP_LEN // CMP_STRIDE + 1
    ns = -(-t // SLC_BLOCK)
    return pl.pallas_call(
        functools.partial(_nsa_prompt_kernel, tq=tq, nc=nc, ns=ns), grid=(b, nq),
        in_specs=[pl.BlockSpec((tq, MIX), lambda i, q: (i * nq + q, _COL['nsa_q'] // MIX)),
                  pl.BlockSpec((t, kw), lambda i, q: (i, _COL['kv_slc'] // kw)),
                  pl.BlockSpec((t, kw), lambda i, q: (i, _COL['kv_win'] // kw)),
                  pl.BlockSpec((1, nchunk, kw), lambda i, q: (i, 0, 0)),
                  pl.BlockSpec((tq, LANE), lambda i, q: (i * nq + q, _COL['nsa_gate'] // LANE)),
                  pl.BlockSpec((LANE, LANE), lambda i, q: (0, 0))],
        out_specs=pl.BlockSpec((tq, MIX), lambda i, q: (i * nq + q, 0)),
        out_shape=jax.ShapeDtypeStruct((b * t, MIX), F32),
        compiler_params=_cparams(("parallel", "arbitrary")), name="nsa_prompt")(c, c, c, kvc, c, ov)


def _dec_rows():
    rid = lax.broadcasted_iota(jnp.int32, (N_HEADS * T_PAD, 1), 0)
    t8 = rid % T_PAD
    slope = jnp.concatenate([jnp.full((T_PAD, 1), SLOPES[h], F32) for h in range(N_HEADS)], axis=0)
    return t8, slope


def _expand_heads(q8):
    lane = lax.broadcasted_iota(jnp.int32, q8.shape, 1)
    return jnp.concatenate([jnp.where(lane // HEAD_DIM == h, q8, 0.0) for h in range(N_HEADS)], axis=0)


def _collapse_heads(res):
    lane = lax.broadcasted_iota(jnp.int32, (T_PAD, res.shape[1]), 1)
    out = jnp.zeros((T_PAD, res.shape[1]), F32)
    for h in range(N_HEADS):
        out = out + jnp.where(lane // HEAD_DIM == h, res[h * T_PAD:(h + 1) * T_PAD], 0.0)
    return out


def _moba_dec_kernel(pt_ref, q_ref, kn_ref, vn_ref, *rest, npages, past):
    page_refs, o_ref = rest[:npages], rest[npages]
    page = page_refs[0].shape[2]
    t8, slope = _dec_rows()
    qpos = past + t8
    lane = lax.broadcasted_iota(jnp.int32, (1, LANE), 1)
    q8 = q_ref[0]
    qx_f = _expand_heads(q8)
    qx = (qx_f * SCALE).astype(BF16)
    per_blk = MOBA_BLOCK // page
    nb_past = past // MOBA_BLOCK
    lanei = lax.broadcasted_iota(jnp.int32, (MIX, LANE), 1)
    kmean = jnp.zeros((MIX, LANE), F32)
    for n in range(nb_past):
        tot = page_refs[n * per_blk][0, 0:MIX, :]
        for p in range(n * per_blk + 1, (n + 1) * per_blk):
            tot = tot + page_refs[p][0, 0:MIX, :]
        col = jnp.sum(tot, axis=1, keepdims=True) * (1.0 / MOBA_BLOCK)
        kmean = kmean + jnp.where(lanei == n, col, 0.0)
    gate = jnp.dot(qx_f, kmean, precision=HI, preferred_element_type=F32)
    jl = lax.broadcasted_iota(jnp.int32, gate.shape, 1)
    cur = qpos // MOBA_BLOCK
    gate = jnp.where(jl < cur, gate, NEG)
    nb = -(-(past + T_PAD) // MOBA_BLOCK)
    sel = jnp.where(_topn_mask(gate, nb, max(1, min(MOBA_TOPK, nb - 1))) & (jl < cur), 1.0, 0.0)
    s_tiles, v_tiles = [], []
    for p in range(npages):
        kt = page_refs[p][0, 0:MIX, :].astype(BF16)
        v_tiles.append((page_refs[p][0, MIX:2 * MIX, :].astype(BF16), True))
        n = (p * page) // MOBA_BLOCK
        d = (qpos - (p * page + lane)).astype(F32)
        valid = (sel[:, n:n + 1] > 0.5) & (d >= 0)
        s_tiles.append(jnp.where(valid, jnp.dot(qx, kt, preferred_element_type=F32) - slope * d, NEG))
    k = _pad_rows(kn_ref[0], LANE).astype(BF16)
    v_tiles.append((_pad_rows(vn_ref[0], LANE).astype(BF16), False))
    d = (t8 - lane).astype(F32)
    valid = (d >= 0) & (lane < T_PAD)
    s_tiles.append(jnp.where(valid, _dot_nt(qx, k) - slope * d, NEG))
    o_ref[0] = _collapse_heads(_attend_tiles(s_tiles, v_tiles))


def _fox_dec_kernel(pt_ref, q_ref, kn_ref, vn_ref, ln_ref, *rest, npages, past):
    logf_refs, page_refs, o_ref = rest[:npages], rest[npages:2 * npages], rest[2 * npages]
    t8, _ = _dec_rows()
    lane = lax.broadcasted_iota(jnp.int32, (1, LANE), 1)
    qx = (_expand_heads(q_ref[0]) * SCALE).astype(BF16)
    r = lax.broadcasted_iota(jnp.int32, (LANE, LANE), 0)
    c = lax.broadcasted_iota(jnp.int32, (LANE, LANE), 1)
    tri = jnp.where(r <= c, 1.0, 0.0).astype(F32)

    def head_rows(x):
        return jnp.concatenate([jnp.broadcast_to(x[h:h + 1], (T_PAD, x.shape[1])) for h in range(N_HEADS)], axis=0)

    carry = jnp.zeros((N_HEADS, 1), F32)
    cum = []
    for ref in list(logf_refs) + [ln_ref]:
        cs = jnp.dot(ref[0], tri, precision=HI, preferred_element_type=F32) + carry
        cum.append(cs)
        carry = cs[:, LANE - 1:LANE]
    c_ref = head_rows(cum[npages - 1][:, LANE - 1:LANE])

    s_tiles, v_tiles = [], []
    for p in range(npages):
        kt = page_refs[p][0, 0:MIX, :].astype(BF16)
        v_tiles.append((page_refs[p][0, MIX:2 * MIX, :].astype(BF16), True))
        s_tiles.append(jnp.dot(qx, kt, preferred_element_type=F32) + (c_ref - head_rows(cum[p])))
    k = _pad_rows(kn_ref[0], LANE).astype(BF16)
    v_tiles.append((_pad_rows(vn_ref[0], LANE).astype(BF16), False))
    valid = (lane <= t8) & (lane < T_PAD)
    s_tiles.append(jnp.where(valid, _dot_nt(qx, k) + (c_ref - head_rows(cum[npages])), NEG))
    o_ref[0] = _collapse_heads(_attend_tiles(s_tiles, v_tiles))


def _page_specs(pool, npages, first):
    blk = (1,) + pool.shape[1:]
    return [pl.BlockSpec(blk, lambda b, pt, p=p: (first + pt[b, p], 0, 0)) for p in range(npages)]


def _paged_mha_dec(kernel, name, c3, col, page_table, pool, first, extra_args, extra_specs, past):
    bs, npages = page_table.shape
    qb = col // MIX
    in_specs = [pl.BlockSpec((1, T_PAD, MIX), lambda b, pt: (b, 0, qb)),
                pl.BlockSpec((1, T_PAD, MIX), lambda b, pt: (b, 0, qb + 1)),
                pl.BlockSpec((1, T_PAD, MIX), lambda b, pt: (b, 0, qb + 2))]
    in_specs += extra_specs
    in_specs += _page_specs(pool, npages, first)
    return pl.pallas_call(
        functools.partial(kernel, npages=npages, past=past),
        grid_spec=pltpu.PrefetchScalarGridSpec(
            num_scalar_prefetch=1, grid=(bs,), in_specs=in_specs,
            out_specs=pl.BlockSpec((1, T_PAD, MIX), lambda b, pt: (b, 0, 0))),
        out_shape=jax.ShapeDtypeStruct((bs, T_PAD, MIX), F32),
        compiler_params=_cparams(("parallel",)), name=name)(
            page_table, c3, c3, c3, *extra_args, *([pool] * npages))


def _nsa_dec_kernel(pt_ref, q_ref, ns_ref, nw_ref, g_ref, cmp_ref, ov_ref, wc_ref, *rest, npages, past, nc, ns):
    page_refs, o_ref = rest[:npages], rest[npages]
    page = page_refs[0].shape[2]
    kw = KV_G * HEAD_DIM
    t8, slope = _dec_rows()
    qpos = past + t8
    lane = lax.broadcasted_iota(jnp.int32, (1, LANE), 1)
    lane8 = lax.broadcasted_iota(jnp.int32, (T_PAD, LANE), 1)
    q8 = q_ref[0] * SCALE
    rows = []
    for h in range(N_HEADS):
        x = q8[:, (h // 2) * LANE:(h // 2 + 1) * LANE]
        dst = h // HG
        if h % 2 != dst:
            x = pltpu.roll(x, HEAD_DIM, 1)
        rows.append(jnp.where((lane8 // HEAD_DIM) == dst, x, 0.0))
    qx = jnp.concatenate(rows, axis=0).astype(BF16)

    cm = cmp_ref[0]
    dist = (qpos - (lane * CMP_STRIDE + CMP_LEN - 1)).astype(F32)
    ok_c = (dist >= 0) & (lane < nc)
    s = jnp.where(ok_c, _dot_nt(qx, cm[:, 0:kw].astype(BF16)) - slope * dist, NEG)
    m = jnp.max(s, axis=1, keepdims=True)
    e = jnp.where(ok_c, jnp.exp(s - m), 0.0)
    p_c = e / jnp.maximum(jnp.sum(e, axis=1, keepdims=True), 1e-30)
    o_cmp = jnp.dot(p_c.astype(BF16), cm[:, kw:2 * kw].astype(BF16), preferred_element_type=F32)

    psum = []
    for g in range(KV_G):
        acc = p_c[g * HG * T_PAD:g * HG * T_PAD + T_PAD]
        for h in range(1, HG):
            r0 = (g * HG + h) * T_PAD
            acc = acc + p_c[r0:r0 + T_PAD]
        psum.append(acc)
    imp = jnp.dot(jnp.concatenate(psum, axis=0), ov_ref[...], precision=HI, preferred_element_type=F32)
    jl = lax.broadcasted_iota(jnp.int32, imp.shape, 1)
    tg = lax.broadcasted_iota(jnp.int32, (KV_G * T_PAD, 1), 0) % T_PAD
    cur = (past + tg) // SLC_BLOCK
    forced = (jl == 0) | (jl == cur) | (jl == cur - 1)
    imp = jnp.where(forced, FORCE, imp)
    imp = jnp.where(jl <= cur, imp, NEG)
    sel = jnp.where(_topn_mask(imp, ns, min(SLC_TOPN, ns)) & (jl <= cur), 1.0, 0.0)
    sel_rows = jnp.concatenate([sel[(h // HG) * T_PAD:(h // HG + 1) * T_PAD] for h in range(N_HEADS)], axis=0)

    def new_tile(ref, extra_valid):
        k = _pad_rows(ref[0, :, 0:kw], LANE).astype(BF16)
        v = _pad_rows(ref[0, :, kw:2 * kw], LANE).astype(BF16)
        d = (t8 - lane).astype(F32)
        valid = (d >= 0) & (lane < T_PAD) & extra_valid
        return jnp.where(valid, _dot_nt(qx, k) - slope * d, NEG), (v, False)

    s_tiles, v_tiles = [], []
    per = page // SLC_BLOCK
    for p in range(npages):
        kt = page_refs[p][0, 0:kw, :].astype(BF16)
        v_tiles.append((page_refs[p][0, kw:2 * kw, :].astype(BF16), True))
        picked = jnp.zeros((N_HEADS * T_PAD, LANE), jnp.bool_)
        for j in range(per):
            blk = p * per + j
            picked = picked | ((lane // SLC_BLOCK == j) & (sel_rows[:, blk:blk + 1] > 0.5))
        d = (qpos - (p * page + lane)).astype(F32)
        s_tiles.append(jnp.where(picked & (d >= 0), jnp.dot(qx, kt, preferred_element_type=F32) - slope * d, NEG))
    blk_new = past // SLC_BLOCK
    s_new, v_new = new_tile(ns_ref, sel_rows[:, blk_new:blk_new + 1] > 0.5)
    o_slc = _attend_tiles(s_tiles + [s_new], v_tiles + [v_new])

    s_tiles, v_tiles = [], []
    wb = wc_ref.shape[2]
    w_off = past - wb
    for j in range(wb // LANE):
        kt = wc_ref[0, 0:kw, j * LANE:(j + 1) * LANE].astype(BF16)
        v_tiles.append((wc_ref[0, kw:2 * kw, j * LANE:(j + 1) * LANE].astype(BF16), True))
        d = qpos - (w_off + j * LANE + lane)
        valid = (d >= 0) & (d < WINDOW)
        s_tiles.append(jnp.where(valid, jnp.dot(qx, kt, preferred_element_type=F32) - slope * d.astype(F32), NEG))
    s_new, v_new = new_tile(nw_ref, True)
    o_win = _attend_tiles(s_tiles + [s_new], v_tiles + [v_new])

    sg = _sigmoid(g_ref[0])
    for h in range(N_HEADS):
        r0, l0 = h * T_PAD, (h // HG) * HEAD_DIM
        o = (sg[:, h:h + 1] * o_cmp[r0:r0 + T_PAD, l0:l0 + HEAD_DIM]
             + sg[:, N_HEADS + h:N_HEADS + h + 1] * o_slc[r0:r0 + T_PAD, l0:l0 + HEAD_DIM]
             + sg[:, 2 * N_HEADS + h:2 * N_HEADS + h + 1] * o_win[r0:r0 + T_PAD, l0:l0 + HEAD_DIM])
        o_ref[0, :, h * HEAD_DIM:(h + 1) * HEAD_DIM] = o


def _nsa_dec(c3, kvc, ov, win_cache, win_first, page_table, pool, first, past, t_real):
    bs, npages = page_table.shape
    kw = 2 * KV_G * HEAD_DIM
    nchunk = kvc.shape[1]
    nc = nchunk - CMP_LEN // CMP_STRIDE + 1
    ns = -(-(past + t_real) // SLC_BLOCK)
    wb = win_cache.shape[2]
    in_specs = [pl.BlockSpec((1, T_PAD, MIX), lambda b, pt: (b, 0, _COL['nsa_q'] // MIX)),
                pl.BlockSpec((1, T_PAD, kw), lambda b, pt: (b, 0, _COL['kv_slc'] // kw)),
                pl.BlockSpec((1, T_PAD, kw), lambda b, pt: (b, 0, _COL['kv_win'] // kw)),
                pl.BlockSpec((1, T_PAD, LANE), lambda b, pt: (b, 0, _COL['nsa_gate'] // LANE)),
                pl.BlockSpec((1, nchunk, kw), lambda b, pt: (b, 0, 0)),
                pl.BlockSpec((LANE, LANE), lambda b, pt: (0, 0)),
                pl.BlockSpec((1, kw, wb), lambda b, pt: (win_first + b, 0, 0))]
    in_specs += _page_specs(pool, npages, first)
    return pl.pallas_call(
        functools.partial(_nsa_dec_kernel, npages=npages, past=past, nc=nc, ns=ns),
        grid_spec=pltpu.PrefetchScalarGridSpec(
            num_scalar_prefetch=1, grid=(bs,), in_specs=in_specs,
            out_specs=pl.BlockSpec((1, T_PAD, MIX), lambda b, pt: (b, 0, 0))),
        out_shape=jax.ShapeDtypeStruct((bs, T_PAD, MIX), F32),
        compiler_params=_cparams(("parallel",)), name="nsa_decode")(
            page_table, c3, c3, c3, c3, kvc, ov, win_cache, *([pool] * npages))


def _s5_disc_kernel(ar_ref, ai_ref, ls_ref, btr_ref, bti_ref, abr_ref, abi_ref, bbr_ref, bbi_ref):
    ar, ai = ar_ref[...], ai_ref[...]
    step = jnp.exp(ls_ref[...])
    mag = jnp.exp(ar * step)
    abr = mag * jnp.cos(ai * step)
    abi = mag * jnp.sin(ai * step)
    den = ar * ar + ai * ai
    zr = (ar * (abr - 1.0) + ai * abi) / den
    zi = (ar * abi - ai * (abr - 1.0)) / den
    abr_ref[...] = abr
    abi_ref[...] = abi
    btr, bti = btr_ref[...], bti_ref[...]
    bbr_ref[...] = zr * btr - zi * bti
    bbi_ref[...] = zr * bti + zi * btr


def _s5_disc(a_re, a_im, log_step, b_re, b_im):
    rep = lambda a: jnp.repeat(a, S5_CH, axis=0)
    n = S5_GROUPS * S5_CH
    args = (rep(a_re), rep(a_im), rep(jnp.broadcast_to(log_step[:, None], (S5_GROUPS, S5_STATE))),
            b_re.transpose(0, 2, 1).reshape(n, S5_STATE), b_im.transpose(0, 2, 1).reshape(n, S5_STATE))
    shp = jax.ShapeDtypeStruct((n, S5_STATE), F32)
    return pl.pallas_call(_s5_disc_kernel, out_shape=(shp, shp, shp, shp), name="s5_discretise")(*args)


_S5_CH_ROWS = 128


def _s5_scan_kernel(*refs, seg, has_h0):
    if has_h0:
        (u_ref, bre_ref, bim_ref, ar_ref, ai_ref, cre_ref, cim_ref, d_ref, h0r_ref, h0i_ref,
         y_ref, hlr_ref, hli_ref, hr_s, hi_s) = refs
    else:
        (u_ref, bre_ref, bim_ref, ar_ref, ai_ref, cre_ref, cim_ref, d_ref,
         y_ref, hlr_ref, hli_ref, hr_s, hi_s) = refs
    rows = u_ref.shape[0]
    ch = min(_S5_CH_ROWS, rows)
    pad = ch
    nch = rows // ch
    ar, ai = ar_ref[0], ai_ref[0]
    bre, bim, cre, cim = (r[0].astype(BF16) for r in (bre_ref, bim_ref, cre_ref, cim_ref))
    hr_s[0:pad, :] = jnp.zeros((pad, hr_s.shape[1]), F32)
    hi_s[0:pad, :] = jnp.zeros((pad, hi_s.shape[1]), F32)

    def init_body(i, _):
        r0 = pl.multiple_of(i * ch, ch)
        u = u_ref[pl.ds(r0, ch), :]
        ub = u.astype(BF16)
        br = jnp.dot(ub, bre, preferred_element_type=F32)
        bi = jnp.dot(ub, bim, preferred_element_type=F32)
        if has_h0:
            h0r, h0i = h0r_ref[pl.ds(r0, ch), :], h0i_ref[pl.ds(r0, ch), :]
            br = br + (ar * h0r - ai * h0i)
            bi = bi + (ar * h0i + ai * h0r)
        hr_s[pl.ds(pad + r0, ch), :] = br
        hi_s[pl.ds(pad + r0, ch), :] = bi
        return 0

    lax.fori_loop(0, nch, init_body, 0)

    rl = lax.broadcasted_iota(jnp.int32, (ch, 1), 0)
    pr, pi = ar, ai
    d = 1
    two_level = seg == rows and rows > ch
    span = ch if two_level else seg
    while d < span:
        first = d // ch

        def pass_body(i, _, d=d, pr=pr, pi=pi):
            r0 = pl.multiple_of((nch - 1 - i) * ch, ch)
            cr = hr_s[pl.ds(pad + r0, ch), :]
            ci = hi_s[pl.ds(pad + r0, ch), :]
            if d < SUBLANE:
                lo = pad - SUBLANE
                sr = pltpu.roll(hr_s[pl.ds(lo + r0, ch + SUBLANE), :], d, 0)[SUBLANE:]
                si = pltpu.roll(hi_s[pl.ds(lo + r0, ch + SUBLANE), :], d, 0)[SUBLANE:]
            else:
                sr = hr_s[pl.ds(pad + r0 - d, ch), :]
                si = hi_s[pl.ds(pad + r0 - d, ch), :]
            if d < ch or seg < rows:
                keep = ((r0 + rl) % span) >= d
                sr = jnp.where(keep, sr, 0.0)
                si = jnp.where(keep, si, 0.0)
            hr_s[pl.ds(pad + r0, ch), :] = cr + (pr * sr - pi * si)
            hi_s[pl.ds(pad + r0, ch), :] = ci + (pr * si + pi * sr)
            return 0

        lax.fori_loop(0, nch - first, pass_body, 0)
        pr, pi = pr * pr - pi * pi, 2.0 * pr * pi
        d *= 2

    if two_level:
        tr, ti = jnp.broadcast_to(ar, (ch, ar.shape[1])), jnp.broadcast_to(ai, (ch, ai.shape[1]))
        d = 1
        while d < ch:
            sr, si = pltpu.roll(tr, d, 0), pltpu.roll(ti, d, 0)
            keep = rl >= d
            tr, ti = jnp.where(keep, tr * sr - ti * si, tr), jnp.where(keep, tr * si + ti * sr, ti)
            d *= 2
        lasts = [(hr_s[pad + (c + 1) * ch - 1:pad + (c + 1) * ch, :], hi_s[pad + (c + 1) * ch - 1:pad + (c + 1) * ch, :])
                 for c in range(nch - 1)]
        kr, ki = lasts[0]
        for c in range(1, nch):
            if c > 1:
                lr, li = lasts[c - 1]
                kr, ki = lr + (pr * kr - pi * ki), li + (pr * ki + pi * kr)
            rows_c = slice(pad + c * ch, pad + (c + 1) * ch)
            hr_s[rows_c, :] = hr_s[rows_c, :] + (tr * kr - ti * ki)
            hi_s[rows_c, :] = hi_s[rows_c, :] + (tr * ki + ti * kr)

    def out_body(i, _):
        r0 = pl.multiple_of(i * ch, ch)
        hr = hr_s[pl.ds(pad + r0, ch), :]
        hi = hi_s[pl.ds(pad + r0, ch), :]
        y = (jnp.dot(hr.astype(BF16), cre, preferred_element_type=F32)
             - jnp.dot(hi.astype(BF16), cim, preferred_element_type=F32))
        y_ref[pl.ds(r0, ch), :] = y + d_ref[0] * u_ref[pl.ds(r0, ch), :]
        return 0

    lax.fori_loop(0, nch, out_body, 0)
    nl = hlr_ref.shape[0]
    hlr_ref[...] = hr_s[pad + rows - nl:pad + rows, :]
    hli_ref[...] = hi_s[pad + rows - nl:pad + rows, :]


def _s5_scan(c, sw, rows, seg, h0=None):
    n = c.shape[0]
    nt = n // rows
    gl = LANE // S5_CH
    lt = S5_GROUPS // gl
    w = gl * S5_STATE
    nl = (rows // seg) * SUBLANE if seg == SUBLANE else SUBLANE
    ub = _COL['s5_u'] // LANE
    in_specs = [pl.BlockSpec((rows, LANE), lambda i, j: (i, ub + j)),
                pl.BlockSpec((1, LANE, w), lambda i, j: (j, 0, 0)),
                pl.BlockSpec((1, LANE, w), lambda i, j: (j, 0, 0)),
                pl.BlockSpec((1, 1, w), lambda i, j: (j, 0, 0)),
                pl.BlockSpec((1, 1, w), lambda i, j: (j, 0, 0)),
                pl.BlockSpec((1, w, LANE), lambda i, j: (j, 0, 0)),
                pl.BlockSpec((1, w, LANE), lambda i, j: (j, 0, 0)),
                pl.BlockSpec((1, 1, LANE), lambda i, j: (j, 0, 0))]
    args = [c, sw['bre'], sw['bim'], sw['ar'], sw['ai'], sw['cre'], sw['cim'], sw['d']]
    if h0 is not None:
        in_specs += [pl.BlockSpec((rows, w), lambda i, j: (i, j))] * 2
        args += list(h0)
    hshape = jax.ShapeDtypeStruct((nt * nl, S5_GROUPS * S5_STATE), F32)
    return pl.pallas_call(
        functools.partial(_s5_scan_kernel, seg=seg, has_h0=h0 is not None), grid=(nt, lt),
        in_specs=in_specs,
        out_specs=(pl.BlockSpec((rows, LANE), lambda i, j: (i, j)),
                   pl.BlockSpec((nl, w), lambda i, j: (i, j)), pl.BlockSpec((nl, w), lambda i, j: (i, j))),
        out_shape=(jax.ShapeDtypeStruct((n, MIX), F32), hshape, hshape),
        scratch_shapes=[pltpu.VMEM((min(_S5_CH_ROWS, rows) + rows, w), F32)] * 2,
        compiler_params=_cparams(("parallel", "arbitrary")), name="s5_scan")(*args)


def _s5_weights(lp):
    abr, abi, bbr, bbi = _s5_disc(lp['s5_a_re'], lp['s5_a_im'], lp['s5_log_step'], lp['s5_b_re'], lp['s5_b_im'])
    gl = LANE // S5_CH
    lt = S5_GROUPS // gl
    eye = jnp.eye(gl, dtype=F32)

    def bdiag(bb):
        return jnp.einsum('jgcn,gh->jgchn', bb.reshape(lt, gl, S5_CH, S5_STATE), eye).reshape(
            lt, gl * S5_CH, gl * S5_STATE)

    def cdiag(cc):
        return jnp.einsum('jgcn,gh->jgnhc', cc.reshape(lt, gl, S5_CH, S5_STATE), eye).reshape(
            lt, gl * S5_STATE, gl * S5_CH)

    return dict(bre=bdiag(bbr), bim=bdiag(bbi),
                ar=abr[::S5_CH].reshape(lt, 1, gl * S5_STATE), ai=abi[::S5_CH].reshape(lt, 1, gl * S5_STATE),
                cre=cdiag(lp['s5_c_re']), cim=cdiag(lp['s5_c_im']), d=lp['s5_d'].reshape(lt, 1, LANE))


def _glu_kernel(y_ref, w_ref, o_ref):
    y = y_ref[...]
    g = 0.5 * y * (1.0 + jnp.tanh(np.float32(np.sqrt(2.0 / np.pi)) * (y + np.float32(0.044715) * (y * y * y))))
    z = jnp.dot(g.astype(BF16), w_ref[...], preferred_element_type=F32)
    o_ref[...] = z[:, :MIX] * _sigmoid(z[:, MIX:])


def _glu(y, w_l, tm):
    w, l = w_l
    n = y.shape[0]
    return pl.pallas_call(
        _glu_kernel, grid=(n // tm,),
        in_specs=[pl.BlockSpec((tm, MIX), lambda i: (i, 0)),
                  pl.BlockSpec((None, MIX, 2 * MIX), lambda i: (l, 0, 0))],
        out_specs=pl.BlockSpec((tm, MIX), lambda i: (i, 0)),
        out_shape=jax.ShapeDtypeStruct((n, MIX), F32),
        compiler_params=_cparams(("parallel",)), name="s5_glu")(y, w)


def _layer_norm(x, g, b):
    mu = jnp.mean(x, axis=-1, keepdims=True)
    xc = x - mu
    var = jnp.mean(xc * xc, axis=-1, keepdims=True)
    return xc * lax.rsqrt(var + LN_EPS) * g + b


def _merge_kernel(o0_ref, o1_ref, o2_ref, o3_ref, mg_ref, wb_ref, x_ref, wo_ref, g_ref, b_ref, out_ref, acc_ref,
                  *, alpha):
    i = pl.program_id(1)

    @pl.when(i == 0)
    def _():
        acc_ref[...] = jnp.zeros_like(acc_ref)

    for k, o_ref in enumerate((o0_ref, o1_ref, o2_ref, o3_ref)):
        @pl.when(i == k)
        def _(o_ref=o_ref):
            proj = jnp.dot(o_ref[...].astype(BF16), wb_ref[0], preferred_element_type=F32)
            acc_ref[...] += _sigmoid(mg_ref[...]) * proj

    @pl.when(i == N_BRANCH - 1)
    def _():
        mixed = jnp.dot(acc_ref[...].astype(BF16), wo_ref[...], preferred_element_type=F32)
        out_ref[...] = _layer_norm(alpha * x_ref[...] + mixed, g_ref[...], b_ref[...])


def _merge(outs, c, x, wb_l, wo_l, g, b, tm, alpha):
    (wb, l), wo = wb_l, wo_l[0]
    n = x.shape[0]
    o_spec = pl.BlockSpec((tm, MIX), lambda r, i: (r, 0))
    return pl.pallas_call(
        functools.partial(_merge_kernel, alpha=alpha), grid=(n // tm, N_BRANCH),
        in_specs=[o_spec, o_spec, o_spec, o_spec,
                  pl.BlockSpec((tm, D_MODEL), lambda r, i: (r, i)),
                  pl.BlockSpec((None, 1, MIX, D_MODEL), lambda r, i: (l, i, 0, 0)),
                  pl.BlockSpec((tm, D_MODEL), lambda r, i: (r, 0)),
                  pl.BlockSpec((None, D_MODEL, D_MODEL), lambda r, i: (l, 0, 0), pipeline_mode=pl.Buffered(1)),
                  pl.BlockSpec((1, D_MODEL), lambda r, i: (0, 0)),
                  pl.BlockSpec((1, D_MODEL), lambda r, i: (0, 0))],
        out_specs=pl.BlockSpec((tm, D_MODEL), lambda r, i: (r, 0)),
        out_shape=jax.ShapeDtypeStruct((n, D_MODEL), F32),
        scratch_shapes=[pltpu.VMEM((tm, D_MODEL), F32)],
        compiler_params=_cparams(("parallel", "arbitrary")), name="merge_out_ln")(
            *outs, c, wb, x, wo, g, b)


def _ffn_kernel(*refs, halo, seg, alpha):
    (h_ref, wg_ref, wv_ref, cwg_ref, cwv_ref, cbg_ref, cbv_ref, wd_ref, lg_ref, lb_ref) = refs[:10]
    pg_ref, pv_ref, out_ref, hb_ref, acc_ref = refs[10:]
    j = pl.program_id(1)

    @pl.when(j == 0)
    def _():
        hb_ref[...] = h_ref[...].astype(BF16)
        acc_ref[...] = jnp.zeros_like(acc_ref)

    hb = hb_ref[...]
    tm = hb.shape[0]
    tf = wg_ref.shape[1]
    rid = lax.broadcasted_iota(jnp.int32, (tm, 1), 0)
    rid8 = lax.broadcasted_iota(jnp.int32, (SUBLANE, 1), 0)

    def conv(u, cw, cb, prev):
        r1 = pltpu.roll(u, 1, 0)
        r2 = pltpu.roll(u, 2, 0)
        if halo:
            p6, p7 = prev[SUBLANE - 2:SUBLANE - 1], prev[SUBLANE - 1:SUBLANE]
            top1 = jnp.where(rid8 == 0, p7, r1[:SUBLANE])
            top2 = jnp.where(rid8 == 0, p6, jnp.where(rid8 == 1, p7, r2[:SUBLANE]))
            u1 = jnp.concatenate([top1, r1[SUBLANE:]], axis=0)
            u2 = jnp.concatenate([top2, r2[SUBLANE:]], axis=0)
        else:
            t = rid % seg
            state = jnp.where(t >= seg - (CONV_W - 1), prev, u)
            u1 = jnp.where(t >= 1, r1, pltpu.roll(state, tm - (seg - 1), 0))
            u2 = jnp.where(t >= 2, r2, pltpu.roll(state, tm - (seg - 2), 0))
        return cb + (cw[0:1] * u2 + cw[1:2] * u1 + cw[2:3] * u)

    ug = jnp.dot(hb, wg_ref[...], preferred_element_type=F32)
    uv = jnp.dot(hb, wv_ref[...], preferred_element_type=F32)
    gate = conv(ug, cwg_ref[...], cbg_ref[...], pg_ref[0] if halo else pg_ref[...])
    val = conv(uv, cwv_ref[...], cbv_ref[...], pv_ref[0] if halo else pv_ref[...])
    act = (gate * _sigmoid(gate) * val).astype(BF16)
    acc_ref[...] += jnp.dot(act, wd_ref[...], preferred_element_type=F32)

    @pl.when(j == pl.num_programs(1) - 1)
    def _():
        out_ref[...] = _layer_norm(alpha * h_ref[...] + acc_ref[...], lg_ref[...], lb_ref[...])


def _ffn(h, lw, tm, tf, alpha, prev=None, state_rows=None, seg=None):
    n = h.shape[0]
    nf = D_FF // tf
    halo = prev is not None
    (w_up, l), w_down = lw['w_up'], lw['w_down'][0]
    in_specs = [pl.BlockSpec((tm, D_MODEL), lambda r, j: (r, 0), pipeline_mode=pl.Buffered(1)),
                pl.BlockSpec((None, D_MODEL, tf), lambda r, j: (l, 0, j)),
                pl.BlockSpec((None, D_MODEL, tf), lambda r, j: (l, 0, nf + j)),
                pl.BlockSpec((CONV_W, tf), lambda r, j: (0, j)),
                pl.BlockSpec((CONV_W, tf), lambda r, j: (0, nf + j)),
                pl.BlockSpec((1, tf), lambda r, j: (0, j)),
                pl.BlockSpec((1, tf), lambda r, j: (0, nf + j)),
                pl.BlockSpec((None, tf, D_MODEL), lambda r, j: (l, j, 0)),
                pl.BlockSpec((1, D_MODEL), lambda r, j: (0, 0)),
                pl.BlockSpec((1, D_MODEL), lambda r, j: (0, 0))]
    args = [h, w_up, w_up, lw['conv_w'], lw['conv_w'], lw['conv_b'], lw['conv_b'], w_down,
            lw['ln2_g'], lw['ln2_b']]
    if halo:
        in_specs += [pl.BlockSpec((1, SUBLANE, tf), lambda r, j: (r, 0, j)),
                     pl.BlockSpec((1, SUBLANE, tf), lambda r, j: (r, 0, nf + j))]
        args += [prev, prev]
    else:
        in_specs += [pl.BlockSpec((tm, tf), lambda r, j: (r, j)), pl.BlockSpec((tm, tf), lambda r, j: (r, nf + j))]
        args += [state_rows, state_rows]
    return pl.pallas_call(
        functools.partial(_ffn_kernel, halo=halo, seg=seg, alpha=alpha), grid=(n // tm, nf),
        in_specs=in_specs,
        out_specs=pl.BlockSpec((tm, D_MODEL), lambda r, j: (r, 0), pipeline_mode=pl.Buffered(1)),
        out_shape=jax.ShapeDtypeStruct((n, D_MODEL), F32),
        scratch_shapes=[pltpu.VMEM((tm, D_MODEL), BF16), pltpu.VMEM((tm, D_MODEL), F32)],
        compiler_params=_cparams(("parallel", "arbitrary")), name="conv_ffn_ln")(*args)


def _prep_shared(p):
    return dict(w_in=_pack_w_in(p['w_in']), w_glu=p['s5_w_glu'].astype(BF16), w_branch=p['w_branch'].astype(BF16),
                w_out=p['w_out'].astype(BF16), w_up=p['ffn_w_up'].astype(BF16), w_down=p['ffn_w_down'].astype(BF16))


def _prep_layer(l, p, shared):
    lp = {k: v[l] for k, v in p.items() if k not in ('w_in', 's5_w_glu', 'w_branch', 'w_out', 'ffn_w_up', 'ffn_w_down')}
    lw = {k: (v, l) for k, v in shared.items()}
    lw.update(
        fox_b=jnp.zeros((1, LANE), F32).at[0, :N_HEADS].set(lp['fox_b_f']),
        cmp=_cmp_weights(lp['nsa_cmp_pos'], lp['nsa_cmp_wk'], lp['nsa_cmp_wv']),
        s5=_s5_weights(lp),
        ln1_g=lp['ln1_g'].reshape(1, -1), ln1_b=lp['ln1_b'].reshape(1, -1),
        conv_w=lp['ffn_conv_w'], conv_b=lp['ffn_conv_b'].reshape(1, -1),
        ln2_g=lp['ln2_g'].reshape(1, -1), ln2_b=lp['ln2_b'].reshape(1, -1))
    return lw


def _prompt_layer(x, b, t, lw, alpha):
    n = b * t
    kw = 2 * KV_G * HEAD_DIM
    tr = min(1024, n)
    c = _matmul(x, lw['w_in'], tr, 1536, w_rows_are_outputs=True)
    logf = _logf(c, lw['fox_b'], tr)
    cumr = _cumsum_prompt(logf.reshape(b, t, N_HEADS).transpose(0, 2, 1))
    o_fox = _fox_prompt(c, cumr[:, :, None, :], b, t, 512)
    o_moba = _moba_prompt(c, b, t, 2 * MOBA_BLOCK)
    kvc = _cmp_prompt(c, b, t, lw['cmp'])
    nchunk = t // CMP_STRIDE
    ov = _overlap_matrix(nchunk - CMP_LEN // CMP_STRIDE + 1, -(-t // SLC_BLOCK))
    o_nsa = _nsa_prompt(c, kvc, ov, b, t, 256)
    y_s5, hlr, hli = _s5_scan(c, lw['s5'], t, t)
    o_s5 = _glu(y_s5, lw['w_glu'], tr)
    h = _merge((o_nsa, o_s5, o_moba, o_fox), c, x, lw['w_branch'], lw['w_out'], lw['ln1_g'], lw['ln1_b'], 512, alpha)

    tm = min(1024, t)
    nt = n // tm
    edge = h.reshape(nt, tm, D_MODEL)[:, tm - (CONV_W - 1):].reshape(nt * (CONV_W - 1), D_MODEL)
    edge = _pad_rows(edge, -(-edge.shape[0] // SUBLANE) * SUBLANE)
    u_edge = _matmul(edge, lw['w_up'], edge.shape[0], 2 * D_FF // 8)[:nt * (CONV_W - 1)].reshape(nt, CONV_W - 1, 2 * D_FF)
    per_seq = t // tm
    conv_state = u_edge[per_seq - 1::per_seq]
    starts_seq = (jnp.arange(nt) % per_seq == 0)[:, None, None]
    prev = jnp.where(starts_seq, 0.0, jnp.roll(u_edge, 1, axis=0))
    prev = jnp.pad(prev, ((0, 0), (SUBLANE - (CONV_W - 1), 0), (0, 0)))
    y = _ffn(h, lw, tm, 512, alpha, prev=prev)

    win_rows = min(WINDOW, t)
    states = (
        _kv_state(c, b, t, _COL['kv_cmp'], kw),
        _kv_state(c, b, t, _COL['kv_slc'], kw),
        _kv_state(c, b, t, _COL['moba'] + MIX, 2 * MIX),
        _kv_state(c, b, t, _COL['fox'] + MIX, 2 * MIX),
        logf.reshape(b, t, N_HEADS),
        _kv_state(c, b, t, _COL['kv_win'], kw, t_from=t - win_rows),
        hlr.reshape(b, SUBLANE, S5_GROUPS, S5_STATE)[:, SUBLANE - 1],
        hli.reshape(b, SUBLANE, S5_GROUPS, S5_STATE)[:, SUBLANE - 1],
        conv_state)
    return y, states


def _sample_layer(x, bs, t_real, lw, alpha, past, page_table, past_len):
    n = bs * T_PAD
    kw = 2 * KV_G * HEAD_DIM
    c = _matmul(x, lw['w_in'], n, 1536, w_rows_are_outputs=True)
    c3 = c.reshape(bs, T_PAD, WP)
    logf = _logf(c, lw['fox_b'], n)
    logf3 = logf.reshape(bs, T_PAD, N_HEADS)
    tmask = (jnp.arange(T_PAD) < t_real)[None, :, None]
    new_t = jnp.pad(jnp.where(tmask, logf3, 0.0).transpose(0, 2, 1), ((0, 0), (0, 0), (0, LANE - T_PAD)))
    first = past['first_page']
    npages = page_table.shape[1]
    o_fox = _paged_mha_dec(
        _fox_dec_kernel, "fox_decode", c3, _COL['fox'], page_table, past['fox'], first,
        [new_t] + [past['fox_logf']] * npages,
        [pl.BlockSpec((1, N_HEADS, LANE), lambda b, pt: (b, 0, 0))] + _page_specs(past['fox_logf'], npages, first),
        past_len)
    o_moba = _paged_mha_dec(_moba_dec_kernel, "moba_decode", c3, _COL['moba'], page_table, past['moba'], first,
                            [], [], past_len)
    kvc = _cmp_paged(page_table, past['nsa_cmp'], first, lw['cmp'])
    nchunk = kvc.shape[1]
    ov = _overlap_matrix(nchunk - CMP_LEN // CMP_STRIDE + 1, -(-(past_len + t_real) // SLC_BLOCK))
    o_nsa = _nsa_dec(c3, kvc, ov, past['nsa_win'], past['first_seq'], page_table, past['nsa_slc'], first,
                     past_len, t_real)
    h0 = [jnp.pad(s.reshape(bs, 1, -1), ((0, 0), (0, T_PAD - 1), (0, 0))).reshape(n, -1) for s in past['s5']]
    y_s5, hlr, hli = _s5_scan(c, lw['s5'], n, T_PAD, h0=h0)
    o_s5 = _glu(y_s5, lw['w_glu'], n)
    h = _merge((o_nsa.reshape(n, MIX), o_s5, o_moba.reshape(n, MIX), o_fox.reshape(n, MIX)), c, x,
               lw['w_branch'], lw['w_out'], lw['ln1_g'], lw['ln1_b'], min(512, n), alpha)

    buf = past['ffn_conv']
    state_rows = jnp.pad(buf, ((0, 0), (T_PAD - (CONV_W - 1), 0), (0, 0))).reshape(n, 2 * D_FF)
    y = _ffn(h, lw, min(512, n), 512, alpha, state_rows=state_rows, seg=T_PAD)
    last2 = h.reshape(bs, T_PAD, D_MODEL)[:, t_real - (CONV_W - 1):t_real].reshape(bs * (CONV_W - 1), D_MODEL)
    conv_state = _matmul(last2, lw['w_up'], last2.shape[0], 2 * D_FF // 8).reshape(bs, CONV_W - 1, 2 * D_FF)

    tr = lambda a: a[:, :t_real]
    kv_win_new = tr(c3[:, :, _COL['kv_win']:_COL['kv_win'] + kw])
    states = (
        tr(c3[:, :, _COL['kv_cmp']:_COL['kv_cmp'] + kw]).reshape(bs, t_real, 2, KV_G, HEAD_DIM),
        tr(c3[:, :, _COL['kv_slc']:_COL['kv_slc'] + kw]).reshape(bs, t_real, 2, KV_G, HEAD_DIM),
        tr(c3[:, :, _COL['moba'] + MIX:_COL['moba'] + 3 * MIX]).reshape(bs, t_real, 2, N_HEADS, HEAD_DIM),
        tr(c3[:, :, _COL['fox'] + MIX:_COL['fox'] + 3 * MIX]).reshape(bs, t_real, 2, N_HEADS, HEAD_DIM),
        tr(logf3),
        kv_win_new,
        hlr.reshape(bs, T_PAD, S5_GROUPS, S5_STATE)[:, t_real - 1],
        hli.reshape(bs, T_PAD, S5_GROUPS, S5_STATE)[:, t_real - 1],
        conv_state)
    return y, states


def kernel(x_prompt, x_sample, cache_nsa_cmp_kv, cache_nsa_slc_kv, cache_moba_kv, cache_fox_kv, cache_fox_logf,
           page_table, cache_nsa_win_kv, state_s5_re, state_s5_im, state_ffn_conv, w_in, fox_b_f, nsa_cmp_pos,
           nsa_cmp_wk, nsa_cmp_wv, s5_a_re, s5_a_im, s5_b_re, s5_b_im, s5_c_re, s5_c_im, s5_d, s5_log_step,
           s5_w_glu, w_branch, w_out, ln1_g, ln1_b, ffn_w_up, ffn_conv_w, ffn_conv_b, ffn_w_down, ln2_g, ln2_b):
    depth = w_in.shape[0]
    b, t, d = x_prompt.shape
    bs, ts, _ = x_sample.shape
    n_phys, page = cache_nsa_cmp_kv.shape[1:3]
    past_len = page_table.shape[1] * page
    assert d == D_MODEL and w_in.shape[2] == IN_WIDTH and ffn_w_down.shape[1] == D_FF
    assert ts <= T_PAD - (CONV_W - 1) and past_len % MOBA_BLOCK == 0 and page == LANE and (past_len + ts) // CMP_STRIDE * CMP_STRIDE <= past_len
    alpha = float((2 * depth) ** 0.25)
    params = dict(w_in=w_in, fox_b_f=fox_b_f, nsa_cmp_pos=nsa_cmp_pos, nsa_cmp_wk=nsa_cmp_wk, nsa_cmp_wv=nsa_cmp_wv,
                  s5_a_re=s5_a_re, s5_a_im=s5_a_im, s5_b_re=s5_b_re, s5_b_im=s5_b_im, s5_c_re=s5_c_re,
                  s5_c_im=s5_c_im, s5_d=s5_d, s5_log_step=s5_log_step, s5_w_glu=s5_w_glu, w_branch=w_branch,
                  w_out=w_out, ln1_g=ln1_g, ln1_b=ln1_b, ffn_w_up=ffn_w_up, ffn_conv_w=ffn_conv_w,
                  ffn_conv_b=ffn_conv_b, ffn_w_down=ffn_w_down, ln2_g=ln2_g, ln2_b=ln2_b)
    kw = 2 * KV_G * HEAD_DIM
    yp = x_prompt.reshape(b * t, d)
    ys = jnp.pad(x_sample, ((0, 0), (0, T_PAD - ts), (0, 0))).reshape(bs * T_PAD, d)
    st_p, st_s = [], []

    def feature_major(cache):
        dd, nn, rr = cache.shape[:3]
        return cache.transpose(0, 1, 3, 4, 5, 2).reshape(dd * nn, -1, rr)

    cmp_fm, slc_fm, moba_fm, fox_fm, win_fm = (
        feature_major(a) for a in (cache_nsa_cmp_kv, cache_nsa_slc_kv, cache_moba_kv, cache_fox_kv, cache_nsa_win_kv))
    logf_hm = cache_fox_logf.transpose(0, 1, 3, 2).reshape(depth * n_phys, N_HEADS, page)
    shared = _prep_shared(params)
    for l in range(depth):
        lw = _prep_layer(l, params, shared)
        past = dict(
            nsa_cmp=cmp_fm, nsa_slc=slc_fm, moba=moba_fm, fox=fox_fm, fox_logf=logf_hm, nsa_win=win_fm,
            first_page=l * n_phys, first_seq=l * bs,
            s5=(state_s5_re[l], state_s5_im[l]),
            ffn_conv=state_ffn_conv[l])
        yp, sp = _prompt_layer(yp, b, t, lw, alpha)
        ys, ss = _sample_layer(ys, bs, ts, lw, alpha, past, page_table, past_len)
        st_p.append(sp)
        st_s.append(ss)
    sp = [jnp.stack(z) for z in zip(*st_p)]
    ss = [jnp.stack(z) for z in zip(*st_s)]
    wb = win_fm.shape[2]
    full_win = jnp.concatenate([win_fm.reshape(depth, bs, kw, wb), ss[5].transpose(0, 1, 3, 2)], axis=3)
    keep = min(WINDOW, wb + ts)
    ss[5] = full_win[..., wb + ts - keep:].transpose(0, 1, 3, 2).reshape(depth, bs, keep, 2, KV_G, HEAD_DIM)
    out = [yp.reshape(b, t, d), ys.reshape(bs, T_PAD, d)[:, :ts]]
    for a, c in zip(sp, ss):
        out += [a, c]
    return tuple(out)
```

```python
import functools

import numpy as np
import jax
import jax.numpy as jnp
from jax import lax
from jax.experimental import pallas as pl
from jax.experimental.pallas import tpu as pltpu

F32 = jnp.float32
BF16 = jnp.bfloat16
HI = lax.Precision.HIGHEST

LANE = 128
SUBLANE = 8
VMEM_LIMIT = 56 * 1024 * 1024

D_MODEL = 2048
HEAD_DIM = 64
N_BRANCH = 4
MIX = D_MODEL // N_BRANCH
N_HEADS = MIX // HEAD_DIM
KV_G = 2
HG = N_HEADS // KV_G
CMP_LEN = 32
CMP_STRIDE = 16
SLC_BLOCK = 64
SLC_TOPN = 16
WINDOW = 512
MOBA_BLOCK = 256
MOBA_TOPK = 3
S5_CH = 16
S5_GROUPS = MIX // S5_CH
S5_STATE = 64
D_FF = 5632
CONV_W = 3
LN_EPS = 1e-5
SCALE = HEAD_DIM ** -0.5
NEG = -1e30
FORCE = 1e4
T_PAD = 8
SLOPES = tuple(float(v) for v in np.asarray(2.0 ** (-8.0 * np.arange(1, N_HEADS + 1) / N_HEADS), np.float32))

_SPLITS = (('nsa_q', MIX), ('kv_cmp', 2 * KV_G * HEAD_DIM), ('kv_slc', 2 * KV_G * HEAD_DIM),
           ('kv_win', 2 * KV_G * HEAD_DIM), ('nsa_gate', 3 * N_HEADS), ('s5_u', MIX),
           ('moba', 3 * MIX), ('fox', 3 * MIX), ('fox_f', N_HEADS), ('merge', N_BRANCH * D_MODEL))
_SRC = {}
_o = 0
for _n, _w in _SPLITS:
    _SRC[_n] = (_o, _w)
    _o += _w
IN_WIDTH = _o
_COL = dict(merge=0, nsa_q=8192, kv_cmp=8704, kv_slc=8960, kv_win=9216, nsa_gate=9472, s5_u=9600,
            moba=10240, fox=11776, fox_f=13312)
WP = 13824


def _cparams(sem):
    return pltpu.CompilerParams(dimension_semantics=sem, vmem_limit_bytes=VMEM_LIMIT)


def _dot_nt(a, b, precision=None):
    return lax.dot_general(a, b, (((1,), (1,)), ((), ())), precision=precision, preferred_element_type=F32)


def _sigmoid(x):
    return 1.0 / (1.0 + jnp.exp(-x))


def _pack_w_in(w):
    wt = w.transpose(0, 2, 1)
    depth, _, d = wt.shape
    order = ('merge', 'nsa_q', 'kv_cmp', 'kv_slc', 'kv_win', 'nsa_gate', 's5_u', 'moba', 'fox', 'fox_f')
    parts, pos = [], 0
    for name in order:
        if _COL[name] > pos:
            parts.append(jnp.zeros((depth, _COL[name] - pos, d), w.dtype))
        s, wd = _SRC[name]
        parts.append(wt[:, s:s + wd])
        pos = _COL[name] + wd
    parts.append(jnp.zeros((depth, WP - pos, d), w.dtype))
    return jnp.concatenate(parts, axis=1).astype(BF16)


def _mm_kernel(x_ref, w_ref, o_ref, xb_ref, *, w_rows_are_outputs):
    @pl.when(pl.program_id(1) == 0)
    def _():
        xb_ref[...] = x_ref[...].astype(BF16)

    if w_rows_are_outputs:
        o_ref[...] = _dot_nt(xb_ref[...], w_ref[...])
    else:
        o_ref[...] = jnp.dot(xb_ref[...], w_ref[...], preferred_element_type=F32)


def _matmul(x, w_l, tm, tn, w_rows_are_outputs=False):
    w, l = w_l
    m, k = x.shape
    if w_rows_are_outputs:
        n = w.shape[1]
        w_spec = pl.BlockSpec((None, tn, k), lambda i, j: (l, j, 0))
    else:
        n = w.shape[2]
        w_spec = pl.BlockSpec((None, k, tn), lambda i, j: (l, 0, j))
    return pl.pallas_call(
        functools.partial(_mm_kernel, w_rows_are_outputs=w_rows_are_outputs), grid=(m // tm, n // tn),
        in_specs=[pl.BlockSpec((tm, k), lambda i, j: (i, 0)), w_spec],
        out_specs=pl.BlockSpec((tm, tn), lambda i, j: (i, j)),
        out_shape=jax.ShapeDtypeStruct((m, n), F32),
        scratch_shapes=[pltpu.VMEM((tm, k), BF16)],
        compiler_params=_cparams(("parallel", "arbitrary")), name="mm")(x, w)


def _transpose_kernel(c_ref, o_ref):
    o_ref[...] = c_ref[...].T


def _kv_state(c, b, t, col, width, t_from=0):
    tf = min(512, width)
    tt = 512
    nt, n0 = t // tt, t_from // tt
    st = pl.pallas_call(
        _transpose_kernel, grid=(b, width // tf, nt - n0),
        in_specs=[pl.BlockSpec((tt, tf), lambda i, f, q: (i * nt + n0 + q, col // tf + f))],
        out_specs=pl.BlockSpec((None, tf, tt), lambda i, f, q: (i, f, q)),
        out_shape=jax.ShapeDtypeStruct((b, width, t - t_from), F32),
        compiler_params=_cparams(("parallel", "parallel", "parallel")), name="kv_state")(c)
    return st.reshape(b, 2, width // (2 * HEAD_DIM), HEAD_DIM, t - t_from).transpose(0, 4, 1, 2, 3)


def _logf_kernel(c_ref, b_ref, o_ref):
    x = c_ref[...] + b_ref[...]
    y = jnp.minimum(x, 0.0) - jnp.log1p(jnp.exp(-jnp.abs(x)))
    o_ref[...] = y[:, :N_HEADS]


def _logf(c, b_pad, tm):
    n = c.shape[0]
    return pl.pallas_call(
        _logf_kernel, grid=(n // tm,),
        in_specs=[pl.BlockSpec((tm, LANE), lambda i: (i, _COL['fox_f'] // LANE)),
                  pl.BlockSpec((1, LANE), lambda i: (0, 0))],
        out_specs=pl.BlockSpec((tm, N_HEADS), lambda i: (i, 0)),
        out_shape=jax.ShapeDtypeStruct((n, N_HEADS), F32),
        compiler_params=_cparams(("parallel",)), name="logf")(c, b_pad)


def _cumsum_kernel(*refs, n_in):
    in_refs, o_ref = refs[-n_in - 1:-1], refs[-1]
    r = lax.broadcasted_iota(jnp.int32, (LANE, LANE), 0)
    c = lax.broadcasted_iota(jnp.int32, (LANE, LANE), 1)
    tri = jnp.where(r <= c, 1.0, 0.0).astype(F32)
    carry = jnp.zeros((N_HEADS, 1), F32)
    off = 0
    for ref in in_refs:
        for j in range(ref.shape[-1] // LANE):
            x = ref[0, :, j * LANE:(j + 1) * LANE]
            cs = jnp.dot(x, tri, precision=HI, preferred_element_type=F32) + carry
            o_ref[0, :, off:off + LANE] = cs
            carry = cs[:, LANE - 1:LANE]
            off += LANE


def _cumsum_prompt(logf_t):
    b, h, t = logf_t.shape
    return pl.pallas_call(
        functools.partial(_cumsum_kernel, n_in=1), grid=(b,),
        in_specs=[pl.BlockSpec((1, h, t), lambda i: (i, 0, 0))],
        out_specs=pl.BlockSpec((1, h, t), lambda i: (i, 0, 0)),
        out_shape=jax.ShapeDtypeStruct((b, h, t), F32),
        compiler_params=_cparams(("parallel",)), name="fox_cumsum_prompt")(logf_t)


M_FLOOR = -1e29


def _softmax_step(m, l, s):
    m_new = jnp.maximum(m, jnp.max(s, axis=1, keepdims=True))
    p = jnp.exp(s - m_new)
    alpha = jnp.exp(m - m_new)
    return m_new, alpha * l + jnp.sum(p, axis=1, keepdims=True), alpha, p.astype(BF16)


def _pipelined_attention(chains, lo, hi, score_fn, value_fn, rows, tk, dv, last_fix=None):
    def flush(c, kj, s, p_prev, a_prev, acc):
        prev = jnp.maximum(kj - 1, lo)
        return a_prev * acc + jnp.dot(p_prev, value_fn(c, prev), preferred_element_type=F32)

    def body(kj, carry):
        out = []
        for c in range(chains):
            s, p_prev, a_prev, m, l, acc = carry[c]
            acc = flush(c, kj, s, p_prev, a_prev, acc)
            m, l, alpha, p = _softmax_step(m, l, s)
            out.append((score_fn(c, kj + 1), p, alpha, m, l, acc))
        return tuple(out)

    init = tuple((score_fn(c, lo), jnp.zeros((rows, tk), BF16), jnp.ones((rows, 1), F32),
                  jnp.full((rows, 1), M_FLOOR, F32), jnp.zeros((rows, 1), F32), jnp.zeros((rows, dv), F32))
                 for c in range(chains))
    carry = lax.fori_loop(lo, hi, body, init)
    outs = []
    for c in range(chains):
        s, p_prev, a_prev, m, l, acc = carry[c]
        acc = flush(c, hi, s, p_prev, a_prev, acc)
        if last_fix is not None:
            s = last_fix(c, s)
        m, l, alpha, p = _softmax_step(m, l, s)
        acc = alpha * acc + jnp.dot(p, value_fn(c, hi), preferred_element_type=F32)
        outs.append(acc / jnp.maximum(l, 1e-30))
    return outs


def _attend_tiles(s_tiles, v_tiles):
    m = functools.reduce(jnp.maximum, [jnp.max(s, axis=1, keepdims=True) for s in s_tiles])
    l, acc = 0.0, 0.0
    for s, (v, feature_major) in zip(s_tiles, v_tiles):
        e = jnp.where(s > 0.5 * NEG, jnp.exp(s - m), 0.0)
        l = l + jnp.sum(e, axis=1, keepdims=True)
        eb = e.astype(BF16)
        acc = acc + (_dot_nt(eb, v) if feature_major else jnp.dot(eb, v, preferred_element_type=F32))
    return acc / jnp.maximum(l, 1e-30)


def _topn_mask_wide(v, ncols, topn):
    rows = v.shape[0]
    nc8 = -(-ncols // SUBLANE) * SUBLANE
    vt = v.T[:nc8, :]
    jr = lax.broadcasted_iota(jnp.int32, vt.shape, 0)
    rank = jnp.zeros(vt.shape, F32)
    for j2 in range(ncols):
        cand = vt[j2:j2 + 1, :]
        beats = (cand > vt) | ((cand == vt) & (jr > j2))
        rank = rank + jnp.where(beats, 1.0, 0.0)
    top = jnp.where(rank < topn, 1.0, 0.0)
    return _pad_rows(top, v.shape[1]).T > 0.5


def _topn_mask(v, ncols, topn):
    jl = lax.broadcasted_iota(jnp.int32, v.shape, 1)
    rank = jnp.zeros(v.shape, F32)
    for j2 in range(ncols):
        col = v[:, j2:j2 + 1]
        beats = (col > v) | ((col == v) & (jl > j2))
        rank = rank + jnp.where(beats, 1.0, 0.0)
    return rank < topn


def _pad_rows(a, rows):
    return jnp.concatenate([a, jnp.zeros((rows - a.shape[0], a.shape[1]), a.dtype)], axis=0)


def _fox_prompt_kernel(q_ref, k_ref, v_ref, cr_ref, o_ref, *, tq):
    qi = pl.program_id(2)
    q0 = pl.multiple_of(qi * tq, tq)
    row = lax.broadcasted_iota(jnp.int32, (tq, tq), 0)
    col = lax.broadcasted_iota(jnp.int32, (tq, tq), 1)
    qs = [(q_ref[:, h2 * HEAD_DIM:(h2 + 1) * HEAD_DIM] * SCALE).astype(BF16) for h2 in range(2)]
    c0 = [cr_ref[0, h2, :, pl.ds(q0, LANE)][:, 0:1] for h2 in range(2)]

    def score(h2, kj):
        ks = pl.multiple_of(kj * tq, tq)
        k = k_ref[pl.ds(ks, tq), h2 * HEAD_DIM:(h2 + 1) * HEAD_DIM].astype(BF16)
        return _dot_nt(qs[h2], k) + (c0[h2] - cr_ref[0, h2, :, pl.ds(ks, tq)])

    def value(h2, kj):
        ks = pl.multiple_of(kj * tq, tq)
        return v_ref[pl.ds(ks, tq), h2 * HEAD_DIM:(h2 + 1) * HEAD_DIM].astype(BF16)

    outs = _pipelined_attention(2, 0, qi, score, value, tq, tq, HEAD_DIM,
                                last_fix=lambda h2, s: jnp.where(col <= row, s, NEG))
    o_ref[...] = jnp.concatenate(outs, axis=1)


def _fox_prompt(c, cumr, b, t, tq):
    nq = t // tq
    base = _COL['fox'] // LANE
    hp_n = N_HEADS // 2
    return pl.pallas_call(
        functools.partial(_fox_prompt_kernel, tq=tq), grid=(b, hp_n, nq),
        in_specs=[pl.BlockSpec((tq, LANE), lambda i, h, q: (i * nq + q, base + h)),
                  pl.BlockSpec((t, LANE), lambda i, h, q: (i, base + hp_n + h)),
                  pl.BlockSpec((t, LANE), lambda i, h, q: (i, base + 2 * hp_n + h)),
                  pl.BlockSpec((1, 2, 1, t), lambda i, h, q: (i, h, 0, 0))],
        out_specs=pl.BlockSpec((tq, LANE), lambda i, h, q: (i * nq + q, h)),
        out_shape=jax.ShapeDtypeStruct((b * t, MIX), F32),
        compiler_params=_cparams(("parallel", "arbitrary", "arbitrary")), name="fox_prompt")(
            c, c, c, cumr)


def _head_slope(hp, h2):
    s = jnp.float32(SLOPES[h2])
    for k in range(1, N_HEADS // 2):
        s = jnp.where(hp == k, jnp.float32(SLOPES[2 * k + h2]), s)
    return s


def _moba_prompt_kernel(q_ref, k_ref, v_ref, o_ref, kmean_ref, *, tq, nb):
    hp = pl.program_id(1)
    qi = pl.program_id(2)
    bpt = tq // MOBA_BLOCK

    @pl.when(qi == 0)
    def _():
        kmean_ref[...] = jnp.zeros_like(kmean_ref)
        for n in range(nb):
            kmean_ref[n:n + 1, :] = jnp.mean(k_ref[n * MOBA_BLOCK:(n + 1) * MOBA_BLOCK, :], axis=0, keepdims=True)

    q0 = pl.multiple_of(qi * tq, tq)
    row = lax.broadcasted_iota(jnp.int32, (tq, tq), 0)
    col = lax.broadcasted_iota(jnp.int32, (tq, tq), 1)
    colpos = lax.broadcasted_iota(jnp.int32, (1, tq), 1)
    jl = lax.broadcasted_iota(jnp.int32, (tq, LANE), 1)
    cur = qi * bpt + lax.broadcasted_iota(jnp.int32, (tq, 1), 0) // MOBA_BLOCK
    qs, slopes, blockbias = [], [], []
    for h2 in range(2):
        lo = h2 * HEAD_DIM
        qf = q_ref[:, lo:lo + HEAD_DIM]
        gate = _dot_nt(qf, kmean_ref[:, lo:lo + HEAD_DIM], precision=HI)
        gate = jnp.where(jl < cur, gate, NEG)
        picked = (_topn_mask_wide(gate, nb, MOBA_TOPK) & (jl < cur)) | (jl == cur)
        qs.append((qf * SCALE).astype(BF16))
        slopes.append(_head_slope(hp, h2))
        blockbias.append(jnp.where(picked, 0.0, NEG))

    def score(h2, kj):
        ks = pl.multiple_of(kj * tq, tq)
        k = k_ref[pl.ds(ks, tq), h2 * HEAD_DIM:(h2 + 1) * HEAD_DIM].astype(BF16)
        bias = None
        for j in reversed(range(bpt)):
            rowbias = jnp.min(jnp.where(jl == kj * bpt + j, blockbias[h2], 0.0), axis=1, keepdims=True)
            bias = rowbias if bias is None else jnp.where(colpos < (j + 1) * MOBA_BLOCK, rowbias, bias)
        return _dot_nt(qs[h2], k) + slopes[h2] * (ks - q0 + colpos).astype(F32) + bias

    def value(h2, kj):
        ks = pl.multiple_of(kj * tq, tq)
        return v_ref[pl.ds(ks, tq), h2 * HEAD_DIM:(h2 + 1) * HEAD_DIM].astype(BF16)

    outs = _pipelined_attention(2, 0, qi, score, value, tq, tq, HEAD_DIM,
                                last_fix=lambda h2, s: jnp.where(col <= row, s, NEG))
    o_ref[...] = jnp.concatenate(outs, axis=1)


def _moba_prompt(c, b, t, tq):
    nq = t // tq
    base = _COL['moba'] // LANE
    hp_n = N_HEADS // 2
    return pl.pallas_call(
        functools.partial(_moba_prompt_kernel, tq=tq, nb=t // MOBA_BLOCK), grid=(b, hp_n, nq),
        in_specs=[pl.BlockSpec((tq, LANE), lambda i, h, q: (i * nq + q, base + h)),
                  pl.BlockSpec((t, LANE), lambda i, h, q: (i, base + hp_n + h)),
                  pl.BlockSpec((t, LANE), lambda i, h, q: (i, base + 2 * hp_n + h))],
        out_specs=pl.BlockSpec((tq, LANE), lambda i, h, q: (i * nq + q, h)),
        out_shape=jax.ShapeDtypeStruct((b * t, MIX), F32),
        scratch_shapes=[pltpu.VMEM((LANE, LANE), F32)],
        compiler_params=_cparams(("parallel", "arbitrary", "arbitrary")), name="moba_prompt")(c, c, c)


def _cmp_kernel(*refs, n_in, feature_major):
    if feature_major:
        xs_ref, refs = refs[-1], refs[:-1]
        x_refs = refs[-5 - n_in:-5]
        page = x_refs[0].shape[2]
        halves = x_refs[0].shape[1] // LANE
        for p, r in enumerate(x_refs):
            for j in range(halves):
                xs_ref[j, p * page:(p + 1) * page, :] = r[0, j * LANE:(j + 1) * LANE, :].T
        nchunk = n_in * page // CMP_STRIDE
        x = jnp.concatenate([xs_ref[j, pl.ds(l, nchunk, stride=CMP_STRIDE), :]
                             for l in range(CMP_STRIDE) for j in range(halves)], axis=1)
    else:
        x = refs[-6][0]
    pos_ref, wcat_ref, w0_ref, w1_ref, o_ref = refs[-5:]
    xb = x.astype(BF16)
    p0 = jnp.dot(xb, w0_ref[...], preferred_element_type=F32)
    p1 = jnp.dot(xb, w1_ref[...], preferred_element_type=F32)
    bias = jnp.dot(pos_ref[...].astype(BF16), wcat_ref[...], preferred_element_type=F32)[0:1]
    o_ref[0] = p0 + pltpu.roll(p1, p1.shape[0] - 1, 0) + bias


def _cmp_weights(pos, wk, wv):
    r = CMP_LEN // CMP_STRIDE
    w = jnp.stack([wk, wv]).reshape(2, r, CMP_STRIDE, HEAD_DIM, HEAD_DIM)
    e2 = jnp.eye(2, dtype=w.dtype)
    eg = jnp.eye(KV_G, dtype=w.dtype)
    big = jnp.einsum('krlde,kK,gG->rlkgdKGe', w, e2, eg)
    big = big.reshape(r, CMP_STRIDE * 2 * KV_G * HEAD_DIM, 2 * KV_G * HEAD_DIM).astype(BF16)
    wcat = jnp.concatenate([wk, wk, wv, wv], axis=1).astype(BF16)
    posb = jnp.zeros((SUBLANE, CMP_LEN * HEAD_DIM), F32).at[0].set(pos.reshape(-1))
    return posb, wcat, big[0], big[1]


def _const_specs(arrays):
    return [pl.BlockSpec(a.shape, lambda *_, nd=a.ndim: (0,) * nd) for a in arrays]


def _cmp_prompt(c, b, t, cw):
    kw = 2 * KV_G * HEAD_DIM
    nchunk = t // CMP_STRIDE
    x = c[:, _COL['kv_cmp']:_COL['kv_cmp'] + kw].reshape(b, nchunk, CMP_STRIDE * kw)
    return pl.pallas_call(
        functools.partial(_cmp_kernel, n_in=1, feature_major=False), grid=(b,),
        in_specs=[pl.BlockSpec((1, nchunk, CMP_STRIDE * kw), lambda i: (i, 0, 0))] + _const_specs(cw),
        out_specs=pl.BlockSpec((1, nchunk, kw), lambda i: (i, 0, 0)),
        out_shape=jax.ShapeDtypeStruct((b, nchunk, kw), F32),
        compiler_params=_cparams(("parallel",)), name="nsa_cmp_prompt")(x, *cw)


def _cmp_paged(page_table, pool, first, cw):
    bs, npages = page_table.shape
    kw, page = pool.shape[1:]
    nchunk = npages * page // CMP_STRIDE
    return pl.pallas_call(
        functools.partial(_cmp_kernel, n_in=npages, feature_major=True),
        grid_spec=pltpu.PrefetchScalarGridSpec(
            num_scalar_prefetch=1, grid=(bs,), in_specs=_page_specs(pool, npages, first) + _const_specs(cw),
            out_specs=pl.BlockSpec((1, nchunk, kw), lambda b, pt: (b, 0, 0)),
            scratch_shapes=[pltpu.VMEM((kw // LANE, npages * page, LANE), F32)]),
        out_shape=jax.ShapeDtypeStruct((bs, nchunk, kw), F32),
        compiler_params=_cparams(("parallel",)), name="nsa_cmp_paged")(page_table, *([pool] * npages), *cw)


def _overlap_matrix(nc, ns):
    i_c = np.arange(LANE)[:, None] * CMP_STRIDE
    j_s = np.arange(LANE)[None, :] * SLC_BLOCK
    ov = (i_c < j_s + SLC_BLOCK) & (i_c + CMP_LEN > j_s)
    ov &= (np.arange(LANE)[:, None] < nc) & (np.arange(LANE)[None, :] < ns)
    return jnp.asarray(ov, F32)


def _nsa_prompt_kernel(q_ref, slc_ref, win_ref, cmp_ref, g_ref, ov_ref, o_ref, *, tq, nc, ns):
    qi = pl.program_id(1)
    q0 = pl.multiple_of(qi * tq, tq)
    rows = HG * tq
    rl = lax.broadcasted_iota(jnp.int32, (tq, 1), 0)
    t1 = q0 + rl
    t4 = jnp.concatenate([t1] * HG, axis=0)
    lane = lax.broadcasted_iota(jnp.int32, (1, LANE), 1)
    colpos = lax.broadcasted_iota(jnp.int32, (1, tq), 1)
    dloc = lax.broadcasted_iota(jnp.int32, (tq, tq), 0) - lax.broadcasted_iota(jnp.int32, (tq, tq), 1)
    sg = _sigmoid(g_ref[...])
    jl = lax.broadcasted_iota(jnp.int32, (tq, LANE), 1)
    cur = t1 // SLC_BLOCK
    erow = lax.broadcasted_iota(jnp.int32, (LANE, tq), 0)
    ecol = lax.broadcasted_iota(jnp.int32, (LANE, tq), 1)
    wtiles = WINDOW // tq
    qs, slope, o_cmp, blockbias = [], [], [], []
    for g in range(KV_G):
        kl = g * HEAD_DIM
        vl = KV_G * HEAD_DIM + g * HEAD_DIM
        qg = jnp.concatenate([q_ref[:, (g * HG + h) * HEAD_DIM:(g * HG + h + 1) * HEAD_DIM] for h in range(HG)],
                             axis=0)
        qs.append((qg * SCALE).astype(BF16))
        slope.append(jnp.concatenate([jnp.full((tq, 1), SLOPES[g * HG + h], F32) for h in range(HG)], axis=0))

        kc = cmp_ref[0, :, kl:kl + HEAD_DIM].astype(BF16)
        vc = cmp_ref[0, :, vl:vl + HEAD_DIM].astype(BF16)
        dist = (t4 - (lane * CMP_STRIDE + CMP_LEN - 1)).astype(F32)
        ok_c = (dist >= 0) & (lane < nc)
        s = jnp.where(ok_c, _dot_nt(qs[g], kc) - slope[g] * dist, NEG)
        m = jnp.max(s, axis=1, keepdims=True)
        e = jnp.where(ok_c, jnp.exp(s - m), 0.0)
        p_c = e / jnp.maximum(jnp.sum(e, axis=1, keepdims=True), 1e-30)
        o_cmp.append(jnp.dot(p_c.astype(BF16), vc, preferred_element_type=F32))

        psum = p_c[0:tq]
        for h in range(1, HG):
            psum = psum + p_c[h * tq:(h + 1) * tq]
        imp = jnp.dot(psum, ov_ref[...], precision=HI, preferred_element_type=F32)
        forced = (jl == 0) | (jl == cur) | (jl == cur - 1)
        imp = jnp.where(forced, FORCE, imp)
        imp = jnp.where(jl <= cur, imp, NEG)
        picked = _topn_mask_wide(imp, ns, min(SLC_TOPN, ns)) & (jl <= cur)
        blockbias.append(jnp.where(picked, 0.0, NEG).astype(BF16))

    halves = 2
    hh_per = HG // halves
    hrows = hh_per * tq

    def scores(ref, c, ks, bias):
        g, r0 = c // halves, (c % halves) * hrows
        k = ref[pl.ds(ks, tq), g * HEAD_DIM:(g + 1) * HEAD_DIM].astype(BF16)
        cpos = (ks - q0 + colpos).astype(F32)
        return (_dot_nt(qs[g][r0:r0 + hrows], k) + slope[g][r0:r0 + hrows] * cpos
                + jnp.concatenate([bias] * hh_per, axis=0))

    def values(ref, c, kj):
        g = c // halves
        ks = pl.multiple_of(kj * tq, tq)
        vl = KV_G * HEAD_DIM + g * HEAD_DIM
        return ref[pl.ds(ks, tq), vl:vl + HEAD_DIM].astype(BF16)

    def slc_score(c, kj):
        ks = pl.multiple_of(kj * tq, tq)
        expand = jnp.where((ks + ecol) // SLC_BLOCK == erow, 1.0, 0.0).astype(BF16)
        return scores(slc_ref, c, ks, jnp.dot(blockbias[c // halves], expand, preferred_element_type=F32))

    def win_score(c, kj):
        ks = pl.multiple_of(kj * tq, tq)
        d = dloc + (q0 - ks)
        return scores(win_ref, c, ks, jnp.where((d >= 0) & (d < WINDOW), 0.0, NEG))

    causal = jnp.concatenate([dloc] * hh_per, axis=0) >= 0
    nchains = KV_G * halves
    o_slc = _pipelined_attention(nchains, 0, qi, slc_score, functools.partial(values, slc_ref), hrows, tq, HEAD_DIM,
                                 last_fix=lambda c, s: jnp.where(causal, s, NEG))
    o_win = _pipelined_attention(nchains, jnp.maximum(qi - wtiles, 0), qi, win_score,
                                 functools.partial(values, win_ref), hrows, tq, HEAD_DIM)

    for g in range(KV_G):
        for h in range(HG):
            hh = g * HG + h
            r0 = h * tq
            c, rc = g * halves + h // hh_per, (h % hh_per) * tq
            o = (sg[:, hh:hh + 1] * o_cmp[g][r0:r0 + tq]
                 + sg[:, N_HEADS + hh:N_HEADS + hh + 1] * o_slc[c][rc:rc + tq]
                 + sg[:, 2 * N_HEADS + hh:2 * N_HEADS + hh + 1] * o_win[c][rc:rc + tq])
            o_ref[:, hh * HEAD_DIM:(hh + 1) * HEAD_DIM] = o


def _nsa_prompt(c, kvc, ov, b, t, tq):
    nq = t // tq
    kw = 2 * KV_G * HEAD_DIM
    nchunk = t // CMP_STRIDE
    nc = nchunk - CMP_LEN // CMP_STRIDE + 1
    ns = -(-t // SLC_BLOCK)
    return pl.pallas_call(
        functools.partial(_nsa_prompt_kernel, tq=tq, nc=nc, ns=ns), grid=(b, nq),
        in_specs=[pl.BlockSpec((tq, MIX), lambda i, q: (i * nq + q, _COL['nsa_q'] // MIX)),
                  pl.BlockSpec((t, kw), lambda i, q: (i, _COL['kv_slc'] // kw)),
                  pl.BlockSpec((t, kw), lambda i, q: (i, _COL['kv_win'] // kw)),
                  pl.BlockSpec((1, nchunk, kw), lambda i, q: (i, 0, 0)),
                  pl.BlockSpec((tq, LANE), lambda i, q: (i * nq + q, _COL['nsa_gate'] // LANE)),
                  pl.BlockSpec((LANE, LANE), lambda i, q: (0, 0))],
        out_specs=pl.BlockSpec((tq, MIX), lambda i, q: (i * nq + q, 0)),
        out_shape=jax.ShapeDtypeStruct((b * t, MIX), F32),
        compiler_params=_cparams(("parallel", "arbitrary")), name="nsa_prompt")(c, c, c, kvc, c, ov)


def _dec_rows():
    rid = lax.broadcasted_iota(jnp.int32, (N_HEADS * T_PAD, 1), 0)
    t8 = rid % T_PAD
    slope = jnp.concatenate([jnp.full((T_PAD, 1), SLOPES[h], F32) for h in range(N_HEADS)], axis=0)
    return t8, slope


def _expand_heads(q8):
    lane = lax.broadcasted_iota(jnp.int32, q8.shape, 1)
    return jnp.concatenate([jnp.where(lane // HEAD_DIM == h, q8, 0.0) for h in range(N_HEADS)], axis=0)


def _collapse_heads(res):
    lane = lax.broadcasted_iota(jnp.int32, (T_PAD, res.shape[1]), 1)
    out = jnp.zeros((T_PAD, res.shape[1]), F32)
    for h in range(N_HEADS):
        out = out + jnp.where(lane // HEAD_DIM == h, res[h * T_PAD:(h + 1) * T_PAD], 0.0)
    return out


def _moba_dec_kernel(pt_ref, q_ref, kn_ref, vn_ref, *rest, npages, past):
    page_refs, o_ref = rest[:npages], rest[npages]
    page = page_refs[0].shape[2]
    t8, slope = _dec_rows()
    qpos = past + t8
    lane = lax.broadcasted_iota(jnp.int32, (1, LANE), 1)
    q8 = q_ref[0]
    qx_f = _expand_heads(q8)
    qx = (qx_f * SCALE).astype(BF16)
    per_blk = MOBA_BLOCK // page
    nb_past = past // MOBA_BLOCK
    lanei = lax.broadcasted_iota(jnp.int32, (MIX, LANE), 1)
    kmean = jnp.zeros((MIX, LANE), F32)
    for n in range(nb_past):
        tot = page_refs[n * per_blk][0, 0:MIX, :]
        for p in range(n * per_blk + 1, (n + 1) * per_blk):
            tot = tot + page_refs[p][0, 0:MIX, :]
        col = jnp.sum(tot, axis=1, keepdims=True) * (1.0 / MOBA_BLOCK)
        kmean = kmean + jnp.where(lanei == n, col, 0.0)
    gate = jnp.dot(qx_f, kmean, precision=HI, preferred_element_type=F32)
    jl = lax.broadcasted_iota(jnp.int32, gate.shape, 1)
    cur = qpos // MOBA_BLOCK
    gate = jnp.where(jl < cur, gate, NEG)
    nb = -(-(past + T_PAD) // MOBA_BLOCK)
    sel = jnp.where(_topn_mask(gate, nb, max(1, min(MOBA_TOPK, nb - 1))) & (jl < cur), 1.0, 0.0)
    s_tiles, v_tiles = [], []
    for p in range(npages):
        kt = page_refs[p][0, 0:MIX, :].astype(BF16)
        v_tiles.append((page_refs[p][0, MIX:2 * MIX, :].astype(BF16), True))
        n = (p * page) // MOBA_BLOCK
        d = (qpos - (p * page + lane)).astype(F32)
        valid = (sel[:, n:n + 1] > 0.5) & (d >= 0)
        s_tiles.append(jnp.where(valid, jnp.dot(qx, kt, preferred_element_type=F32) - slope * d, NEG))
    k = _pad_rows(kn_ref[0], LANE).astype(BF16)
    v_tiles.append((_pad_rows(vn_ref[0], LANE).astype(BF16), False))
    d = (t8 - lane).astype(F32)
    valid = (d >= 0) & (lane < T_PAD)
    s_tiles.append(jnp.where(valid, _dot_nt(qx, k) - slope * d, NEG))
    o_ref[0] = _collapse_heads(_attend_tiles(s_tiles, v_tiles))


def _fox_dec_kernel(pt_ref, q_ref, kn_ref, vn_ref, ln_ref, *rest, npages, past):
    logf_refs, page_refs, o_ref = rest[:npages], rest[npages:2 * npages], rest[2 * npages]
    t8, _ = _dec_rows()
    lane = lax.broadcasted_iota(jnp.int32, (1, LANE), 1)
    qx = (_expand_heads(q_ref[0]) * SCALE).astype(BF16)
    r = lax.broadcasted_iota(jnp.int32, (LANE, LANE), 0)
    c = lax.broadcasted_iota(jnp.int32, (LANE, LANE), 1)
    tri = jnp.where(r <= c, 1.0, 0.0).astype(F32)

    def head_rows(x):
        return jnp.concatenate([jnp.broadcast_to(x[h:h + 1], (T_PAD, x.shape[1])) for h in range(N_HEADS)], axis=0)

    carry = jnp.zeros((N_HEADS, 1), F32)
    cum = []
    for ref in list(logf_refs) + [ln_ref]:
        cs = jnp.dot(ref[0], tri, precision=HI, preferred_element_type=F32) + carry
        cum.append(cs)
        carry = cs[:, LANE - 1:LANE]
    c_ref = head_rows(cum[npages - 1][:, LANE - 1:LANE])

    s_tiles, v_tiles = [], []
    for p in range(npages):
        kt = page_refs[p][0, 0:MIX, :].astype(BF16)
        v_tiles.append((page_refs[p][0, MIX:2 * MIX, :].astype(BF16), True))
        s_tiles.append(jnp.dot(qx, kt, preferred_element_type=F32) + (c_ref - head_rows(cum[p])))
    k = _pad_rows(kn_ref[0], LANE).astype(BF16)
    v_tiles.append((_pad_rows(vn_ref[0], LANE).astype(BF16), False))
    valid = (lane <= t8) & (lane < T_PAD)
    s_tiles.append(jnp.where(valid, _dot_nt(qx, k) + (c_ref - head_rows(cum[npages])), NEG))
    o_ref[0] = _collapse_heads(_attend_tiles(s_tiles, v_tiles))


def _page_specs(pool, npages, first):
    blk = (1,) + pool.shape[1:]
    return [pl.BlockSpec(blk, lambda b, pt, p=p: (first + pt[b, p], 0, 0)) for p in range(npages)]


def _paged_mha_dec(kernel, name, c3, col, page_table, pool, first, extra_args, extra_specs, past):
    bs, npages = page_table.shape
    qb = col // MIX
    in_specs = [pl.BlockSpec((1, T_PAD, MIX), lambda b, pt: (b, 0, qb)),
                pl.BlockSpec((1, T_PAD, MIX), lambda b, pt: (b, 0, qb + 1)),
                pl.BlockSpec((1, T_PAD, MIX), lambda b, pt: (b, 0, qb + 2))]
    in_specs += extra_specs
    in_specs += _page_specs(pool, npages, first)
    return pl.pallas_call(
        functools.partial(kernel, npages=npages, past=past),
        grid_spec=pltpu.PrefetchScalarGridSpec(
            num_scalar_prefetch=1, grid=(bs,), in_specs=in_specs,
            out_specs=pl.BlockSpec((1, T_PAD, MIX), lambda b, pt: (b, 0, 0))),
        out_shape=jax.ShapeDtypeStruct((bs, T_PAD, MIX), F32),
        compiler_params=_cparams(("parallel",)), name=name)(
            page_table, c3, c3, c3, *extra_args, *([pool] * npages))


def _nsa_dec_kernel(pt_ref, q_ref, ns_ref, nw_ref, g_ref, cmp_ref, ov_ref, wc_ref, *rest, npages, past, nc, ns):
    page_refs, o_ref = rest[:npages], rest[npages]
    page = page_refs[0].shape[2]
    kw = KV_G * HEAD_DIM
    t8, slope = _dec_rows()
    qpos = past + t8
    lane = lax.broadcasted_iota(jnp.int32, (1, LANE), 1)
    lane8 = lax.broadcasted_iota(jnp.int32, (T_PAD, LANE), 1)
    q8 = q_ref[0] * SCALE
    rows = []
    for h in range(N_HEADS):
        x = q8[:, (h // 2) * LANE:(h // 2 + 1) * LANE]
        dst = h // HG
        if h % 2 != dst:
            x = pltpu.roll(x, HEAD_DIM, 1)
        rows.append(jnp.where((lane8 // HEAD_DIM) == dst, x, 0.0))
    qx = jnp.concatenate(rows, axis=0).astype(BF16)

    cm = cmp_ref[0]
    dist = (qpos - (lane * CMP_STRIDE + CMP_LEN - 1)).astype(F32)
    ok_c = (dist >= 0) & (lane < nc)
    s = jnp.where(ok_c, _dot_nt(qx, cm[:, 0:kw].astype(BF16)) - slope * dist, NEG)
    m = jnp.max(s, axis=1, keepdims=True)
    e = jnp.where(ok_c, jnp.exp(s - m), 0.0)
    p_c = e / jnp.maximum(jnp.sum(e, axis=1, keepdims=True), 1e-30)
    o_cmp = jnp.dot(p_c.astype(BF16), cm[:, kw:2 * kw].astype(BF16), preferred_element_type=F32)

    psum = []
    for g in range(KV_G):
        acc = p_c[g * HG * T_PAD:g * HG * T_PAD + T_PAD]
        for h in range(1, HG):
            r0 = (g * HG + h) * T_PAD
            acc = acc + p_c[r0:r0 + T_PAD]
        psum.append(acc)
    imp = jnp.dot(jnp.concatenate(psum, axis=0), ov_ref[...], precision=HI, preferred_element_type=F32)
    jl = lax.broadcasted_iota(jnp.int32, imp.shape, 1)
    tg = lax.broadcasted_iota(jnp.int32, (KV_G * T_PAD, 1), 0) % T_PAD
    cur = (past + tg) // SLC_BLOCK
    forced = (jl == 0) | (jl == cur) | (jl == cur - 1)
    imp = jnp.where(forced, FORCE, imp)
    imp = jnp.where(jl <= cur, imp, NEG)
    sel = jnp.where(_topn_mask(imp, ns, min(SLC_TOPN, ns)) & (jl <= cur), 1.0, 0.0)
    sel_rows = jnp.concatenate([sel[(h // HG) * T_PAD:(h // HG + 1) * T_PAD] for h in range(N_HEADS)], axis=0)

    def new_tile(ref, extra_valid):
        k = _pad_rows(ref[0, :, 0:kw], LANE).astype(BF16)
        v = _pad_rows(ref[0, :, kw:2 * kw], LANE).astype(BF16)
        d = (t8 - lane).astype(F32)
        valid = (d >= 0) & (lane < T_PAD) & extra_valid
        return jnp.where(valid, _dot_nt(qx, k) - slope * d, NEG), (v, False)

    s_tiles, v_tiles = [], []
    per = page // SLC_BLOCK
    for p in range(npages):
        kt = page_refs[p][0, 0:kw, :].astype(BF16)
        v_tiles.append((page_refs[p][0, kw:2 * kw, :].astype(BF16), True))
        picked = jnp.zeros((N_HEADS * T_PAD, LANE), jnp.bool_)
        for j in range(per):
            blk = p * per + j
            picked = picked | ((lane // SLC_BLOCK == j) & (sel_rows[:, blk:blk + 1] > 0.5))
        d = (qpos - (p * page + lane)).astype(F32)
        s_tiles.append(jnp.where(picked & (d >= 0), jnp.dot(qx, kt, preferred_element_type=F32) - slope * d, NEG))
    blk_new = past // SLC_BLOCK
    s_new, v_new = new_tile(ns_ref, sel_rows[:, blk_new:blk_new + 1] > 0.5)
    o_slc = _attend_tiles(s_tiles + [s_new], v_tiles + [v_new])

    s_tiles, v_tiles = [], []
    wb = wc_ref.shape[2]
    w_off = past - wb
    for j in range(wb // LANE):
        kt = wc_ref[0, 0:kw, j * LANE:(j + 1) * LANE].astype(BF16)
        v_tiles.append((wc_ref[0, kw:2 * kw, j * LANE:(j + 1) * LANE].astype(BF16), True))
        d = qpos - (w_off + j * LANE + lane)
        valid = (d >= 0) & (d < WINDOW)
        s_tiles.append(jnp.where(valid, jnp.dot(qx, kt, preferred_element_type=F32) - slope * d.astype(F32), NEG))
    s_new, v_new = new_tile(nw_ref, True)
    o_win = _attend_tiles(s_tiles + [s_new], v_tiles + [v_new])

    sg = _sigmoid(g_ref[0])
    for h in range(N_HEADS):
        r0, l0 = h * T_PAD, (h // HG) * HEAD_DIM
        o = (sg[:, h:h + 1] * o_cmp[r0:r0 + T_PAD, l0:l0 + HEAD_DIM]
             + sg[:, N_HEADS + h:N_HEADS + h + 1] * o_slc[r0:r0 + T_PAD, l0:l0 + HEAD_DIM]
             + sg[:, 2 * N_HEADS + h:2 * N_HEADS + h + 1] * o_win[r0:r0 + T_PAD, l0:l0 + HEAD_DIM])
        o_ref[0, :, h * HEAD_DIM:(h + 1) * HEAD_DIM] = o


def _nsa_dec(c3, kvc, ov, win_cache, win_first, page_table, pool, first, past, t_real):
    bs, npages = page_table.shape
    kw = 2 * KV_G * HEAD_DIM
    nchunk = kvc.shape[1]
    nc = nchunk - CMP_LEN // CMP_STRIDE + 1
    ns = -(-(past + t_real) // SLC_BLOCK)
    wb = win_cache.shape[2]
    in_specs = [pl.BlockSpec((1, T_PAD, MIX), lambda b, pt: (b, 0, _COL['nsa_q'] // MIX)),
                pl.BlockSpec((1, T_PAD, kw), lambda b, pt: (b, 0, _COL['kv_slc'] // kw)),
                pl.BlockSpec((1, T_PAD, kw), lambda b, pt: (b, 0, _COL['kv_win'] // kw)),
                pl.BlockSpec((1, T_PAD, LANE), lambda b, pt: (b, 0, _COL['nsa_gate'] // LANE)),
                pl.BlockSpec((1, nchunk, kw), lambda b, pt: (b, 0, 0)),
                pl.BlockSpec((LANE, LANE), lambda b, pt: (0, 0)),
                pl.BlockSpec((1, kw, wb), lambda b, pt: (win_first + b, 0, 0))]
    in_specs += _page_specs(pool, npages, first)
    return pl.pallas_call(
        functools.partial(_nsa_dec_kernel, npages=npages, past=past, nc=nc, ns=ns),
        grid_spec=pltpu.PrefetchScalarGridSpec(
            num_scalar_prefetch=1, grid=(bs,), in_specs=in_specs,
            out_specs=pl.BlockSpec((1, T_PAD, MIX), lambda b, pt: (b, 0, 0))),
        out_shape=jax.ShapeDtypeStruct((bs, T_PAD, MIX), F32),
        compiler_params=_cparams(("parallel",)), name="nsa_decode")(
            page_table, c3, c3, c3, c3, kvc, ov, win_cache, *([pool] * npages))


def _s5_disc_kernel(ar_ref, ai_ref, ls_ref, btr_ref, bti_ref, abr_ref, abi_ref, bbr_ref, bbi_ref):
    ar, ai = ar_ref[...], ai_ref[...]
    step = jnp.exp(ls_ref[...])
    mag = jnp.exp(ar * step)
    abr = mag * jnp.cos(ai * step)
    abi = mag * jnp.sin(ai * step)
    den = ar * ar + ai * ai
    zr = (ar * (abr - 1.0) + ai * abi) / den
    zi = (ar * abi - ai * (abr - 1.0)) / den
    abr_ref[...] = abr
    abi_ref[...] = abi
    btr, bti = btr_ref[...], bti_ref[...]
    bbr_ref[...] = zr * btr - zi * bti
    bbi_ref[...] = zr * bti + zi * btr


def _s5_disc(a_re, a_im, log_step, b_re, b_im):
    rep = lambda a: jnp.repeat(a, S5_CH, axis=0)
    n = S5_GROUPS * S5_CH
    args = (rep(a_re), rep(a_im), rep(jnp.broadcast_to(log_step[:, None], (S5_GROUPS, S5_STATE))),
            b_re.transpose(0, 2, 1).reshape(n, S5_STATE), b_im.transpose(0, 2, 1).reshape(n, S5_STATE))
    shp = jax.ShapeDtypeStruct((n, S5_STATE), F32)
    return pl.pallas_call(_s5_disc_kernel, out_shape=(shp, shp, shp, shp), name="s5_discretise")(*args)


_S5_CH_ROWS = 128


def _s5_scan_kernel(*refs, seg, has_h0):
    if has_h0:
        (u_ref, bre_ref, bim_ref, ar_ref, ai_ref, cre_ref, cim_ref, d_ref, h0r_ref, h0i_ref,
         y_ref, hlr_ref, hli_ref, hr_s, hi_s) = refs
    else:
        (u_ref, bre_ref, bim_ref, ar_ref, ai_ref, cre_ref, cim_ref, d_ref,
         y_ref, hlr_ref, hli_ref, hr_s, hi_s) = refs
    rows = u_ref.shape[0]
    ch = min(_S5_CH_ROWS, rows)
    pad = ch
    nch = rows // ch
    ar, ai = ar_ref[0], ai_ref[0]
    bre, bim, cre, cim = (r[0].astype(BF16) for r in (bre_ref, bim_ref, cre_ref, cim_ref))
    hr_s[0:pad, :] = jnp.zeros((pad, hr_s.shape[1]), F32)
    hi_s[0:pad, :] = jnp.zeros((pad, hi_s.shape[1]), F32)

    def init_body(i, _):
        r0 = pl.multiple_of(i * ch, ch)
        u = u_ref[pl.ds(r0, ch), :]
        ub = u.astype(BF16)
        br = jnp.dot(ub, bre, preferred_element_type=F32)
        bi = jnp.dot(ub, bim, preferred_element_type=F32)
        if has_h0:
            h0r, h0i = h0r_ref[pl.ds(r0, ch), :], h0i_ref[pl.ds(r0, ch), :]
            br = br + (ar * h0r - ai * h0i)
            bi = bi + (ar * h0i + ai * h0r)
        hr_s[pl.ds(pad + r0, ch), :] = br
        hi_s[pl.ds(pad + r0, ch), :] = bi
        return 0

    lax.fori_loop(0, nch, init_body, 0)

    rl = lax.broadcasted_iota(jnp.int32, (ch, 1), 0)
    pr, pi = ar, ai
    d = 1
    two_level = seg == rows and rows > ch
    span = ch if two_level else seg
    while d < span:
        first = d // ch

        def pass_body(i, _, d=d, pr=pr, pi=pi):
            r0 = pl.multiple_of((nch - 1 - i) * ch, ch)
            cr = hr_s[pl.ds(pad + r0, ch), :]
            ci = hi_s[pl.ds(pad + r0, ch), :]
            if d < SUBLANE:
                lo = pad - SUBLANE
                sr = pltpu.roll(hr_s[pl.ds(lo + r0, ch + SUBLANE), :], d, 0)[SUBLANE:]
                si = pltpu.roll(hi_s[pl.ds(lo + r0, ch + SUBLANE), :], d, 0)[SUBLANE:]
            else:
                sr = hr_s[pl.ds(pad + r0 - d, ch), :]
                si = hi_s[pl.ds(pad + r0 - d, ch), :]
            if d < ch or seg < rows:
                keep = ((r0 + rl) % span) >= d
                sr = jnp.where(keep, sr, 0.0)
                si = jnp.where(keep, si, 0.0)
            hr_s[pl.ds(pad + r0, ch), :] = cr + (pr * sr - pi * si)
            hi_s[pl.ds(pad + r0, ch), :] = ci + (pr * si + pi * sr)
            return 0

        lax.fori_loop(0, nch - first, pass_body, 0)
        pr, pi = pr * pr - pi * pi, 2.0 * pr * pi
        d *= 2

    if two_level:
        tr, ti = jnp.broadcast_to(ar, (ch, ar.shape[1])), jnp.broadcast_to(ai, (ch, ai.shape[1]))
        d = 1
        while d < ch:
            sr, si = pltpu.roll(tr, d, 0), pltpu.roll(ti, d, 0)
            keep = rl >= d
            tr, ti = jnp.where(keep, tr * sr - ti * si, tr), jnp.where(keep, tr * si + ti * sr, ti)
            d *= 2
        lasts = [(hr_s[pad + (c + 1) * ch - 1:pad + (c + 1) * ch, :], hi_s[pad + (c + 1) * ch - 1:pad + (c + 1) * ch, :])
                 for c in range(nch - 1)]
        kr, ki = lasts[0]
        for c in range(1, nch):
            if c > 1:
                lr, li = lasts[c - 1]
                kr, ki = lr + (pr * kr - pi * ki), li + (pr * ki + pi * kr)
            rows_c = slice(pad + c * ch, pad + (c + 1) * ch)
            hr_s[rows_c, :] = hr_s[rows_c, :] + (tr * kr - ti * ki)
            hi_s[rows_c, :] = hi_s[rows_c, :] + (tr * ki + ti * kr)

    def out_body(i, _):
        r0 = pl.multiple_of(i * ch, ch)
        hr = hr_s[pl.ds(pad + r0, ch), :]
        hi = hi_s[pl.ds(pad + r0, ch), :]
        y = (jnp.dot(hr.astype(BF16), cre, preferred_element_type=F32)
             - jnp.dot(hi.astype(BF16), cim, preferred_element_type=F32))
        y_ref[pl.ds(r0, ch), :] = y + d_ref[0] * u_ref[pl.ds(r0, ch), :]
        return 0

    lax.fori_loop(0, nch, out_body, 0)
    nl = hlr_ref.shape[0]
    hlr_ref[...] = hr_s[pad + rows - nl:pad + rows, :]
    hli_ref[...] = hi_s[pad + rows - nl:pad + rows, :]


def _s5_scan(c, sw, rows, seg, h0=None):
    n = c.shape[0]
    nt = n // rows
    gl = LANE // S5_CH
    lt = S5_GROUPS // gl
    w = gl * S5_STATE
    nl = (rows // seg) * SUBLANE if seg == SUBLANE else SUBLANE
    ub = _COL['s5_u'] // LANE
    in_specs = [pl.BlockSpec((rows, LANE), lambda i, j: (i, ub + j)),
                pl.BlockSpec((1, LANE, w), lambda i, j: (j, 0, 0)),
                pl.BlockSpec((1, LANE, w), lambda i, j: (j, 0, 0)),
                pl.BlockSpec((1, 1, w), lambda i, j: (j, 0, 0)),
                pl.BlockSpec((1, 1, w), lambda i, j: (j, 0, 0)),
                pl.BlockSpec((1, w, LANE), lambda i, j: (j, 0, 0)),
                pl.BlockSpec((1, w, LANE), lambda i, j: (j, 0, 0)),
                pl.BlockSpec((1, 1, LANE), lambda i, j: (j, 0, 0))]
    args = [c, sw['bre'], sw['bim'], sw['ar'], sw['ai'], sw['cre'], sw['cim'], sw['d']]
    if h0 is not None:
        in_specs += [pl.BlockSpec((rows, w), lambda i, j: (i, j))] * 2
        args += list(h0)
    hshape = jax.ShapeDtypeStruct((nt * nl, S5_GROUPS * S5_STATE), F32)
    return pl.pallas_call(
        functools.partial(_s5_scan_kernel, seg=seg, has_h0=h0 is not None), grid=(nt, lt),
        in_specs=in_specs,
        out_specs=(pl.BlockSpec((rows, LANE), lambda i, j: (i, j)),
                   pl.BlockSpec((nl, w), lambda i, j: (i, j)), pl.BlockSpec((nl, w), lambda i, j: (i, j))),
        out_shape=(jax.ShapeDtypeStruct((n, MIX), F32), hshape, hshape),
        scratch_shapes=[pltpu.VMEM((min(_S5_CH_ROWS, rows) + rows, w), F32)] * 2,
        compiler_params=_cparams(("parallel", "arbitrary")), name="s5_scan")(*args)


def _s5_weights(lp):
    abr, abi, bbr, bbi = _s5_disc(lp['s5_a_re'], lp['s5_a_im'], lp['s5_log_step'], lp['s5_b_re'], lp['s5_b_im'])
    gl = LANE // S5_CH
    lt = S5_GROUPS // gl
    eye = jnp.eye(gl, dtype=F32)

    def bdiag(bb):
        return jnp.einsum('jgcn,gh->jgchn', bb.reshape(lt, gl, S5_CH, S5_STATE), eye).reshape(
            lt, gl * S5_CH, gl * S5_STATE)

    def cdiag(cc):
        return jnp.einsum('jgcn,gh->jgnhc', cc.reshape(lt, gl, S5_CH, S5_STATE), eye).reshape(
            lt, gl * S5_STATE, gl * S5_CH)

    return dict(bre=bdiag(bbr), bim=bdiag(bbi),
                ar=abr[::S5_CH].reshape(lt, 1, gl * S5_STATE), ai=abi[::S5_CH].reshape(lt, 1, gl * S5_STATE),
                cre=cdiag(lp['s5_c_re']), cim=cdiag(lp['s5_c_im']), d=lp['s5_d'].reshape(lt, 1, LANE))


def _glu_kernel(y_ref, w_ref, o_ref):
    y = y_ref[...]
    g = 0.5 * y * (1.0 + jnp.tanh(np.float32(np.sqrt(2.0 / np.pi)) * (y + np.float32(0.044715) * (y * y * y))))
    z = jnp.dot(g.astype(BF16), w_ref[...], preferred_element_type=F32)
    o_ref[...] = z[:, :MIX] * _sigmoid(z[:, MIX:])


def _glu(y, w_l, tm):
    w, l = w_l
    n = y.shape[0]
    return pl.pallas_call(
        _glu_kernel, grid=(n // tm,),
        in_specs=[pl.BlockSpec((tm, MIX), lambda i: (i, 0)),
                  pl.BlockSpec((None, MIX, 2 * MIX), lambda i: (l, 0, 0))],
        out_specs=pl.BlockSpec((tm, MIX), lambda i: (i, 0)),
        out_shape=jax.ShapeDtypeStruct((n, MIX), F32),
        compiler_params=_cparams(("parallel",)), name="s5_glu")(y, w)


def _layer_norm(x, g, b):
    mu = jnp.mean(x, axis=-1, keepdims=True)
    xc = x - mu
    var = jnp.mean(xc * xc, axis=-1, keepdims=True)
    return xc * lax.rsqrt(var + LN_EPS) * g + b


def _merge_kernel(o0_ref, o1_ref, o2_ref, o3_ref, mg_ref, wb_ref, x_ref, wo_ref, g_ref, b_ref, out_ref, acc_ref,
                  *, alpha):
    i = pl.program_id(1)

    @pl.when(i == 0)
    def _():
        acc_ref[...] = jnp.zeros_like(acc_ref)

    for k, o_ref in enumerate((o0_ref, o1_ref, o2_ref, o3_ref)):
        @pl.when(i == k)
        def _(o_ref=o_ref):
            proj = jnp.dot(o_ref[...].astype(BF16), wb_ref[0], preferred_element_type=F32)
            acc_ref[...] += _sigmoid(mg_ref[...]) * proj

    @pl.when(i == N_BRANCH - 1)
    def _():
        mixed = jnp.dot(acc_ref[...].astype(BF16), wo_ref[...], preferred_element_type=F32)
        out_ref[...] = _layer_norm(alpha * x_ref[...] + mixed, g_ref[...], b_ref[...])


def _merge(outs, c, x, wb_l, wo_l, g, b, tm, alpha):
    (wb, l), wo = wb_l, wo_l[0]
    n = x.shape[0]
    o_spec = pl.BlockSpec((tm, MIX), lambda r, i: (r, 0))
    return pl.pallas_call(
        functools.partial(_merge_kernel, alpha=alpha), grid=(n // tm, N_BRANCH),
        in_specs=[o_spec, o_spec, o_spec, o_spec,
                  pl.BlockSpec((tm, D_MODEL), lambda r, i: (r, i)),
                  pl.BlockSpec((None, 1, MIX, D_MODEL), lambda r, i: (l, i, 0, 0)),
                  pl.BlockSpec((tm, D_MODEL), lambda r, i: (r, 0)),
                  pl.BlockSpec((None, D_MODEL, D_MODEL), lambda r, i: (l, 0, 0), pipeline_mode=pl.Buffered(1)),
                  pl.BlockSpec((1, D_MODEL), lambda r, i: (0, 0)),
                  pl.BlockSpec((1, D_MODEL), lambda r, i: (0, 0))],
        out_specs=pl.BlockSpec((tm, D_MODEL), lambda r, i: (r, 0)),
        out_shape=jax.ShapeDtypeStruct((n, D_MODEL), F32),
        scratch_shapes=[pltpu.VMEM((tm, D_MODEL), F32)],
        compiler_params=_cparams(("parallel", "arbitrary")), name="merge_out_ln")(
            *outs, c, wb, x, wo, g, b)


def _ffn_kernel(*refs, halo, seg, alpha):
    (h_ref, wg_ref, wv_ref, cwg_ref, cwv_ref, cbg_ref, cbv_ref, wd_ref, lg_ref, lb_ref) = refs[:10]
    pg_ref, pv_ref, out_ref, hb_ref, acc_ref = refs[10:]
    j = pl.program_id(1)

    @pl.when(j == 0)
    def _():
        hb_ref[...] = h_ref[...].astype(BF16)
        acc_ref[...] = jnp.zeros_like(acc_ref)

    hb = hb_ref[...]
    tm = hb.shape[0]
    tf = wg_ref.shape[1]
    rid = lax.broadcasted_iota(jnp.int32, (tm, 1), 0)
    rid8 = lax.broadcasted_iota(jnp.int32, (SUBLANE, 1), 0)

    def conv(u, cw, cb, prev):
        r1 = pltpu.roll(u, 1, 0)
        r2 = pltpu.roll(u, 2, 0)
        if halo:
            p6, p7 = prev[SUBLANE - 2:SUBLANE - 1], prev[SUBLANE - 1:SUBLANE]
            top1 = jnp.where(rid8 == 0, p7, r1[:SUBLANE])
            top2 = jnp.where(rid8 == 0, p6, jnp.where(rid8 == 1, p7, r2[:SUBLANE]))
            u1 = jnp.concatenate([top1, r1[SUBLANE:]], axis=0)
            u2 = jnp.concatenate([top2, r2[SUBLANE:]], axis=0)
        else:
            t = rid % seg
            state = jnp.where(t >= seg - (CONV_W - 1), prev, u)
            u1 = jnp.where(t >= 1, r1, pltpu.roll(state, tm - (seg - 1), 0))
            u2 = jnp.where(t >= 2, r2, pltpu.roll(state, tm - (seg - 2), 0))
        return cb + (cw[0:1] * u2 + cw[1:2] * u1 + cw[2:3] * u)

    ug = jnp.dot(hb, wg_ref[...], preferred_element_type=F32)
    uv = jnp.dot(hb, wv_ref[...], preferred_element_type=F32)
    gate = conv(ug, cwg_ref[...], cbg_ref[...], pg_ref[0] if halo else pg_ref[...])
    val = conv(uv, cwv_ref[...], cbv_ref[...], pv_ref[0] if halo else pv_ref[...])
    act = (gate * _sigmoid(gate) * val).astype(BF16)
    acc_ref[...] += jnp.dot(act, wd_ref[...], preferred_element_type=F32)

    @pl.when(j == pl.num_programs(1) - 1)
    def _():
        out_ref[...] = _layer_norm(alpha * h_ref[...] + acc_ref[...], lg_ref[...], lb_ref[...])


def _ffn(h, lw, tm, tf, alpha, prev=None, state_rows=None, seg=None):
    n = h.shape[0]
    nf = D_FF // tf
    halo = prev is not None
    (w_up, l), w_down = lw['w_up'], lw['w_down'][0]
    in_specs = [pl.BlockSpec((tm, D_MODEL), lambda r, j: (r, 0), pipeline_mode=pl.Buffered(1)),
                pl.BlockSpec((None, D_MODEL, tf), lambda r, j: (l, 0, j)),
                pl.BlockSpec((None, D_MODEL, tf), lambda r, j: (l, 0, nf + j)),
                pl.BlockSpec((CONV_W, tf), lambda r, j: (0, j)),
                pl.BlockSpec((CONV_W, tf), lambda r, j: (0, nf + j)),
                pl.BlockSpec((1, tf), lambda r, j: (0, j)),
                pl.BlockSpec((1, tf), lambda r, j: (0, nf + j)),
                pl.BlockSpec((None, tf, D_MODEL), lambda r, j: (l, j, 0)),
                pl.BlockSpec((1, D_MODEL), lambda r, j: (0, 0)),
                pl.BlockSpec((1, D_MODEL), lambda r, j: (0, 0))]
    args = [h, w_up, w_up, lw['conv_w'], lw['conv_w'], lw['conv_b'], lw['conv_b'], w_down,
            lw['ln2_g'], lw['ln2_b']]
    if halo:
        in_specs += [pl.BlockSpec((1, SUBLANE, tf), lambda r, j: (r, 0, j)),
                     pl.BlockSpec((1, SUBLANE, tf), lambda r, j: (r, 0, nf + j))]
        args += [prev, prev]
    else:
        in_specs += [pl.BlockSpec((tm, tf), lambda r, j: (r, j)), pl.BlockSpec((tm, tf), lambda r, j: (r, nf + j))]
        args += [state_rows, state_rows]
    return pl.pallas_call(
        functools.partial(_ffn_kernel, halo=halo, seg=seg, alpha=alpha), grid=(n // tm, nf),
        in_specs=in_specs,
        out_specs=pl.BlockSpec((tm, D_MODEL), lambda r, j: (r, 0), pipeline_mode=pl.Buffered(1)),
        out_shape=jax.ShapeDtypeStruct((n, D_MODEL), F32),
        scratch_shapes=[pltpu.VMEM((tm, D_MODEL), BF16), pltpu.VMEM((tm, D_MODEL), F32)],
        compiler_params=_cparams(("parallel", "arbitrary")), name="conv_ffn_ln")(*args)


def _prep_shared(p):
    return dict(w_in=_pack_w_in(p['w_in']), w_glu=p['s5_w_glu'].astype(BF16), w_branch=p['w_branch'].astype(BF16),
                w_out=p['w_out'].astype(BF16), w_up=p['ffn_w_up'].astype(BF16), w_down=p['ffn_w_down'].astype(BF16))


def _prep_layer(l, p, shared):
    lp = {k: v[l] for k, v in p.items() if k not in ('w_in', 's5_w_glu', 'w_branch', 'w_out', 'ffn_w_up', 'ffn_w_down')}
    lw = {k: (v, l) for k, v in shared.items()}
    lw.update(
        fox_b=jnp.zeros((1, LANE), F32).at[0, :N_HEADS].set(lp['fox_b_f']),
        cmp=_cmp_weights(lp['nsa_cmp_pos'], lp['nsa_cmp_wk'], lp['nsa_cmp_wv']),
        s5=_s5_weights(lp),
        ln1_g=lp['ln1_g'].reshape(1, -1), ln1_b=lp['ln1_b'].reshape(1, -1),
        conv_w=lp['ffn_conv_w'], conv_b=lp['ffn_conv_b'].reshape(1, -1),
        ln2_g=lp['ln2_g'].reshape(1, -1), ln2_b=lp['ln2_b'].reshape(1, -1))
    return lw


def _prompt_layer(x, b, t, lw, alpha):
    n = b * t
    kw = 2 * KV_G * HEAD_DIM
    tr = min(1024, n)
    c = _matmul(x, lw['w_in'], tr, 1536, w_rows_are_outputs=True)
    logf = _logf(c, lw['fox_b'], tr)
    cumr = _cumsum_prompt(logf.reshape(b, t, N_HEADS).transpose(0, 2, 1))
    o_fox = _fox_prompt(c, cumr[:, :, None, :], b, t, 512)
    o_moba = _moba_prompt(c, b, t, 2 * MOBA_BLOCK)
    kvc = _cmp_prompt(c, b, t, lw['cmp'])
    nchunk = t // CMP_STRIDE
    ov = _overlap_matrix(nchunk - CMP_LEN // CMP_STRIDE + 1, -(-t // SLC_BLOCK))
    o_nsa = _nsa_prompt(c, kvc, ov, b, t, 256)
    y_s5, hlr, hli = _s5_scan(c, lw['s5'], t, t)
    o_s5 = _glu(y_s5, lw['w_glu'], tr)
    h = _merge((o_nsa, o_s5, o_moba, o_fox), c, x, lw['w_branch'], lw['w_out'], lw['ln1_g'], lw['ln1_b'], 512, alpha)

    tm = min(1024, t)
    nt = n // tm
    edge = h.reshape(nt, tm, D_MODEL)[:, tm - (CONV_W - 1):].reshape(nt * (CONV_W - 1), D_MODEL)
    edge = _pad_rows(edge, -(-edge.shape[0] // SUBLANE) * SUBLANE)
    u_edge = _matmul(edge, lw['w_up'], edge.shape[0], 2 * D_FF // 8)[:nt * (CONV_W - 1)].reshape(nt, CONV_W - 1, 2 * D_FF)
    per_seq = t // tm
    conv_state = u_edge[per_seq - 1::per_seq]
    starts_seq = (jnp.arange(nt) % per_seq == 0)[:, None, None]
    prev = jnp.where(starts_seq, 0.0, jnp.roll(u_edge, 1, axis=0))
    prev = jnp.pad(prev, ((0, 0), (SUBLANE - (CONV_W - 1), 0), (0, 0)))
    y = _ffn(h, lw, tm, 512, alpha, prev=prev)

    win_rows = min(WINDOW, t)
    states = (
        _kv_state(c, b, t, _COL['kv_cmp'], kw),
        _kv_state(c, b, t, _COL['kv_slc'], kw),
        _kv_state(c, b, t, _COL['moba'] + MIX, 2 * MIX),
        _kv_state(c, b, t, _COL['fox'] + MIX, 2 * MIX),
        logf.reshape(b, t, N_HEADS),
        _kv_state(c, b, t, _COL['kv_win'], kw, t_from=t - win_rows),
        hlr.reshape(b, SUBLANE, S5_GROUPS, S5_STATE)[:, SUBLANE - 1],
        hli.reshape(b, SUBLANE, S5_GROUPS, S5_STATE)[:, SUBLANE - 1],
        conv_state)
    return y, states


def _sample_layer(x, bs, t_real, lw, alpha, past, page_table, past_len):
    n = bs * T_PAD
    kw = 2 * KV_G * HEAD_DIM
    c = _matmul(x, lw['w_in'], n, 1536, w_rows_are_outputs=True)
    c3 = c.reshape(bs, T_PAD, WP)
    logf = _logf(c, lw['fox_b'], n)
    logf3 = logf.reshape(bs, T_PAD, N_HEADS)
    tmask = (jnp.arange(T_PAD) < t_real)[None, :, None]
    new_t = jnp.pad(jnp.where(tmask, logf3, 0.0).transpose(0, 2, 1), ((0, 0), (0, 0), (0, LANE - T_PAD)))
    first = past['first_page']
    npages = page_table.shape[1]
    o_fox = _paged_mha_dec(
        _fox_dec_kernel, "fox_decode", c3, _COL['fox'], page_table, past['fox'], first,
        [new_t] + [past['fox_logf']] * npages,
        [pl.BlockSpec((1, N_HEADS, LANE), lambda b, pt: (b, 0, 0))] + _page_specs(past['fox_logf'], npages, first),
        past_len)
    o_moba = _paged_mha_dec(_moba_dec_kernel, "moba_decode", c3, _COL['moba'], page_table, past['moba'], first,
                            [], [], past_len)
    kvc = _cmp_paged(page_table, past['nsa_cmp'], first, lw['cmp'])
    nchunk = kvc.shape[1]
    ov = _overlap_matrix(nchunk - CMP_LEN // CMP_STRIDE + 1, -(-(past_len + t_real) // SLC_BLOCK))
    o_nsa = _nsa_dec(c3, kvc, ov, past['nsa_win'], past['first_seq'], page_table, past['nsa_slc'], first,
                     past_len, t_real)
    h0 = [jnp.pad(s.reshape(bs, 1, -1), ((0, 0), (0, T_PAD - 1), (0, 0))).reshape(n, -1) for s in past['s5']]
    y_s5, hlr, hli = _s5_scan(c, lw['s5'], n, T_PAD, h0=h0)
    o_s5 = _glu(y_s5, lw['w_glu'], n)
    h = _merge((o_nsa.reshape(n, MIX), o_s5, o_moba.reshape(n, MIX), o_fox.reshape(n, MIX)), c, x,
               lw['w_branch'], lw['w_out'], lw['ln1_g'], lw['ln1_b'], min(512, n), alpha)

    buf = past['ffn_conv']
    state_rows = jnp.pad(buf, ((0, 0), (T_PAD - (CONV_W - 1), 0), (0, 0))).reshape(n, 2 * D_FF)
    y = _ffn(h, lw, min(512, n), 512, alpha, state_rows=state_rows, seg=T_PAD)
    last2 = h.reshape(bs, T_PAD, D_MODEL)[:, t_real - (CONV_W - 1):t_real].reshape(bs * (CONV_W - 1), D_MODEL)
    conv_state = _matmul(last2, lw['w_up'], last2.shape[0], 2 * D_FF // 8).reshape(bs, CONV_W - 1, 2 * D_FF)

    tr = lambda a: a[:, :t_real]
    kv_win_new = tr(c3[:, :, _COL['kv_win']:_COL['kv_win'] + kw])
    states = (
        tr(c3[:, :, _COL['kv_cmp']:_COL['kv_cmp'] + kw]).reshape(bs, t_real, 2, KV_G, HEAD_DIM),
        tr(c3[:, :, _COL['kv_slc']:_COL['kv_slc'] + kw]).reshape(bs, t_real, 2, KV_G, HEAD_DIM),
        tr(c3[:, :, _COL['moba'] + MIX:_COL['moba'] + 3 * MIX]).reshape(bs, t_real, 2, N_HEADS, HEAD_DIM),
        tr(c3[:, :, _COL['fox'] + MIX:_COL['fox'] + 3 * MIX]).reshape(bs, t_real, 2, N_HEADS, HEAD_DIM),
        tr(logf3),
        kv_win_new,
        hlr.reshape(bs, T_PAD, S5_GROUPS, S5_STATE)[:, t_real - 1],
        hli.reshape(bs, T_PAD, S5_GROUPS, S5_STATE)[:, t_real - 1],
        conv_state)
    return y, states


def kernel(x_prompt, x_sample, cache_nsa_cmp_kv, cache_nsa_slc_kv, cache_moba_kv, cache_fox_kv, cache_fox_logf,
           page_table, cache_nsa_win_kv, state_s5_re, state_s5_im, state_ffn_conv, w_in, fox_b_f, nsa_cmp_pos,
           nsa_cmp_wk, nsa_cmp_wv, s5_a_re, s5_a_im, s5_b_re, s5_b_im, s5_c_re, s5_c_im, s5_d, s5_log_step,
           s5_w_glu, w_branch, w_out, ln1_g, ln1_b, ffn_w_up, ffn_conv_w, ffn_conv_b, ffn_w_down, ln2_g, ln2_b):
    depth = w_in.shape[0]
    b, t, d = x_prompt.shape
    bs, ts, _ = x_sample.shape
    n_phys, page = cache_nsa_cmp_kv.shape[1:3]
    past_len = page_table.shape[1] * page
    assert d == D_MODEL and w_in.shape[2] == IN_WIDTH and ffn_w_down.shape[1] == D_FF
    assert ts <= T_PAD - (CONV_W - 1) and past_len % MOBA_BLOCK == 0 and page == LANE and (past_len + ts) // CMP_STRIDE * CMP_STRIDE <= past_len
    alpha = float((2 * depth) ** 0.25)
    params = dict(w_in=w_in, fox_b_f=fox_b_f, nsa_cmp_pos=nsa_cmp_pos, nsa_cmp_wk=nsa_cmp_wk, nsa_cmp_wv=nsa_cmp_wv,
                  s5_a_re=s5_a_re, s5_a_im=s5_a_im, s5_b_re=s5_b_re, s5_b_im=s5_b_im, s5_c_re=s5_c_re,
                  s5_c_im=s5_c_im, s5_d=s5_d, s5_log_step=s5_log_step, s5_w_glu=s5_w_glu, w_branch=w_branch,
                  w_out=w_out, ln1_g=ln1_g, ln1_b=ln1_b, ffn_w_up=ffn_w_up, ffn_conv_w=ffn_conv_w,
                  ffn_conv_b=ffn_conv_b, ffn_w_down=ffn_w_down, ln2_g=ln2_g, ln2_b=ln2_b)
    kw = 2 * KV_G * HEAD_DIM
    yp = x_prompt.reshape(b * t, d)
    ys = jnp.pad(x_sample, ((0, 0), (0, T_PAD - ts), (0, 0))).reshape(bs * T_PAD, d)
    st_p, st_s = [], []

    def feature_major(cache):
        dd, nn, rr = cache.shape[:3]
        return cache.transpose(0, 1, 3, 4, 5, 2).reshape(dd * nn, -1, rr)

    cmp_fm, slc_fm, moba_fm, fox_fm, win_fm = (
        feature_major(a) for a in (cache_nsa_cmp_kv, cache_nsa_slc_kv, cache_moba_kv, cache_fox_kv, cache_nsa_win_kv))
    logf_hm = cache_fox_logf.transpose(0, 1, 3, 2).reshape(depth * n_phys, N_HEADS, page)
    shared = _prep_shared(params)
    for l in range(depth):
        lw = _prep_layer(l, params, shared)
        past = dict(
            nsa_cmp=cmp_fm, nsa_slc=slc_fm, moba=moba_fm, fox=fox_fm, fox_logf=logf_hm, nsa_win=win_fm,
            first_page=l * n_phys, first_seq=l * bs,
            s5=(state_s5_re[l], state_s5_im[l]),
            ffn_conv=state_ffn_conv[l])
        yp, sp = _prompt_layer(yp, b, t, lw, alpha)
        ys, ss = _sample_layer(ys, bs, ts, lw, alpha, past, page_table, past_len)
        st_p.append(sp)
        st_s.append(ss)
    sp = [jnp.stack(z) for z in zip(*st_p)]
    ss = [jnp.stack(z) for z in zip(*st_s)]
    wb = win_fm.shape[2]
    full_win = jnp.concatenate([win_fm.reshape(depth, bs, kw, wb), ss[5].transpose(0, 1, 3, 2)], axis=3)
    keep = min(WINDOW, wb + ts)
    ss[5] = full_win[..., wb + ts - keep:].transpose(0, 1, 3, 2).reshape(depth, bs, keep, 2, KV_G, HEAD_DIM)
    out = [yp.reshape(b, t, d), ys.reshape(bs, T_PAD, d)[:, :ts]]
    for a, c in zip(sp, ss):
        out += [a, c]
    return tuple(out)
```
